```python
import math
import jax, jax.numpy as jnp
from jax import lax
import numpy as np


D_MODEL = 1024
BATCH = 8
SEQ = 16384
DEPTH = 2

N_META = 16
CHUNK = 128
PAD = CHUNK - N_META
EPS = 1e-6

CONV_A_WIDTH = D_MODEL
CONV_A_K = 3
SSD_HEAD_DIM = 64
SSD_HEADS = D_MODEL // SSD_HEAD_DIM
SSD_INNER = SSD_HEADS * SSD_HEAD_DIM
SSD_GROUPS = 4
SSD_STATE = 128
SSD_CONV_K = 4
SSD_CONV_DIM = SSD_INNER + 2 * SSD_GROUPS * SSD_STATE
RET_HEADS = 4
RET_QK_DIM = 256
RET_V_DIM = D_MODEL // RET_HEADS
RET_WIDTH = RET_HEADS * RET_V_DIM
ROPE_BASE = 10000.0
SB_HEADS = 8
SB_HEAD_DIM = D_MODEL // SB_HEADS
SB_WIDTH = SB_HEADS * SB_HEAD_DIM
N_BRANCH = 4
BRANCH_WIDTH = D_MODEL
D_FF = ((8 * D_MODEL // 3 + 255) // 256) * 256

IN_SIZES = (
    CONV_A_WIDTH, CONV_A_WIDTH, CONV_A_WIDTH,
    SSD_INNER, SSD_CONV_DIM, SSD_HEADS,
    RET_HEADS * RET_QK_DIM, RET_HEADS * RET_QK_DIM,
    RET_WIDTH, RET_WIDTH,
    SB_WIDTH, SB_WIDTH, SB_WIDTH,
    N_BRANCH * D_MODEL,
)
IN_WIDTH = sum(IN_SIZES)
IN_SPLITS = tuple(np.cumsum(IN_SIZES)[:-1].tolist())

kernel_name = 'hybrid_gated_conv_ssd_retention_stickbreaking'


def rmsnorm(x, w):
    xf = x.astype(jnp.float32)
    y = xf * lax.rsqrt(jnp.mean(xf * xf, axis=-1, keepdims=True) + EPS)
    return (y * w.astype(jnp.float32)).astype(x.dtype)


def causal_dwconv(u, w):
    k_taps = w.shape[0]
    length = u.shape[1]
    up = jnp.pad(u, ((0, 0), (k_taps - 1, 0), (0, 0)))
    out = up[:, 0:length] * w[0]
    for i in range(1, k_taps):
        out = out + up[:, i:i + length] * w[i]
    return out


def short_conv_mixer(b_gate, c_gate, xa, conv_w, valid):
    vm = valid[None, :, None].astype(xa.dtype)
    u = c_gate * xa * vm
    return (b_gate * causal_dwconv(u, conv_w)).astype(xa.dtype)


def ssd_mixer(z, xbc, dt_raw, conv_w, conv_b, dt_bias, a_log, d_skip, norm_w, valid):
    f32 = jnp.float32
    b, L, _ = z.shape
    nc = L // CHUNK
    hpg = SSD_HEADS // SSD_GROUPS
    vm = valid[None, :, None].astype(xbc.dtype)
    xbc = jax.nn.silu(causal_dwconv(xbc * vm, conv_w) + conv_b)
    xs, bm, cm = jnp.split(xbc, (SSD_INNER, SSD_INNER + SSD_GROUPS * SSD_STATE), axis=-1)
    xs = (xs * vm).astype(f32).reshape(b, nc, CHUNK, SSD_GROUPS, hpg, SSD_HEAD_DIM)
    bc = bm.astype(f32).reshape(b, nc, CHUNK, SSD_GROUPS, SSD_STATE)
    cc = cm.astype(f32).reshape(b, nc, CHUNK, SSD_GROUPS, SSD_STATE)
    dt = jax.nn.softplus(dt_raw.astype(f32) + dt_bias.astype(f32))
    a = (-jnp.exp(a_log.astype(f32)) * dt).reshape(b, nc, CHUNK, SSD_GROUPS, hpg)
    xdt = xs * dt.reshape(b, nc, CHUNK, SSD_GROUPS, hpg)[..., None]
    acs = jnp.moveaxis(jnp.cumsum(a, axis=2), 2, -1)
    causal = jnp.tril(jnp.ones((CHUNK, CHUNK), dtype=bool))
    seg = jnp.exp(jnp.where(causal, acs[..., :, None] - acs[..., None, :], -jnp.inf))
    cb = jnp.einsum('bclgn,bcsgn->bcgls', cc, bc)
    y_diag = jnp.einsum('bcgjls,bcsgjp->bclgjp', cb[:, :, :, None] * seg, xdt)
    decay_states = jnp.exp(acs[..., -1:] - acs)
    states = jnp.einsum('bclgn,bcgjl,bclgjp->bcgjpn', bc, decay_states, xdt)
    chunk_decay = jnp.exp(acs[..., -1])

    def step(hstate, inp):
        st, dec = inp
        return hstate * dec[..., None, None] + st, hstate

    h0 = jnp.zeros((b, SSD_GROUPS, hpg, SSD_HEAD_DIM, SSD_STATE), f32)
    _, prev = lax.scan(step, h0, (jnp.moveaxis(states, 1, 0), jnp.moveaxis(chunk_decay, 1, 0)))
    prev = jnp.moveaxis(prev, 0, 1)
    y_off = jnp.einsum('bclgn,bcgjpn,bcgjl->bclgjp', cc, prev, jnp.exp(acs))
    y = y_diag + y_off + xs * d_skip.astype(f32).reshape(SSD_GROUPS, hpg)[..., None]
    y = y.reshape(b, L, SSD_INNER) * jax.nn.silu(z.astype(f32))
    yg = y.reshape(b, L, SSD_GROUPS, SSD_INNER // SSD_GROUPS)
    yg = yg * lax.rsqrt(jnp.mean(yg * yg, axis=-1, keepdims=True) + EPS)
    return (yg.reshape(b, L, SSD_INNER) * norm_w.astype(f32)).astype(z.dtype)


def rotate(x, pos):
    half = x.shape[-1] // 2
    inv = ROPE_BASE ** (-jnp.arange(half, dtype=jnp.float32) / half)
    ang = pos.astype(jnp.float32)[:, None] * inv[None, :]
    cos = jnp.cos(ang)[None, :, None, :]
    sin = jnp.sin(ang)[None, :, None, :]
    x1, x2 = x[..., :half], x[..., half:]
    return jnp.concatenate([x1 * cos - x2 * sin, x1 * sin + x2 * cos], axis=-1)


def retention_mixer(q, k, v, g, valid):
    f32 = jnp.float32
    b, L, _ = q.shape
    nc = L // CHUNK
    pos = jnp.arange(L)
    qr = rotate(q.astype(f32).reshape(b, L, RET_HEADS, RET_QK_DIM), pos)
    kr = rotate(k.astype(f32).reshape(b, L, RET_HEADS, RET_QK_DIM), pos) * (RET_QK_DIM ** -0.5)
    vr = v.astype(f32).reshape(b, L, RET_HEADS, RET_V_DIM) * valid.astype(f32)[None, :, None, None]
    log_gamma = jnp.log(1.0 - jnp.power(2.0, -5.0 - jnp.arange(RET_HEADS, dtype=f32)))
    idx = jnp.arange(CHUNK, dtype=f32)
    rel = idx[:, None] - idx[None, :]
    dmask = jnp.where(rel >= 0, jnp.exp(log_gamma[:, None, None] * jnp.maximum(rel, 0.0)), 0.0)
    qc = qr.reshape(b, nc, CHUNK, RET_HEADS, RET_QK_DIM)
    kc = kr.reshape(b, nc, CHUNK, RET_HEADS, RET_QK_DIM)
    vc = vr.reshape(b, nc, CHUNK, RET_HEADS, RET_V_DIM)
    scores = jnp.einsum('bclhd,bcshd->bchls', qc, kc) * dmask
    y_in = jnp.einsum('bchls,bcshe->bclhe', scores, vc)
    k_decay = jnp.exp(log_gamma[:, None] * (CHUNK - 1 - idx)[None, :])
    kv = jnp.einsum('bcshd,hs,bcshe->bchde', kc, k_decay, vc)
    chunk_decay = jnp.exp(log_gamma * CHUNK)

    def step(r, kv_c):
        return r * chunk_decay[:, None, None] + kv_c, r

    r0 = jnp.zeros((b, RET_HEADS, RET_QK_DIM, RET_V_DIM), f32)
    _, prev = lax.scan(step, r0, jnp.moveaxis(kv, 1, 0))
    prev = jnp.moveaxis(prev, 0, 1)
    q_decay = jnp.exp(log_gamma[None, :] * (idx + 1.0)[:, None])
    y_cr = jnp.einsum('bclhd,bchde->bclhe', qc, prev) * q_decay[:, :, None]
    y = (y_in + y_cr).reshape(b, L, RET_HEADS, RET_V_DIM)
    mu = jnp.mean(y, axis=-1, keepdims=True)
    var = jnp.mean(jnp.square(y - mu), axis=-1, keepdims=True)
    y = ((y - mu) * lax.rsqrt(var + EPS)).reshape(b, L, RET_WIDTH)
    return (y * jax.nn.silu(g.astype(f32))).astype(q.dtype)


def stick_breaking_mixer(q, k, v, valid):
    f32 = jnp.float32
    b, L, _ = q.shape
    nb = L // CHUNK
    qh = q.reshape(b, L, SB_HEADS, SB_HEAD_DIM)
    kh = k.reshape(b, L, SB_HEADS, SB_HEAD_DIM)
    vh = v.reshape(b, L, SB_HEADS, SB_HEAD_DIM)
    qb = jnp.moveaxis(qh.reshape(b, nb, CHUNK, SB_HEADS, SB_HEAD_DIM), 1, 0)
    key_pos = jnp.arange(L)
    scale = SB_HEAD_DIM ** -0.5

    def block(args):
        q_blk, i = args
        t = i * CHUNK + jnp.arange(CHUNK)
        z = jnp.einsum('bthd,bshd->bhts', q_blk, kh).astype(f32) * scale
        m = (key_pos[None, :] < t[:, None]) & valid[None, :]
        l_neg = jnp.where(m, jax.nn.log_sigmoid(-z), 0.0)
        log_w = jax.nn.log_sigmoid(z) + lax.cumsum(l_neg, axis=3, reverse=True) - l_neg
        w = jnp.where(m, jnp.exp(log_w), 0.0)
        return jnp.einsum('bhts,bshe->bthe', w.astype(vh.dtype), vh)

    out = lax.map(block, (qb, jnp.arange(nb)))
    return jnp.moveaxis(out, 0, 1).reshape(b, L, SB_WIDTH).astype(q.dtype)


def hybrid_layer(h_res, valid, w_in, conv_a, ssd_conv_w, ssd_conv_b, ssd_dt_bias, ssd_a_log,
                 ssd_d, ssd_norm, w_branch, w_out, w_ffn_in, w_ffn_out,
                 n_mix_pre, n_mix_post, n_ffn_pre, n_ffn_post):
    b, L, _ = h_res.shape
    h = rmsnorm(h_res, n_mix_pre)
    proj = jnp.einsum('bld,de->ble', h, w_in)
    (a_b, a_c, a_x, s_z, s_xbc, s_dt, r_q, r_k, r_v, r_g,
     sb_q, sb_k, sb_v, gate_logits) = jnp.split(proj, IN_SPLITS, axis=-1)
    y_a = short_conv_mixer(a_b, a_c, a_x, conv_a, valid)
    y_b = ssd_mixer(s_z, s_xbc, s_dt, ssd_conv_w, ssd_conv_b, ssd_dt_bias, ssd_a_log, ssd_d, ssd_norm, valid)
    y_c = retention_mixer(r_q, r_k, r_v, r_g, valid)
    y_d = stick_breaking_mixer(sb_q, sb_k, sb_v, valid)
    branches = jnp.stack([y_a, y_b, y_c, y_d], axis=2).astype(h.dtype)
    up = jnp.einsum('blnw,nwd->blnd', branches, w_branch)
    gates = jax.nn.sigmoid(gate_logits.reshape(b, L, N_BRANCH, D_MODEL))
    merged = jnp.sum(gates * up, axis=2)
    mix = jnp.einsum('bld,de->ble', merged, w_out)
    h_res = h_res + rmsnorm(mix, n_mix_post)
    f = jnp.einsum('bld,df->blf', rmsnorm(h_res, n_ffn_pre), w_ffn_in)
    f_gate, f_up = jnp.split(f, 2, axis=-1)
    f = jnp.einsum('blf,fd->bld', jax.nn.silu(f_gate) * f_up, w_ffn_out)
    return h_res + rmsnorm(f, n_ffn_post)


def _fwd_setup_inputs(seed: int = 0) -> dict:
    key = jax.random.key(seed)
    ks = jax.random.split(key, 18)
    f32 = jnp.float32

    def nrm(k, shape, scale):
        return jax.random.normal(k, shape, f32) * scale

    dt0 = jnp.exp(jax.random.uniform(ks[6], (DEPTH, SSD_HEADS), f32, math.log(1e-3), math.log(1e-1)))
    return {
        'x': nrm(ks[0], (BATCH, SEQ, D_MODEL), 1.0),
        'meta': nrm(ks[1], (N_META, D_MODEL), 1.0),
        'w_in': nrm(ks[2], (DEPTH, D_MODEL, IN_WIDTH), D_MODEL ** -0.5),
        'conv_a': nrm(ks[3], (DEPTH, CONV_A_K, CONV_A_WIDTH), CONV_A_K ** -0.5),
        'ssd_conv_w': nrm(ks[4], (DEPTH, SSD_CONV_K, SSD_CONV_DIM), SSD_CONV_K ** -0.5),
        'ssd_conv_b': nrm(ks[5], (DEPTH, SSD_CONV_DIM), 0.02),
        'ssd_dt_bias': dt0 + jnp.log(-jnp.expm1(-dt0)),
        'ssd_a_log': jnp.log(jax.random.uniform(ks[7], (DEPTH, SSD_HEADS), f32, 1.0, 16.0)),
        'ssd_d': 1.0 + nrm(ks[8], (DEPTH, SSD_HEADS), 0.1),
        'ssd_norm': 1.0 + nrm(ks[9], (DEPTH, SSD_INNER), 0.02),
        'w_branch': nrm(ks[10], (DEPTH, N_BRANCH, BRANCH_WIDTH, D_MODEL), BRANCH_WIDTH ** -0.5),
        'w_out': nrm(ks[11], (DEPTH, D_MODEL, D_MODEL), D_MODEL ** -0.5),
        'w_ffn_in': nrm(ks[12], (DEPTH, D_MODEL, 2 * D_FF), D_MODEL ** -0.5),
        'w_ffn_out': nrm(ks[13], (DEPTH, D_FF, D_MODEL), D_FF ** -0.5),
        'norm_mix_pre': 1.0 + nrm(ks[14], (DEPTH, D_MODEL), 0.02),
        'norm_mix_post': 1.0 + nrm(ks[15], (DEPTH, D_MODEL), 0.02),
        'norm_ffn_pre': 1.0 + nrm(ks[16], (DEPTH, D_MODEL), 0.02),
        'norm_ffn_post': 1.0 + nrm(ks[17], (DEPTH, D_MODEL), 0.02),
    }


def _fwd_reference(x, meta, w_in, conv_a, ssd_conv_w, ssd_conv_b, ssd_dt_bias, ssd_a_log, ssd_d,
              ssd_norm, w_branch, w_out, w_ffn_in, w_ffn_out,
              norm_mix_pre, norm_mix_post, norm_ffn_pre, norm_ffn_post):
    b = x.shape[0]
    dtype = x.dtype
    h = jnp.concatenate([
        jnp.zeros((b, PAD, D_MODEL), dtype),
        jnp.broadcast_to(meta.astype(dtype)[None], (b, N_META, D_MODEL)),
        x,
    ], axis=1)
    valid = jnp.arange(h.shape[1]) >= PAD
    for l in range(DEPTH):
        h = hybrid_layer(h, valid, w_in[l], conv_a[l], ssd_conv_w[l], ssd_conv_b[l],
                         ssd_dt_bias[l], ssd_a_log[l], ssd_d[l], ssd_norm[l], w_branch[l],
                         w_out[l], w_ffn_in[l], w_ffn_out[l], norm_mix_pre[l],
                         norm_mix_post[l], norm_ffn_pre[l], norm_ffn_post[l])
    return h[:, CHUNK:]


import jax as _jax
import jax.numpy as _jnp

TWIN_FORMAT = 'train_step'
FWD_PARAMS = ['x', 'meta', 'w_in', 'conv_a', 'ssd_conv_w', 'ssd_conv_b', 'ssd_dt_bias', 'ssd_a_log', 'ssd_d', 'ssd_norm', 'w_branch', 'w_out', 'w_ffn_in', 'w_ffn_out', 'norm_mix_pre', 'norm_mix_post', 'norm_ffn_pre', 'norm_ffn_post']
TWIN_WEIGHTS = ['meta', 'w_in', 'conv_a', 'ssd_conv_w', 'ssd_conv_b', 'ssd_dt_bias', 'ssd_a_log', 'ssd_d', 'ssd_norm', 'w_branch', 'w_out', 'w_ffn_in', 'w_ffn_out', 'norm_mix_pre', 'norm_mix_post', 'norm_ffn_pre', 'norm_ffn_post']
TWIN_DIFF_INPUT = 'x'
TWIN_INPUTS = ['x', 'meta', 'w_in', 'conv_a', 'ssd_conv_w', 'ssd_conv_b', 'ssd_dt_bias', 'ssd_a_log', 'ssd_d', 'ssd_norm', 'w_branch', 'w_out', 'w_ffn_in', 'w_ffn_out', 'norm_mix_pre', 'norm_mix_post', 'norm_ffn_pre', 'norm_ffn_post', 'loss_target', 'm_meta', 'm_w_in', 'm_conv_a', 'm_ssd_conv_w', 'm_ssd_conv_b', 'm_ssd_dt_bias', 'm_ssd_a_log', 'm_ssd_d', 'm_ssd_norm', 'm_w_branch', 'm_w_out', 'm_w_ffn_in', 'm_w_ffn_out', 'm_norm_mix_pre', 'm_norm_mix_post', 'm_norm_ffn_pre', 'm_norm_ffn_post', 'v_meta', 'v_w_in', 'v_conv_a', 'v_ssd_conv_w', 'v_ssd_conv_b', 'v_ssd_dt_bias', 'v_ssd_a_log', 'v_ssd_d', 'v_ssd_norm', 'v_w_branch', 'v_w_out', 'v_w_ffn_in', 'v_w_ffn_out', 'v_norm_mix_pre', 'v_norm_mix_post', 'v_norm_ffn_pre', 'v_norm_ffn_post']
TWIN_OUTPUTS = ['loss', 'grad_x', 'grad_meta', 'grad_w_in', 'grad_conv_a', 'grad_ssd_conv_w', 'grad_ssd_conv_b', 'grad_ssd_dt_bias', 'grad_ssd_a_log', 'grad_ssd_d', 'grad_ssd_norm', 'grad_w_branch', 'grad_w_out', 'grad_w_ffn_in', 'grad_w_ffn_out', 'grad_norm_mix_pre', 'grad_norm_mix_post', 'grad_norm_ffn_pre', 'grad_norm_ffn_post', 'delta_meta', 'delta_w_in', 'delta_conv_a', 'delta_ssd_conv_w', 'delta_ssd_conv_b', 'delta_ssd_dt_bias', 'delta_ssd_a_log', 'delta_ssd_d', 'delta_ssd_norm', 'delta_w_branch', 'delta_w_out', 'delta_w_ffn_in', 'delta_w_ffn_out', 'delta_norm_mix_pre', 'delta_norm_mix_post', 'delta_norm_ffn_pre', 'delta_norm_ffn_post', 'new_m_meta', 'new_m_w_in', 'new_m_conv_a', 'new_m_ssd_conv_w', 'new_m_ssd_conv_b', 'new_m_ssd_dt_bias', 'new_m_ssd_a_log', 'new_m_ssd_d', 'new_m_ssd_norm', 'new_m_w_branch', 'new_m_w_out', 'new_m_w_ffn_in', 'new_m_w_ffn_out', 'new_m_norm_mix_pre', 'new_m_norm_mix_post', 'new_m_norm_ffn_pre', 'new_m_norm_ffn_post', 'new_v_meta', 'new_v_w_in', 'new_v_conv_a', 'new_v_ssd_conv_w', 'new_v_ssd_conv_b', 'new_v_ssd_dt_bias', 'new_v_ssd_a_log', 'new_v_ssd_d', 'new_v_ssd_norm', 'new_v_w_branch', 'new_v_w_out', 'new_v_w_ffn_in', 'new_v_w_ffn_out', 'new_v_norm_mix_pre', 'new_v_norm_mix_post', 'new_v_norm_ffn_pre', 'new_v_norm_ffn_post']
TWIN_LEAF_KINDS = {'loss': 'loss', 'grad_x': 'grad_x', 'grad_meta': 'grad_w', 'grad_w_in': 'grad_w', 'grad_conv_a': 'grad_w', 'grad_ssd_conv_w': 'grad_w', 'grad_ssd_conv_b': 'grad_w', 'grad_ssd_dt_bias': 'grad_w', 'grad_ssd_a_log': 'grad_w', 'grad_ssd_d': 'grad_w', 'grad_ssd_norm': 'grad_w', 'grad_w_branch': 'grad_w', 'grad_w_out': 'grad_w', 'grad_w_ffn_in': 'grad_w', 'grad_w_ffn_out': 'grad_w', 'grad_norm_mix_pre': 'grad_w', 'grad_norm_mix_post': 'grad_w', 'grad_norm_ffn_pre': 'grad_w', 'grad_norm_ffn_post': 'grad_w', 'delta_meta': 'delta_w', 'delta_w_in': 'delta_w', 'delta_conv_a': 'delta_w', 'delta_ssd_conv_w': 'delta_w', 'delta_ssd_conv_b': 'delta_w', 'delta_ssd_dt_bias': 'delta_w', 'delta_ssd_a_log': 'delta_w', 'delta_ssd_d': 'delta_w', 'delta_ssd_norm': 'delta_w', 'delta_w_branch': 'delta_w', 'delta_w_out': 'delta_w', 'delta_w_ffn_in': 'delta_w', 'delta_w_ffn_out': 'delta_w', 'delta_norm_mix_pre': 'delta_w', 'delta_norm_mix_post': 'delta_w', 'delta_norm_ffn_pre': 'delta_w', 'delta_norm_ffn_post': 'delta_w', 'new_m_meta': 'new_m', 'new_m_w_in': 'new_m', 'new_m_conv_a': 'new_m', 'new_m_ssd_conv_w': 'new_m', 'new_m_ssd_conv_b': 'new_m', 'new_m_ssd_dt_bias': 'new_m', 'new_m_ssd_a_log': 'new_m', 'new_m_ssd_d': 'new_m', 'new_m_ssd_norm': 'new_m', 'new_m_w_branch': 'new_m', 'new_m_w_out': 'new_m', 'new_m_w_ffn_in': 'new_m', 'new_m_w_ffn_out': 'new_m', 'new_m_norm_mix_pre': 'new_m', 'new_m_norm_mix_post': 'new_m', 'new_m_norm_ffn_pre': 'new_m', 'new_m_norm_ffn_post': 'new_m', 'new_v_meta': 'new_v', 'new_v_w_in': 'new_v', 'new_v_conv_a': 'new_v', 'new_v_ssd_conv_w': 'new_v', 'new_v_ssd_conv_b': 'new_v', 'new_v_ssd_dt_bias': 'new_v', 'new_v_ssd_a_log': 'new_v', 'new_v_ssd_d': 'new_v', 'new_v_ssd_norm': 'new_v', 'new_v_w_branch': 'new_v', 'new_v_w_out': 'new_v', 'new_v_w_ffn_in': 'new_v', 'new_v_w_ffn_out': 'new_v', 'new_v_norm_mix_pre': 'new_v', 'new_v_norm_mix_post': 'new_v', 'new_v_norm_ffn_pre': 'new_v', 'new_v_norm_ffn_post': 'new_v'}


def _forward(args):
    return _fwd_reference(*[args[k] for k in FWD_PARAMS])


def _output_shape():
    def fwd():
        inp = _fwd_setup_inputs(0)
        return _fwd_reference(*[inp[k] for k in FWD_PARAMS])
    out = _jax.eval_shape(fwd)
    return out.shape, out.dtype

N_MICROBATCH = 1
ADAM_LR = 0.001
ADAM_B1 = 0.9
ADAM_B2 = 0.999
ADAM_EPS = 1e-08
ADAM_WD = 0.01
ADAM_STEP = 10
PER_EXAMPLE_BATCH_AXIS = {'x': 0, 'loss_target': 0}
SHARED_INPUTS = []
_WEIGHT_DTYPES = {'meta': _jnp.float32, 'w_in': _jnp.float32, 'conv_a': _jnp.float32, 'ssd_conv_w': _jnp.float32, 'ssd_conv_b': _jnp.float32, 'ssd_dt_bias': _jnp.float32, 'ssd_a_log': _jnp.float32, 'ssd_d': _jnp.float32, 'ssd_norm': _jnp.float32, 'w_branch': _jnp.float32, 'w_out': _jnp.float32, 'w_ffn_in': _jnp.float32, 'w_ffn_out': _jnp.float32, 'norm_mix_pre': _jnp.float32, 'norm_mix_post': _jnp.float32, 'norm_ffn_pre': _jnp.float32, 'norm_ffn_post': _jnp.float32}
MOMENT_SCALE = {'meta': 1.105241e-01, 'w_in': 6.478410e-01, 'conv_a': 9.395290e-01, 'ssd_conv_w': 9.507108e-01, 'ssd_conv_b': 2.729273e+00, 'ssd_dt_bias': 1.546661e+00, 'ssd_a_log': 3.389423e+00, 'ssd_d': 9.530978e+00, 'ssd_norm': 1.821065e+00, 'w_branch': 1.004586e+00, 'w_out': 2.147654e+00, 'w_ffn_in': 9.033692e-01, 'w_ffn_out': 1.811258e+00, 'norm_mix_pre': 2.648064e+00, 'norm_mix_post': 1.274483e+02, 'norm_ffn_pre': 1.970084e+00, 'norm_ffn_post': 1.274498e+02}


def _to_microbatches(a, axis):
    t = _jnp.moveaxis(a, axis, 0)
    t = t.reshape((N_MICROBATCH, t.shape[0] // N_MICROBATCH) + t.shape[1:])
    return _jnp.moveaxis(t, 1, axis + 1)


def setup_inputs(seed: int = 0) -> dict:
    inp = _fwd_setup_inputs(seed)
    key = _jax.random.fold_in(_jax.random.key(seed), 7919)
    shape, _ = _output_shape()
    out = dict(inp)
    out["loss_target"] = _jax.random.normal(_jax.random.fold_in(key, 0), shape, _jnp.float32)
    for i, name in enumerate(TWIN_WEIGHTS):
        w = inp[name].astype(_jnp.float32)
        if MOMENT_SCALE is None:
            s = _jnp.sqrt(_jnp.mean(_jnp.square(w)) + 1e-30)
        else:
            s = MOMENT_SCALE[name]
        km, kv = _jax.random.split(_jax.random.fold_in(key, i + 1))
        out[name] = w
        out["m_" + name] = s * _jax.random.normal(km, w.shape, _jnp.float32)
        out["v_" + name] = (s * s) * _jax.random.uniform(kv, w.shape, _jnp.float32, 0.5, 1.5)
    if N_MICROBATCH > 1:
        for name, axis in PER_EXAMPLE_BATCH_AXIS.items():
            out[name] = _to_microbatches(out[name], axis)
    return {'x': out['x'], 'meta': out['meta'], 'w_in': out['w_in'], 'conv_a': out['conv_a'], 'ssd_conv_w': out['ssd_conv_w'], 'ssd_conv_b': out['ssd_conv_b'], 'ssd_dt_bias': out['ssd_dt_bias'], 'ssd_a_log': out['ssd_a_log'], 'ssd_d': out['ssd_d'], 'ssd_norm': out['ssd_norm'], 'w_branch': out['w_branch'], 'w_out': out['w_out'], 'w_ffn_in': out['w_ffn_in'], 'w_ffn_out': out['w_ffn_out'], 'norm_mix_pre': out['norm_mix_pre'], 'norm_mix_post': out['norm_mix_post'], 'norm_ffn_pre': out['norm_ffn_pre'], 'norm_ffn_post': out['norm_ffn_post'], 'loss_target': out['loss_target'], 'm_meta': out['m_meta'], 'm_w_in': out['m_w_in'], 'm_conv_a': out['m_conv_a'], 'm_ssd_conv_w': out['m_ssd_conv_w'], 'm_ssd_conv_b': out['m_ssd_conv_b'], 'm_ssd_dt_bias': out['m_ssd_dt_bias'], 'm_ssd_a_log': out['m_ssd_a_log'], 'm_ssd_d': out['m_ssd_d'], 'm_ssd_norm': out['m_ssd_norm'], 'm_w_branch': out['m_w_branch'], 'm_w_out': out['m_w_out'], 'm_w_ffn_in': out['m_w_ffn_in'], 'm_w_ffn_out': out['m_w_ffn_out'], 'm_norm_mix_pre': out['m_norm_mix_pre'], 'm_norm_mix_post': out['m_norm_mix_post'], 'm_norm_ffn_pre': out['m_norm_ffn_pre'], 'm_norm_ffn_post': out['m_norm_ffn_post'], 'v_meta': out['v_meta'], 'v_w_in': out['v_w_in'], 'v_conv_a': out['v_conv_a'], 'v_ssd_conv_w': out['v_ssd_conv_w'], 'v_ssd_conv_b': out['v_ssd_conv_b'], 'v_ssd_dt_bias': out['v_ssd_dt_bias'], 'v_ssd_a_log': out['v_ssd_a_log'], 'v_ssd_d': out['v_ssd_d'], 'v_ssd_norm': out['v_ssd_norm'], 'v_w_branch': out['v_w_branch'], 'v_w_out': out['v_w_out'], 'v_w_ffn_in': out['v_w_ffn_in'], 'v_w_ffn_out': out['v_w_ffn_out'], 'v_norm_mix_pre': out['v_norm_mix_pre'], 'v_norm_mix_post': out['v_norm_mix_post'], 'v_norm_ffn_pre': out['v_norm_ffn_pre'], 'v_norm_ffn_post': out['v_norm_ffn_post']}


def _loss(weights, diff, rest, loss_target):
    with _jax.named_scope("forward"):
        args = {**rest, TWIN_DIFF_INPUT: diff, **{k: w.astype(_WEIGHT_DTYPES[k]) for k, w in weights.items()}}
        y = _forward(args)
    with _jax.named_scope("loss_head"):
        err = _jnp.square(y.astype(_jnp.float32) - loss_target)
        return 0.5 * _jnp.sum(_jnp.mean(err, axis=-1)) if err.ndim else 0.5 * err


def _adamw(w, g, m, v):
    m = ADAM_B1 * m + (1.0 - ADAM_B1) * g
    v = ADAM_B2 * v + (1.0 - ADAM_B2) * _jnp.square(g)
    m_hat = m / (1.0 - ADAM_B1 ** ADAM_STEP)
    v_hat = v / (1.0 - ADAM_B2 ** ADAM_STEP)
    delta = -ADAM_LR * (m_hat / (_jnp.sqrt(v_hat) + ADAM_EPS) + ADAM_WD * w)
    return delta, m, v


def reference(x, meta, w_in, conv_a, ssd_conv_w, ssd_conv_b, ssd_dt_bias, ssd_a_log, ssd_d, ssd_norm, w_branch, w_out, w_ffn_in, w_ffn_out, norm_mix_pre, norm_mix_post, norm_ffn_pre, norm_ffn_post, loss_target, m_meta, m_w_in, m_conv_a, m_ssd_conv_w, m_ssd_conv_b, m_ssd_dt_bias, m_ssd_a_log, m_ssd_d, m_ssd_norm, m_w_branch, m_w_out, m_w_ffn_in, m_w_ffn_out, m_norm_mix_pre, m_norm_mix_post, m_norm_ffn_pre, m_norm_ffn_post, v_meta, v_w_in, v_conv_a, v_ssd_conv_w, v_ssd_conv_b, v_ssd_dt_bias, v_ssd_a_log, v_ssd_d, v_ssd_norm, v_w_branch, v_w_out, v_w_ffn_in, v_w_ffn_out, v_norm_mix_pre, v_norm_mix_post, v_norm_ffn_pre, v_norm_ffn_post):
    given = dict(x=x, meta=meta, w_in=w_in, conv_a=conv_a, ssd_conv_w=ssd_conv_w, ssd_conv_b=ssd_conv_b, ssd_dt_bias=ssd_dt_bias, ssd_a_log=ssd_a_log, ssd_d=ssd_d, ssd_norm=ssd_norm, w_branch=w_branch, w_out=w_out, w_ffn_in=w_ffn_in, w_ffn_out=w_ffn_out, norm_mix_pre=norm_mix_pre, norm_mix_post=norm_mix_post, norm_ffn_pre=norm_ffn_pre, norm_ffn_post=norm_ffn_post, loss_target=loss_target, m_meta=m_meta, m_w_in=m_w_in, m_conv_a=m_conv_a, m_ssd_conv_w=m_ssd_conv_w, m_ssd_conv_b=m_ssd_conv_b, m_ssd_dt_bias=m_ssd_dt_bias, m_ssd_a_log=m_ssd_a_log, m_ssd_d=m_ssd_d, m_ssd_norm=m_ssd_norm, m_w_branch=m_w_branch, m_w_out=m_w_out, m_w_ffn_in=m_w_ffn_in, m_w_ffn_out=m_w_ffn_out, m_norm_mix_pre=m_norm_mix_pre, m_norm_mix_post=m_norm_mix_post, m_norm_ffn_pre=m_norm_ffn_pre, m_norm_ffn_post=m_norm_ffn_post, v_meta=v_meta, v_w_in=v_w_in, v_conv_a=v_conv_a, v_ssd_conv_w=v_ssd_conv_w, v_ssd_conv_b=v_ssd_conv_b, v_ssd_dt_bias=v_ssd_dt_bias, v_ssd_a_log=v_ssd_a_log, v_ssd_d=v_ssd_d, v_ssd_norm=v_ssd_norm, v_w_branch=v_w_branch, v_w_out=v_w_out, v_w_ffn_in=v_w_ffn_in, v_w_ffn_out=v_w_ffn_out, v_norm_mix_pre=v_norm_mix_pre, v_norm_mix_post=v_norm_mix_post, v_norm_ffn_pre=v_norm_ffn_pre, v_norm_ffn_post=v_norm_ffn_post)
    weights = {n: given[n] for n in TWIN_WEIGHTS}
    shared = {n: given[n] for n in SHARED_INPUTS}
    per_example = {n: given[n] for n in ['x']}
    grad_fn = _jax.value_and_grad(_loss, argnums=(0, 1))

    def one_microbatch(ex, loss_target):
        ex = dict(ex)
        diff = ex.pop(TWIN_DIFF_INPUT)
        return grad_fn(weights, diff, {**shared, **ex}, loss_target)

    if N_MICROBATCH == 1:
        loss, (grad_w, grad_x) = one_microbatch(per_example, given["loss_target"])
    else:
        def body(carry, xs):
            loss_sum, grad_sum = carry
            l_k, (gw_k, gx_k) = one_microbatch(xs[0], xs[1])
            with _jax.named_scope("update"):
                return (loss_sum + l_k, _jax.tree.map(_jnp.add, grad_sum, gw_k)), gx_k

        init = (_jnp.zeros((), _jnp.float32), _jax.tree.map(_jnp.zeros_like, weights))
        (loss, grad_w), grad_x = _jax.lax.scan(body, init, (per_example, given["loss_target"]))
    with _jax.named_scope("update"):
        delta_w, new_m, new_v = {}, {}, {}
        for n in TWIN_WEIGHTS:
            delta_w[n], new_m[n], new_v[n] = _adamw(weights[n], grad_w[n], given["m_" + n], given["v_" + n])
    return (loss, grad_x, *[grad_w[n] for n in TWIN_WEIGHTS], *[delta_w[n] for n in TWIN_WEIGHTS],
            *[new_m[n] for n in TWIN_WEIGHTS], *[new_v[n] for n in TWIN_WEIGHTS])
```

```python
import functools
import math

import numpy as np
import jax
import jax.numpy as jnp
from jax import lax
from jax.experimental import pallas as pl
from jax.experimental.pallas import tpu as pltpu

F32, BF16 = jnp.float32, jnp.bfloat16
HI = lax.Precision.HIGHEST
MESH_ID = pl.DeviceIdType.MESH

D = 1024
CH = 128
N_META = 16
N_PAD = CH - N_META
EPS = 1e-6
N_DEV = 8
DEPTH = 2
SSD_HEADS = 16
RET_HEADS = 4
SB_HEADS = 8
D_FF = 2816
ROPE_BASE = 10000.0

NF = 14336
COL_GATE = 10

ADAM_LR, ADAM_B1, ADAM_B2, ADAM_EPS, ADAM_WD, ADAM_STEP = 0.001, 0.9, 0.999, 1e-08, 0.01, 10

VMEM_BYTES = 48 * 1024 * 1024


def _pick(n, cands):
    for c in cands:
        if n % c == 0:
            return c
    raise ValueError((n, cands))


def _tok_block(t):
    return _pick(t, (384, 128))


def _cparams(ngrid, vmem=VMEM_BYTES):
    return pltpu.CompilerParams(dimension_semantics=("arbitrary",) * ngrid, vmem_limit_bytes=vmem)


def _iota(shape, dim):
    return lax.broadcasted_iota(jnp.int32, shape, dim)


def _sigmoid(x):
    return 1.0 / (1.0 + jnp.exp(-x))


def _silu(x):
    return x * _sigmoid(x)


def _dsilu(x):
    s = _sigmoid(x)
    return s * (1.0 + x * (1.0 - s))


def _softplus(x):
    return jnp.maximum(x, 0.0) + jnp.log(1.0 + jnp.exp(-jnp.abs(x)))


def _dot(a, b):
    return jnp.dot(a.astype(BF16), b.astype(BF16), preferred_element_type=F32)


def _dot_nt(a, b):
    return lax.dot_general(a.astype(BF16), b.astype(BF16), (((1,), (1,)), ((), ())), preferred_element_type=F32)


def _dot_tn(a, b):
    return lax.dot_general(a.astype(BF16), b.astype(BF16), (((0,), (0,)), ((), ())), preferred_element_type=F32)


def _dot_hi(a, b):
    return jnp.dot(a, b, precision=HI, preferred_element_type=F32)


def mm(a, b, *, name, out_dtype=F32, add=None, tm=None, tn=None, tk=None):
    m, k = a.shape
    k2, n = b.shape
    assert k == k2
    tm = tm or _pick(m, (1376, 384, 128))
    tn = tn or _pick(n, (512, 384, 256, 128))
    tk = tk or _pick(k, (1024, 1408, 512, 384, 128))
    nk = k // tk
    has_add = add is not None

    def body(*refs):
        if has_add:
            a_ref, b_ref, c_ref, o_ref = refs[:4]
            scr = refs[4:]
        else:
            a_ref, b_ref, o_ref = refs[:3]
            c_ref = None
            scr = refs[3:]
        x = _dot(a_ref[...], b_ref[...])
        if nk == 1:
            if has_add:
                x = x + c_ref[...]
            o_ref[...] = x.astype(out_dtype)
        else:
            acc = scr[0]
            kk = pl.program_id(2)

            @pl.when(kk == 0)
            def _():
                acc[...] = x

            @pl.when(kk > 0)
            def _():
                acc[...] += x

            @pl.when(kk == nk - 1)
            def _():
                r = acc[...]
                if has_add:
                    r = r + c_ref[...]
                o_ref[...] = r.astype(out_dtype)

    in_specs = [pl.BlockSpec((tm, tk), lambda i, j, kk: (i, kk)), pl.BlockSpec((tk, tn), lambda i, j, kk: (kk, j))]
    args = [a, b]
    if has_add:
        in_specs.append(pl.BlockSpec((tm, tn), lambda i, j, kk: (i, j)))
        args.append(add)
    return pl.pallas_call(
        body, name=name, out_shape=jax.ShapeDtypeStruct((m, n), out_dtype), grid=(m // tm, n // tn, nk),
        in_specs=in_specs, out_specs=pl.BlockSpec((tm, tn), lambda i, j, kk: (i, j)),
        scratch_shapes=[pltpu.VMEM((tm, tn), F32)] if nk > 1 else [],
        compiler_params=_cparams(3))(*args)


def mm_tn(a, b, *, name, tm=None, tn=None, tk=None):
    t, m = a.shape
    t2, n = b.shape
    assert t == t2
    tm = tm or _pick(m, (1024, 1408, 512, 128))
    tn = tn or _pick(n, (512, 384, 256, 128))
    tk = tk or _pick(t, (1376, 384, 128))
    nk = t // tk

    def body(a_ref, b_ref, o_ref):
        x = _dot_tn(a_ref[...], b_ref[...])
        kk = pl.program_id(2)

        @pl.when(kk == 0)
        def _():
            o_ref[...] = x

        @pl.when(kk > 0)
        def _():
            o_ref[...] += x

    return pl.pallas_call(
        body, name=name, out_shape=jax.ShapeDtypeStruct((m, n), F32), grid=(m // tm, n // tn, nk),
        in_specs=[pl.BlockSpec((tk, tm), lambda i, j, kk: (kk, i)), pl.BlockSpec((tk, tn), lambda i, j, kk: (kk, j))],
        out_specs=pl.BlockSpec((tm, tn), lambda i, j, kk: (i, j)),
        compiler_params=_cparams(3))(a, b)


def rms_fwd(x, w, *, name, out_dtype=F32, res=None):
    t, d = x.shape
    tb = _tok_block(t)
    has_res = res is not None

    def body(*refs):
        if has_res:
            x_ref, w_ref, r_ref, o_ref = refs
        else:
            x_ref, w_ref, o_ref = refs
        xv = x_ref[...]
        y = xv * lax.rsqrt(jnp.mean(xv * xv, axis=-1, keepdims=True) + EPS) * w_ref[...]
        if has_res:
            y = y + r_ref[...]
        o_ref[...] = y.astype(out_dtype)

    blk = pl.BlockSpec((tb, d), lambda i: (i, 0))
    wspec = pl.BlockSpec((1, d), lambda i: (0, 0))
    in_specs = [blk, wspec] + ([blk] if has_res else [])
    args = [x, w] + ([res] if has_res else [])
    return pl.pallas_call(body, name=name, out_shape=jax.ShapeDtypeStruct((t, d), out_dtype), grid=(t // tb,),
                          in_specs=in_specs, out_specs=blk, compiler_params=_cparams(1))(*args)


def rms_bwd(x, w, dy, *, name, add=None):
    t, d = x.shape
    tb = _tok_block(t)
    has_add = add is not None

    def body(*refs):
        if has_add:
            x_ref, w_ref, dy_ref, a_ref, dx_ref, dw_ref = refs
        else:
            x_ref, w_ref, dy_ref, dx_ref, dw_ref = refs
        xv = x_ref[...]
        dyv = dy_ref[...]
        r = lax.rsqrt(jnp.mean(xv * xv, axis=-1, keepdims=True) + EPS)
        g = dyv * w_ref[...]
        dx = r * g - xv * (r * r * r) * jnp.mean(xv * g, axis=-1, keepdims=True)
        if has_add:
            dx = dx + a_ref[...]
        dx_ref[...] = dx
        part = jnp.sum(dyv * xv * r, axis=0, keepdims=True)

        @pl.when(pl.program_id(0) == 0)
        def _():
            dw_ref[...] = part

        @pl.when(pl.program_id(0) > 0)
        def _():
            dw_ref[...] += part

    blk = pl.BlockSpec((tb, d), lambda i: (i, 0))
    wspec = pl.BlockSpec((1, d), lambda i: (0, 0))
    in_specs = [blk, wspec, blk] + ([blk] if has_add else [])
    args = [x, w, dy] + ([add] if has_add else [])
    return pl.pallas_call(body, name=name,
                          out_shape=[jax.ShapeDtypeStruct((t, d), F32), jax.ShapeDtypeStruct((1, d), F32)],
                          grid=(t // tb,), in_specs=in_specs, out_specs=[blk, wspec], compiler_params=_cparams(1))(*args)


def _shift_down(cur, prev8, k):
    z = jnp.concatenate([prev8, cur], axis=0)
    return pltpu.roll(z, k, 0)[8:]


def _shift_up(cur, next8, k):
    n = cur.shape[0] + 8
    z = jnp.concatenate([cur, next8], axis=0)
    return pltpu.roll(z, n - k, 0)[:cur.shape[0]]


def _prev8_spec(tb, width, col):
    return pl.BlockSpec((8, width), lambda i: (jnp.maximum(i * (tb // 8) - 1, 0), col))


def _next8_spec(tb, width, col, t):
    return pl.BlockSpec((8, width), lambda i: (jnp.minimum((i + 1) * (tb // 8), t // 8 - 1), col))


def _row_valid(i, tb, n, offset=0):
    rows = i * tb + offset + _iota((n, 1), 0)
    return (rows >= N_PAD).astype(F32)


def conv_a_fwd(proj, w8, *, name):
    t = proj.shape[0]
    tb = _tok_block(t)

    def body(b_ref, c_ref, x_ref, cp_ref, xp_ref, w_ref, o_ref):
        i = pl.program_id(0)
        u = c_ref[...] * x_ref[...] * _row_valid(i, tb, tb)
        up = cp_ref[...] * xp_ref[...] * _row_valid(i, tb, 8, -8) * (i > 0).astype(F32)
        w = w_ref[...]
        conv = w[2:3] * u + w[1:2] * _shift_down(u, up, 1) + w[0:1] * _shift_down(u, up, 2)
        o_ref[...] = b_ref[...] * conv

    blk = lambda col: pl.BlockSpec((tb, D), lambda i: (i, col))
    return pl.pallas_call(
        body, name=name, out_shape=jax.ShapeDtypeStruct((t, D), F32), grid=(t // tb,),
        in_specs=[blk(0), blk(1), blk(2), _prev8_spec(tb, D, 1), _prev8_spec(tb, D, 2), pl.BlockSpec((8, D), lambda i: (0, 0))],
        out_specs=pl.BlockSpec((tb, D), lambda i: (i, 0)), compiler_params=_cparams(1))(proj, proj, proj, proj, proj, w8)


def conv_a_bwd(proj, w8, dy, *, name):
    t = proj.shape[0]
    tb = _tok_block(t)
    nblk = t // tb

    def body(b_ref, c_ref, x_ref, cp_ref, xp_ref, dy_ref, dyn_ref, bn_ref, w_ref, o_ref, dw_ref):
        i = pl.program_id(0)
        vm = _row_valid(i, tb, tb)
        cv, xv, bv, dyv = c_ref[...], x_ref[...], b_ref[...], dy_ref[...]
        u = cv * xv * vm
        up = cp_ref[...] * xp_ref[...] * _row_valid(i, tb, 8, -8) * (i > 0).astype(F32)
        w = w_ref[...]
        u1 = _shift_down(u, up, 1)
        u2 = _shift_down(u, up, 2)
        conv = w[2:3] * u + w[1:2] * u1 + w[0:1] * u2
        dconv = dyv * bv
        dconv_n = dyn_ref[...] * bn_ref[...] * (i < nblk - 1).astype(F32)
        du = w[2:3] * dconv + w[1:2] * _shift_up(dconv, dconv_n, 1) + w[0:1] * _shift_up(dconv, dconv_n, 2)
        o_ref[:, 0:D] = dyv * conv
        o_ref[:, D:2 * D] = du * xv * vm
        o_ref[:, 2 * D:3 * D] = du * cv * vm

        @pl.when(i == 0)
        def _():
            dw_ref[...] = jnp.zeros_like(dw_ref)

        dw_ref[0:1, :] += jnp.sum(dconv * u2, axis=0, keepdims=True)
        dw_ref[1:2, :] += jnp.sum(dconv * u1, axis=0, keepdims=True)
        dw_ref[2:3, :] += jnp.sum(dconv * u, axis=0, keepdims=True)

    blk = lambda col: pl.BlockSpec((tb, D), lambda i: (i, col))
    w8spec = pl.BlockSpec((8, D), lambda i: (0, 0))
    return pl.pallas_call(
        body, name=name,
        out_shape=[jax.ShapeDtypeStruct((t, 3 * D), F32), jax.ShapeDtypeStruct((8, D), F32)], grid=(nblk,),
        in_specs=[blk(0), blk(1), blk(2), _prev8_spec(tb, D, 1), _prev8_spec(tb, D, 2), blk(0),
                  _next8_spec(tb, D, 0, t), _next8_spec(tb, D, 0, t), w8spec],
        out_specs=[pl.BlockSpec((tb, 3 * D), lambda i: (i, 0)), w8spec],
        compiler_params=_cparams(1))(proj, proj, proj, proj, proj, dy, dy, proj, w8)


XBC_W = 2048


def ssd_conv_fwd(proj, w8, b, *, name):
    t = proj.shape[0]
    tb = _tok_block(t)

    def body(x_ref, xp_ref, w_ref, b_ref, o_ref):
        i = pl.program_id(0)
        xm = x_ref[...] * _row_valid(i, tb, tb)
        xmp = xp_ref[...] * _row_valid(i, tb, 8, -8) * (i > 0).astype(F32)
        w = w_ref[...]
        c = w[3:4] * xm + w[2:3] * _shift_down(xm, xmp, 1) + w[1:2] * _shift_down(xm, xmp, 2) + w[0:1] * _shift_down(xm, xmp, 3)
        o_ref[...] = _silu(c + b_ref[...])

    return pl.pallas_call(
        body, name=name, out_shape=jax.ShapeDtypeStruct((t, XBC_W), F32), grid=(t // tb,),
        in_specs=[pl.BlockSpec((tb, XBC_W), lambda i: (i, 2)), _prev8_spec(tb, XBC_W, 2),
                  pl.BlockSpec((8, XBC_W), lambda i: (0, 0)), pl.BlockSpec((1, XBC_W), lambda i: (0, 0))],
        out_specs=pl.BlockSpec((tb, XBC_W), lambda i: (i, 0)), compiler_params=_cparams(1))(proj, proj, w8, b)


def ssd_conv_bwd_pre(proj, w8, b, dxa, *, name):
    t = proj.shape[0]
    tb = _tok_block(t)

    def body(x_ref, xp_ref, w_ref, b_ref, d_ref, o_ref, dw_ref, db_ref):
        i = pl.program_id(0)
        xm = x_ref[...] * _row_valid(i, tb, tb)
        xmp = xp_ref[...] * _row_valid(i, tb, 8, -8) * (i > 0).astype(F32)
        w = w_ref[...]
        x1, x2, x3 = _shift_down(xm, xmp, 1), _shift_down(xm, xmp, 2), _shift_down(xm, xmp, 3)
        c = w[3:4] * xm + w[2:3] * x1 + w[1:2] * x2 + w[0:1] * x3 + b_ref[...]
        dpre = d_ref[...] * _dsilu(c)
        o_ref[...] = dpre

        @pl.when(i == 0)
        def _():
            dw_ref[...] = jnp.zeros_like(dw_ref)
            db_ref[...] = jnp.zeros_like(db_ref)

        dw_ref[0:1, :] += jnp.sum(dpre * x3, axis=0, keepdims=True)
        dw_ref[1:2, :] += jnp.sum(dpre * x2, axis=0, keepdims=True)
        dw_ref[2:3, :] += jnp.sum(dpre * x1, axis=0, keepdims=True)
        dw_ref[3:4, :] += jnp.sum(dpre * xm, axis=0, keepdims=True)
        db_ref[...] += jnp.sum(dpre, axis=0, keepdims=True)

    w8spec = pl.BlockSpec((8, XBC_W), lambda i: (0, 0))
    bspec = pl.BlockSpec((1, XBC_W), lambda i: (0, 0))
    return pl.pallas_call(
        body, name=name,
        out_shape=[jax.ShapeDtypeStruct((t, XBC_W), F32), jax.ShapeDtypeStruct((8, XBC_W), F32), jax.ShapeDtypeStruct((1, XBC_W), F32)],
        grid=(t // tb,),
        in_specs=[pl.BlockSpec((tb, XBC_W), lambda i: (i, 2)), _prev8_spec(tb, XBC_W, 2), w8spec, bspec,
                  pl.BlockSpec((tb, XBC_W), lambda i: (i, 0))],
        out_specs=[pl.BlockSpec((tb, XBC_W), lambda i: (i, 0)), w8spec, bspec],
        compiler_params=_cparams(1))(proj, proj, w8, b, dxa)


def ssd_conv_bwd_in(dpre, w8, *, name):
    t = dpre.shape[0]
    tb = _tok_block(t)
    nblk = t // tb

    def body(d_ref, dn_ref, w_ref, o_ref):
        i = pl.program_id(0)
        d = d_ref[...]
        dn = dn_ref[...] * (i < nblk - 1).astype(F32)
        w = w_ref[...]
        dx = w[3:4] * d + w[2:3] * _shift_up(d, dn, 1) + w[1:2] * _shift_up(d, dn, 2) + w[0:1] * _shift_up(d, dn, 3)
        o_ref[...] = dx * _row_valid(i, tb, tb)

    return pl.pallas_call(
        body, name=name, out_shape=jax.ShapeDtypeStruct((t, XBC_W), F32), grid=(nblk,),
        in_specs=[pl.BlockSpec((tb, XBC_W), lambda i: (i, 0)), _next8_spec(tb, XBC_W, 0, t), pl.BlockSpec((8, XBC_W), lambda i: (0, 0))],
        out_specs=pl.BlockSpec((tb, XBC_W), lambda i: (i, 0)), compiler_params=_cparams(1))(dpre, dpre, w8)


def _col(x, h):
    return jnp.sum(jnp.where(_iota(x.shape, 1) == h, x, 0.0), axis=1, keepdims=True)


def _row(x, h):
    return jnp.sum(jnp.where(_iota(x.shape, 0) == h, x, 0.0), axis=0, keepdims=True)


def _ssd_common(xa, dtr, dtb, alog, c):
    vm = _row_valid(c, CH, CH)
    xs = xa[:, :D] * vm
    dt = _softplus(dtr + dtb)
    a = -jnp.exp(alog) * dt
    tri = (_iota((CH, CH), 0) >= _iota((CH, CH), 1)).astype(F32)
    acs = _dot_hi(tri, a)
    return vm, xs, dt, a, acs, acs.T


def _pair_lanes(v0, v1):
    lane = _iota((1, CH), 1)
    return jnp.where(lane < 64, v0, v1)


def _ssd_pair_fwd(q, xs, xa, dt, acs, acs_t, dsk, hin, g_mat):
    g = q // 2
    h0, h1 = 2 * q, 2 * q + 1
    causal = _iota((CH, CH), 0) >= _iota((CH, CH), 1)
    bg = xa[:, D + CH * g:D + CH * (g + 1)]
    cg = xa[:, D + 512 + CH * g:D + 512 + CH * (g + 1)]
    xs_p = xs[:, CH * q:CH * (q + 1)]
    ac0, ac1 = _col(acs, h0), _col(acs, h1)
    ar0, ar1 = _row(acs_t, h0), _row(acs_t, h1)
    l0 = jnp.exp(jnp.where(causal, ac0 - ar0, -1e30))
    l1 = jnp.exp(jnp.where(causal, ac1 - ar1, -1e30))
    dt_p = _pair_lanes(_col(dt, h0), _col(dt, h1))
    x = xs_p * dt_p
    m0, m1 = g_mat * l0, g_mat * l1
    lane = _iota((CH, CH), 1)
    yd = jnp.where(lane < 64, _dot(m0, x), _dot(m1, x))
    ac_p = _pair_lanes(ac0, ac1)
    eac = jnp.exp(ac_p)
    yoff_raw = _dot_nt(cg, hin)
    al0 = jnp.sum(jnp.where(_iota((CH, 1), 0) == CH - 1, ac0, 0.0), axis=0, keepdims=True)
    al1 = jnp.sum(jnp.where(_iota((CH, 1), 0) == CH - 1, ac1, 0.0), axis=0, keepdims=True)
    dsv = jnp.exp(_pair_lanes(al0, al1) - ac_p)
    s = _dot_tn(x * dsv, bg)
    cd = jnp.where(_iota((CH, 1), 0) < 64, jnp.exp(al0), jnp.exp(al1))
    d_p = _pair_lanes(_col(dsk, h0), _col(dsk, h1))
    y = yd + yoff_raw * eac + xs_p * d_p
    return dict(bg=bg, cg=cg, xs_p=xs_p, l0=l0, l1=l1, m0=m0, m1=m1, dt_p=dt_p, x=x, eac=eac, yoff_raw=yoff_raw,
                dsv=dsv, s=s, cd=cd, d_p=d_p, y=y, al0=al0, al1=al1)


def _ssd_gate_norm(y, z, nw):
    yv = y * _silu(z)
    outs, rs = [], []
    for g in range(4):
        yg = yv[:, 256 * g:256 * (g + 1)]
        r = lax.rsqrt(jnp.mean(yg * yg, axis=-1, keepdims=True) + EPS)
        outs.append(yg * r * nw[:, 256 * g:256 * (g + 1)])
        rs.append(r)
    return yv, jnp.concatenate(outs, axis=1), rs


def ssd_fwd(xa, proj, pdt, dtb, alog, dsk, nw, *, name):
    t = xa.shape[0]
    nc = t // CH

    def body(xa_ref, dtr_ref, z_ref, dtb_ref, alog_ref, dsk_ref, nw_ref, y_ref, hs_ref, h_scr):
        c = pl.program_id(0)

        @pl.when(c == 0)
        def _():
            h_scr[...] = jnp.zeros_like(h_scr)

        xa_v = xa_ref[...]
        vm, xs, dt, a, acs, acs_t = _ssd_common(xa_v, dtr_ref[...], dtb_ref[...], alog_ref[...], c)
        dsk_v = dsk_ref[...]
        ys = []
        g_mat = None
        for q in range(8):
            if q % 2 == 0:
                g = q // 2
                g_mat = _dot_nt(xa_v[:, D + 512 + CH * g:D + 512 + CH * (g + 1)], xa_v[:, D + CH * g:D + CH * (g + 1)])
            hin = h_scr[q]
            hs_ref[0, q] = hin
            p = _ssd_pair_fwd(q, xs, xa_v, dt, acs, acs_t, dsk_v, hin, g_mat)
            h_scr[q] = hin * p["cd"] + p["s"]
            ys.append(p["y"])
        y = jnp.concatenate(ys, axis=1)
        _, out, _ = _ssd_gate_norm(y, z_ref[...], nw_ref[...])
        y_ref[...] = out

    small = pl.BlockSpec((1, CH), lambda c: (0, 0))
    return pl.pallas_call(
        body, name=name,
        out_shape=[jax.ShapeDtypeStruct((t, D), F32), jax.ShapeDtypeStruct((nc, 8, CH, CH), F32)], grid=(nc,),
        in_specs=[pl.BlockSpec((CH, XBC_W), lambda c: (c, 0)), pl.BlockSpec((CH, CH), lambda c: (c, 0)),
                  pl.BlockSpec((CH, D), lambda c: (c, 3)), small, small, small, pl.BlockSpec((1, D), lambda c: (0, 0))],
        out_specs=[pl.BlockSpec((CH, D), lambda c: (c, 0)), pl.BlockSpec((1, 8, CH, CH), lambda c: (c, 0, 0, 0))],
        scratch_shapes=[pltpu.VMEM((8, CH, CH), F32)], compiler_params=_cparams(1))(xa, pdt, proj, dtb, alog, dsk, nw)


def ssd_bwd(xa, proj, pdt, hs, dyb, dtb, alog, dsk, nw, *, name):
    t = xa.shape[0]
    nc = t // CH

    def body(xa_ref, dtr_ref, z_ref, hs_ref, dy_ref, dtb_ref, alog_ref, dsk_ref, nw_ref,
             dz_ref, dxa_ref, ddt_ref, gdtb_ref, galog_ref, gdsk_ref, gnw_ref, dh_scr):
        step = pl.program_id(0)
        c = nc - 1 - step

        @pl.when(step == 0)
        def _():
            dh_scr[...] = jnp.zeros_like(dh_scr)
            gdtb_ref[...] = jnp.zeros_like(gdtb_ref)
            galog_ref[...] = jnp.zeros_like(galog_ref)
            gdsk_ref[...] = jnp.zeros_like(gdsk_ref)
            gnw_ref[...] = jnp.zeros_like(gnw_ref)

        xa_v = xa_ref[...]
        dtr = dtr_ref[...]
        dtb_v = dtb_ref[...]
        alog_v = alog_ref[...]
        vm, xs, dt, a, acs, acs_t = _ssd_common(xa_v, dtr, dtb_v, alog_v, c)
        dsk_v = dsk_ref[...]
        z = z_ref[...]
        nw_v = nw_ref[...]
        lane1 = _iota((1, CH), 1)
        sub1 = _iota((CH, 1), 0)
        lane = _iota((CH, CH), 1)

        pairs = []
        g_mats = []
        for q in range(8):
            if q % 2 == 0:
                g = q // 2
                g_mats.append(_dot_nt(xa_v[:, D + 512 + CH * g:D + 512 + CH * (g + 1)], xa_v[:, D + CH * g:D + CH * (g + 1)]))
            pairs.append(_ssd_pair_fwd(q, xs, xa_v, dt, acs, acs_t, dsk_v, hs_ref[0, q], g_mats[q // 2]))
        y_pre = jnp.concatenate([p["y"] for p in pairs], axis=1)

        dout = dy_ref[...]
        sz = _silu(z)
        yv = y_pre * sz
        dyv_parts = []
        gnw_parts = []
        for g in range(4):
            sl = slice(256 * g, 256 * (g + 1))
            yg = yv[:, sl]
            r = lax.rsqrt(jnp.mean(yg * yg, axis=-1, keepdims=True) + EPS)
            gy = dout[:, sl] * nw_v[:, sl]
            dyv_parts.append(r * gy - yg * (r * r * r) * jnp.mean(yg * gy, axis=-1, keepdims=True))
            gnw_parts.append(jnp.sum(dout[:, sl] * yg * r, axis=0, keepdims=True))
        dyv = jnp.concatenate(dyv_parts, axis=1)
        gnw_ref[...] += jnp.concatenate(gnw_parts, axis=1)
        dz_ref[...] = dyv * y_pre * _dsilu(z)
        dy_pre = dyv * sz

        dacs_c = jnp.zeros((CH, CH), F32)
        dacs_r = jnp.zeros((CH, CH), F32)
        ddt = jnp.zeros((CH, CH), F32)
        gdsk = jnp.zeros((1, CH), F32)
        dxs_parts = []
        db_g = [None] * 4
        dc_g = [None] * 4
        dg_g = [None] * 4

        def acc(lst, g, v):
            lst[g] = v if lst[g] is None else lst[g] + v

        for q in range(8):
            p = pairs[q]
            g = q // 2
            h0, h1 = 2 * q, 2 * q + 1
            dy = dy_pre[:, CH * q:CH * (q + 1)]
            hin = hs_ref[0, q]
            dhout = dh_scr[q]
            x = p["x"]
            m_lo = lane < 64
            dxs = dy * p["d_p"]
            t_sk = dy * p["xs_p"]
            gdsk = gdsk + jnp.where(lane1 == h0, jnp.sum(jnp.where(m_lo, t_sk, 0.0)), 0.0) \
                        + jnp.where(lane1 == h1, jnp.sum(jnp.where(m_lo, 0.0, t_sk)), 0.0)
            dx = jnp.zeros((CH, CH), F32)
            for k, (hh, mk, lk, mm_) in enumerate(((h0, m_lo, p["l0"], p["m0"]), (h1, ~m_lo, p["l1"], p["m1"]))):
                dyk = jnp.where(mk, dy, 0.0)
                dm = _dot_nt(dyk, x)
                dx = dx + jnp.where(mk, _dot_tn(mm_, dy), 0.0)
                acc(dg_g, g, dm * lk)
                qm = dm * mm_
                dacs_c = dacs_c + jnp.where(lane1 == hh, jnp.sum(qm, axis=1, keepdims=True), 0.0)
                dacs_r = dacs_r - jnp.where(sub1 == hh, jnp.sum(qm, axis=0, keepdims=True), 0.0)
            dye = dy * p["eac"]
            acc(dc_g, g, _dot(dye, hin))
            t_off = dy * p["yoff_raw"] * p["eac"]
            dacs_c = dacs_c + jnp.where(lane1 == h0, jnp.sum(jnp.where(m_lo, t_off, 0.0), axis=1, keepdims=True), 0.0) \
                            + jnp.where(lane1 == h1, jnp.sum(jnp.where(m_lo, 0.0, t_off), axis=1, keepdims=True), 0.0)
            dhin = _dot_tn(dye, p["cg"]) + dhout * p["cd"]
            w1 = _dot_nt(p["bg"], dhout)
            dx = dx + p["dsv"] * w1
            t_ds = x * w1 * p["dsv"]
            dd0 = jnp.sum(jnp.where(m_lo, t_ds, 0.0), axis=1, keepdims=True)
            dd1 = jnp.sum(jnp.where(m_lo, 0.0, t_ds), axis=1, keepdims=True)
            acc(db_g, g, _dot(x * p["dsv"], dhout))
            t_cd = dhout * hin
            sub_lo = _iota((CH, CH), 0) < 64
            dcd0 = jnp.sum(jnp.where(sub_lo, t_cd, 0.0)) * jnp.exp(p["al0"])
            dcd1 = jnp.sum(jnp.where(sub_lo, 0.0, t_cd)) * jnp.exp(p["al1"])
            last = (sub1 == CH - 1)
            dacs_c = dacs_c + jnp.where(lane1 == h0, jnp.where(last, jnp.sum(dd0) + dcd0, 0.0) - dd0, 0.0) \
                            + jnp.where(lane1 == h1, jnp.where(last, jnp.sum(dd1) + dcd1, 0.0) - dd1, 0.0)
            dh_scr[q] = dhin
            dxs = dxs + dx * p["dt_p"]
            t_dt = dx * p["xs_p"]
            ddt = ddt + jnp.where(lane1 == h0, jnp.sum(jnp.where(m_lo, t_dt, 0.0), axis=1, keepdims=True), 0.0) \
                      + jnp.where(lane1 == h1, jnp.sum(jnp.where(m_lo, 0.0, t_dt), axis=1, keepdims=True), 0.0)
            dxs_parts.append(dxs)

        for g in range(4):
            bg, cg = pairs[2 * g]["bg"], pairs[2 * g]["cg"]
            dc_g[g] = dc_g[g] + _dot(dg_g[g], bg)
            db_g[g] = db_g[g] + _dot_tn(dg_g[g], cg)

        dacs = dacs_c + dacs_r.T
        rtri = (_iota((CH, CH), 1) >= _iota((CH, CH), 0)).astype(F32)
        da = _dot_hi(rtri, dacs)
        ddt = ddt - da * jnp.exp(alog_v)
        galog_ref[...] += jnp.sum(da * a, axis=0, keepdims=True)
        dpre = ddt * _sigmoid(dtr + dtb_v) * (lane1 < SSD_HEADS).astype(F32)
        ddt_ref[...] = dpre
        gdtb_ref[...] += jnp.sum(dpre, axis=0, keepdims=True)
        gdsk_ref[...] += gdsk
        dxa_ref[:, 0:D] = jnp.concatenate(dxs_parts, axis=1) * vm
        dxa_ref[:, D:D + 512] = jnp.concatenate(db_g, axis=1)
        dxa_ref[:, D + 512:D + 1024] = jnp.concatenate(dc_g, axis=1)

    small = pl.BlockSpec((1, CH), lambda s: (0, 0))
    wide = pl.BlockSpec((1, D), lambda s: (0, 0))
    rev = lambda s: nc - 1 - s
    return pl.pallas_call(
        body, name=name,
        out_shape=[jax.ShapeDtypeStruct((t, D), F32), jax.ShapeDtypeStruct((t, XBC_W), F32), jax.ShapeDtypeStruct((t, CH), F32),
                   jax.ShapeDtypeStruct((1, CH), F32), jax.ShapeDtypeStruct((1, CH), F32), jax.ShapeDtypeStruct((1, CH), F32),
                   jax.ShapeDtypeStruct((1, D), F32)],
        grid=(nc,),
        in_specs=[pl.BlockSpec((CH, XBC_W), lambda s: (rev(s), 0)), pl.BlockSpec((CH, CH), lambda s: (rev(s), 0)),
                  pl.BlockSpec((CH, D), lambda s: (rev(s), 3)), pl.BlockSpec((1, 8, CH, CH), lambda s: (rev(s), 0, 0, 0)),
                  pl.BlockSpec((CH, D), lambda s: (rev(s), 0)), small, small, small, wide],
        out_specs=[pl.BlockSpec((CH, D), lambda s: (rev(s), 0)), pl.BlockSpec((CH, XBC_W), lambda s: (rev(s), 0)),
                   pl.BlockSpec((CH, CH), lambda s: (rev(s), 0)), small, small, small, wide],
        scratch_shapes=[pltpu.VMEM((8, CH, CH), F32)], compiler_params=_cparams(1))(xa, pdt, proj, hs, dyb, dtb, alog, dsk, nw)


RET_DK = 256


def _log_gamma(h):
    return math.log(1.0 - 2.0 ** (-5.0 - h))


def _rope(x, cos, sin):
    x1, x2 = x[:, :128], x[:, 128:]
    return jnp.concatenate([x1 * cos - x2 * sin, x1 * sin + x2 * cos], axis=1)


def _unrope(d, cos, sin):
    d1, d2 = d[:, :128], d[:, 128:]
    return jnp.concatenate([d1 * cos + d2 * sin, d2 * cos - d1 * sin], axis=1)


def _ret_head_fwd(h, q, k, v, cos, sin, vm, r_in):
    lg = _log_gamma(h)
    sl = slice(RET_DK * h, RET_DK * (h + 1))
    qr = _rope(q[:, sl], cos, sin)
    kr = _rope(k[:, sl], cos, sin) * (RET_DK ** -0.5)
    vr = v[:, sl] * vm
    rel = (_iota((CH, CH), 0) - _iota((CH, CH), 1)).astype(F32)
    dmask = jnp.where(rel >= 0, jnp.exp(lg * jnp.maximum(rel, 0.0)), 0.0)
    idx = _iota((CH, 1), 0).astype(F32)
    kdec = jnp.exp(lg * (CH - 1 - idx))
    qdec = jnp.exp(lg * (idx + 1.0))
    scores = _dot_nt(qr, kr) * dmask
    y = _dot(scores, vr) + _dot(qr, r_in) * qdec
    kv = _dot_tn(kr * kdec, vr)
    return dict(qr=qr, kr=kr, vr=vr, dmask=dmask, kdec=kdec, qdec=qdec, scores=scores, y=y, kv=kv, cdec=math.exp(lg * CH))


def _group_norm(y):
    mu = jnp.mean(y, axis=-1, keepdims=True)
    yc = y - mu
    r = lax.rsqrt(jnp.mean(yc * yc, axis=-1, keepdims=True) + EPS)
    return yc * r, r


def ret_fwd(proj, cos, sin, *, name):
    t = proj.shape[0]
    nc = t // CH

    def body(q_ref, k_ref, v_ref, g_ref, cos_ref, sin_ref, y_ref, rs_ref, r_scr):
        c = pl.program_id(0)

        @pl.when(c == 0)
        def _():
            r_scr[...] = jnp.zeros_like(r_scr)

        vm = _row_valid(c, CH, CH)
        q, k, v, gt = q_ref[...], k_ref[...], v_ref[...], g_ref[...]
        cos, sin = cos_ref[...], sin_ref[...]
        for h in range(RET_HEADS):
            r_in = r_scr[h]
            rs_ref[0, h] = r_in
            p = _ret_head_fwd(h, q, k, v, cos, sin, vm, r_in)
            r_scr[h] = r_in * p["cdec"] + p["kv"]
            yn, _ = _group_norm(p["y"])
            sl = slice(RET_DK * h, RET_DK * (h + 1))
            y_ref[:, sl] = yn * _silu(gt[:, sl])

    blk = lambda col: pl.BlockSpec((CH, D), lambda c: (c, col))
    tab = pl.BlockSpec((CH, CH), lambda c: (c, 0))
    return pl.pallas_call(
        body, name=name,
        out_shape=[jax.ShapeDtypeStruct((t, D), F32), jax.ShapeDtypeStruct((nc, RET_HEADS, RET_DK, RET_DK), F32)], grid=(nc,),
        in_specs=[blk(6), blk(7), blk(8), blk(9), tab, tab],
        out_specs=[pl.BlockSpec((CH, D), lambda c: (c, 0)), pl.BlockSpec((1, RET_HEADS, RET_DK, RET_DK), lambda c: (c, 0, 0, 0))],
        scratch_shapes=[pltpu.VMEM((RET_HEADS, RET_DK, RET_DK), F32)], compiler_params=_cparams(1))(proj, proj, proj, proj, cos, sin)


def ret_bwd(proj, cos, sin, rs, dyc, *, name):
    t = proj.shape[0]
    nc = t // CH

    def body(q_ref, k_ref, v_ref, g_ref, cos_ref, sin_ref, rs_ref, dy_ref, o_ref, dr_scr):
        step = pl.program_id(0)
        c = nc - 1 - step

        @pl.when(step == 0)
        def _():
            dr_scr[...] = jnp.zeros_like(dr_scr)

        vm = _row_valid(c, CH, CH)
        q, k, v, gt = q_ref[...], k_ref[...], v_ref[...], g_ref[...]
        cos, sin = cos_ref[...], sin_ref[...]
        dout = dy_ref[...]
        for h in range(RET_HEADS):
            sl = slice(RET_DK * h, RET_DK * (h + 1))
            r_in = rs_ref[0, h]
            p = _ret_head_fwd(h, q, k, v, cos, sin, vm, r_in)
            yn, r = _group_norm(p["y"])
            gh = gt[:, sl]
            do = dout[:, sl]
            dg = do * yn * _dsilu(gh)
            dyn = do * _silu(gh)
            dy = r * (dyn - jnp.mean(dyn, axis=-1, keepdims=True) - yn * jnp.mean(dyn * yn, axis=-1, keepdims=True))
            dr_out = dr_scr[h]
            dyc_ = dy * p["qdec"]
            dqr = _dot_nt(dyc_, r_in)
            dr_scr[h] = dr_out * p["cdec"] + _dot_tn(p["qr"], dyc_)
            dkr = _dot_nt(p["vr"], dr_out) * p["kdec"]
            dv = _dot(p["kr"] * p["kdec"], dr_out)
            ds = _dot_nt(dy, p["vr"]) * p["dmask"]
            dqr = dqr + _dot(ds, p["kr"])
            dkr = dkr + _dot_tn(ds, p["qr"])
            dv = dv + _dot_tn(p["scores"], dy)
            o_ref[:, RET_DK * h:RET_DK * (h + 1)] = _unrope(dqr, cos, sin)
            o_ref[:, D + RET_DK * h:D + RET_DK * (h + 1)] = _unrope(dkr, cos, sin) * (RET_DK ** -0.5)
            o_ref[:, 2 * D + RET_DK * h:2 * D + RET_DK * (h + 1)] = dv * vm
            o_ref[:, 3 * D + RET_DK * h:3 * D + RET_DK * (h + 1)] = dg

    rev = lambda s: nc - 1 - s
    blk = lambda col: pl.BlockSpec((CH, D), lambda s: (rev(s), col))
    tab = pl.BlockSpec((CH, CH), lambda s: (rev(s), 0))
    return pl.pallas_call(
        body, name=name, out_shape=jax.ShapeDtypeStruct((t, 4 * D), F32), grid=(nc,),
        in_specs=[blk(6), blk(7), blk(8), blk(9), tab, tab,
                  pl.BlockSpec((1, RET_HEADS, RET_DK, RET_DK), lambda s: (rev(s), 0, 0, 0)), pl.BlockSpec((CH, D), lambda s: (rev(s), 0))],
        out_specs=pl.BlockSpec((CH, 4 * D), lambda s: (rev(s), 0)),
        scratch_shapes=[pltpu.VMEM((RET_HEADS, RET_DK, RET_DK), F32)], compiler_params=_cparams(1))(proj, proj, proj, proj, cos, sin, rs, dyc)


SB_D = 128
SB_SCALE = SB_D ** -0.5


def _split_hi_lo(x):
    hi = x.astype(BF16)
    lo = (x - hi.astype(F32)).astype(BF16)
    return hi, lo


def _suffix_and_total(x, mat):
    hi, lo = _split_hi_lo(x)
    r = jnp.dot(hi, mat, preferred_element_type=F32) + jnp.dot(lo, mat, preferred_element_type=F32)
    return r[:, :128], r[:, 128:]


def _sb_block(qv, kj, row0, col0, tq):
    z = _dot_nt(qv, kj) * SB_SCALE
    ls = jnp.minimum(z, 0.0) - jnp.log(1.0 + jnp.exp(-jnp.abs(z)))
    tpos = row0 + _iota((tq, 128), 0)
    spos = col0 + _iota((tq, 128), 1)
    m = (spos < tpos) & (spos >= N_PAD)
    lneg = jnp.where(m, ls - z, 0.0)
    return z, ls, lneg, m


def sb_fwd(qkv, *, name):
    t = qkv.shape[0]
    tq = _tok_block(t)
    nq = t // tq
    per = tq // 128

    def body(q_ref, k_ref, v_ref, o_ref, at_ref):
        i = pl.program_id(1)
        qv = q_ref[...]
        su = (_iota((128, 128), 0) > _iota((128, 128), 1)).astype(BF16)
        mat = jnp.concatenate([su, jnp.ones((128, 128), BF16)], axis=1)

        def step(it, carry):
            a_run, acc = carry
            j = (i + 1) * per - 1 - it
            off = pl.multiple_of(j * 128, 128)
            kj = k_ref[pl.ds(off, 128), :]
            vj = v_ref[pl.ds(off, 128), :]
            z, ls, lneg, m = _sb_block(qv, kj, i * tq, j * 128, tq)
            suf, tot = _suffix_and_total(lneg, mat)
            w = jnp.where(m, jnp.exp(ls + a_run + suf), 0.0)
            return a_run + tot, acc + _dot(w, vj)

        zero = jnp.zeros((tq, 128), F32)
        a_tot, acc = lax.fori_loop(0, (i + 1) * per, step, (zero, zero))
        o_ref[...] = acc
        at_ref[...] = a_tot

    blk = pl.BlockSpec((tq, 128), lambda h, i: (i, h))
    return pl.pallas_call(
        body, name=name, out_shape=[jax.ShapeDtypeStruct((t, D), F32), jax.ShapeDtypeStruct((t, D), F32)], grid=(SB_HEADS, nq),
        in_specs=[blk, pl.BlockSpec((t, 128), lambda h, i: (0, SB_HEADS + h)),
                  pl.BlockSpec((t, 128), lambda h, i: (0, 2 * SB_HEADS + h))],
        out_specs=[blk, blk], compiler_params=_cparams(2))(qkv, qkv, qkv)


def sb_bwd(qkv, atot, dout, *, name):
    t = qkv.shape[0]
    tq = _tok_block(t)
    nq = t // tq
    per = tq // 128

    def body(q_ref, k_ref, v_ref, at_ref, do_ref, dq_ref, dk_ref, dv_ref):
        i = pl.program_id(1)

        @pl.when(i == 0)
        def _():
            dk_ref[...] = jnp.zeros_like(dk_ref)
            dv_ref[...] = jnp.zeros_like(dv_ref)

        qv = q_ref[...]
        dob = do_ref[...].astype(BF16)
        a_tot = at_ref[...]
        su = (_iota((128, 128), 0) > _iota((128, 128), 1)).astype(BF16)
        sl = (_iota((128, 128), 0) < _iota((128, 128), 1)).astype(BF16)
        ones = jnp.ones((128, 128), BF16)
        mat_u = jnp.concatenate([su, ones], axis=1)
        mat_l = jnp.concatenate([sl, ones], axis=1)

        def step(j, carry):
            p_run, e_run, dq = carry
            off = pl.multiple_of(j * 128, 128)
            kj = k_ref[pl.ds(off, 128), :]
            vj = v_ref[pl.ds(off, 128), :]
            z, ls, lneg, m = _sb_block(qv, kj, i * tq, j * 128, tq)
            suf, tot = _suffix_and_total(lneg, mat_u)
            p_run = p_run + tot
            w = jnp.where(m, jnp.exp(ls + (a_tot - p_run) + suf), 0.0)
            e = w * _dot_nt(dob, vj)
            epre, etot = _suffix_and_total(e, mat_l)
            big_e = e_run + epre
            sig = jnp.exp(ls)
            dz = (e * (1.0 - sig) - jnp.where(m, big_e * sig, 0.0)) * SB_SCALE
            dzb = dz.astype(BF16)
            dk_ref[0, pl.ds(off, 128), :] += _dot_tn(dzb, qv)
            dv_ref[0, pl.ds(off, 128), :] += _dot_tn(w, dob)
            return p_run, e_run + etot, dq + _dot(dzb, kj)

        zero = jnp.zeros((tq, 128), F32)
        _, _, dq = lax.fori_loop(0, (i + 1) * per, step, (zero, zero, zero))
        dq_ref[...] = dq

    head_blk = pl.BlockSpec((1, t, 128), lambda h, i: (h, 0, 0))
    return pl.pallas_call(
        body, name=name,
        out_shape=[jax.ShapeDtypeStruct((t, D), F32), jax.ShapeDtypeStruct((SB_HEADS, t, 128), F32),
                   jax.ShapeDtypeStruct((SB_HEADS, t, 128), F32)],
        grid=(SB_HEADS, nq),
        in_specs=[pl.BlockSpec((tq, 128), lambda h, i: (i, h)), pl.BlockSpec((t, 128), lambda h, i: (0, SB_HEADS + h)),
                  pl.BlockSpec((t, 128), lambda h, i: (0, 2 * SB_HEADS + h)), pl.BlockSpec((tq, 128), lambda h, i: (i, h)),
                  pl.BlockSpec((tq, 128), lambda h, i: (i, h))],
        out_specs=[pl.BlockSpec((tq, 128), lambda h, i: (i, h)), head_blk, head_blk],
        compiler_params=_cparams(2, 60 * 1024 * 1024))(qkv, qkv, qkv, atot, dout)


def branch_fwd(y, proj, w, n, *, name, add=None):
    t = y.shape[0]
    tb = _tok_block(t)
    has_add = add is not None

    def body(*refs):
        if has_add:
            y_ref, g_ref, w_ref, a_ref, o_ref = refs
        else:
            y_ref, g_ref, w_ref, o_ref = refs
        r = _sigmoid(g_ref[...]) * _dot(y_ref[...], w_ref[...])
        if has_add:
            r = r + a_ref[...]
        o_ref[...] = r

    blk = pl.BlockSpec((tb, D), lambda i: (i, 0))
    in_specs = [blk, pl.BlockSpec((tb, D), lambda i: (i, COL_GATE + n)), pl.BlockSpec((D, D), lambda i: (0, 0))] + ([blk] if has_add else [])
    args = [y, proj, w] + ([add] if has_add else [])
    return pl.pallas_call(body, name=name, out_shape=jax.ShapeDtypeStruct((t, D), F32), grid=(t // tb,),
                          in_specs=in_specs, out_specs=blk, compiler_params=_cparams(1))(*args)


def branch_bwd(y, proj, w, wt, n, dmerged, *, name):
    t = y.shape[0]
    tb = _tok_block(t)

    def body(y_ref, g_ref, w_ref, wt_ref, dm_ref, dg_ref, dup_ref, dy_ref):
        up = _dot(y_ref[...], w_ref[...])
        gate = _sigmoid(g_ref[...])
        dm = dm_ref[...]
        dg_ref[...] = dm * up * gate * (1.0 - gate)
        dup = (dm * gate).astype(BF16)
        dup_ref[...] = dup
        dy_ref[...] = _dot(dup, wt_ref[...])

    blk = pl.BlockSpec((tb, D), lambda i: (i, 0))
    wspec = pl.BlockSpec((D, D), lambda i: (0, 0))
    return pl.pallas_call(
        body, name=name,
        out_shape=[jax.ShapeDtypeStruct((t, D), F32), jax.ShapeDtypeStruct((t, D), BF16), jax.ShapeDtypeStruct((t, D), F32)],
        grid=(t // tb,),
        in_specs=[blk, pl.BlockSpec((tb, D), lambda i: (i, COL_GATE + n)), wspec, wspec, blk],
        out_specs=[blk, blk, blk], compiler_params=_cparams(1))(y, proj, w, wt, dmerged)


def swiglu_fwd(f, *, name):
    t = f.shape[0]
    tb = _tok_block(t)

    def body(g_ref, u_ref, o_ref):
        o_ref[...] = (_silu(g_ref[...]) * u_ref[...]).astype(BF16)

    return pl.pallas_call(body, name=name, out_shape=jax.ShapeDtypeStruct((t, D_FF), BF16), grid=(t // tb,),
                          in_specs=[pl.BlockSpec((tb, D_FF), lambda i: (i, 0)), pl.BlockSpec((tb, D_FF), lambda i: (i, 1))],
                          out_specs=pl.BlockSpec((tb, D_FF), lambda i: (i, 0)), compiler_params=_cparams(1))(f, f)


def swiglu_bwd(f, dact, *, name):
    t = f.shape[0]
    tb = _tok_block(t)

    def body(g_ref, u_ref, d_ref, o_ref):
        g, u, d = g_ref[...], u_ref[...], d_ref[...]
        o_ref[:, 0:D_FF] = d * u * _dsilu(g)
        o_ref[:, D_FF:2 * D_FF] = d * _silu(g)

    return pl.pallas_call(body, name=name, out_shape=jax.ShapeDtypeStruct((t, 2 * D_FF), F32), grid=(t // tb,),
                          in_specs=[pl.BlockSpec((tb, D_FF), lambda i: (i, 0)), pl.BlockSpec((tb, D_FF), lambda i: (i, 1)),
                                    pl.BlockSpec((tb, D_FF), lambda i: (i, 0))],
                          out_specs=pl.BlockSpec((tb, 2 * D_FF), lambda i: (i, 0)), compiler_params=_cparams(1))(f, f, dact)


def loss_head(h, target, *, name):
    t = h.shape[0]
    nb = t // CH

    def body(h_ref, t_ref, l_ref, d_ref):
        i = pl.program_id(0)

        @pl.when(i == 0)
        def _():
            l_ref[...] = jnp.zeros_like(l_ref)
            d_ref[...] = jnp.zeros_like(d_ref)

        @pl.when(i > 0)
        def _():
            err = h_ref[...] - t_ref[...]
            d_ref[...] = err * (1.0 / D)
            l_ref[...] += jnp.sum(err * err) * (0.5 / D)

    return pl.pallas_call(
        body, name=name, out_shape=[jax.ShapeDtypeStruct((8, 128), F32), jax.ShapeDtypeStruct((t, D), F32)], grid=(nb,),
        in_specs=[pl.BlockSpec((CH, D), lambda i: (i, 0)), pl.BlockSpec((CH, D), lambda i: (jnp.maximum(i - 1, 0), 0))],
        out_specs=[pl.BlockSpec((8, 128), lambda i: (0, 0)), pl.BlockSpec((CH, D), lambda i: (i, 0))],
        compiler_params=_cparams(1))(h, target)


def _pad_rows8(w):
    return jnp.concatenate([w, jnp.zeros((8 - w.shape[0], w.shape[1]), w.dtype)], axis=0)


def _pad_lanes(v, n=CH):
    return jnp.concatenate([v, jnp.zeros((n - v.shape[0],), v.dtype)])[None, :]


def prep_layer(p):
    w = p["w_in"]
    zeros = jnp.zeros((D, CH - SSD_HEADS), w.dtype)
    w_f = jnp.concatenate([w[:, :6144], w[:, 6160:10256], w[:, 13328:17424]], axis=1)
    w_dt = jnp.concatenate([w[:, 6144:6160], zeros], axis=1)
    w_sb = w[:, 10256:13328]
    return dict(
        w_f=w_f, w_sb=w_sb, w_dt=w_dt, w_f_t=w_f.T, w_sb_t=w_sb.T, w_dt_t=w_dt.T,
        w_br=p["w_branch"], w_br_t=jnp.swapaxes(p["w_branch"], 1, 2), w_out=p["w_out"], w_out_t=p["w_out"].T,
        w_fi=p["w_ffn_in"], w_fi_t=p["w_ffn_in"].T, w_fo=p["w_ffn_out"], w_fo_t=p["w_ffn_out"].T,
        conv_a8=_pad_rows8(p["conv_a"]), conv_s8=_pad_rows8(p["ssd_conv_w"]), conv_sb=p["ssd_conv_b"][None, :],
        dtb=_pad_lanes(p["ssd_dt_bias"]), alog=_pad_lanes(p["ssd_a_log"]), dsk=_pad_lanes(p["ssd_d"]), nw=p["ssd_norm"][None, :],
        n1=p["norm_mix_pre"][None, :], n2=p["norm_mix_post"][None, :], n3=p["norm_ffn_pre"][None, :], n4=p["norm_ffn_post"][None, :])


def layer_fwd(h0, w, cos, sin, l):
    nm = lambda s: f"l{l}_{s}"
    hn = rms_fwd(h0, w["n1"], name=nm("rms1"), out_dtype=BF16)
    proj = mm(hn, w["w_f"], name=nm("proj_f"))
    qkv = mm(hn, w["w_sb"], name=nm("proj_sb"), out_dtype=BF16)
    pdt = mm(hn, w["w_dt"], name=nm("proj_dt"))
    y_a = conv_a_fwd(proj, w["conv_a8"], name=nm("conv_a"))
    xa = ssd_conv_fwd(proj, w["conv_s8"], w["conv_sb"], name=nm("ssd_conv"))
    y_b, hs = ssd_fwd(xa, proj, pdt, w["dtb"], w["alog"], w["dsk"], w["nw"], name=nm("ssd"))
    y_c, rs = ret_fwd(proj, cos, sin, name=nm("ret"))
    y_d, sb_atot = sb_fwd(qkv, name=nm("sb"))
    ys = (y_a, y_b, y_c, y_d)
    merged = None
    for n in range(4):
        merged = branch_fwd(ys[n], proj, w["w_br"][n], n, name=nm(f"branch{n}"), add=merged)
    mix = mm(merged, w["w_out"], name=nm("mix"))
    h1 = rms_fwd(mix, w["n2"], name=nm("rms2"), res=h0)
    hn2 = rms_fwd(h1, w["n3"], name=nm("rms3"), out_dtype=BF16)
    f = mm(hn2, w["w_fi"], name=nm("ffn_in"))
    act = swiglu_fwd(f, name=nm("swiglu"))
    f2 = mm(act, w["w_fo"], name=nm("ffn_out"))
    h2 = rms_fwd(f2, w["n4"], name=nm("rms4"), res=h1)
    saved = dict(h0=h0, hn=hn, proj=proj, qkv=qkv, pdt=pdt, xa=xa, hs=hs, rs=rs, ys=ys, sb_atot=sb_atot, merged=merged, mix=mix, h1=h1, hn2=hn2,
                 f=f, act=act, f2=f2)
    return h2, saved


def layer_bwd(dh2, s, w, cos, sin, l):
    nm = lambda t: f"l{l}_{t}"
    g = {}
    df2, g["n4"] = rms_bwd(s["f2"], w["n4"], dh2, name=nm("rms4_b"))
    g["w_fo"] = mm_tn(s["act"], df2, name=nm("ffn_out_dw"))
    dact = mm(df2, w["w_fo_t"], name=nm("ffn_out_dx"), out_dtype=BF16)
    df = swiglu_bwd(s["f"], dact, name=nm("swiglu_b"))
    g["w_fi"] = mm_tn(s["hn2"], df, name=nm("ffn_in_dw"))
    dhn2 = mm(df, w["w_fi_t"], name=nm("ffn_in_dx"))
    dh1, g["n3"] = rms_bwd(s["h1"], w["n3"], dhn2, name=nm("rms3_b"), add=dh2)
    dmix, g["n2"] = rms_bwd(s["mix"], w["n2"], dh1, name=nm("rms2_b"))
    g["w_out"] = mm_tn(s["merged"], dmix, name=nm("mix_dw"))
    dmerged = mm(dmix, w["w_out_t"], name=nm("mix_dx"))
    dgate, dys, dwb = [], [], []
    for n in range(4):
        dg_n, dup_n, dy_n = branch_bwd(s["ys"][n], s["proj"], w["w_br"][n], w["w_br_t"][n], n, dmerged, name=nm(f"branch{n}_b"))
        dgate.append(dg_n)
        dys.append(dy_n)
        dwb.append(mm_tn(s["ys"][n], dup_n, name=nm(f"branch{n}_dw")))
    g["w_br"] = jnp.stack(dwb)
    d_a, g["conv_a8"] = conv_a_bwd(s["proj"], w["conv_a8"], dys[0], name=nm("conv_a_b"))
    dz, dxa, ddt, g["dtb"], g["alog"], g["dsk"], g["nw"] = ssd_bwd(
        s["xa"], s["proj"], s["pdt"], s["hs"], dys[1], w["dtb"], w["alog"], w["dsk"], w["nw"], name=nm("ssd_b"))
    dpre, g["conv_s8"], g["conv_sb"] = ssd_conv_bwd_pre(s["proj"], w["conv_s8"], w["conv_sb"], dxa, name=nm("ssd_conv_b1"))
    dxbc = ssd_conv_bwd_in(dpre, w["conv_s8"], name=nm("ssd_conv_b2"))
    d_r = ret_bwd(s["proj"], cos, sin, s["rs"], dys[2], name=nm("ret_b"))
    dq, dk, dv = sb_bwd(s["qkv"], s["sb_atot"], dys[3], name=nm("sb_b"))
    t = dq.shape[0]
    d_sb = jnp.concatenate([dq, jnp.swapaxes(dk, 0, 1).reshape(t, D), jnp.swapaxes(dv, 0, 1).reshape(t, D)], axis=1)
    segs = [(d_a, 0), (dz, 3072), (dxbc, 4096), (d_r, 6144), (dgate[0], 10240), (dgate[1], 11264), (dgate[2], 12288),
            (dgate[3], 13312)]
    dws = [mm_tn(s["hn"], d, name=nm(f"proj_dw{k}")) for k, (d, _) in enumerate(segs)]
    g["w_f"] = jnp.concatenate(dws, axis=1)
    g["w_sb"] = mm_tn(s["hn"], d_sb, name=nm("proj_dw_sb"))
    g["w_dt"] = mm_tn(s["hn"], ddt, name=nm("proj_dw_dt"))
    dhn = mm(d_sb, w["w_sb_t"], name=nm("proj_dx_sb"))
    dhn = mm(ddt, w["w_dt_t"], name=nm("proj_dx_dt"), add=dhn)
    for k, (d, c0) in enumerate(segs):
        dhn = mm(d, w["w_f_t"][c0:c0 + d.shape[1]], name=nm(f"proj_dx{k}"), add=dhn)
    dh0, g["n1"] = rms_bwd(s["h0"], w["n1"], dhn, name=nm("rms1_b"), add=dh1)
    return dh0, g


def layer_grads_to_params(g):
    wf, wsb = g["w_f"], g["w_sb"]
    w_in = jnp.concatenate([wf[:, :6144], g["w_dt"][:, :SSD_HEADS], wf[:, 6144:10240], wsb, wf[:, 10240:14336]], axis=1)
    return dict(
        w_in=w_in, conv_a=g["conv_a8"][:3], ssd_conv_w=g["conv_s8"][:4], ssd_conv_b=g["conv_sb"][0],
        ssd_dt_bias=g["dtb"][0, :SSD_HEADS], ssd_a_log=g["alog"][0, :SSD_HEADS], ssd_d=g["dsk"][0, :SSD_HEADS], ssd_norm=g["nw"][0],
        w_branch=g["w_br"], w_out=g["w_out"], w_ffn_in=g["w_fi"], w_ffn_out=g["w_fo"],
        norm_mix_pre=g["n1"][0], norm_mix_post=g["n2"][0], norm_ffn_pre=g["n3"][0], norm_ffn_post=g["n4"][0])


def rope_tables(t):
    half = RET_DK // 2
    inv = ROPE_BASE ** (-jnp.arange(half, dtype=F32) / half)
    ang = jnp.arange(t).astype(F32)[:, None] * inv[None, :]
    return jnp.cos(ang), jnp.sin(ang)


def local_step(x, target, meta, layers):
    h = jnp.concatenate([jnp.zeros((N_PAD, D), F32), meta, x], axis=0)
    t = h.shape[0]
    cos, sin = rope_tables(t)
    ws = [prep_layer(p) for p in layers]
    saved = []
    for l, w in enumerate(ws):
        h, s = layer_fwd(h, w, cos, sin, l)
        saved.append(s)
    loss, dh = loss_head(h, target, name="loss_head")
    grads = [None] * len(ws)
    for l in reversed(range(len(ws))):
        dh, g = layer_bwd(dh, saved[l], ws[l], cos, sin, l)
        grads[l] = layer_grads_to_params(g)
    return loss, dh[CH:], dh[N_PAD:CH], grads


def _my_place():
    return lax.axis_index("x"), lax.axis_index("y"), lax.axis_index("c")


def _flat(px, py, pc):
    return 4 * px + 2 * py + pc


ANY = pl.BlockSpec(memory_space=pl.ANY)


def all_gather(x_shard, *, name):
    shape = x_shard.shape

    def body(x_ref, out_ref, send_sems, recv_sems, local_sem):
        x, y, c = _my_place()
        me, sibling = (x, y, c), (x, y, 1 - c)
        chips = [(1 - x, y), (x, 1 - y), (1 - x, 1 - y)]

        def rows(px, py, pc):
            return out_ref.at[_flat(px, py, pc)]

        def copy(k, block, to, src=None):
            return pltpu.make_async_remote_copy(
                src_ref=rows(*block) if src is None else src, dst_ref=rows(*block),
                send_sem=send_sems.at[k], recv_sem=recv_sems.at[k], device_id=to, device_id_type=MESH_ID)

        mine = pltpu.make_async_copy(x_ref, rows(*me), local_sem)
        mine.start()
        first = [copy(0, me, sibling, src=x_ref)]
        first += [copy(1 + j, me, (*chip, c), src=x_ref) for j, chip in enumerate(chips)]
        for cp in first:
            cp.start()
        passed = [copy(4 + j, (*chip, c), sibling) for j, chip in enumerate(chips)]
        for j, chip in enumerate(chips):
            copy(1 + j, (*chip, c), me).wait_recv()
            passed[j].start()
        copy(0, sibling, me).wait_recv()
        for j, chip in enumerate(chips):
            copy(4 + j, (*chip, 1 - c), me).wait_recv()
        for cp in first + passed:
            cp.wait_send()
        mine.wait()

    return pl.pallas_call(
        body, name=name, out_shape=jax.ShapeDtypeStruct((N_DEV,) + shape, x_shard.dtype),
        in_specs=[ANY], out_specs=ANY,
        scratch_shapes=[pltpu.SemaphoreType.DMA((7,)), pltpu.SemaphoreType.DMA((7,)), pltpu.SemaphoreType.DMA],
    )(x_shard)


def all_to_all(g, *, name):
    def body(g_ref, out_ref, send_sems, recv_sems, local_sem):
        x, y, c = _my_place()
        me = _flat(x, y, c)
        mine = pltpu.make_async_copy(g_ref.at[me], out_ref.at[me], local_sem)
        mine.start()
        peers = []
        for k in range(1, N_DEV):
            px = jnp.bitwise_xor(x, (k >> 2) & 1)
            py = jnp.bitwise_xor(y, (k >> 1) & 1)
            pc = jnp.bitwise_xor(c, k & 1)
            peers.append((px, py, pc))
        sends = []
        for k, peer in enumerate(peers):
            cp = pltpu.make_async_remote_copy(
                src_ref=g_ref.at[_flat(*peer)], dst_ref=out_ref.at[me],
                send_sem=send_sems.at[k], recv_sem=recv_sems.at[k], device_id=peer, device_id_type=MESH_ID)
            cp.start()
            sends.append(cp)
        for k, peer in enumerate(peers):
            slot = out_ref.at[_flat(*peer)]
            pltpu.make_async_remote_copy(
                src_ref=slot, dst_ref=slot, send_sem=send_sems.at[k], recv_sem=recv_sems.at[k],
                device_id=peer, device_id_type=MESH_ID).wait_recv()
        for cp in sends:
            cp.wait_send()
        mine.wait()

    return pl.pallas_call(
        body, name=name, out_shape=jax.ShapeDtypeStruct(g.shape, g.dtype), in_specs=[ANY], out_specs=ANY,
        scratch_shapes=[pltpu.SemaphoreType.DMA((7,)), pltpu.SemaphoreType.DMA((7,)), pltpu.SemaphoreType.DMA],
    )(g)


def sum_slots(a, *, name):
    def body(a_ref, o_ref):
        s = a_ref[0]
        for d in range(1, N_DEV):
            s = s + a_ref[d]
        o_ref[...] = s

    return pl.pallas_call(body, name=name, out_shape=jax.ShapeDtypeStruct(a.shape[1:], a.dtype))(a)


def _adamw_math(w, g, m, v):
    m = ADAM_B1 * m + (1.0 - ADAM_B1) * g
    v = ADAM_B2 * v + (1.0 - ADAM_B2) * (g * g)
    m_hat = m / (1.0 - ADAM_B1 ** ADAM_STEP)
    v_hat = v / (1.0 - ADAM_B2 ** ADAM_STEP)
    delta = -ADAM_LR * (m_hat / (jnp.sqrt(v_hat) + ADAM_EPS) + ADAM_WD * w)
    return delta, m, v


def adamw_big(recv, w, m, v, *, name):
    r, cols = w.shape
    tb = 128

    def body(r_ref, w_ref, m_ref, v_ref, g_ref, d_ref, nm_ref, nv_ref):
        g = r_ref[0]
        for d in range(1, N_DEV):
            g = g + r_ref[d]
        g_ref[...] = g
        d_ref[...], nm_ref[...], nv_ref[...] = _adamw_math(w_ref[...], g, m_ref[...], v_ref[...])

    blk = pl.BlockSpec((tb, cols), lambda i: (i, 0))
    out = jax.ShapeDtypeStruct((r, cols), F32)
    return pl.pallas_call(
        body, name=name, out_shape=[out] * 4, grid=(r // tb,),
        in_specs=[pl.BlockSpec((N_DEV, tb, cols), lambda i: (0, i, 0)), blk, blk, blk], out_specs=[blk] * 4,
        compiler_params=_cparams(1))(recv, w, m, v)


def adamw_small(w, g, m, v, *, name):
    def body(w_ref, g_ref, m_ref, v_ref, d_ref, nm_ref, nv_ref):
        d_ref[...], nm_ref[...], nv_ref[...] = _adamw_math(w_ref[...], g_ref[...], m_ref[...], v_ref[...])

    out = jax.ShapeDtypeStruct(w.shape, F32)
    return pl.pallas_call(body, name=name, out_shape=[out] * 3)(w, g, m, v)


BIG = ("w_in", "w_branch", "w_out", "w_ffn_in", "w_ffn_out")
BIG_SHARD = {"w_in": (DEPTH, D, 2178), "w_branch": (DEPTH, 4, 128, D), "w_out": (DEPTH, 128, D),
             "w_ffn_in": (DEPTH, D, 704), "w_ffn_out": (DEPTH, 352, D)}
BIG_FULL = {"w_in": ((1, 2, 0, 3), (DEPTH, D, 17424)), "w_branch": ((1, 2, 0, 3, 4), (DEPTH, 4, D, D)),
            "w_out": ((1, 0, 2, 3), (DEPTH, D, D)), "w_ffn_in": ((1, 2, 0, 3), (DEPTH, D, 2 * D_FF)),
            "w_ffn_out": ((1, 0, 2, 3), (DEPTH, D_FF, D))}
BIG_ROWS = {n: int(np.prod(s)) // D for n, s in BIG_SHARD.items()}
BIG_R = 7808


def pack_big(shards, dtype):
    parts = [shards[n].astype(dtype).reshape(BIG_ROWS[n], D) for n in BIG]
    parts.append(jnp.zeros((BIG_R - sum(BIG_ROWS.values()), D), dtype))
    return jnp.concatenate(parts, axis=0)


def unpack_big(flat):
    out, o = {}, 0
    for n in BIG:
        out[n] = flat[o:o + BIG_ROWS[n]].reshape(BIG_SHARD[n])
        o += BIG_ROWS[n]
    return out


def unpack_big_full(gathered):
    out, o = {}, 0
    for n in BIG:
        perm, full = BIG_FULL[n]
        out[n] = gathered[:, o:o + BIG_ROWS[n]].reshape((N_DEV,) + BIG_SHARD[n]).transpose(perm).reshape(full)
        o += BIG_ROWS[n]
    return out


def pack_big_full(full):
    parts = []
    for n in BIG:
        perm, _ = BIG_FULL[n]
        split = tuple(np.array((N_DEV,) + BIG_SHARD[n])[list(perm)])
        inv = tuple(int(i) for i in np.argsort(perm))
        parts.append(full[n].reshape(split).transpose(inv).reshape(N_DEV, BIG_ROWS[n], D))
    parts.append(jnp.zeros((N_DEV, BIG_R - sum(BIG_ROWS.values()), D), F32))
    return jnp.concatenate(parts, axis=1)


def _rows128(a):
    a = a.reshape(-1)
    pad = (-a.shape[0]) % CH
    if pad:
        a = jnp.concatenate([a, jnp.zeros((pad,), a.dtype)])
    return a.reshape(-1, CH)


def _pack_rows(arrs, total):
    parts = [_rows128(a) for a in arrs]
    n = sum(p.shape[0] for p in parts)
    parts.append(jnp.zeros((total - n, CH), F32))
    return jnp.concatenate(parts, axis=0)


def _unpack_rows(flat, shapes):
    out, o = [], 0
    for s in shapes:
        size = int(np.prod(s))
        rows = -(-size // CH)
        out.append(flat[o:o + rows].reshape(-1)[:size].reshape(s))
        o += rows
    return out


SMALL_SHARDED = ("meta", "conv_a", "ssd_conv_w")
SMALL_SHARD_SHAPE = {"meta": (N_META, 128), "conv_a": (DEPTH, 3, 128), "ssd_conv_w": (DEPTH, 4, 256)}
SMALL_FULL_SHAPE = {"meta": (N_META, D), "conv_a": (DEPTH, 3, D), "ssd_conv_w": (DEPTH, 4, 2048)}
SMALL_REPL = ("ssd_conv_b", "ssd_dt_bias", "ssd_a_log", "ssd_d", "ssd_norm", "norm_mix_pre", "norm_mix_post", "norm_ffn_pre",
              "norm_ffn_post")
SMALL_REPL_SHAPE = {"ssd_conv_b": (DEPTH, 2048), "ssd_dt_bias": (DEPTH, SSD_HEADS), "ssd_a_log": (DEPTH, SSD_HEADS),
                    "ssd_d": (DEPTH, SSD_HEADS), "ssd_norm": (DEPTH, D), "norm_mix_pre": (DEPTH, D), "norm_mix_post": (DEPTH, D),
                    "norm_ffn_pre": (DEPTH, D), "norm_ffn_post": (DEPTH, D)}


def _gather_small_full(gathered, n):
    nd = gathered.ndim
    perm = tuple(range(1, nd - 1)) + (0, nd - 1)
    return gathered.transpose(perm).reshape(SMALL_FULL_SHAPE[n])


WEIGHTS = ("meta", "w_in", "conv_a", "ssd_conv_w", "ssd_conv_b", "ssd_dt_bias", "ssd_a_log", "ssd_d", "ssd_norm", "w_branch", "w_out",
           "w_ffn_in", "w_ffn_out", "norm_mix_pre", "norm_mix_post", "norm_ffn_pre", "norm_ffn_post")


def kernel(x, meta, w_in, conv_a, ssd_conv_w, ssd_conv_b, ssd_dt_bias, ssd_a_log, ssd_d, ssd_norm, w_branch, w_out, w_ffn_in, w_ffn_out, norm_mix_pre, norm_mix_post, norm_ffn_pre, norm_ffn_post, loss_target, m_meta, m_w_in, m_conv_a, m_ssd_conv_w, m_ssd_conv_b, m_ssd_dt_bias, m_ssd_a_log, m_ssd_d, m_ssd_norm, m_w_branch, m_w_out, m_w_ffn_in, m_w_ffn_out, m_norm_mix_pre, m_norm_mix_post, m_norm_ffn_pre, m_norm_ffn_post, v_meta, v_w_in, v_conv_a, v_ssd_conv_w, v_ssd_conv_b, v_ssd_dt_bias, v_ssd_a_log, v_ssd_d, v_ssd_norm, v_w_branch, v_w_out, v_w_ffn_in, v_w_ffn_out, v_norm_mix_pre, v_norm_mix_post, v_norm_ffn_pre, v_norm_ffn_post):
    w = dict(meta=meta, w_in=w_in, conv_a=conv_a, ssd_conv_w=ssd_conv_w, ssd_conv_b=ssd_conv_b, ssd_dt_bias=ssd_dt_bias,
             ssd_a_log=ssd_a_log, ssd_d=ssd_d, ssd_norm=ssd_norm, w_branch=w_branch, w_out=w_out, w_ffn_in=w_ffn_in,
             w_ffn_out=w_ffn_out, norm_mix_pre=norm_mix_pre, norm_mix_post=norm_mix_post, norm_ffn_pre=norm_ffn_pre,
             norm_ffn_post=norm_ffn_post)
    m = dict(meta=m_meta, w_in=m_w_in, conv_a=m_conv_a, ssd_conv_w=m_ssd_conv_w, ssd_conv_b=m_ssd_conv_b, ssd_dt_bias=m_ssd_dt_bias,
             ssd_a_log=m_ssd_a_log, ssd_d=m_ssd_d, ssd_norm=m_ssd_norm, w_branch=m_w_branch, w_out=m_w_out, w_ffn_in=m_w_ffn_in,
             w_ffn_out=m_w_ffn_out, norm_mix_pre=m_norm_mix_pre, norm_mix_post=m_norm_mix_post, norm_ffn_pre=m_norm_ffn_pre,
             norm_ffn_post=m_norm_ffn_post)
    v = dict(meta=v_meta, w_in=v_w_in, conv_a=v_conv_a, ssd_conv_w=v_ssd_conv_w, ssd_conv_b=v_ssd_conv_b, ssd_dt_bias=v_ssd_dt_bias,
             ssd_a_log=v_ssd_a_log, ssd_d=v_ssd_d, ssd_norm=v_ssd_norm, w_branch=v_w_branch, w_out=v_w_out, w_ffn_in=v_w_ffn_in,
             w_ffn_out=v_w_ffn_out, norm_mix_pre=v_norm_mix_pre, norm_mix_post=v_norm_mix_post, norm_ffn_pre=v_norm_ffn_pre,
             norm_ffn_post=v_norm_ffn_post)
    xi, yi, ci = _my_place()
    dev = _flat(xi, yi, ci)

    full = unpack_big_full(all_gather(pack_big(w, BF16), name="gather_big"))
    small_shard = _pack_rows([w[n] for n in SMALL_SHARDED], 40)
    small_all = all_gather(small_shard, name="gather_small")
    small_full = {}
    o = 0
    for n in SMALL_SHARDED:
        rows = int(np.prod(SMALL_SHARD_SHAPE[n])) // CH
        small_full[n] = _gather_small_full(small_all[:, o:o + rows].reshape((N_DEV,) + SMALL_SHARD_SHAPE[n]), n)
        o += rows

    layers = []
    for l in range(DEPTH):
        p = {n: full[n][l] for n in BIG}
        p["conv_a"] = small_full["conv_a"][l]
        p["ssd_conv_w"] = small_full["ssd_conv_w"][l]
        for n in SMALL_REPL:
            p[n] = w[n][l]
        layers.append(p)

    loss_blk, grad_x, gmeta, grads = local_step(x[0], loss_target[0], small_full["meta"], layers)

    gfull = {n: jnp.stack([grads[l][n] for l in range(DEPTH)]) for n in BIG}
    recv = all_to_all(pack_big_full(gfull), name="exchange_big")
    g_flat, d_flat, nm_flat, nv_flat = adamw_big(recv, pack_big(w, F32), pack_big(m, F32), pack_big(v, F32), name="adamw_big")
    out_g, out_d, out_m, out_v = unpack_big(g_flat), unpack_big(d_flat), unpack_big(nm_flat), unpack_big(nv_flat)

    small_names = SMALL_SHARDED + SMALL_REPL
    small_grads = [gmeta] + [jnp.stack([grads[l][n] for l in range(DEPTH)]) for n in small_names[1:]]
    small_shapes = [SMALL_FULL_SHAPE[n] for n in SMALL_SHARDED] + [SMALL_REPL_SHAPE[n] for n in SMALL_REPL]
    sm = _pack_rows(small_grads + [loss_blk[0:1]], 424)
    sm_sum = sum_slots(all_gather(sm, name="gather_small_grads"), name="sum_small_grads")
    summed = _unpack_rows(sm_sum, small_shapes + [(1, CH)])
    loss = summed[-1][0, 0]
    sg = dict(zip(small_names, summed[:-1]))
    for n in SMALL_SHARDED:
        width = SMALL_SHARD_SHAPE[n][-1]
        sg[n] = lax.dynamic_slice_in_dim(sg[n], dev * width, width, axis=sg[n].ndim - 1)
    pk = lambda d: _pack_rows([d[n] for n in small_names], 160)
    sd, snm, snv = adamw_small(pk(w), pk(sg), pk(m), pk(v), name="adamw_small")
    shard_shapes = [SMALL_SHARD_SHAPE[n] for n in SMALL_SHARDED] + [SMALL_REPL_SHAPE[n] for n in SMALL_REPL]
    for dst, flat in ((out_d, sd), (out_m, snm), (out_v, snv)):
        dst.update(zip(small_names, _unpack_rows(flat, shard_shapes)))
    out_g.update(sg)

    return (loss, grad_x[None], *[out_g[n] for n in WEIGHTS], *[out_d[n] for n in WEIGHTS], *[out_m[n] for n in WEIGHTS],
            *[out_v[n] for n in WEIGHTS])
```

```python
import functools
import math

import numpy as np
import jax
import jax.numpy as jnp
from jax import lax
from jax.experimental import pallas as pl
from jax.experimental.pallas import tpu as pltpu

F32, BF16 = jnp.float32, jnp.bfloat16
HI = lax.Precision.HIGHEST
MESH_ID = pl.DeviceIdType.MESH

D = 1024
CH = 128
N_META = 16
N_PAD = CH - N_META
EPS = 1e-6
N_DEV = 8
DEPTH = 2
SSD_HEADS = 16
RET_HEADS = 4
SB_HEADS = 8
D_FF = 2816
ROPE_BASE = 10000.0

NF = 14336
COL_GATE = 10

ADAM_LR, ADAM_B1, ADAM_B2, ADAM_EPS, ADAM_WD, ADAM_STEP = 0.001, 0.9, 0.999, 1e-08, 0.01, 10

VMEM_BYTES = 48 * 1024 * 1024


def _pick(n, cands):
    for c in cands:
        if n % c == 0:
            return c
    raise ValueError((n, cands))


def _tok_block(t):
    return _pick(t, (384, 128))


def _cparams(ngrid, vmem=VMEM_BYTES):
    return pltpu.CompilerParams(dimension_semantics=("arbitrary",) * ngrid, vmem_limit_bytes=vmem)


def _iota(shape, dim):
    return lax.broadcasted_iota(jnp.int32, shape, dim)


def _sigmoid(x):
    return 1.0 / (1.0 + jnp.exp(-x))


def _silu(x):
    return x * _sigmoid(x)


def _dsilu(x):
    s = _sigmoid(x)
    return s * (1.0 + x * (1.0 - s))


def _softplus(x):
    return jnp.maximum(x, 0.0) + jnp.log(1.0 + jnp.exp(-jnp.abs(x)))


def _dot(a, b):
    return jnp.dot(a.astype(BF16), b.astype(BF16), preferred_element_type=F32)


def _dot_nt(a, b):
    return lax.dot_general(a.astype(BF16), b.astype(BF16), (((1,), (1,)), ((), ())), preferred_element_type=F32)


def _dot_tn(a, b):
    return lax.dot_general(a.astype(BF16), b.astype(BF16), (((0,), (0,)), ((), ())), preferred_element_type=F32)


def _dot_hi(a, b):
    return jnp.dot(a, b, precision=HI, preferred_element_type=F32)


def mm(a, b, *, name, out_dtype=F32, add=None, tm=None, tn=None, tk=None):
    m, k = a.shape
    k2, n = b.shape
    assert k == k2
    tm = tm or _pick(m, (1376, 384, 128))
    tn = tn or _pick(n, (512, 384, 256, 128))
    tk = tk or _pick(k, (1024, 1408, 512, 384, 128))
    nk = k // tk
    has_add = add is not None

    def body(*refs):
        if has_add:
            a_ref, b_ref, c_ref, o_ref = refs[:4]
            scr = refs[4:]
        else:
            a_ref, b_ref, o_ref = refs[:3]
            c_ref = None
            scr = refs[3:]
        x = _dot(a_ref[...], b_ref[...])
        if nk == 1:
            if has_add:
                x = x + c_ref[...]
            o_ref[...] = x.astype(out_dtype)
        else:
            acc = scr[0]
            kk = pl.program_id(2)

            @pl.when(kk == 0)
            def _():
                acc[...] = x

            @pl.when(kk > 0)
            def _():
                acc[...] += x

            @pl.when(kk == nk - 1)
            def _():
                r = acc[...]
                if has_add:
                    r = r + c_ref[...]
                o_ref[...] = r.astype(out_dtype)

    in_specs = [pl.BlockSpec((tm, tk), lambda i, j, kk: (i, kk)), pl.BlockSpec((tk, tn), lambda i, j, kk: (kk, j))]
    args = [a, b]
    if has_add:
        in_specs.append(pl.BlockSpec((tm, tn), lambda i, j, kk: (i, j)))
        args.append(add)
    return pl.pallas_call(
        body, name=name, out_shape=jax.ShapeDtypeStruct((m, n), out_dtype), grid=(m // tm, n // tn, nk),
        in_specs=in_specs, out_specs=pl.BlockSpec((tm, tn), lambda i, j, kk: (i, j)),
        scratch_shapes=[pltpu.VMEM((tm, tn), F32)] if nk > 1 else [],
        compiler_params=_cparams(3))(*args)


def mm_tn(a, b, *, name, tm=None, tn=None, tk=None):
    t, m = a.shape
    t2, n = b.shape
    assert t == t2
    tm = tm or _pick(m, (1024, 1408, 512, 128))
    tn = tn or _pick(n, (512, 384, 256, 128))
    tk = tk or _pick(t, (1376, 384, 128))
    nk = t // tk

    def body(a_ref, b_ref, o_ref):
        x = _dot_tn(a_ref[...], b_ref[...])
        kk = pl.program_id(2)

        @pl.when(kk == 0)
        def _():
            o_ref[...] = x

        @pl.when(kk > 0)
        def _():
            o_ref[...] += x

    return pl.pallas_call(
        body, name=name, out_shape=jax.ShapeDtypeStruct((m, n), F32), grid=(m // tm, n // tn, nk),
        in_specs=[pl.BlockSpec((tk, tm), lambda i, j, kk: (kk, i)), pl.BlockSpec((tk, tn), lambda i, j, kk: (kk, j))],
        out_specs=pl.BlockSpec((tm, tn), lambda i, j, kk: (i, j)),
        compiler_params=_cparams(3))(a, b)


def rms_fwd(x, w, *, name, out_dtype=F32, res=None):
    t, d = x.shape
    tb = _tok_block(t)
    has_res = res is not None

    def body(*refs):
        if has_res:
            x_ref, w_ref, r_ref, o_ref = refs
        else:
            x_ref, w_ref, o_ref = refs
        xv = x_ref[...]
        y = xv * lax.rsqrt(jnp.mean(xv * xv, axis=-1, keepdims=True) + EPS) * w_ref[...]
        if has_res:
            y = y + r_ref[...]
        o_ref[...] = y.astype(out_dtype)

    blk = pl.BlockSpec((tb, d), lambda i: (i, 0))
    wspec = pl.BlockSpec((1, d), lambda i: (0, 0))
    in_specs = [blk, wspec] + ([blk] if has_res else [])
    args = [x, w] + ([res] if has_res else [])
    return pl.pallas_call(body, name=name, out_shape=jax.ShapeDtypeStruct((t, d), out_dtype), grid=(t // tb,),
                          in_specs=in_specs, out_specs=blk, compiler_params=_cparams(1))(*args)


def rms_bwd(x, w, dy, *, name, add=None):
    t, d = x.shape
    tb = _tok_block(t)
    has_add = add is not None

    def body(*refs):
        if has_add:
            x_ref, w_ref, dy_ref, a_ref, dx_ref, dw_ref = refs
        else:
            x_ref, w_ref, dy_ref, dx_ref, dw_ref = refs
        xv = x_ref[...]
        dyv = dy_ref[...]
        r = lax.rsqrt(jnp.mean(xv * xv, axis=-1, keepdims=True) + EPS)
        g = dyv * w_ref[...]
        dx = r * g - xv * (r * r * r) * jnp.mean(xv * g, axis=-1, keepdims=True)
        if has_add:
            dx = dx + a_ref[...]
        dx_ref[...] = dx
        part = jnp.sum(dyv * xv * r, axis=0, keepdims=True)

        @pl.when(pl.program_id(0) == 0)
        def _():
            dw_ref[...] = part

        @pl.when(pl.program_id(0) > 0)
        def _():
            dw_ref[...] += part

    blk = pl.BlockSpec((tb, d), lambda i: (i, 0))
    wspec = pl.BlockSpec((1, d), lambda i: (0, 0))
    in_specs = [blk, wspec, blk] + ([blk] if has_add else [])
    args = [x, w, dy] + ([add] if has_add else [])
    return pl.pallas_call(body, name=name,
                          out_shape=[jax.ShapeDtypeStruct((t, d), F32), jax.ShapeDtypeStruct((1, d), F32)],
                          grid=(t // tb,), in_specs=in_specs, out_specs=[blk, wspec], compiler_params=_cparams(1))(*args)


def _shift_down(cur, prev8, k):
    z = jnp.concatenate([prev8, cur], axis=0)
    return pltpu.roll(z, k, 0)[8:]


def _shift_up(cur, next8, k):
    n = cur.shape[0] + 8
    z = jnp.concatenate([cur, next8], axis=0)
    return pltpu.roll(z, n - k, 0)[:cur.shape[0]]


def _prev8_spec(tb, width, col):
    return pl.BlockSpec((8, width), lambda i: (jnp.maximum(i * (tb // 8) - 1, 0), col))


def _next8_spec(tb, width, col, t):
    return pl.BlockSpec((8, width), lambda i: (jnp.minimum((i + 1) * (tb // 8), t // 8 - 1), col))


def _row_valid(i, tb, n, offset=0):
    rows = i * tb + offset + _iota((n, 1), 0)
    return (rows >= N_PAD).astype(F32)


def conv_a_fwd(proj, w8, *, name):
    t = proj.shape[0]
    tb = _tok_block(t)

    def body(b_ref, c_ref, x_ref, cp_ref, xp_ref, w_ref, o_ref):
        i = pl.program_id(0)
        u = c_ref[...] * x_ref[...] * _row_valid(i, tb, tb)
        up = cp_ref[...] * xp_ref[...] * _row_valid(i, tb, 8, -8) * (i > 0).astype(F32)
        w = w_ref[...]
        conv = w[2:3] * u + w[1:2] * _shift_down(u, up, 1) + w[0:1] * _shift_down(u, up, 2)
        o_ref[...] = b_ref[...] * conv

    blk = lambda col: pl.BlockSpec((tb, D), lambda i: (i, col))
    return pl.pallas_call(
        body, name=name, out_shape=jax.ShapeDtypeStruct((t, D), F32), grid=(t // tb,),
        in_specs=[blk(0), blk(1), blk(2), _prev8_spec(tb, D, 1), _prev8_spec(tb, D, 2), pl.BlockSpec((8, D), lambda i: (0, 0))],
        out_specs=pl.BlockSpec((tb, D), lambda i: (i, 0)), compiler_params=_cparams(1))(proj, proj, proj, proj, proj, w8)


def conv_a_bwd(proj, w8, dy, *, name):
    t = proj.shape[0]
    tb = _tok_block(t)
    nblk = t // tb

    def body(b_ref, c_ref, x_ref, cp_ref, xp_ref, dy_ref, dyn_ref, bn_ref, w_ref, o_ref, dw_ref):
        i = pl.program_id(0)
        vm = _row_valid(i, tb, tb)
        cv, xv, bv, dyv = c_ref[...], x_ref[...], b_ref[...], dy_ref[...]
        u = cv * xv * vm
        up = cp_ref[...] * xp_ref[...] * _row_valid(i, tb, 8, -8) * (i > 0).astype(F32)
        w = w_ref[...]
        u1 = _shift_down(u, up, 1)
        u2 = _shift_down(u, up, 2)
        conv = w[2:3] * u + w[1:2] * u1 + w[0:1] * u2
        dconv = dyv * bv
        dconv_n = dyn_ref[...] * bn_ref[...] * (i < nblk - 1).astype(F32)
        du = w[2:3] * dconv + w[1:2] * _shift_up(dconv, dconv_n, 1) + w[0:1] * _shift_up(dconv, dconv_n, 2)
        o_ref[:, 0:D] = dyv * conv
        o_ref[:, D:2 * D] = du * xv * vm
        o_ref[:, 2 * D:3 * D] = du * cv * vm

        @pl.when(i == 0)
        def _():
            dw_ref[...] = jnp.zeros_like(dw_ref)

        dw_ref[0:1, :] += jnp.sum(dconv * u2, axis=0, keepdims=True)
        dw_ref[1:2, :] += jnp.sum(dconv * u1, axis=0, keepdims=True)
        dw_ref[2:3, :] += jnp.sum(dconv * u, axis=0, keepdims=True)

    blk = lambda col: pl.BlockSpec((tb, D), lambda i: (i, col))
    w8spec = pl.BlockSpec((8, D), lambda i: (0, 0))
    return pl.pallas_call(
        body, name=name,
        out_shape=[jax.ShapeDtypeStruct((t, 3 * D), F32), jax.ShapeDtypeStruct((8, D), F32)], grid=(nblk,),
        in_specs=[blk(0), blk(1), blk(2), _prev8_spec(tb, D, 1), _prev8_spec(tb, D, 2), blk(0),
                  _next8_spec(tb, D, 0, t), _next8_spec(tb, D, 0, t), w8spec],
        out_specs=[pl.BlockSpec((tb, 3 * D), lambda i: (i, 0)), w8spec],
        compiler_params=_cparams(1))(proj, proj, proj, proj, proj, dy, dy, proj, w8)


XBC_W = 2048


def ssd_conv_fwd(proj, w8, b, *, name):
    t = proj.shape[0]
    tb = _tok_block(t)

    def body(x_ref, xp_ref, w_ref, b_ref, o_ref):
        i = pl.program_id(0)
        xm = x_ref[...] * _row_valid(i, tb, tb)
        xmp = xp_ref[...] * _row_valid(i, tb, 8, -8) * (i > 0).astype(F32)
        w = w_ref[...]
        c = w[3:4] * xm + w[2:3] * _shift_down(xm, xmp, 1) + w[1:2] * _shift_down(xm, xmp, 2) + w[0:1] * _shift_down(xm, xmp, 3)
        o_ref[...] = _silu(c + b_ref[...])

    return pl.pallas_call(
        body, name=name, out_shape=jax.ShapeDtypeStruct((t, XBC_W), F32), grid=(t // tb,),
        in_specs=[pl.BlockSpec((tb, XBC_W), lambda i: (i, 2)), _prev8_spec(tb, XBC_W, 2),
                  pl.BlockSpec((8, XBC_W), lambda i: (0, 0)), pl.BlockSpec((1, XBC_W), lambda i: (0, 0))],
        out_specs=pl.BlockSpec((tb, XBC_W), lambda i: (i, 0)), compiler_params=_cparams(1))(proj, proj, w8, b)


def ssd_conv_bwd_pre(proj, w8, b, dxa, *, name):
    t = proj.shape[0]
    tb = _tok_block(t)

    def body(x_ref, xp_ref, w_ref, b_ref, d_ref, o_ref, dw_ref, db_ref):
        i = pl.program_id(0)
        xm = x_ref[...] * _row_valid(i, tb, tb)
        xmp = xp_ref[...] * _row_valid(i, tb, 8, -8) * (i > 0).astype(F32)
        w = w_ref[...]
        x1, x2, x3 = _shift_down(xm, xmp, 1), _shift_down(xm, xmp, 2), _shift_down(xm, xmp, 3)
        c = w[3:4] * xm + w[2:3] * x1 + w[1:2] * x2 + w[0:1] * x3 + b_ref[...]
        dpre = d_ref[...] * _dsilu(c)
        o_ref[...] = dpre

        @pl.when(i == 0)
        def _():
            dw_ref[...] = jnp.zeros_like(dw_ref)
            db_ref[...] = jnp.zeros_like(db_ref)

        dw_ref[0:1, :] += jnp.sum(dpre * x3, axis=0, keepdims=True)
        dw_ref[1:2, :] += jnp.sum(dpre * x2, axis=0, keepdims=True)
        dw_ref[2:3, :] += jnp.sum(dpre * x1, axis=0, keepdims=True)
        dw_ref[3:4, :] += jnp.sum(dpre * xm, axis=0, keepdims=True)
        db_ref[...] += jnp.sum(dpre, axis=0, keepdims=True)

    w8spec = pl.BlockSpec((8, XBC_W), lambda i: (0, 0))
    bspec = pl.BlockSpec((1, XBC_W), lambda i: (0, 0))
    return pl.pallas_call(
        body, name=name,
        out_shape=[jax.ShapeDtypeStruct((t, XBC_W), F32), jax.ShapeDtypeStruct((8, XBC_W), F32), jax.ShapeDtypeStruct((1, XBC_W), F32)],
        grid=(t // tb,),
        in_specs=[pl.BlockSpec((tb, XBC_W), lambda i: (i, 2)), _prev8_spec(tb, XBC_W, 2), w8spec, bspec,
                  pl.BlockSpec((tb, XBC_W), lambda i: (i, 0))],
        out_specs=[pl.BlockSpec((tb, XBC_W), lambda i: (i, 0)), w8spec, bspec],
        compiler_params=_cparams(1))(proj, proj, w8, b, dxa)


def ssd_conv_bwd_in(dpre, w8, *, name):
    t = dpre.shape[0]
    tb = _tok_block(t)
    nblk = t // tb

    def body(d_ref, dn_ref, w_ref, o_ref):
        i = pl.program_id(0)
        d = d_ref[...]
        dn = dn_ref[...] * (i < nblk - 1).astype(F32)
        w = w_ref[...]
        dx = w[3:4] * d + w[2:3] * _shift_up(d, dn, 1) + w[1:2] * _shift_up(d, dn, 2) + w[0:1] * _shift_up(d, dn, 3)
        o_ref[...] = dx * _row_valid(i, tb, tb)

    return pl.pallas_call(
        body, name=name, out_shape=jax.ShapeDtypeStruct((t, XBC_W), F32), grid=(nblk,),
        in_specs=[pl.BlockSpec((tb, XBC_W), lambda i: (i, 0)), _next8_spec(tb, XBC_W, 0, t), pl.BlockSpec((8, XBC_W), lambda i: (0, 0))],
        out_specs=pl.BlockSpec((tb, XBC_W), lambda i: (i, 0)), compiler_params=_cparams(1))(dpre, dpre, w8)


def _col(x, h):
    return jnp.sum(jnp.where(_iota(x.shape, 1) == h, x, 0.0), axis=1, keepdims=True)


def _row(x, h):
    return jnp.sum(jnp.where(_iota(x.shape, 0) == h, x, 0.0), axis=0, keepdims=True)


def _ssd_common(xa, dtr, dtb, alog, c):
    vm = _row_valid(c, CH, CH)
    xs = xa[:, :D] * vm
    dt = _softplus(dtr + dtb)
    a = -jnp.exp(alog) * dt
    tri = (_iota((CH, CH), 0) >= _iota((CH, CH), 1)).astype(F32)
    acs = _dot_hi(tri, a)
    return vm, xs, dt, a, acs, acs.T


def _pair_lanes(v0, v1):
    lane = _iota((1, CH), 1)
    return jnp.where(lane < 64, v0, v1)


def _ssd_pair_fwd(q, xs, xa, dt, acs, acs_t, dsk, hin, g_mat):
    g = q // 2
    h0, h1 = 2 * q, 2 * q + 1
    causal = _iota((CH, CH), 0) >= _iota((CH, CH), 1)
    bg = xa[:, D + CH * g:D + CH * (g + 1)]
    cg = xa[:, D + 512 + CH * g:D + 512 + CH * (g + 1)]
    xs_p = xs[:, CH * q:CH * (q + 1)]
    ac0, ac1 = _col(acs, h0), _col(acs, h1)
    ar0, ar1 = _row(acs_t, h0), _row(acs_t, h1)
    l0 = jnp.exp(jnp.where(causal, ac0 - ar0, -1e30))
    l1 = jnp.exp(jnp.where(causal, ac1 - ar1, -1e30))
    dt_p = _pair_lanes(_col(dt, h0), _col(dt, h1))
    x = xs_p * dt_p
    m0, m1 = g_mat * l0, g_mat * l1
    lane = _iota((CH, CH), 1)
    yd = jnp.where(lane < 64, _dot(m0, x), _dot(m1, x))
    ac_p = _pair_lanes(ac0, ac1)
    eac = jnp.exp(ac_p)
    yoff_raw = _dot_nt(cg, hin)
    al0 = jnp.sum(jnp.where(_iota((CH, 1), 0) == CH - 1, ac0, 0.0), axis=0, keepdims=True)
    al1 = jnp.sum(jnp.where(_iota((CH, 1), 0) == CH - 1, ac1, 0.0), axis=0, keepdims=True)
    dsv = jnp.exp(_pair_lanes(al0, al1) - ac_p)
    s = _dot_tn(x * dsv, bg)
    cd = jnp.where(_iota((CH, 1), 0) < 64, jnp.exp(al0), jnp.exp(al1))
    d_p = _pair_lanes(_col(dsk, h0), _col(dsk, h1))
    y = yd + yoff_raw * eac + xs_p * d_p
    return dict(bg=bg, cg=cg, xs_p=xs_p, l0=l0, l1=l1, m0=m0, m1=m1, dt_p=dt_p, x=x, eac=eac, yoff_raw=yoff_raw,
                dsv=dsv, s=s, cd=cd, d_p=d_p, y=y, al0=al0, al1=al1)


def _ssd_gate_norm(y, z, nw):
    yv = y * _silu(z)
    outs, rs = [], []
    for g in range(4):
        yg = yv[:, 256 * g:256 * (g + 1)]
        r = lax.rsqrt(jnp.mean(yg * yg, axis=-1, keepdims=True) + EPS)
        outs.append(yg * r * nw[:, 256 * g:256 * (g + 1)])
        rs.append(r)
    return yv, jnp.concatenate(outs, axis=1), rs


def ssd_fwd(xa, proj, pdt, dtb, alog, dsk, nw, *, name):
    t = xa.shape[0]
    nc = t // CH

    def body(xa_ref, dtr_ref, z_ref, dtb_ref, alog_ref, dsk_ref, nw_ref, y_ref, hs_ref, h_scr):
        c = pl.program_id(0)

        @pl.when(c == 0)
        def _():
            h_scr[...] = jnp.zeros_like(h_scr)

        xa_v = xa_ref[...]
        vm, xs, dt, a, acs, acs_t = _ssd_common(xa_v, dtr_ref[...], dtb_ref[...], alog_ref[...], c)
        dsk_v = dsk_ref[...]
        ys = []
        g_mat = None
        for q in range(8):
            if q % 2 == 0:
                g = q // 2
                g_mat = _dot_nt(xa_v[:, D + 512 + CH * g:D + 512 + CH * (g + 1)], xa_v[:, D + CH * g:D + CH * (g + 1)])
            hin = h_scr[q]
            hs_ref[0, q] = hin
            p = _ssd_pair_fwd(q, xs, xa_v, dt, acs, acs_t, dsk_v, hin, g_mat)
            h_scr[q] = hin * p["cd"] + p["s"]
            ys.append(p["y"])
        y = jnp.concatenate(ys, axis=1)
        _, out, _ = _ssd_gate_norm(y, z_ref[...], nw_ref[...])
        y_ref[...] = out

    small = pl.BlockSpec((1, CH), lambda c: (0, 0))
    return pl.pallas_call(
        body, name=name,
        out_shape=[jax.ShapeDtypeStruct((t, D), F32), jax.ShapeDtypeStruct((nc, 8, CH, CH), F32)], grid=(nc,),
        in_specs=[pl.BlockSpec((CH, XBC_W), lambda c: (c, 0)), pl.BlockSpec((CH, CH), lambda c: (c, 0)),
                  pl.BlockSpec((CH, D), lambda c: (c, 3)), small, small, small, pl.BlockSpec((1, D), lambda c: (0, 0))],
        out_specs=[pl.BlockSpec((CH, D), lambda c: (c, 0)), pl.BlockSpec((1, 8, CH, CH), lambda c: (c, 0, 0, 0))],
        scratch_shapes=[pltpu.VMEM((8, CH, CH), F32)], compiler_params=_cparams(1))(xa, pdt, proj, dtb, alog, dsk, nw)


def ssd_bwd(xa, proj, pdt, hs, dyb, dtb, alog, dsk, nw, *, name):
    t = xa.shape[0]
    nc = t // CH

    def body(xa_ref, dtr_ref, z_ref, hs_ref, dy_ref, dtb_ref, alog_ref, dsk_ref, nw_ref,
             dz_ref, dxa_ref, ddt_ref, gdtb_ref, galog_ref, gdsk_ref, gnw_ref, dh_scr):
        step = pl.program_id(0)
        c = nc - 1 - step

        @pl.when(step == 0)
        def _():
            dh_scr[...] = jnp.zeros_like(dh_scr)
            gdtb_ref[...] = jnp.zeros_like(gdtb_ref)
            galog_ref[...] = jnp.zeros_like(galog_ref)
            gdsk_ref[...] = jnp.zeros_like(gdsk_ref)
            gnw_ref[...] = jnp.zeros_like(gnw_ref)

        xa_v = xa_ref[...]
        dtr = dtr_ref[...]
        dtb_v = dtb_ref[...]
        alog_v = alog_ref[...]
        vm, xs, dt, a, acs, acs_t = _ssd_common(xa_v, dtr, dtb_v, alog_v, c)
        dsk_v = dsk_ref[...]
        z = z_ref[...]
        nw_v = nw_ref[...]
        lane1 = _iota((1, CH), 1)
        sub1 = _iota((CH, 1), 0)
        lane = _iota((CH, CH), 1)

        pairs = []
        g_mats = []
        for q in range(8):
            if q % 2 == 0:
                g = q // 2
                g_mats.append(_dot_nt(xa_v[:, D + 512 + CH * g:D + 512 + CH * (g + 1)], xa_v[:, D + CH * g:D + CH * (g + 1)]))
            pairs.append(_ssd_pair_fwd(q, xs, xa_v, dt, acs, acs_t, dsk_v, hs_ref[0, q], g_mats[q // 2]))
        y_pre = jnp.concatenate([p["y"] for p in pairs], axis=1)

        dout = dy_ref[...]
        sz = _silu(z)
        yv = y_pre * sz
        dyv_parts = []
        gnw_parts = []
        for g in range(4):
            sl = slice(256 * g, 256 * (g + 1))
            yg = yv[:, sl]
            r = lax.rsqrt(jnp.mean(yg * yg, axis=-1, keepdims=True) + EPS)
            gy = dout[:, sl] * nw_v[:, sl]
            dyv_parts.append(r * gy - yg * (r * r * r) * jnp.mean(yg * gy, axis=-1, keepdims=True))
            gnw_parts.append(jnp.sum(dout[:, sl] * yg * r, axis=0, keepdims=True))
        dyv = jnp.concatenate(dyv_parts, axis=1)
        gnw_ref[...] += jnp.concatenate(gnw_parts, axis=1)
        dz_ref[...] = dyv * y_pre * _dsilu(z)
        dy_pre = dyv * sz

        dacs_c = jnp.zeros((CH, CH), F32)
        dacs_r = jnp.zeros((CH, CH), F32)
        ddt = jnp.zeros((CH, CH), F32)
        gdsk = jnp.zeros((1, CH), F32)
        dxs_parts = []
        db_g = [None] * 4
        dc_g = [None] * 4
        dg_g = [None] * 4

        def acc(lst, g, v):
            lst[g] = v if lst[g] is None else lst[g] + v

        for q in range(8):
            p = pairs[q]
            g = q // 2
            h0, h1 = 2 * q, 2 * q + 1
            dy = dy_pre[:, CH * q:CH * (q + 1)]
            hin = hs_ref[0, q]
            dhout = dh_scr[q]
            x = p["x"]
            m_lo = lane < 64
            dxs = dy * p["d_p"]
            t_sk = dy * p["xs_p"]
            gdsk = gdsk + jnp.where(lane1 == h0, jnp.sum(jnp.where(m_lo, t_sk, 0.0)), 0.0) \
                        + jnp.where(lane1 == h1, jnp.sum(jnp.where(m_lo, 0.0, t_sk)), 0.0)
            dx = jnp.zeros((CH, CH), F32)
            for k, (hh, mk, lk, mm_) in enumerate(((h0, m_lo, p["l0"], p["m0"]), (h1, ~m_lo, p["l1"], p["m1"]))):
                dyk = jnp.where(mk, dy, 0.0)
                dm = _dot_nt(dyk, x)
                dx = dx + jnp.where(mk, _dot_tn(mm_, dy), 0.0)
                acc(dg_g, g, dm * lk)
                qm = dm * mm_
                dacs_c = dacs_c + jnp.where(lane1 == hh, jnp.sum(qm, axis=1, keepdims=True), 0.0)
                dacs_r = dacs_r - jnp.where(sub1 == hh, jnp.sum(qm, axis=0, keepdims=True), 0.0)
            dye = dy * p["eac"]
            acc(dc_g, g, _dot(dye, hin))
            t_off = dy * p["yoff_raw"] * p["eac"]
            dacs_c = dacs_c + jnp.where(lane1 == h0, jnp.sum(jnp.where(m_lo, t_off, 0.0), axis=1, keepdims=True), 0.0) \
                            + jnp.where(lane1 == h1, jnp.sum(jnp.where(m_lo, 0.0, t_off), axis=1, keepdims=True), 0.0)
            dhin = _dot_tn(dye, p["cg"]) + dhout * p["cd"]
            w1 = _dot_nt(p["bg"], dhout)
            dx = dx + p["dsv"] * w1
            t_ds = x * w1 * p["dsv"]
            dd0 = jnp.sum(jnp.where(m_lo, t_ds, 0.0), axis=1, keepdims=True)
            dd1 = jnp.sum(jnp.where(m_lo, 0.0, t_ds), axis=1, keepdims=True)
            acc(db_g, g, _dot(x * p["dsv"], dhout))
            t_cd = dhout * hin
            sub_lo = _iota((CH, CH), 0) < 64
            dcd0 = jnp.sum(jnp.where(sub_lo, t_cd, 0.0)) * jnp.exp(p["al0"])
            dcd1 = jnp.sum(jnp.where(sub_lo, 0.0, t_cd)) * jnp.exp(p["al1"])
            last = (sub1 == CH - 1)
            dacs_c = dacs_c + jnp.where(lane1 == h0, jnp.where(last, jnp.sum(dd0) + dcd0, 0.0) - dd0, 0.0) \
                            + jnp.where(lane1 == h1, jnp.where(last, jnp.sum(dd1) + dcd1, 0.0) - dd1, 0.0)
            dh_scr[q] = dhin
            dxs = dxs + dx * p["dt_p"]
            t_dt = dx * p["xs_p"]
            ddt = ddt + jnp.where(lane1 == h0, jnp.sum(jnp.where(m_lo, t_dt, 0.0), axis=1, keepdims=True), 0.0) \
                      + jnp.where(lane1 == h1, jnp.sum(jnp.where(m_lo, 0.0, t_dt), axis=1, keepdims=True), 0.0)
            dxs_parts.append(dxs)

        for g in range(4):
            bg, cg = pairs[2 * g]["bg"], pairs[2 * g]["cg"]
            dc_g[g] = dc_g[g] + _dot(dg_g[g], bg)
            db_g[g] = db_g[g] + _dot_tn(dg_g[g], cg)

        dacs = dacs_c + dacs_r.T
        rtri = (_iota((CH, CH), 1) >= _iota((CH, CH), 0)).astype(F32)
        da = _dot_hi(rtri, dacs)
        ddt = ddt - da * jnp.exp(alog_v)
        galog_ref[...] += jnp.sum(da * a, axis=0, keepdims=True)
        dpre = ddt * _sigmoid(dtr + dtb_v) * (lane1 < SSD_HEADS).astype(F32)
        ddt_ref[...] = dpre
        gdtb_ref[...] += jnp.sum(dpre, axis=0, keepdims=True)
        gdsk_ref[...] += gdsk
        dxa_ref[:, 0:D] = jnp.concatenate(dxs_parts, axis=1) * vm
        dxa_ref[:, D:D + 512] = jnp.concatenate(db_g, axis=1)
        dxa_ref[:, D + 512:D + 1024] = jnp.concatenate(dc_g, axis=1)

    small = pl.BlockSpec((1, CH), lambda s: (0, 0))
    wide = pl.BlockSpec((1, D), lambda s: (0, 0))
    rev = lambda s: nc - 1 - s
    return pl.pallas_call(
        body, name=name,
        out_shape=[jax.ShapeDtypeStruct((t, D), F32), jax.ShapeDtypeStruct((t, XBC_W), F32), jax.ShapeDtypeStruct((t, CH), F32),
                   jax.ShapeDtypeStruct((1, CH), F32), jax.ShapeDtypeStruct((1, CH), F32), jax.ShapeDtypeStruct((1, CH), F32),
                   jax.ShapeDtypeStruct((1, D), F32)],
        grid=(nc,),
        in_specs=[pl.BlockSpec((CH, XBC_W), lambda s: (rev(s), 0)), pl.BlockSpec((CH, CH), lambda s: (rev(s), 0)),
                  pl.BlockSpec((CH, D), lambda s: (rev(s), 3)), pl.BlockSpec((1, 8, CH, CH), lambda s: (rev(s), 0, 0, 0)),
                  pl.BlockSpec((CH, D), lambda s: (rev(s), 0)), small, small, small, wide],
        out_specs=[pl.BlockSpec((CH, D), lambda s: (rev(s), 0)), pl.BlockSpec((CH, XBC_W), lambda s: (rev(s), 0)),
                   pl.BlockSpec((CH, CH), lambda s: (rev(s), 0)), small, small, small, wide],
        scratch_shapes=[pltpu.VMEM((8, CH, CH), F32)], compiler_params=_cparams(1))(xa, pdt, proj, hs, dyb, dtb, alog, dsk, nw)


RET_DK = 256


def _log_gamma(h):
    return math.log(1.0 - 2.0 ** (-5.0 - h))


def _rope(x, cos, sin):
    x1, x2 = x[:, :128], x[:, 128:]
    return jnp.concatenate([x1 * cos - x2 * sin, x1 * sin + x2 * cos], axis=1)


def _unrope(d, cos, sin):
    d1, d2 = d[:, :128], d[:, 128:]
    return jnp.concatenate([d1 * cos + d2 * sin, d2 * cos - d1 * sin], axis=1)


def _ret_head_fwd(h, q, k, v, cos, sin, vm, r_in):
    lg = _log_gamma(h)
    sl = slice(RET_DK * h, RET_DK * (h + 1))
    qr = _rope(q[:, sl], cos, sin)
    kr = _rope(k[:, sl], cos, sin) * (RET_DK ** -0.5)
    vr = v[:, sl] * vm
    rel = (_iota((CH, CH), 0) - _iota((CH, CH), 1)).astype(F32)
    dmask = jnp.where(rel >= 0, jnp.exp(lg * jnp.maximum(rel, 0.0)), 0.0)
    idx = _iota((CH, 1), 0).astype(F32)
    kdec = jnp.exp(lg * (CH - 1 - idx))
    qdec = jnp.exp(lg * (idx + 1.0))
    scores = _dot_nt(qr, kr) * dmask
    y = _dot(scores, vr) + _dot(qr, r_in) * qdec
    kv = _dot_tn(kr * kdec, vr)
    return dict(qr=qr, kr=kr, vr=vr, dmask=dmask, kdec=kdec, qdec=qdec, scores=scores, y=y, kv=kv, cdec=math.exp(lg * CH))


def _group_norm(y):
    mu = jnp.mean(y, axis=-1, keepdims=True)
    yc = y - mu
    r = lax.rsqrt(jnp.mean(yc * yc, axis=-1, keepdims=True) + EPS)
    return yc * r, r


def ret_fwd(proj, cos, sin, *, name):
    t = proj.shape[0]
    nc = t // CH

    def body(q_ref, k_ref, v_ref, g_ref, cos_ref, sin_ref, y_ref, rs_ref, r_scr):
        c = pl.program_id(0)

        @pl.when(c == 0)
        def _():
            r_scr[...] = jnp.zeros_like(r_scr)

        vm = _row_valid(c, CH, CH)
        q, k, v, gt = q_ref[...], k_ref[...], v_ref[...], g_ref[...]
        cos, sin = cos_ref[...], sin_ref[...]
        for h in range(RET_HEADS):
            r_in = r_scr[h]
            rs_ref[0, h] = r_in
            p = _ret_head_fwd(h, q, k, v, cos, sin, vm, r_in)
            r_scr[h] = r_in * p["cdec"] + p["kv"]
            yn, _ = _group_norm(p["y"])
            sl = slice(RET_DK * h, RET_DK * (h + 1))
            y_ref[:, sl] = yn * _silu(gt[:, sl])

    blk = lambda col: pl.BlockSpec((CH, D), lambda c: (c, col))
    tab = pl.BlockSpec((CH, CH), lambda c: (c, 0))
    return pl.pallas_call(
        body, name=name,
        out_shape=[jax.ShapeDtypeStruct((t, D), F32), jax.ShapeDtypeStruct((nc, RET_HEADS, RET_DK, RET_DK), F32)], grid=(nc,),
        in_specs=[blk(6), blk(7), blk(8), blk(9), tab, tab],
        out_specs=[pl.BlockSpec((CH, D), lambda c: (c, 0)), pl.BlockSpec((1, RET_HEADS, RET_DK, RET_DK), lambda c: (c, 0, 0, 0))],
        scratch_shapes=[pltpu.VMEM((RET_HEADS, RET_DK, RET_DK), F32)], compiler_params=_cparams(1))(proj, proj, proj, proj, cos, sin)


def ret_bwd(proj, cos, sin, rs, dyc, *, name):
    t = proj.shape[0]
    nc = t // CH

    def body(q_ref, k_ref, v_ref, g_ref, cos_ref, sin_ref, rs_ref, dy_ref, o_ref, dr_scr):
        step = pl.program_id(0)
        c = nc - 1 - step

        @pl.when(step == 0)
        def _():
            dr_scr[...] = jnp.zeros_like(dr_scr)

        vm = _row_valid(c, CH, CH)
        q, k, v, gt = q_ref[...], k_ref[...], v_ref[...], g_ref[...]
        cos, sin = cos_ref[...], sin_ref[...]
        dout = dy_ref[...]
        for h in range(RET_HEADS):
            sl = slice(RET_DK * h, RET_DK * (h + 1))
            r_in = rs_ref[0, h]
            p = _ret_head_fwd(h, q, k, v, cos, sin, vm, r_in)
            yn, r = _group_norm(p["y"])
            gh = gt[:, sl]
            do = dout[:, sl]
            dg = do * yn * _dsilu(gh)
            dyn = do * _silu(gh)
            dy = r * (dyn - jnp.mean(dyn, axis=-1, keepdims=True) - yn * jnp.mean(dyn * yn, axis=-1, keepdims=True))
            dr_out = dr_scr[h]
            dyc_ = dy * p["qdec"]
            dqr = _dot_nt(dyc_, r_in)
            dr_scr[h] = dr_out * p["cdec"] + _dot_tn(p["qr"], dyc_)
            dkr = _dot_nt(p["vr"], dr_out) * p["kdec"]
            dv = _dot(p["kr"] * p["kdec"], dr_out)
            ds = _dot_nt(dy, p["vr"]) * p["dmask"]
            dqr = dqr + _dot(ds, p["kr"])
            dkr = dkr + _dot_tn(ds, p["qr"])
            dv = dv + _dot_tn(p["scores"], dy)
            o_ref[:, RET_DK * h:RET_DK * (h + 1)] = _unrope(dqr, cos, sin)
            o_ref[:, D + RET_DK * h:D + RET_DK * (h + 1)] = _unrope(dkr, cos, sin) * (RET_DK ** -0.5)
            o_ref[:, 2 * D + RET_DK * h:2 * D + RET_DK * (h + 1)] = dv * vm
            o_ref[:, 3 * D + RET_DK * h:3 * D + RET_DK * (h + 1)] = dg

    rev = lambda s: nc - 1 - s
    blk = lambda col: pl.BlockSpec((CH, D), lambda s: (rev(s), col))
    tab = pl.BlockSpec((CH, CH), lambda s: (rev(s), 0))
    return pl.pallas_call(
        body, name=name, out_shape=jax.ShapeDtypeStruct((t, 4 * D), F32), grid=(nc,),
        in_specs=[blk(6), blk(7), blk(8), blk(9), tab, tab,
                  pl.BlockSpec((1, RET_HEADS, RET_DK, RET_DK), lambda s: (rev(s), 0, 0, 0)), pl.BlockSpec((CH, D), lambda s: (rev(s), 0))],
        out_specs=pl.BlockSpec((CH, 4 * D), lambda s: (rev(s), 0)),
        scratch_shapes=[pltpu.VMEM((RET_HEADS, RET_DK, RET_DK), F32)], compiler_params=_cparams(1))(proj, proj, proj, proj, cos, sin, rs, dyc)


SB_D = 128
SB_SCALE = SB_D ** -0.5


def _split_hi_lo(x):
    hi = x.astype(BF16)
    lo = (x - hi.astype(F32)).astype(BF16)
    return hi, lo


def _sum_matrix(kind):
    a, b = _iota((128, 128), 0), _iota((128, 128), 1)
    tri = ((b > a) if kind == "after" else (b < a)).astype(BF16)
    return jnp.concatenate([tri, tri], axis=1)


def _key_sums(x, mat2):
    hi, lo = _split_hi_lo(x)
    return jnp.dot(mat2, jnp.concatenate([hi, lo], axis=0), preferred_element_type=F32)


def _sb_mask(d_kq, key_idx, first_key, q_minus_k):
    return (d_kq < q_minus_k) & (key_idx >= N_PAD - first_key)


def _sb_log_sigmoid(z):
    return jnp.minimum(z, 0.0) - jnp.log(1.0 + jnp.exp(-jnp.abs(z)))


def sb_fwd(qkv, *, name):
    t = qkv.shape[0]
    tq = _tok_block(t)
    nq = t // tq
    per = tq // 128

    nblk = t // 128

    def body(q_ref, k_ref, v_ref, o_ref, at_ref):
        i = pl.program_id(1)
        top = (i + 1) * per - 1
        mat_after = _sum_matrix("after")
        qs = [q_ref[pl.ds(128 * r, 128), :] for r in range(per)]
        d_kq = _iota((128, 128), 0) - _iota((128, 128), 1)
        key_idx = _iota((128, 128), 0)

        def rows_of(ref, b):
            off = pl.multiple_of(jnp.clip(b, 0, nblk - 1) * 128, 128)
            return ref[pl.ds(off, 128), :]

        def mask(r, b):
            return _sb_mask(d_kq, key_idx, b * 128, (i * per + r - b) * 128)

        def step(it, carry):
            zs, lss, sufs, tots, a_runs, accs = carry
            b = top - it
            kb = rows_of(k_ref, b)
            z_new = tuple(_dot_nt(kb, qs[r]) * SB_SCALE for r in range(per))
            ls_new, suf_new, tot_new = [], [], []
            for r in range(per):
                ls = _sb_log_sigmoid(zs[r])
                lneg = jnp.where(mask(r, b + 1), ls - zs[r], 0.0)
                suf = _key_sums(lneg, mat_after)
                ls_new.append(ls)
                suf_new.append(suf)
                tot_new.append(suf[0:1, :] + lneg[0:1, :])
            vb = rows_of(v_ref, b + 2)
            a_new, acc_new = [], []
            for r in range(per):
                w = jnp.where(mask(r, b + 2), jnp.exp(lss[r] + a_runs[r] + sufs[r]), 0.0)
                a_new.append(a_runs[r] + tots[r])
                acc_new.append(accs[r] + _dot_tn(w, vb))
            return z_new, tuple(ls_new), tuple(suf_new), tuple(tot_new), tuple(a_new), tuple(acc_new)

        zeros = tuple(qs[r].astype(F32) * 0.0 for r in range(per))
        zrow = tuple(z[0:1, :] for z in zeros)
        _, _, _, _, a_runs, accs = lax.fori_loop(0, top + 3, step, (zeros, zeros, zeros, zrow, zrow, zeros))
        at_ref[...] = jnp.zeros_like(at_ref)
        for r in range(per):
            o_ref[pl.ds(128 * r, 128), :] = accs[r]
            at_ref[0, 0, r:r + 1, :] = a_runs[r]

    blk = pl.BlockSpec((tq, 128), lambda h, i: (i, h))
    return pl.pallas_call(
        body, name=name,
        out_shape=[jax.ShapeDtypeStruct((t, D), F32), jax.ShapeDtypeStruct((SB_HEADS, nq, 8, 128), F32)], grid=(SB_HEADS, nq),
        in_specs=[blk, pl.BlockSpec((t, 128), lambda h, i: (0, SB_HEADS + h)),
                  pl.BlockSpec((t, 128), lambda h, i: (0, 2 * SB_HEADS + h))],
        out_specs=[blk, pl.BlockSpec((1, 1, 8, 128), lambda h, i: (h, i, 0, 0))], compiler_params=_cparams(2))(qkv, qkv, qkv)


def sb_bwd(qkv, atot, dout, *, name):
    t = qkv.shape[0]
    tq = _tok_block(t)
    nq = t // tq
    per = tq // 128

    nblk = t // 128

    def body(q_ref, k_ref, v_ref, at_ref, do_ref, dq_ref, dk_ref, dv_ref):
        i = pl.program_id(1)
        top = (i + 1) * per - 1

        @pl.when(i == 0)
        def _():
            dk_ref[...] = jnp.zeros_like(dk_ref)
            dv_ref[...] = jnp.zeros_like(dv_ref)

        qs = [q_ref[pl.ds(128 * r, 128), :] for r in range(per)]
        dos = [do_ref[pl.ds(128 * r, 128), :].astype(BF16) for r in range(per)]
        q_all = q_ref[...]
        do_all = do_ref[...].astype(BF16)
        a_tots = [at_ref[0, 0, r:r + 1, :] for r in range(per)]
        mat_after = _sum_matrix("after")
        mat_before = _sum_matrix("before")
        d_kq = _iota((128, 128), 0) - _iota((128, 128), 1)
        key_idx = _iota((128, 128), 0)

        def offset(b):
            return pl.multiple_of(jnp.clip(b, 0, nblk - 1) * 128, 128)

        def mask(r, b):
            return _sb_mask(d_kq, key_idx, b * 128, (i * per + r - b) * 128)

        def step(b, carry):
            lss, dws, sufs, tots, p_runs, e_runs, dqs = carry
            off = offset(b)
            kb = k_ref[pl.ds(off, 128), :]
            vb = v_ref[pl.ds(off, 128), :]
            zs = [_dot_nt(kb, qs[r]) * SB_SCALE for r in range(per)]
            dw_new = tuple(_dot_nt(vb, dos[r]) for r in range(per))
            off1 = offset(b - 1)
            k1 = k_ref[pl.ds(off1, 128), :]
            ws, es, epres, sigs, p_new = [], [], [], [], []
            for r in range(per):
                p = p_runs[r] + tots[r]
                w = jnp.where(mask(r, b - 1), jnp.exp(lss[r] + (a_tots[r] - p) + sufs[r]), 0.0)
                e = w * dws[r]
                p_new.append(p)
                ws.append(w.astype(BF16))
                es.append(e)
                epres.append(_key_sums(e, mat_before))
                sigs.append(jnp.exp(lss[r]))
            dv_ref[0, pl.ds(off1, 128), :] += _dot(jnp.concatenate(ws, axis=1), do_all)
            ls_new, suf_new, tot_new = [], [], []
            for r in range(per):
                ls = _sb_log_sigmoid(zs[r])
                lneg = jnp.where(mask(r, b), ls - zs[r], 0.0)
                suf = _key_sums(lneg, mat_after)
                ls_new.append(ls)
                suf_new.append(suf)
                tot_new.append(suf[0:1, :] + lneg[0:1, :])
            dzs, e_new, dq_new = [], [], []
            for r in range(per):
                t2 = jnp.where(mask(r, b - 1), (e_runs[r] + epres[r]) * sigs[r], 0.0)
                dz = ((es[r] * (1.0 - sigs[r]) - t2) * SB_SCALE).astype(BF16)
                e_new.append(e_runs[r] + epres[r][127:128, :] + es[r][127:128, :])
                dq_new.append(dqs[r] + _dot_tn(dz, k1))
                dzs.append(dz)
            dk_ref[0, pl.ds(off1, 128), :] += _dot(jnp.concatenate(dzs, axis=1), q_all)
            return tuple(ls_new), dw_new, tuple(suf_new), tuple(tot_new), tuple(p_new), tuple(e_new), tuple(dq_new)

        zeros = tuple(qs[r].astype(F32) * 0.0 for r in range(per))
        zrow = tuple(z[0:1, :] for z in zeros)
        carry = lax.fori_loop(0, top + 2, step, (zeros, zeros, zeros, zrow, zrow, zrow, zeros))
        for r in range(per):
            dq_ref[pl.ds(128 * r, 128), :] = carry[6][r]

    head_blk = pl.BlockSpec((1, t, 128), lambda h, i: (h, 0, 0))
    return pl.pallas_call(
        body, name=name,
        out_shape=[jax.ShapeDtypeStruct((t, D), F32), jax.ShapeDtypeStruct((SB_HEADS, t, 128), F32),
                   jax.ShapeDtypeStruct((SB_HEADS, t, 128), F32)],
        grid=(SB_HEADS, nq),
        in_specs=[pl.BlockSpec((tq, 128), lambda h, i: (i, h)), pl.BlockSpec((t, 128), lambda h, i: (0, SB_HEADS + h)),
                  pl.BlockSpec((t, 128), lambda h, i: (0, 2 * SB_HEADS + h)),
                  pl.BlockSpec((1, 1, 8, 128), lambda h, i: (h, i, 0, 0)), pl.BlockSpec((tq, 128), lambda h, i: (i, h))],
        out_specs=[pl.BlockSpec((tq, 128), lambda h, i: (i, h)), head_blk, head_blk],
        compiler_params=_cparams(2, 60 * 1024 * 1024))(qkv, qkv, qkv, atot, dout)


def branch_fwd(y, proj, w, n, *, name, add=None):
    t = y.shape[0]
    tb = _tok_block(t)
    has_add = add is not None

    def body(*refs):
        if has_add:
            y_ref, g_ref, w_ref, a_ref, o_ref = refs
        else:
            y_ref, g_ref, w_ref, o_ref = refs
        r = _sigmoid(g_ref[...]) * _dot(y_ref[...], w_ref[...])
        if has_add:
            r = r + a_ref[...]
        o_ref[...] = r

    blk = pl.BlockSpec((tb, D), lambda i: (i, 0))
    in_specs = [blk, pl.BlockSpec((tb, D), lambda i: (i, COL_GATE + n)), pl.BlockSpec((D, D), lambda i: (0, 0))] + ([blk] if has_add else [])
    args = [y, proj, w] + ([add] if has_add else [])
    return pl.pallas_call(body, name=name, out_shape=jax.ShapeDtypeStruct((t, D), F32), grid=(t // tb,),
                          in_specs=in_specs, out_specs=blk, compiler_params=_cparams(1))(*args)


def branch_bwd(y, proj, w, wt, n, dmerged, *, name):
    t = y.shape[0]
    tb = _tok_block(t)

    def body(y_ref, g_ref, w_ref, wt_ref, dm_ref, dg_ref, dup_ref, dy_ref):
        up = _dot(y_ref[...], w_ref[...])
        gate = _sigmoid(g_ref[...])
        dm = dm_ref[...]
        dg_ref[...] = dm * up * gate * (1.0 - gate)
        dup = (dm * gate).astype(BF16)
        dup_ref[...] = dup
        dy_ref[...] = _dot(dup, wt_ref[...])

    blk = pl.BlockSpec((tb, D), lambda i: (i, 0))
    wspec = pl.BlockSpec((D, D), lambda i: (0, 0))
    return pl.pallas_call(
        body, name=name,
        out_shape=[jax.ShapeDtypeStruct((t, D), F32), jax.ShapeDtypeStruct((t, D), BF16), jax.ShapeDtypeStruct((t, D), F32)],
        grid=(t // tb,),
        in_specs=[blk, pl.BlockSpec((tb, D), lambda i: (i, COL_GATE + n)), wspec, wspec, blk],
        out_specs=[blk, blk, blk], compiler_params=_cparams(1))(y, proj, w, wt, dmerged)


def swiglu_fwd(f, *, name):
    t = f.shape[0]
    tb = _tok_block(t)

    def body(g_ref, u_ref, o_ref):
        o_ref[...] = (_silu(g_ref[...]) * u_ref[...]).astype(BF16)

    return pl.pallas_call(body, name=name, out_shape=jax.ShapeDtypeStruct((t, D_FF), BF16), grid=(t // tb,),
                          in_specs=[pl.BlockSpec((tb, D_FF), lambda i: (i, 0)), pl.BlockSpec((tb, D_FF), lambda i: (i, 1))],
                          out_specs=pl.BlockSpec((tb, D_FF), lambda i: (i, 0)), compiler_params=_cparams(1))(f, f)


def swiglu_bwd(f, dact, *, name):
    t = f.shape[0]
    tb = _tok_block(t)

    def body(g_ref, u_ref, d_ref, o_ref):
        g, u, d = g_ref[...], u_ref[...], d_ref[...]
        o_ref[:, 0:D_FF] = d * u * _dsilu(g)
        o_ref[:, D_FF:2 * D_FF] = d * _silu(g)

    return pl.pallas_call(body, name=name, out_shape=jax.ShapeDtypeStruct((t, 2 * D_FF), F32), grid=(t // tb,),
                          in_specs=[pl.BlockSpec((tb, D_FF), lambda i: (i, 0)), pl.BlockSpec((tb, D_FF), lambda i: (i, 1)),
                                    pl.BlockSpec((tb, D_FF), lambda i: (i, 0))],
                          out_specs=pl.BlockSpec((tb, 2 * D_FF), lambda i: (i, 0)), compiler_params=_cparams(1))(f, f, dact)


def loss_head(h, target, *, name):
    t = h.shape[0]
    nb = t // CH

    def body(h_ref, t_ref, l_ref, d_ref):
        i = pl.program_id(0)

        @pl.when(i == 0)
        def _():
            l_ref[...] = jnp.zeros_like(l_ref)
            d_ref[...] = jnp.zeros_like(d_ref)

        @pl.when(i > 0)
        def _():
            err = h_ref[...] - t_ref[...]
            d_ref[...] = err * (1.0 / D)
            l_ref[...] += jnp.sum(err * err) * (0.5 / D)

    return pl.pallas_call(
        body, name=name, out_shape=[jax.ShapeDtypeStruct((8, 128), F32), jax.ShapeDtypeStruct((t, D), F32)], grid=(nb,),
        in_specs=[pl.BlockSpec((CH, D), lambda i: (i, 0)), pl.BlockSpec((CH, D), lambda i: (jnp.maximum(i - 1, 0), 0))],
        out_specs=[pl.BlockSpec((8, 128), lambda i: (0, 0)), pl.BlockSpec((CH, D), lambda i: (i, 0))],
        compiler_params=_cparams(1))(h, target)


def _pad_rows8(w):
    return jnp.concatenate([w, jnp.zeros((8 - w.shape[0], w.shape[1]), w.dtype)], axis=0)


def _pad_lanes(v, n=CH):
    return jnp.concatenate([v, jnp.zeros((n - v.shape[0],), v.dtype)])[None, :]


def prep_layer(p):
    w = p["w_in"]
    zeros = jnp.zeros((D, CH - SSD_HEADS), w.dtype)
    w_f = jnp.concatenate([w[:, :6144], w[:, 6160:10256], w[:, 13328:17424]], axis=1)
    w_dt = jnp.concatenate([w[:, 6144:6160], zeros], axis=1)
    w_sb = w[:, 10256:13328]
    return dict(
        w_f=w_f, w_sb=w_sb, w_dt=w_dt, w_f_t=w_f.T, w_sb_t=w_sb.T, w_dt_t=w_dt.T,
        w_br=p["w_branch"], w_br_t=jnp.swapaxes(p["w_branch"], 1, 2), w_out=p["w_out"], w_out_t=p["w_out"].T,
        w_fi=p["w_ffn_in"], w_fi_t=p["w_ffn_in"].T, w_fo=p["w_ffn_out"], w_fo_t=p["w_ffn_out"].T,
        conv_a8=_pad_rows8(p["conv_a"]), conv_s8=_pad_rows8(p["ssd_conv_w"]), conv_sb=p["ssd_conv_b"][None, :],
        dtb=_pad_lanes(p["ssd_dt_bias"]), alog=_pad_lanes(p["ssd_a_log"]), dsk=_pad_lanes(p["ssd_d"]), nw=p["ssd_norm"][None, :],
        n1=p["norm_mix_pre"][None, :], n2=p["norm_mix_post"][None, :], n3=p["norm_ffn_pre"][None, :], n4=p["norm_ffn_post"][None, :])


def layer_fwd(h0, w, cos, sin, l):
    nm = lambda s: f"l{l}_{s}"
    hn = rms_fwd(h0, w["n1"], name=nm("rms1"), out_dtype=BF16)
    proj = mm(hn, w["w_f"], name=nm("proj_f"))
    qkv = mm(hn, w["w_sb"], name=nm("proj_sb"), out_dtype=BF16)
    pdt = mm(hn, w["w_dt"], name=nm("proj_dt"))
    y_a = conv_a_fwd(proj, w["conv_a8"], name=nm("conv_a"))
    xa = ssd_conv_fwd(proj, w["conv_s8"], w["conv_sb"], name=nm("ssd_conv"))
    y_b, hs = ssd_fwd(xa, proj, pdt, w["dtb"], w["alog"], w["dsk"], w["nw"], name=nm("ssd"))
    y_c, rs = ret_fwd(proj, cos, sin, name=nm("ret"))
    y_d, sb_atot = sb_fwd(qkv, name=nm("sb"))
    ys = (y_a, y_b, y_c, y_d)
    merged = None
    for n in range(4):
        merged = branch_fwd(ys[n], proj, w["w_br"][n], n, name=nm(f"branch{n}"), add=merged)
    mix = mm(merged, w["w_out"], name=nm("mix"))
    h1 = rms_fwd(mix, w["n2"], name=nm("rms2"), res=h0)
    hn2 = rms_fwd(h1, w["n3"], name=nm("rms3"), out_dtype=BF16)
    f = mm(hn2, w["w_fi"], name=nm("ffn_in"))
    act = swiglu_fwd(f, name=nm("swiglu"))
    f2 = mm(act, w["w_fo"], name=nm("ffn_out"))
    h2 = rms_fwd(f2, w["n4"], name=nm("rms4"), res=h1)
    saved = dict(h0=h0, hn=hn, proj=proj, qkv=qkv, pdt=pdt, xa=xa, hs=hs, rs=rs, ys=ys, sb_atot=sb_atot, merged=merged, mix=mix, h1=h1, hn2=hn2,
                 f=f, act=act, f2=f2)
    return h2, saved


def layer_bwd(dh2, s, w, cos, sin, l):
    nm = lambda t: f"l{l}_{t}"
    g = {}
    df2, g["n4"] = rms_bwd(s["f2"], w["n4"], dh2, name=nm("rms4_b"))
    g["w_fo"] = mm_tn(s["act"], df2, name=nm("ffn_out_dw"))
    dact = mm(df2, w["w_fo_t"], name=nm("ffn_out_dx"), out_dtype=BF16)
    df = swiglu_bwd(s["f"], dact, name=nm("swiglu_b"))
    g["w_fi"] = mm_tn(s["hn2"], df, name=nm("ffn_in_dw"))
    dhn2 = mm(df, w["w_fi_t"], name=nm("ffn_in_dx"))
    dh1, g["n3"] = rms_bwd(s["h1"], w["n3"], dhn2, name=nm("rms3_b"), add=dh2)
    dmix, g["n2"] = rms_bwd(s["mix"], w["n2"], dh1, name=nm("rms2_b"))
    g["w_out"] = mm_tn(s["merged"], dmix, name=nm("mix_dw"))
    dmerged = mm(dmix, w["w_out_t"], name=nm("mix_dx"))
    dgate, dys, dwb = [], [], []
    for n in range(4):
        dg_n, dup_n, dy_n = branch_bwd(s["ys"][n], s["proj"], w["w_br"][n], w["w_br_t"][n], n, dmerged, name=nm(f"branch{n}_b"))
        dgate.append(dg_n)
        dys.append(dy_n)
        dwb.append(mm_tn(s["ys"][n], dup_n, name=nm(f"branch{n}_dw")))
    g["w_br"] = jnp.stack(dwb)
    d_a, g["conv_a8"] = conv_a_bwd(s["proj"], w["conv_a8"], dys[0], name=nm("conv_a_b"))
    dz, dxa, ddt, g["dtb"], g["alog"], g["dsk"], g["nw"] = ssd_bwd(
        s["xa"], s["proj"], s["pdt"], s["hs"], dys[1], w["dtb"], w["alog"], w["dsk"], w["nw"], name=nm("ssd_b"))
    dpre, g["conv_s8"], g["conv_sb"] = ssd_conv_bwd_pre(s["proj"], w["conv_s8"], w["conv_sb"], dxa, name=nm("ssd_conv_b1"))
    dxbc = ssd_conv_bwd_in(dpre, w["conv_s8"], name=nm("ssd_conv_b2"))
    d_r = ret_bwd(s["proj"], cos, sin, s["rs"], dys[2], name=nm("ret_b"))
    dq, dk, dv = sb_bwd(s["qkv"], s["sb_atot"], dys[3], name=nm("sb_b"))
    t = dq.shape[0]
    d_sb = jnp.concatenate([dq, jnp.swapaxes(dk, 0, 1).reshape(t, D), jnp.swapaxes(dv, 0, 1).reshape(t, D)], axis=1)
    segs = [(d_a, 0), (dz, 3072), (dxbc, 4096), (d_r, 6144), (dgate[0], 10240), (dgate[1], 11264), (dgate[2], 12288),
            (dgate[3], 13312)]
    dws = [mm_tn(s["hn"], d, name=nm(f"proj_dw{k}")) for k, (d, _) in enumerate(segs)]
    g["w_f"] = jnp.concatenate(dws, axis=1)
    g["w_sb"] = mm_tn(s["hn"], d_sb, name=nm("proj_dw_sb"))
    g["w_dt"] = mm_tn(s["hn"], ddt, name=nm("proj_dw_dt"))
    dhn = mm(d_sb, w["w_sb_t"], name=nm("proj_dx_sb"))
    dhn = mm(ddt, w["w_dt_t"], name=nm("proj_dx_dt"), add=dhn)
    for k, (d, c0) in enumerate(segs):
        dhn = mm(d, w["w_f_t"][c0:c0 + d.shape[1]], name=nm(f"proj_dx{k}"), add=dhn)
    dh0, g["n1"] = rms_bwd(s["h0"], w["n1"], dhn, name=nm("rms1_b"), add=dh1)
    return dh0, g


def layer_grads_to_params(g):
    wf, wsb = g["w_f"], g["w_sb"]
    w_in = jnp.concatenate([wf[:, :6144], g["w_dt"][:, :SSD_HEADS], wf[:, 6144:10240], wsb, wf[:, 10240:14336]], axis=1)
    return dict(
        w_in=w_in, conv_a=g["conv_a8"][:3], ssd_conv_w=g["conv_s8"][:4], ssd_conv_b=g["conv_sb"][0],
        ssd_dt_bias=g["dtb"][0, :SSD_HEADS], ssd_a_log=g["alog"][0, :SSD_HEADS], ssd_d=g["dsk"][0, :SSD_HEADS], ssd_norm=g["nw"][0],
        w_branch=g["w_br"], w_out=g["w_out"], w_ffn_in=g["w_fi"], w_ffn_out=g["w_fo"],
        norm_mix_pre=g["n1"][0], norm_mix_post=g["n2"][0], norm_ffn_pre=g["n3"][0], norm_ffn_post=g["n4"][0])


def rope_tables(t):
    half = RET_DK // 2
    inv = ROPE_BASE ** (-jnp.arange(half, dtype=F32) / half)
    ang = jnp.arange(t).astype(F32)[:, None] * inv[None, :]
    return jnp.cos(ang), jnp.sin(ang)


def local_step(x, target, meta, layers):
    h = jnp.concatenate([jnp.zeros((N_PAD, D), F32), meta, x], axis=0)
    t = h.shape[0]
    cos, sin = rope_tables(t)
    ws = [prep_layer(p) for p in layers]
    saved = []
    for l, w in enumerate(ws):
        h, s = layer_fwd(h, w, cos, sin, l)
        saved.append(s)
    loss, dh = loss_head(h, target, name="loss_head")
    grads = [None] * len(ws)
    for l in reversed(range(len(ws))):
        dh, g = layer_bwd(dh, saved[l], ws[l], cos, sin, l)
        grads[l] = layer_grads_to_params(g)
    return loss, dh[CH:], dh[N_PAD:CH], grads


def _my_place():
    return lax.axis_index("x"), lax.axis_index("y"), lax.axis_index("c")


def _flat(px, py, pc):
    return 4 * px + 2 * py + pc


ANY = pl.BlockSpec(memory_space=pl.ANY)


def all_gather(x_shard, *, name):
    shape = x_shard.shape

    def body(x_ref, out_ref, send_sems, recv_sems, local_sem):
        x, y, c = _my_place()
        me, sibling = (x, y, c), (x, y, 1 - c)
        chips = [(1 - x, y), (x, 1 - y), (1 - x, 1 - y)]

        def rows(px, py, pc):
            return out_ref.at[_flat(px, py, pc)]

        def copy(k, block, to, src=None):
            return pltpu.make_async_remote_copy(
                src_ref=rows(*block) if src is None else src, dst_ref=rows(*block),
                send_sem=send_sems.at[k], recv_sem=recv_sems.at[k], device_id=to, device_id_type=MESH_ID)

        mine = pltpu.make_async_copy(x_ref, rows(*me), local_sem)
        mine.start()
        first = [copy(0, me, sibling, src=x_ref)]
        first += [copy(1 + j, me, (*chip, c), src=x_ref) for j, chip in enumerate(chips)]
        for cp in first:
            cp.start()
        passed = [copy(4 + j, (*chip, c), sibling) for j, chip in enumerate(chips)]
        for j, chip in enumerate(chips):
            copy(1 + j, (*chip, c), me).wait_recv()
            passed[j].start()
        copy(0, sibling, me).wait_recv()
        for j, chip in enumerate(chips):
            copy(4 + j, (*chip, 1 - c), me).wait_recv()
        for cp in first + passed:
            cp.wait_send()
        mine.wait()

    return pl.pallas_call(
        body, name=name, out_shape=jax.ShapeDtypeStruct((N_DEV,) + shape, x_shard.dtype),
        in_specs=[ANY], out_specs=ANY,
        scratch_shapes=[pltpu.SemaphoreType.DMA((7,)), pltpu.SemaphoreType.DMA((7,)), pltpu.SemaphoreType.DMA],
    )(x_shard)


def all_to_all(g, *, name):
    def body(g_ref, out_ref, send_sems, recv_sems, local_sem):
        x, y, c = _my_place()
        me = _flat(x, y, c)
        mine = pltpu.make_async_copy(g_ref.at[me], out_ref.at[me], local_sem)
        mine.start()
        peers = []
        for k in range(1, N_DEV):
            px = jnp.bitwise_xor(x, (k >> 2) & 1)
            py = jnp.bitwise_xor(y, (k >> 1) & 1)
            pc = jnp.bitwise_xor(c, k & 1)
            peers.append((px, py, pc))
        sends = []
        for k, peer in enumerate(peers):
            cp = pltpu.make_async_remote_copy(
                src_ref=g_ref.at[_flat(*peer)], dst_ref=out_ref.at[me],
                send_sem=send_sems.at[k], recv_sem=recv_sems.at[k], device_id=peer, device_id_type=MESH_ID)
            cp.start()
            sends.append(cp)
        for k, peer in enumerate(peers):
            slot = out_ref.at[_flat(*peer)]
            pltpu.make_async_remote_copy(
                src_ref=slot, dst_ref=slot, send_sem=send_sems.at[k], recv_sem=recv_sems.at[k],
                device_id=peer, device_id_type=MESH_ID).wait_recv()
        for cp in sends:
            cp.wait_send()
        mine.wait()

    return pl.pallas_call(
        body, name=name, out_shape=jax.ShapeDtypeStruct(g.shape, g.dtype), in_specs=[ANY], out_specs=ANY,
        scratch_shapes=[pltpu.SemaphoreType.DMA((7,)), pltpu.SemaphoreType.DMA((7,)), pltpu.SemaphoreType.DMA],
    )(g)


def sum_slots(a, *, name):
    def body(a_ref, o_ref):
        s = a_ref[0]
        for d in range(1, N_DEV):
            s = s + a_ref[d]
        o_ref[...] = s

    return pl.pallas_call(body, name=name, out_shape=jax.ShapeDtypeStruct(a.shape[1:], a.dtype))(a)


def _adamw_math(w, g, m, v):
    m = ADAM_B1 * m + (1.0 - ADAM_B1) * g
    v = ADAM_B2 * v + (1.0 - ADAM_B2) * (g * g)
    m_hat = m / (1.0 - ADAM_B1 ** ADAM_STEP)
    v_hat = v / (1.0 - ADAM_B2 ** ADAM_STEP)
    delta = -ADAM_LR * (m_hat / (jnp.sqrt(v_hat) + ADAM_EPS) + ADAM_WD * w)
    return delta, m, v


def adamw_big(recv, w, m, v, *, name):
    r, cols = w.shape
    tb = 128

    def body(r_ref, w_ref, m_ref, v_ref, g_ref, d_ref, nm_ref, nv_ref):
        g = r_ref[0].astype(F32)
        for d in range(1, N_DEV):
            g = g + r_ref[d].astype(F32)
        g_ref[...] = g
        d_ref[...], nm_ref[...], nv_ref[...] = _adamw_math(w_ref[...], g, m_ref[...], v_ref[...])

    blk = pl.BlockSpec((tb, cols), lambda i: (i, 0))
    out = jax.ShapeDtypeStruct((r, cols), F32)
    return pl.pallas_call(
        body, name=name, out_shape=[out] * 4, grid=(r // tb,),
        in_specs=[pl.BlockSpec((N_DEV, tb, cols), lambda i: (0, i, 0)), blk, blk, blk], out_specs=[blk] * 4,
        compiler_params=_cparams(1))(recv, w, m, v)


def adamw_small(w, g, m, v, *, name):
    def body(w_ref, g_ref, m_ref, v_ref, d_ref, nm_ref, nv_ref):
        d_ref[...], nm_ref[...], nv_ref[...] = _adamw_math(w_ref[...], g_ref[...], m_ref[...], v_ref[...])

    out = jax.ShapeDtypeStruct(w.shape, F32)
    return pl.pallas_call(body, name=name, out_shape=[out] * 3)(w, g, m, v)


BIG = ("w_in", "w_branch", "w_out", "w_ffn_in", "w_ffn_out")
BIG_SHARD = {"w_in": (DEPTH, D, 2178), "w_branch": (DEPTH, 4, 128, D), "w_out": (DEPTH, 128, D),
             "w_ffn_in": (DEPTH, D, 704), "w_ffn_out": (DEPTH, 352, D)}
BIG_FULL = {"w_in": ((1, 2, 0, 3), (DEPTH, D, 17424)), "w_branch": ((1, 2, 0, 3, 4), (DEPTH, 4, D, D)),
            "w_out": ((1, 0, 2, 3), (DEPTH, D, D)), "w_ffn_in": ((1, 2, 0, 3), (DEPTH, D, 2 * D_FF)),
            "w_ffn_out": ((1, 0, 2, 3), (DEPTH, D_FF, D))}
BIG_ROWS = {n: int(np.prod(s)) // D for n, s in BIG_SHARD.items()}
BIG_R = 7808


def pack_big(shards, dtype):
    parts = [shards[n].astype(dtype).reshape(BIG_ROWS[n], D) for n in BIG]
    parts.append(jnp.zeros((BIG_R - sum(BIG_ROWS.values()), D), dtype))
    return jnp.concatenate(parts, axis=0)


def unpack_big(flat):
    out, o = {}, 0
    for n in BIG:
        out[n] = flat[o:o + BIG_ROWS[n]].reshape(BIG_SHARD[n])
        o += BIG_ROWS[n]
    return out


def unpack_big_full(gathered):
    out, o = {}, 0
    for n in BIG:
        perm, full = BIG_FULL[n]
        out[n] = gathered[:, o:o + BIG_ROWS[n]].reshape((N_DEV,) + BIG_SHARD[n]).transpose(perm).reshape(full)
        o += BIG_ROWS[n]
    return out


def pack_big_full(full, dtype):
    parts = []
    for n in BIG:
        perm, _ = BIG_FULL[n]
        split = tuple(int(v) for v in np.array((N_DEV,) + BIG_SHARD[n])[list(perm)])
        inv = tuple(int(i) for i in np.argsort(perm))
        parts.append(full[n].astype(dtype).reshape(split).transpose(inv).reshape(N_DEV, BIG_ROWS[n], D))
    parts.append(jnp.zeros((N_DEV, BIG_R - sum(BIG_ROWS.values()), D), dtype))
    return jnp.concatenate(parts, axis=1)


def _rows128(a):
    a = a.reshape(-1)
    pad = (-a.shape[0]) % CH
    if pad:
        a = jnp.concatenate([a, jnp.zeros((pad,), a.dtype)])
    return a.reshape(-1, CH)


def _pack_rows(arrs, total):
    parts = [_rows128(a) for a in arrs]
    n = sum(p.shape[0] for p in parts)
    parts.append(jnp.zeros((total - n, CH), F32))
    return jnp.concatenate(parts, axis=0)


def _unpack_rows(flat, shapes):
    out, o = [], 0
    for s in shapes:
        size = int(np.prod(s))
        rows = -(-size // CH)
        out.append(flat[o:o + rows].reshape(-1)[:size].reshape(s))
        o += rows
    return out


SMALL_SHARDED = ("meta", "conv_a", "ssd_conv_w")
SMALL_SHARD_SHAPE = {"meta": (N_META, 128), "conv_a": (DEPTH, 3, 128), "ssd_conv_w": (DEPTH, 4, 256)}
SMALL_FULL_SHAPE = {"meta": (N_META, D), "conv_a": (DEPTH, 3, D), "ssd_conv_w": (DEPTH, 4, 2048)}
SMALL_REPL = ("ssd_conv_b", "ssd_dt_bias", "ssd_a_log", "ssd_d", "ssd_norm", "norm_mix_pre", "norm_mix_post", "norm_ffn_pre",
              "norm_ffn_post")
SMALL_REPL_SHAPE = {"ssd_conv_b": (DEPTH, 2048), "ssd_dt_bias": (DEPTH, SSD_HEADS), "ssd_a_log": (DEPTH, SSD_HEADS),
                    "ssd_d": (DEPTH, SSD_HEADS), "ssd_norm": (DEPTH, D), "norm_mix_pre": (DEPTH, D), "norm_mix_post": (DEPTH, D),
                    "norm_ffn_pre": (DEPTH, D), "norm_ffn_post": (DEPTH, D)}


def _gather_small_full(gathered, n):
    nd = gathered.ndim
    perm = tuple(range(1, nd - 1)) + (0, nd - 1)
    return gathered.transpose(perm).reshape(SMALL_FULL_SHAPE[n])


WEIGHTS = ("meta", "w_in", "conv_a", "ssd_conv_w", "ssd_conv_b", "ssd_dt_bias", "ssd_a_log", "ssd_d", "ssd_norm", "w_branch", "w_out",
           "w_ffn_in", "w_ffn_out", "norm_mix_pre", "norm_mix_post", "norm_ffn_pre", "norm_ffn_post")


def kernel(x, meta, w_in, conv_a, ssd_conv_w, ssd_conv_b, ssd_dt_bias, ssd_a_log, ssd_d, ssd_norm, w_branch, w_out, w_ffn_in, w_ffn_out, norm_mix_pre, norm_mix_post, norm_ffn_pre, norm_ffn_post, loss_target, m_meta, m_w_in, m_conv_a, m_ssd_conv_w, m_ssd_conv_b, m_ssd_dt_bias, m_ssd_a_log, m_ssd_d, m_ssd_norm, m_w_branch, m_w_out, m_w_ffn_in, m_w_ffn_out, m_norm_mix_pre, m_norm_mix_post, m_norm_ffn_pre, m_norm_ffn_post, v_meta, v_w_in, v_conv_a, v_ssd_conv_w, v_ssd_conv_b, v_ssd_dt_bias, v_ssd_a_log, v_ssd_d, v_ssd_norm, v_w_branch, v_w_out, v_w_ffn_in, v_w_ffn_out, v_norm_mix_pre, v_norm_mix_post, v_norm_ffn_pre, v_norm_ffn_post):
    w = dict(meta=meta, w_in=w_in, conv_a=conv_a, ssd_conv_w=ssd_conv_w, ssd_conv_b=ssd_conv_b, ssd_dt_bias=ssd_dt_bias,
             ssd_a_log=ssd_a_log, ssd_d=ssd_d, ssd_norm=ssd_norm, w_branch=w_branch, w_out=w_out, w_ffn_in=w_ffn_in,
             w_ffn_out=w_ffn_out, norm_mix_pre=norm_mix_pre, norm_mix_post=norm_mix_post, norm_ffn_pre=norm_ffn_pre,
             norm_ffn_post=norm_ffn_post)
    m = dict(meta=m_meta, w_in=m_w_in, conv_a=m_conv_a, ssd_conv_w=m_ssd_conv_w, ssd_conv_b=m_ssd_conv_b, ssd_dt_bias=m_ssd_dt_bias,
             ssd_a_log=m_ssd_a_log, ssd_d=m_ssd_d, ssd_norm=m_ssd_norm, w_branch=m_w_branch, w_out=m_w_out, w_ffn_in=m_w_ffn_in,
             w_ffn_out=m_w_ffn_out, norm_mix_pre=m_norm_mix_pre, norm_mix_post=m_norm_mix_post, norm_ffn_pre=m_norm_ffn_pre,
             norm_ffn_post=m_norm_ffn_post)
    v = dict(meta=v_meta, w_in=v_w_in, conv_a=v_conv_a, ssd_conv_w=v_ssd_conv_w, ssd_conv_b=v_ssd_conv_b, ssd_dt_bias=v_ssd_dt_bias,
             ssd_a_log=v_ssd_a_log, ssd_d=v_ssd_d, ssd_norm=v_ssd_norm, w_branch=v_w_branch, w_out=v_w_out, w_ffn_in=v_w_ffn_in,
             w_ffn_out=v_w_ffn_out, norm_mix_pre=v_norm_mix_pre, norm_mix_post=v_norm_mix_post, norm_ffn_pre=v_norm_ffn_pre,
             norm_ffn_post=v_norm_ffn_post)
    xi, yi, ci = _my_place()
    dev = _flat(xi, yi, ci)

    full = unpack_big_full(all_gather(pack_big(w, BF16), name="gather_big"))
    small_shard = _pack_rows([w[n] for n in SMALL_SHARDED], 40)
    small_all = all_gather(small_shard, name="gather_small")
    small_full = {}
    o = 0
    for n in SMALL_SHARDED:
        rows = int(np.prod(SMALL_SHARD_SHAPE[n])) // CH
        small_full[n] = _gather_small_full(small_all[:, o:o + rows].reshape((N_DEV,) + SMALL_SHARD_SHAPE[n]), n)
        o += rows

    layers = []
    for l in range(DEPTH):
        p = {n: full[n][l] for n in BIG}
        p["conv_a"] = small_full["conv_a"][l]
        p["ssd_conv_w"] = small_full["ssd_conv_w"][l]
        for n in SMALL_REPL:
            p[n] = w[n][l]
        layers.append(p)

    loss_blk, grad_x, gmeta, grads = local_step(x[0], loss_target[0], small_full["meta"], layers)

    gfull = {n: jnp.stack([grads[l][n] for l in range(DEPTH)]) for n in BIG}
    recv = all_to_all(pack_big_full(gfull, BF16), name="exchange_big")
    g_flat, d_flat, nm_flat, nv_flat = adamw_big(recv, pack_big(w, F32), pack_big(m, F32), pack_big(v, F32), name="adamw_big")
    out_g, out_d, out_m, out_v = unpack_big(g_flat), unpack_big(d_flat), unpack_big(nm_flat), unpack_big(nv_flat)

    small_names = SMALL_SHARDED + SMALL_REPL
    small_grads = [gmeta] + [jnp.stack([grads[l][n] for l in range(DEPTH)]) for n in small_names[1:]]
    small_shapes = [SMALL_FULL_SHAPE[n] for n in SMALL_SHARDED] + [SMALL_REPL_SHAPE[n] for n in SMALL_REPL]
    sm = _pack_rows(small_grads + [loss_blk[0:1]], 424)
    sm_sum = sum_slots(all_gather(sm, name="gather_small_grads"), name="sum_small_grads")
    summed = _unpack_rows(sm_sum, small_shapes + [(1, CH)])
    loss = summed[-1][0, 0]
    sg = dict(zip(small_names, summed[:-1]))
    for n in SMALL_SHARDED:
        width = SMALL_SHARD_SHAPE[n][-1]
        sg[n] = lax.dynamic_slice_in_dim(sg[n], dev * width, width, axis=sg[n].ndim - 1)
    pk = lambda d: _pack_rows([d[n] for n in small_names], 160)
    sd, snm, snv = adamw_small(pk(w), pk(sg), pk(m), pk(v), name="adamw_small")
    shard_shapes = [SMALL_SHARD_SHAPE[n] for n in SMALL_SHARDED] + [SMALL_REPL_SHAPE[n] for n in SMALL_REPL]
    for dst, flat in ((out_d, sd), (out_m, snm), (out_v, snv)):
        dst.update(zip(small_names, _unpack_rows(flat, shard_shapes)))
    out_g.update(sg)

    return (loss, grad_x[None], *[out_g[n] for n in WEIGHTS], *[out_d[n] for n in WEIGHTS], *[out_m[n] for n in WEIGHTS],
            *[out_v[n] for n in WEIGHTS])
```

```python
import functools
import math

import numpy as np
import jax
import jax.numpy as jnp
from jax import lax
from jax.experimental import pallas as pl
from jax.experimental.pallas import tpu as pltpu

F32, BF16 = jnp.float32, jnp.bfloat16
HI = lax.Precision.HIGHEST
MESH_ID = pl.DeviceIdType.MESH

D = 1024
CH = 128
N_META = 16
N_PAD = CH - N_META
EPS = 1e-6
N_DEV = 8
DEPTH = 2
SSD_HEADS = 16
RET_HEADS = 4
SB_HEADS = 8
D_FF = 2816
ROPE_BASE = 10000.0

NF = 14336
COL_GATE = 10

ADAM_LR, ADAM_B1, ADAM_B2, ADAM_EPS, ADAM_WD, ADAM_STEP = 0.001, 0.9, 0.999, 1e-08, 0.01, 10

VMEM_BYTES = 48 * 1024 * 1024


def _pick(n, cands):
    for c in cands:
        if n % c == 0:
            return c
    raise ValueError((n, cands))


def _tok_block(t):
    return _pick(t, (384, 128))


def _cparams(ngrid, vmem=VMEM_BYTES):
    return pltpu.CompilerParams(dimension_semantics=("arbitrary",) * ngrid, vmem_limit_bytes=vmem)


def _iota(shape, dim):
    return lax.broadcasted_iota(jnp.int32, shape, dim)


def _sigmoid(x):
    return 1.0 / (1.0 + jnp.exp(-x))


def _silu(x):
    return x * _sigmoid(x)


def _dsilu(x):
    s = _sigmoid(x)
    return s * (1.0 + x * (1.0 - s))


def _softplus(x):
    return jnp.maximum(x, 0.0) + jnp.log(1.0 + jnp.exp(-jnp.abs(x)))


def _dot(a, b):
    return jnp.dot(a.astype(BF16), b.astype(BF16), preferred_element_type=F32)


def _dot_nt(a, b):
    return lax.dot_general(a.astype(BF16), b.astype(BF16), (((1,), (1,)), ((), ())), preferred_element_type=F32)


def _dot_tn(a, b):
    return lax.dot_general(a.astype(BF16), b.astype(BF16), (((0,), (0,)), ((), ())), preferred_element_type=F32)


def _dot_hi(a, b):
    return jnp.dot(a, b, precision=HI, preferred_element_type=F32)


def mm(a, b, *, name, out_dtype=F32, add=None, tm=None, tn=None, tk=None):
    m, k = a.shape
    k2, n = b.shape
    assert k == k2
    tm = tm or _pick(m, (1376, 384, 128))
    tn = tn or _pick(n, (512, 384, 256, 128))
    tk = tk or _pick(k, (1024, 1408, 512, 384, 128))
    nk = k // tk
    has_add = add is not None

    def body(*refs):
        if has_add:
            a_ref, b_ref, c_ref, o_ref = refs[:4]
            scr = refs[4:]
        else:
            a_ref, b_ref, o_ref = refs[:3]
            c_ref = None
            scr = refs[3:]
        x = _dot(a_ref[...], b_ref[...])
        if nk == 1:
            if has_add:
                x = x + c_ref[...]
            o_ref[...] = x.astype(out_dtype)
        else:
            acc = scr[0]
            kk = pl.program_id(2)

            @pl.when(kk == 0)
            def _():
                acc[...] = x

            @pl.when(kk > 0)
            def _():
                acc[...] += x

            @pl.when(kk == nk - 1)
            def _():
                r = acc[...]
                if has_add:
                    r = r + c_ref[...]
                o_ref[...] = r.astype(out_dtype)

    in_specs = [pl.BlockSpec((tm, tk), lambda i, j, kk: (i, kk)), pl.BlockSpec((tk, tn), lambda i, j, kk: (kk, j))]
    args = [a, b]
    if has_add:
        in_specs.append(pl.BlockSpec((tm, tn), lambda i, j, kk: (i, j)))
        args.append(add)
    return pl.pallas_call(
        body, name=name, out_shape=jax.ShapeDtypeStruct((m, n), out_dtype), grid=(m // tm, n // tn, nk),
        in_specs=in_specs, out_specs=pl.BlockSpec((tm, tn), lambda i, j, kk: (i, j)),
        scratch_shapes=[pltpu.VMEM((tm, tn), F32)] if nk > 1 else [],
        compiler_params=_cparams(3))(*args)


def mm_tn(a, b, *, name, tm=None, tn=None, tk=None):
    t, m = a.shape
    t2, n = b.shape
    assert t == t2
    tm = tm or _pick(m, (1024, 1408, 512, 128))
    tn = tn or _pick(n, (512, 384, 256, 128))
    tk = tk or _pick(t, (1376, 384, 128))
    nk = t // tk

    def body(a_ref, b_ref, o_ref):
        x = _dot_tn(a_ref[...], b_ref[...])
        kk = pl.program_id(2)

        @pl.when(kk == 0)
        def _():
            o_ref[...] = x

        @pl.when(kk > 0)
        def _():
            o_ref[...] += x

    return pl.pallas_call(
        body, name=name, out_shape=jax.ShapeDtypeStruct((m, n), F32), grid=(m // tm, n // tn, nk),
        in_specs=[pl.BlockSpec((tk, tm), lambda i, j, kk: (kk, i)), pl.BlockSpec((tk, tn), lambda i, j, kk: (kk, j))],
        out_specs=pl.BlockSpec((tm, tn), lambda i, j, kk: (i, j)),
        compiler_params=_cparams(3))(a, b)


def rms_fwd(x, w, *, name, out_dtype=F32, res=None):
    t, d = x.shape
    tb = _tok_block(t)
    has_res = res is not None

    def body(*refs):
        if has_res:
            x_ref, w_ref, r_ref, o_ref = refs
        else:
            x_ref, w_ref, o_ref = refs
        xv = x_ref[...]
        y = xv * lax.rsqrt(jnp.mean(xv * xv, axis=-1, keepdims=True) + EPS) * w_ref[...]
        if has_res:
            y = y + r_ref[...]
        o_ref[...] = y.astype(out_dtype)

    blk = pl.BlockSpec((tb, d), lambda i: (i, 0))
    wspec = pl.BlockSpec((1, d), lambda i: (0, 0))
    in_specs = [blk, wspec] + ([blk] if has_res else [])
    args = [x, w] + ([res] if has_res else [])
    return pl.pallas_call(body, name=name, out_shape=jax.ShapeDtypeStruct((t, d), out_dtype), grid=(t // tb,),
                          in_specs=in_specs, out_specs=blk, compiler_params=_cparams(1))(*args)


def rms_bwd(x, w, dy, *, name, add=None):
    t, d = x.shape
    tb = _tok_block(t)
    has_add = add is not None

    def body(*refs):
        if has_add:
            x_ref, w_ref, dy_ref, a_ref, dx_ref, dw_ref = refs
        else:
            x_ref, w_ref, dy_ref, dx_ref, dw_ref = refs
        xv = x_ref[...]
        dyv = dy_ref[...]
        r = lax.rsqrt(jnp.mean(xv * xv, axis=-1, keepdims=True) + EPS)
        g = dyv * w_ref[...]
        dx = r * g - xv * (r * r * r) * jnp.mean(xv * g, axis=-1, keepdims=True)
        if has_add:
            dx = dx + a_ref[...]
        dx_ref[...] = dx
        part = jnp.sum(dyv * xv * r, axis=0, keepdims=True)

        @pl.when(pl.program_id(0) == 0)
        def _():
            dw_ref[...] = part

        @pl.when(pl.program_id(0) > 0)
        def _():
            dw_ref[...] += part

    blk = pl.BlockSpec((tb, d), lambda i: (i, 0))
    wspec = pl.BlockSpec((1, d), lambda i: (0, 0))
    in_specs = [blk, wspec, blk] + ([blk] if has_add else [])
    args = [x, w, dy] + ([add] if has_add else [])
    return pl.pallas_call(body, name=name,
                          out_shape=[jax.ShapeDtypeStruct((t, d), F32), jax.ShapeDtypeStruct((1, d), F32)],
                          grid=(t // tb,), in_specs=in_specs, out_specs=[blk, wspec], compiler_params=_cparams(1))(*args)


def _shift_down(cur, prev8, k):
    z = jnp.concatenate([prev8, cur], axis=0)
    return pltpu.roll(z, k, 0)[8:]


def _shift_up(cur, next8, k):
    n = cur.shape[0] + 8
    z = jnp.concatenate([cur, next8], axis=0)
    return pltpu.roll(z, n - k, 0)[:cur.shape[0]]


def _prev8_spec(tb, width, col):
    return pl.BlockSpec((8, width), lambda i: (jnp.maximum(i * (tb // 8) - 1, 0), col))


def _next8_spec(tb, width, col, t):
    return pl.BlockSpec((8, width), lambda i: (jnp.minimum((i + 1) * (tb // 8), t // 8 - 1), col))


def _row_valid(i, tb, n, offset=0):
    rows = i * tb + offset + _iota((n, 1), 0)
    return (rows >= N_PAD).astype(F32)


def conv_a_fwd(proj, w8, *, name):
    t = proj.shape[0]
    tb = _tok_block(t)

    def body(b_ref, c_ref, x_ref, cp_ref, xp_ref, w_ref, o_ref):
        i = pl.program_id(0)
        u = c_ref[...] * x_ref[...] * _row_valid(i, tb, tb)
        up = cp_ref[...] * xp_ref[...] * _row_valid(i, tb, 8, -8) * (i > 0).astype(F32)
        w = w_ref[...]
        conv = w[2:3] * u + w[1:2] * _shift_down(u, up, 1) + w[0:1] * _shift_down(u, up, 2)
        o_ref[...] = b_ref[...] * conv

    blk = lambda col: pl.BlockSpec((tb, D), lambda i: (i, col))
    return pl.pallas_call(
        body, name=name, out_shape=jax.ShapeDtypeStruct((t, D), F32), grid=(t // tb,),
        in_specs=[blk(0), blk(1), blk(2), _prev8_spec(tb, D, 1), _prev8_spec(tb, D, 2), pl.BlockSpec((8, D), lambda i: (0, 0))],
        out_specs=pl.BlockSpec((tb, D), lambda i: (i, 0)), compiler_params=_cparams(1))(proj, proj, proj, proj, proj, w8)


def conv_a_bwd(proj, w8, dy, *, name):
    t = proj.shape[0]
    tb = _tok_block(t)
    nblk = t // tb

    def body(b_ref, c_ref, x_ref, cp_ref, xp_ref, dy_ref, dyn_ref, bn_ref, w_ref, o_ref, dw_ref):
        i = pl.program_id(0)
        vm = _row_valid(i, tb, tb)
        cv, xv, bv, dyv = c_ref[...], x_ref[...], b_ref[...], dy_ref[...]
        u = cv * xv * vm
        up = cp_ref[...] * xp_ref[...] * _row_valid(i, tb, 8, -8) * (i > 0).astype(F32)
        w = w_ref[...]
        u1 = _shift_down(u, up, 1)
        u2 = _shift_down(u, up, 2)
        conv = w[2:3] * u + w[1:2] * u1 + w[0:1] * u2
        dconv = dyv * bv
        dconv_n = dyn_ref[...] * bn_ref[...] * (i < nblk - 1).astype(F32)
        du = w[2:3] * dconv + w[1:2] * _shift_up(dconv, dconv_n, 1) + w[0:1] * _shift_up(dconv, dconv_n, 2)
        o_ref[:, 0:D] = dyv * conv
        o_ref[:, D:2 * D] = du * xv * vm
        o_ref[:, 2 * D:3 * D] = du * cv * vm

        @pl.when(i == 0)
        def _():
            dw_ref[...] = jnp.zeros_like(dw_ref)

        dw_ref[0:1, :] += jnp.sum(dconv * u2, axis=0, keepdims=True)
        dw_ref[1:2, :] += jnp.sum(dconv * u1, axis=0, keepdims=True)
        dw_ref[2:3, :] += jnp.sum(dconv * u, axis=0, keepdims=True)

    blk = lambda col: pl.BlockSpec((tb, D), lambda i: (i, col))
    w8spec = pl.BlockSpec((8, D), lambda i: (0, 0))
    return pl.pallas_call(
        body, name=name,
        out_shape=[jax.ShapeDtypeStruct((t, 3 * D), F32), jax.ShapeDtypeStruct((8, D), F32)], grid=(nblk,),
        in_specs=[blk(0), blk(1), blk(2), _prev8_spec(tb, D, 1), _prev8_spec(tb, D, 2), blk(0),
                  _next8_spec(tb, D, 0, t), _next8_spec(tb, D, 0, t), w8spec],
        out_specs=[pl.BlockSpec((tb, 3 * D), lambda i: (i, 0)), w8spec],
        compiler_params=_cparams(1))(proj, proj, proj, proj, proj, dy, dy, proj, w8)


XBC_W = 2048


def ssd_conv_fwd(proj, w8, b, *, name):
    t = proj.shape[0]
    tb = _tok_block(t)

    def body(x_ref, xp_ref, w_ref, b_ref, o_ref):
        i = pl.program_id(0)
        xm = x_ref[...] * _row_valid(i, tb, tb)
        xmp = xp_ref[...] * _row_valid(i, tb, 8, -8) * (i > 0).astype(F32)
        w = w_ref[...]
        c = w[3:4] * xm + w[2:3] * _shift_down(xm, xmp, 1) + w[1:2] * _shift_down(xm, xmp, 2) + w[0:1] * _shift_down(xm, xmp, 3)
        o_ref[...] = _silu(c + b_ref[...])

    return pl.pallas_call(
        body, name=name, out_shape=jax.ShapeDtypeStruct((t, XBC_W), F32), grid=(t // tb,),
        in_specs=[pl.BlockSpec((tb, XBC_W), lambda i: (i, 2)), _prev8_spec(tb, XBC_W, 2),
                  pl.BlockSpec((8, XBC_W), lambda i: (0, 0)), pl.BlockSpec((1, XBC_W), lambda i: (0, 0))],
        out_specs=pl.BlockSpec((tb, XBC_W), lambda i: (i, 0)), compiler_params=_cparams(1))(proj, proj, w8, b)


def ssd_conv_bwd_pre(proj, w8, b, dxa, *, name):
    t = proj.shape[0]
    tb = _tok_block(t)

    def body(x_ref, xp_ref, w_ref, b_ref, d_ref, o_ref, dw_ref, db_ref):
        i = pl.program_id(0)
        xm = x_ref[...] * _row_valid(i, tb, tb)
        xmp = xp_ref[...] * _row_valid(i, tb, 8, -8) * (i > 0).astype(F32)
        w = w_ref[...]
        x1, x2, x3 = _shift_down(xm, xmp, 1), _shift_down(xm, xmp, 2), _shift_down(xm, xmp, 3)
        c = w[3:4] * xm + w[2:3] * x1 + w[1:2] * x2 + w[0:1] * x3 + b_ref[...]
        dpre = d_ref[...] * _dsilu(c)
        o_ref[...] = dpre

        @pl.when(i == 0)
        def _():
            dw_ref[...] = jnp.zeros_like(dw_ref)
            db_ref[...] = jnp.zeros_like(db_ref)

        dw_ref[0:1, :] += jnp.sum(dpre * x3, axis=0, keepdims=True)
        dw_ref[1:2, :] += jnp.sum(dpre * x2, axis=0, keepdims=True)
        dw_ref[2:3, :] += jnp.sum(dpre * x1, axis=0, keepdims=True)
        dw_ref[3:4, :] += jnp.sum(dpre * xm, axis=0, keepdims=True)
        db_ref[...] += jnp.sum(dpre, axis=0, keepdims=True)

    w8spec = pl.BlockSpec((8, XBC_W), lambda i: (0, 0))
    bspec = pl.BlockSpec((1, XBC_W), lambda i: (0, 0))
    return pl.pallas_call(
        body, name=name,
        out_shape=[jax.ShapeDtypeStruct((t, XBC_W), F32), jax.ShapeDtypeStruct((8, XBC_W), F32), jax.ShapeDtypeStruct((1, XBC_W), F32)],
        grid=(t // tb,),
        in_specs=[pl.BlockSpec((tb, XBC_W), lambda i: (i, 2)), _prev8_spec(tb, XBC_W, 2), w8spec, bspec,
                  pl.BlockSpec((tb, XBC_W), lambda i: (i, 0))],
        out_specs=[pl.BlockSpec((tb, XBC_W), lambda i: (i, 0)), w8spec, bspec],
        compiler_params=_cparams(1))(proj, proj, w8, b, dxa)


def ssd_conv_bwd_in(dpre, w8, *, name):
    t = dpre.shape[0]
    tb = _tok_block(t)
    nblk = t // tb

    def body(d_ref, dn_ref, w_ref, o_ref):
        i = pl.program_id(0)
        d = d_ref[...]
        dn = dn_ref[...] * (i < nblk - 1).astype(F32)
        w = w_ref[...]
        dx = w[3:4] * d + w[2:3] * _shift_up(d, dn, 1) + w[1:2] * _shift_up(d, dn, 2) + w[0:1] * _shift_up(d, dn, 3)
        o_ref[...] = dx * _row_valid(i, tb, tb)

    return pl.pallas_call(
        body, name=name, out_shape=jax.ShapeDtypeStruct((t, XBC_W), F32), grid=(nblk,),
        in_specs=[pl.BlockSpec((tb, XBC_W), lambda i: (i, 0)), _next8_spec(tb, XBC_W, 0, t), pl.BlockSpec((8, XBC_W), lambda i: (0, 0))],
        out_specs=pl.BlockSpec((tb, XBC_W), lambda i: (i, 0)), compiler_params=_cparams(1))(dpre, dpre, w8)


def _col(x, h):
    return jnp.sum(jnp.where(_iota(x.shape, 1) == h, x, 0.0), axis=1, keepdims=True)


def _row(x, h):
    return jnp.sum(jnp.where(_iota(x.shape, 0) == h, x, 0.0), axis=0, keepdims=True)


def _ssd_common(xa, dtr, dtb, alog, c):
    vm = _row_valid(c, CH, CH)
    xs = xa[:, :D] * vm
    dt = _softplus(dtr + dtb)
    a = -jnp.exp(alog) * dt
    tri = (_iota((CH, CH), 0) >= _iota((CH, CH), 1)).astype(F32)
    acs = _dot_hi(tri, a)
    return vm, xs, dt, a, acs, acs.T


def _pair_lanes(v0, v1):
    lane = _iota((1, CH), 1)
    return jnp.where(lane < 64, v0, v1)


def _ssd_pair_fwd(q, xs, xa, dt, acs, acs_t, dsk, hin, g_mat):
    g = q // 2
    h0, h1 = 2 * q, 2 * q + 1
    causal = _iota((CH, CH), 0) >= _iota((CH, CH), 1)
    bg = xa[:, D + CH * g:D + CH * (g + 1)]
    cg = xa[:, D + 512 + CH * g:D + 512 + CH * (g + 1)]
    xs_p = xs[:, CH * q:CH * (q + 1)]
    ac0, ac1 = _col(acs, h0), _col(acs, h1)
    ar0, ar1 = _row(acs_t, h0), _row(acs_t, h1)
    l0 = jnp.exp(jnp.where(causal, ac0 - ar0, -1e30))
    l1 = jnp.exp(jnp.where(causal, ac1 - ar1, -1e30))
    dt_p = _pair_lanes(_col(dt, h0), _col(dt, h1))
    x = xs_p * dt_p
    m0, m1 = g_mat * l0, g_mat * l1
    lane = _iota((CH, CH), 1)
    yd = jnp.where(lane < 64, _dot(m0, x), _dot(m1, x))
    ac_p = _pair_lanes(ac0, ac1)
    eac = jnp.exp(ac_p)
    yoff_raw = _dot_nt(cg, hin)
    al0 = jnp.sum(jnp.where(_iota((CH, 1), 0) == CH - 1, ac0, 0.0), axis=0, keepdims=True)
    al1 = jnp.sum(jnp.where(_iota((CH, 1), 0) == CH - 1, ac1, 0.0), axis=0, keepdims=True)
    dsv = jnp.exp(_pair_lanes(al0, al1) - ac_p)
    s = _dot_tn(x * dsv, bg)
    cd = jnp.where(_iota((CH, 1), 0) < 64, jnp.exp(al0), jnp.exp(al1))
    d_p = _pair_lanes(_col(dsk, h0), _col(dsk, h1))
    y = yd + yoff_raw * eac + xs_p * d_p
    return dict(bg=bg, cg=cg, xs_p=xs_p, l0=l0, l1=l1, m0=m0, m1=m1, dt_p=dt_p, x=x, eac=eac, yoff_raw=yoff_raw,
                dsv=dsv, s=s, cd=cd, d_p=d_p, y=y, al0=al0, al1=al1)


def _ssd_gate_norm(y, z, nw):
    yv = y * _silu(z)
    outs, rs = [], []
    for g in range(4):
        yg = yv[:, 256 * g:256 * (g + 1)]
        r = lax.rsqrt(jnp.mean(yg * yg, axis=-1, keepdims=True) + EPS)
        outs.append(yg * r * nw[:, 256 * g:256 * (g + 1)])
        rs.append(r)
    return yv, jnp.concatenate(outs, axis=1), rs


def ssd_fwd(xa, proj, pdt, dtb, alog, dsk, nw, *, name):
    t = xa.shape[0]
    nc = t // CH

    def body(xa_ref, dtr_ref, z_ref, dtb_ref, alog_ref, dsk_ref, nw_ref, y_ref, hs_ref, h_scr):
        c = pl.program_id(0)

        @pl.when(c == 0)
        def _():
            h_scr[...] = jnp.zeros_like(h_scr)

        xa_v = xa_ref[...]
        vm, xs, dt, a, acs, acs_t = _ssd_common(xa_v, dtr_ref[...], dtb_ref[...], alog_ref[...], c)
        dsk_v = dsk_ref[...]
        ys = []
        g_mat = None
        for q in range(8):
            if q % 2 == 0:
                g = q // 2
                g_mat = _dot_nt(xa_v[:, D + 512 + CH * g:D + 512 + CH * (g + 1)], xa_v[:, D + CH * g:D + CH * (g + 1)])
            hin = h_scr[q]
            hs_ref[0, q] = hin
            p = _ssd_pair_fwd(q, xs, xa_v, dt, acs, acs_t, dsk_v, hin, g_mat)
            h_scr[q] = hin * p["cd"] + p["s"]
            ys.append(p["y"])
        y = jnp.concatenate(ys, axis=1)
        _, out, _ = _ssd_gate_norm(y, z_ref[...], nw_ref[...])
        y_ref[...] = out

    small = pl.BlockSpec((1, CH), lambda c: (0, 0))
    return pl.pallas_call(
        body, name=name,
        out_shape=[jax.ShapeDtypeStruct((t, D), F32), jax.ShapeDtypeStruct((nc, 8, CH, CH), F32)], grid=(nc,),
        in_specs=[pl.BlockSpec((CH, XBC_W), lambda c: (c, 0)), pl.BlockSpec((CH, CH), lambda c: (c, 0)),
                  pl.BlockSpec((CH, D), lambda c: (c, 3)), small, small, small, pl.BlockSpec((1, D), lambda c: (0, 0))],
        out_specs=[pl.BlockSpec((CH, D), lambda c: (c, 0)), pl.BlockSpec((1, 8, CH, CH), lambda c: (c, 0, 0, 0))],
        scratch_shapes=[pltpu.VMEM((8, CH, CH), F32)], compiler_params=_cparams(1))(xa, pdt, proj, dtb, alog, dsk, nw)


def ssd_bwd(xa, proj, pdt, hs, dyb, dtb, alog, dsk, nw, *, name):
    t = xa.shape[0]
    nc = t // CH

    def body(xa_ref, dtr_ref, z_ref, hs_ref, dy_ref, dtb_ref, alog_ref, dsk_ref, nw_ref,
             dz_ref, dxa_ref, ddt_ref, gdtb_ref, galog_ref, gdsk_ref, gnw_ref, dh_scr):
        step = pl.program_id(0)
        c = nc - 1 - step

        @pl.when(step == 0)
        def _():
            dh_scr[...] = jnp.zeros_like(dh_scr)
            gdtb_ref[...] = jnp.zeros_like(gdtb_ref)
            galog_ref[...] = jnp.zeros_like(galog_ref)
            gdsk_ref[...] = jnp.zeros_like(gdsk_ref)
            gnw_ref[...] = jnp.zeros_like(gnw_ref)

        xa_v = xa_ref[...]
        dtr = dtr_ref[...]
        dtb_v = dtb_ref[...]
        alog_v = alog_ref[...]
        vm, xs, dt, a, acs, acs_t = _ssd_common(xa_v, dtr, dtb_v, alog_v, c)
        dsk_v = dsk_ref[...]
        z = z_ref[...]
        nw_v = nw_ref[...]
        lane1 = _iota((1, CH), 1)
        sub1 = _iota((CH, 1), 0)
        lane = _iota((CH, CH), 1)

        pairs = []
        g_mats = []
        for q in range(8):
            if q % 2 == 0:
                g = q // 2
                g_mats.append(_dot_nt(xa_v[:, D + 512 + CH * g:D + 512 + CH * (g + 1)], xa_v[:, D + CH * g:D + CH * (g + 1)]))
            pairs.append(_ssd_pair_fwd(q, xs, xa_v, dt, acs, acs_t, dsk_v, hs_ref[0, q], g_mats[q // 2]))
        y_pre = jnp.concatenate([p["y"] for p in pairs], axis=1)

        dout = dy_ref[...]
        sz = _silu(z)
        yv = y_pre * sz
        dyv_parts = []
        gnw_parts = []
        for g in range(4):
            sl = slice(256 * g, 256 * (g + 1))
            yg = yv[:, sl]
            r = lax.rsqrt(jnp.mean(yg * yg, axis=-1, keepdims=True) + EPS)
            gy = dout[:, sl] * nw_v[:, sl]
            dyv_parts.append(r * gy - yg * (r * r * r) * jnp.mean(yg * gy, axis=-1, keepdims=True))
            gnw_parts.append(jnp.sum(dout[:, sl] * yg * r, axis=0, keepdims=True))
        dyv = jnp.concatenate(dyv_parts, axis=1)
        gnw_ref[...] += jnp.concatenate(gnw_parts, axis=1)
        dz_ref[...] = dyv * y_pre * _dsilu(z)
        dy_pre = dyv * sz

        dacs_c = jnp.zeros((CH, CH), F32)
        dacs_r = jnp.zeros((CH, CH), F32)
        ddt = jnp.zeros((CH, CH), F32)
        gdsk = jnp.zeros((1, CH), F32)
        dxs_parts = []
        db_g = [None] * 4
        dc_g = [None] * 4
        dg_g = [None] * 4

        def acc(lst, g, v):
            lst[g] = v if lst[g] is None else lst[g] + v

        for q in range(8):
            p = pairs[q]
            g = q // 2
            h0, h1 = 2 * q, 2 * q + 1
            dy = dy_pre[:, CH * q:CH * (q + 1)]
            hin = hs_ref[0, q]
            dhout = dh_scr[q]
            x = p["x"]
            m_lo = lane < 64
            dxs = dy * p["d_p"]
            t_sk = dy * p["xs_p"]
            gdsk = gdsk + jnp.where(lane1 == h0, jnp.sum(jnp.where(m_lo, t_sk, 0.0)), 0.0) \
                        + jnp.where(lane1 == h1, jnp.sum(jnp.where(m_lo, 0.0, t_sk)), 0.0)
            dx = jnp.zeros((CH, CH), F32)
            for k, (hh, mk, lk, mm_) in enumerate(((h0, m_lo, p["l0"], p["m0"]), (h1, ~m_lo, p["l1"], p["m1"]))):
                dyk = jnp.where(mk, dy, 0.0)
                dm = _dot_nt(dyk, x)
                dx = dx + jnp.where(mk, _dot_tn(mm_, dy), 0.0)
                acc(dg_g, g, dm * lk)
                qm = dm * mm_
                dacs_c = dacs_c + jnp.where(lane1 == hh, jnp.sum(qm, axis=1, keepdims=True), 0.0)
                dacs_r = dacs_r - jnp.where(sub1 == hh, jnp.sum(qm, axis=0, keepdims=True), 0.0)
            dye = dy * p["eac"]
            acc(dc_g, g, _dot(dye, hin))
            t_off = dy * p["yoff_raw"] * p["eac"]
            dacs_c = dacs_c + jnp.where(lane1 == h0, jnp.sum(jnp.where(m_lo, t_off, 0.0), axis=1, keepdims=True), 0.0) \
                            + jnp.where(lane1 == h1, jnp.sum(jnp.where(m_lo, 0.0, t_off), axis=1, keepdims=True), 0.0)
            dhin = _dot_tn(dye, p["cg"]) + dhout * p["cd"]
            w1 = _dot_nt(p["bg"], dhout)
            dx = dx + p["dsv"] * w1
            t_ds = x * w1 * p["dsv"]
            dd0 = jnp.sum(jnp.where(m_lo, t_ds, 0.0), axis=1, keepdims=True)
            dd1 = jnp.sum(jnp.where(m_lo, 0.0, t_ds), axis=1, keepdims=True)
            acc(db_g, g, _dot(x * p["dsv"], dhout))
            t_cd = dhout * hin
            sub_lo = _iota((CH, CH), 0) < 64
            dcd0 = jnp.sum(jnp.where(sub_lo, t_cd, 0.0)) * jnp.exp(p["al0"])
            dcd1 = jnp.sum(jnp.where(sub_lo, 0.0, t_cd)) * jnp.exp(p["al1"])
            last = (sub1 == CH - 1)
            dacs_c = dacs_c + jnp.where(lane1 == h0, jnp.where(last, jnp.sum(dd0) + dcd0, 0.0) - dd0, 0.0) \
                            + jnp.where(lane1 == h1, jnp.where(last, jnp.sum(dd1) + dcd1, 0.0) - dd1, 0.0)
            dh_scr[q] = dhin
            dxs = dxs + dx * p["dt_p"]
            t_dt = dx * p["xs_p"]
            ddt = ddt + jnp.where(lane1 == h0, jnp.sum(jnp.where(m_lo, t_dt, 0.0), axis=1, keepdims=True), 0.0) \
                      + jnp.where(lane1 == h1, jnp.sum(jnp.where(m_lo, 0.0, t_dt), axis=1, keepdims=True), 0.0)
            dxs_parts.append(dxs)

        for g in range(4):
            bg, cg = pairs[2 * g]["bg"], pairs[2 * g]["cg"]
            dc_g[g] = dc_g[g] + _dot(dg_g[g], bg)
            db_g[g] = db_g[g] + _dot_tn(dg_g[g], cg)

        dacs = dacs_c + dacs_r.T
        rtri = (_iota((CH, CH), 1) >= _iota((CH, CH), 0)).astype(F32)
        da = _dot_hi(rtri, dacs)
        ddt = ddt - da * jnp.exp(alog_v)
        galog_ref[...] += jnp.sum(da * a, axis=0, keepdims=True)
        dpre = ddt * _sigmoid(dtr + dtb_v) * (lane1 < SSD_HEADS).astype(F32)
        ddt_ref[...] = dpre
        gdtb_ref[...] += jnp.sum(dpre, axis=0, keepdims=True)
        gdsk_ref[...] += gdsk
        dxa_ref[:, 0:D] = jnp.concatenate(dxs_parts, axis=1) * vm
        dxa_ref[:, D:D + 512] = jnp.concatenate(db_g, axis=1)
        dxa_ref[:, D + 512:D + 1024] = jnp.concatenate(dc_g, axis=1)

    small = pl.BlockSpec((1, CH), lambda s: (0, 0))
    wide = pl.BlockSpec((1, D), lambda s: (0, 0))
    rev = lambda s: nc - 1 - s
    return pl.pallas_call(
        body, name=name,
        out_shape=[jax.ShapeDtypeStruct((t, D), F32), jax.ShapeDtypeStruct((t, XBC_W), F32), jax.ShapeDtypeStruct((t, CH), F32),
                   jax.ShapeDtypeStruct((1, CH), F32), jax.ShapeDtypeStruct((1, CH), F32), jax.ShapeDtypeStruct((1, CH), F32),
                   jax.ShapeDtypeStruct((1, D), F32)],
        grid=(nc,),
        in_specs=[pl.BlockSpec((CH, XBC_W), lambda s: (rev(s), 0)), pl.BlockSpec((CH, CH), lambda s: (rev(s), 0)),
                  pl.BlockSpec((CH, D), lambda s: (rev(s), 3)), pl.BlockSpec((1, 8, CH, CH), lambda s: (rev(s), 0, 0, 0)),
                  pl.BlockSpec((CH, D), lambda s: (rev(s), 0)), small, small, small, wide],
        out_specs=[pl.BlockSpec((CH, D), lambda s: (rev(s), 0)), pl.BlockSpec((CH, XBC_W), lambda s: (rev(s), 0)),
                   pl.BlockSpec((CH, CH), lambda s: (rev(s), 0)), small, small, small, wide],
        scratch_shapes=[pltpu.VMEM((8, CH, CH), F32)], compiler_params=_cparams(1))(xa, pdt, proj, hs, dyb, dtb, alog, dsk, nw)


RET_DK = 256


def _log_gamma(h):
    return math.log(1.0 - 2.0 ** (-5.0 - h))


def _rope(x, cos, sin):
    x1, x2 = x[:, :128], x[:, 128:]
    return jnp.concatenate([x1 * cos - x2 * sin, x1 * sin + x2 * cos], axis=1)


def _unrope(d, cos, sin):
    d1, d2 = d[:, :128], d[:, 128:]
    return jnp.concatenate([d1 * cos + d2 * sin, d2 * cos - d1 * sin], axis=1)


def _ret_head_fwd(h, q, k, v, cos, sin, vm, r_in):
    lg = _log_gamma(h)
    sl = slice(RET_DK * h, RET_DK * (h + 1))
    qr = _rope(q[:, sl], cos, sin)
    kr = _rope(k[:, sl], cos, sin) * (RET_DK ** -0.5)
    vr = v[:, sl] * vm
    rel = (_iota((CH, CH), 0) - _iota((CH, CH), 1)).astype(F32)
    dmask = jnp.where(rel >= 0, jnp.exp(lg * jnp.maximum(rel, 0.0)), 0.0)
    idx = _iota((CH, 1), 0).astype(F32)
    kdec = jnp.exp(lg * (CH - 1 - idx))
    qdec = jnp.exp(lg * (idx + 1.0))
    scores = _dot_nt(qr, kr) * dmask
    y = _dot(scores, vr) + _dot(qr, r_in) * qdec
    kv = _dot_tn(kr * kdec, vr)
    return dict(qr=qr, kr=kr, vr=vr, dmask=dmask, kdec=kdec, qdec=qdec, scores=scores, y=y, kv=kv, cdec=math.exp(lg * CH))


def _group_norm(y):
    mu = jnp.mean(y, axis=-1, keepdims=True)
    yc = y - mu
    r = lax.rsqrt(jnp.mean(yc * yc, axis=-1, keepdims=True) + EPS)
    return yc * r, r


def ret_fwd(proj, cos, sin, *, name):
    t = proj.shape[0]
    nc = t // CH

    def body(q_ref, k_ref, v_ref, g_ref, cos_ref, sin_ref, y_ref, rs_ref, r_scr):
        c = pl.program_id(0)

        @pl.when(c == 0)
        def _():
            r_scr[...] = jnp.zeros_like(r_scr)

        vm = _row_valid(c, CH, CH)
        q, k, v, gt = q_ref[...], k_ref[...], v_ref[...], g_ref[...]
        cos, sin = cos_ref[...], sin_ref[...]
        for h in range(RET_HEADS):
            r_in = r_scr[h]
            rs_ref[0, h] = r_in
            p = _ret_head_fwd(h, q, k, v, cos, sin, vm, r_in)
            r_scr[h] = r_in * p["cdec"] + p["kv"]
            yn, _ = _group_norm(p["y"])
            sl = slice(RET_DK * h, RET_DK * (h + 1))
            y_ref[:, sl] = yn * _silu(gt[:, sl])

    blk = lambda col: pl.BlockSpec((CH, D), lambda c: (c, col))
    tab = pl.BlockSpec((CH, CH), lambda c: (c, 0))
    return pl.pallas_call(
        body, name=name,
        out_shape=[jax.ShapeDtypeStruct((t, D), F32), jax.ShapeDtypeStruct((nc, RET_HEADS, RET_DK, RET_DK), F32)], grid=(nc,),
        in_specs=[blk(6), blk(7), blk(8), blk(9), tab, tab],
        out_specs=[pl.BlockSpec((CH, D), lambda c: (c, 0)), pl.BlockSpec((1, RET_HEADS, RET_DK, RET_DK), lambda c: (c, 0, 0, 0))],
        scratch_shapes=[pltpu.VMEM((RET_HEADS, RET_DK, RET_DK), F32)], compiler_params=_cparams(1))(proj, proj, proj, proj, cos, sin)


def ret_bwd(proj, cos, sin, rs, dyc, *, name):
    t = proj.shape[0]
    nc = t // CH

    def body(q_ref, k_ref, v_ref, g_ref, cos_ref, sin_ref, rs_ref, dy_ref, o_ref, dr_scr):
        step = pl.program_id(0)
        c = nc - 1 - step

        @pl.when(step == 0)
        def _():
            dr_scr[...] = jnp.zeros_like(dr_scr)

        vm = _row_valid(c, CH, CH)
        q, k, v, gt = q_ref[...], k_ref[...], v_ref[...], g_ref[...]
        cos, sin = cos_ref[...], sin_ref[...]
        dout = dy_ref[...]
        for h in range(RET_HEADS):
            sl = slice(RET_DK * h, RET_DK * (h + 1))
            r_in = rs_ref[0, h]
            p = _ret_head_fwd(h, q, k, v, cos, sin, vm, r_in)
            yn, r = _group_norm(p["y"])
            gh = gt[:, sl]
            do = dout[:, sl]
            dg = do * yn * _dsilu(gh)
            dyn = do * _silu(gh)
            dy = r * (dyn - jnp.mean(dyn, axis=-1, keepdims=True) - yn * jnp.mean(dyn * yn, axis=-1, keepdims=True))
            dr_out = dr_scr[h]
            dyc_ = dy * p["qdec"]
            dqr = _dot_nt(dyc_, r_in)
            dr_scr[h] = dr_out * p["cdec"] + _dot_tn(p["qr"], dyc_)
            dkr = _dot_nt(p["vr"], dr_out) * p["kdec"]
            dv = _dot(p["kr"] * p["kdec"], dr_out)
            ds = _dot_nt(dy, p["vr"]) * p["dmask"]
            dqr = dqr + _dot(ds, p["kr"])
            dkr = dkr + _dot_tn(ds, p["qr"])
            dv = dv + _dot_tn(p["scores"], dy)
            o_ref[:, RET_DK * h:RET_DK * (h + 1)] = _unrope(dqr, cos, sin)
            o_ref[:, D + RET_DK * h:D + RET_DK * (h + 1)] = _unrope(dkr, cos, sin) * (RET_DK ** -0.5)
            o_ref[:, 2 * D + RET_DK * h:2 * D + RET_DK * (h + 1)] = dv * vm
            o_ref[:, 3 * D + RET_DK * h:3 * D + RET_DK * (h + 1)] = dg

    rev = lambda s: nc - 1 - s
    blk = lambda col: pl.BlockSpec((CH, D), lambda s: (rev(s), col))
    tab = pl.BlockSpec((CH, CH), lambda s: (rev(s), 0))
    return pl.pallas_call(
        body, name=name, out_shape=jax.ShapeDtypeStruct((t, 4 * D), F32), grid=(nc,),
        in_specs=[blk(6), blk(7), blk(8), blk(9), tab, tab,
                  pl.BlockSpec((1, RET_HEADS, RET_DK, RET_DK), lambda s: (rev(s), 0, 0, 0)), pl.BlockSpec((CH, D), lambda s: (rev(s), 0))],
        out_specs=pl.BlockSpec((CH, 4 * D), lambda s: (rev(s), 0)),
        scratch_shapes=[pltpu.VMEM((RET_HEADS, RET_DK, RET_DK), F32)], compiler_params=_cparams(1))(proj, proj, proj, proj, cos, sin, rs, dyc)


SB_D = 128
SB_SCALE = SB_D ** -0.5
SB_CUTOFF = 104.0


def _split_hi_lo(x):
    hi = x.astype(BF16)
    lo = (x - hi.astype(F32)).astype(BF16)
    return hi, lo


def _sum_matrix(kind):
    a, b = _iota((128, 128), 0), _iota((128, 128), 1)
    tri = ((b > a) if kind == "after" else (b < a)).astype(BF16)
    return jnp.concatenate([tri, tri], axis=1)


def _key_sums(x, mat2):
    hi, lo = _split_hi_lo(x)
    return jnp.dot(mat2, jnp.concatenate([hi, lo], axis=0), preferred_element_type=F32)


def _sb_mask(d_kq, key_idx, first_key, q_minus_k):
    return (d_kq < q_minus_k) & (key_idx >= N_PAD - first_key)


def _sb_log_sigmoid(z):
    return jnp.minimum(z, 0.0) - jnp.log(1.0 + jnp.exp(-jnp.abs(z)))


def sb_fwd(qkv, *, name):
    t = qkv.shape[0]
    tq = _tok_block(t)
    nq = t // tq
    per = tq // 128

    def body(q_ref, k_ref, v_ref, o_ref, at_ref, bmin_ref):
        h = pl.program_id(0)
        i = pl.program_id(1)
        top = (i + 1) * per - 1
        mat_after = _sum_matrix("after")
        qs = [q_ref[pl.ds(128 * r, 128), :] for r in range(per)]
        d_kq = _iota((128, 128), 0) - _iota((128, 128), 1)
        key_idx = _iota((128, 128), 0)

        def mask(r, b):
            return _sb_mask(d_kq, key_idx, b * 128, (i * per + r - b) * 128)

        def step(carry):
            b, _, a_runs, accs = carry
            off = pl.multiple_of(b * 128, 128)
            kb = k_ref[pl.ds(off, 128), :]
            vb = v_ref[pl.ds(off, 128), :]
            a_new, acc_new = [], []
            a_max = None
            for r in range(per):
                z = _dot_nt(kb, qs[r]) * SB_SCALE
                ls = _sb_log_sigmoid(z)
                m = mask(r, b)
                lneg = jnp.where(m, ls - z, 0.0)
                suf = _key_sums(lneg, mat_after)
                w = jnp.where(m, jnp.exp(ls + a_runs[r] + suf), 0.0)
                a_r = a_runs[r] + suf[0:1, :] + lneg[0:1, :]
                a_new.append(a_r)
                acc_new.append(accs[r] + _dot_tn(w, vb))
                top_r = jnp.max(a_r)
                a_max = top_r if a_max is None else jnp.maximum(a_max, top_r)
            return b - 1, a_max >= -SB_CUTOFF, tuple(a_new), tuple(acc_new)

        zeros = tuple(qs[r].astype(F32) * 0.0 for r in range(per))
        zrow = tuple(z[0:1, :] for z in zeros)
        b_end, _, a_runs, accs = lax.while_loop(lambda c: jnp.logical_and(c[0] >= 0, c[1]), step, (top, top >= 0, zrow, zeros))
        bmin_ref[h, i] = b_end + 1
        at_ref[...] = jnp.zeros_like(at_ref)
        for r in range(per):
            o_ref[pl.ds(128 * r, 128), :] = accs[r]
            at_ref[0, 0, r:r + 1, :] = a_runs[r]

    blk = pl.BlockSpec((tq, 128), lambda h, i: (i, h))
    return pl.pallas_call(
        body, name=name,
        out_shape=[jax.ShapeDtypeStruct((t, D), F32), jax.ShapeDtypeStruct((SB_HEADS, nq, 8, 128), F32),
                   jax.ShapeDtypeStruct((SB_HEADS, nq), jnp.int32)],
        grid=(SB_HEADS, nq),
        in_specs=[blk, pl.BlockSpec((t, 128), lambda h, i: (0, SB_HEADS + h)),
                  pl.BlockSpec((t, 128), lambda h, i: (0, 2 * SB_HEADS + h))],
        out_specs=[blk, pl.BlockSpec((1, 1, 8, 128), lambda h, i: (h, i, 0, 0)), pl.BlockSpec(memory_space=pltpu.SMEM)],
        compiler_params=_cparams(2))(qkv, qkv, qkv)


def sb_bwd(qkv, atot, bmin, dout, *, name):
    t = qkv.shape[0]
    tq = _tok_block(t)
    nq = t // tq
    per = tq // 128

    nblk = t // 128

    def body(bmin_ref, q_ref, k_ref, v_ref, at_ref, do_ref, dq_ref, dk_ref, dv_ref):
        i = pl.program_id(1)
        top = (i + 1) * per - 1
        b_first = bmin_ref[pl.program_id(0), i]

        @pl.when(i == 0)
        def _():
            dk_ref[...] = jnp.zeros_like(dk_ref)
            dv_ref[...] = jnp.zeros_like(dv_ref)

        qs = [q_ref[pl.ds(128 * r, 128), :] for r in range(per)]
        dos = [do_ref[pl.ds(128 * r, 128), :].astype(BF16) for r in range(per)]
        q_all = q_ref[...]
        do_all = do_ref[...].astype(BF16)
        a_tots = [at_ref[0, 0, r:r + 1, :] for r in range(per)]
        mat_after = _sum_matrix("after")
        mat_before = _sum_matrix("before")
        d_kq = _iota((128, 128), 0) - _iota((128, 128), 1)
        key_idx = _iota((128, 128), 0)

        def offset(b):
            return pl.multiple_of(jnp.clip(b, 0, nblk - 1) * 128, 128)

        def mask(r, b):
            return _sb_mask(d_kq, key_idx, b * 128, (i * per + r - b) * 128)

        def step(b, carry):
            lss, dws, sufs, tots, p_runs, e_runs, dqs = carry
            off = offset(b)
            kb = k_ref[pl.ds(off, 128), :]
            vb = v_ref[pl.ds(off, 128), :]
            zs = [_dot_nt(kb, qs[r]) * SB_SCALE for r in range(per)]
            dw_new = tuple(_dot_nt(vb, dos[r]) for r in range(per))
            off1 = offset(b - 1)
            k1 = k_ref[pl.ds(off1, 128), :]
            ws, es, epres, sigs, p_new = [], [], [], [], []
            for r in range(per):
                p = p_runs[r] + tots[r]
                w = jnp.where(mask(r, b - 1), jnp.exp(lss[r] + (a_tots[r] - p) + sufs[r]), 0.0)
                e = w * dws[r]
                p_new.append(p)
                ws.append(w.astype(BF16))
                es.append(e)
                epres.append(_key_sums(e, mat_before))
                sigs.append(jnp.exp(lss[r]))
            dv_ref[0, pl.ds(off1, 128), :] += _dot(jnp.concatenate(ws, axis=1), do_all)
            ls_new, suf_new, tot_new = [], [], []
            for r in range(per):
                ls = _sb_log_sigmoid(zs[r])
                lneg = jnp.where(mask(r, b), ls - zs[r], 0.0)
                suf = _key_sums(lneg, mat_after)
                ls_new.append(ls)
                suf_new.append(suf)
                tot_new.append(suf[0:1, :] + lneg[0:1, :])
            dzs, e_new, dq_new = [], [], []
            for r in range(per):
                t2 = jnp.where(mask(r, b - 1), (e_runs[r] + epres[r]) * sigs[r], 0.0)
                dz = ((es[r] * (1.0 - sigs[r]) - t2) * SB_SCALE).astype(BF16)
                e_new.append(e_runs[r] + epres[r][127:128, :] + es[r][127:128, :])
                dq_new.append(dqs[r] + _dot_tn(dz, k1))
                dzs.append(dz)
            dk_ref[0, pl.ds(off1, 128), :] += _dot(jnp.concatenate(dzs, axis=1), q_all)
            return tuple(ls_new), dw_new, tuple(suf_new), tuple(tot_new), tuple(p_new), tuple(e_new), tuple(dq_new)

        zeros = tuple(qs[r].astype(F32) * 0.0 for r in range(per))
        zrow = tuple(z[0:1, :] for z in zeros)
        carry = lax.fori_loop(b_first, top + 2, step, (zeros, zeros, zeros, zrow, zrow, zrow, zeros))
        for r in range(per):
            dq_ref[pl.ds(128 * r, 128), :] = carry[6][r]

    head_blk = pl.BlockSpec((1, t, 128), lambda h, i: (h, 0, 0))
    return pl.pallas_call(
        body, name=name,
        out_shape=[jax.ShapeDtypeStruct((t, D), F32), jax.ShapeDtypeStruct((SB_HEADS, t, 128), F32),
                   jax.ShapeDtypeStruct((SB_HEADS, t, 128), F32)],
        grid=(SB_HEADS, nq),
        in_specs=[pl.BlockSpec(memory_space=pltpu.SMEM),
                  pl.BlockSpec((tq, 128), lambda h, i: (i, h)), pl.BlockSpec((t, 128), lambda h, i: (0, SB_HEADS + h)),
                  pl.BlockSpec((t, 128), lambda h, i: (0, 2 * SB_HEADS + h)),
                  pl.BlockSpec((1, 1, 8, 128), lambda h, i: (h, i, 0, 0)), pl.BlockSpec((tq, 128), lambda h, i: (i, h))],
        out_specs=[pl.BlockSpec((tq, 128), lambda h, i: (i, h)), head_blk, head_blk],
        compiler_params=_cparams(2, 60 * 1024 * 1024))(bmin, qkv, qkv, qkv, atot, dout)


def branch_fwd(y, proj, w, n, *, name, add=None):
    t = y.shape[0]
    tb = _tok_block(t)
    has_add = add is not None

    def body(*refs):
        if has_add:
            y_ref, g_ref, w_ref, a_ref, o_ref = refs
        else:
            y_ref, g_ref, w_ref, o_ref = refs
        r = _sigmoid(g_ref[...]) * _dot(y_ref[...], w_ref[...])
        if has_add:
            r = r + a_ref[...]
        o_ref[...] = r

    blk = pl.BlockSpec((tb, D), lambda i: (i, 0))
    in_specs = [blk, pl.BlockSpec((tb, D), lambda i: (i, COL_GATE + n)), pl.BlockSpec((D, D), lambda i: (0, 0))] + ([blk] if has_add else [])
    args = [y, proj, w] + ([add] if has_add else [])
    return pl.pallas_call(body, name=name, out_shape=jax.ShapeDtypeStruct((t, D), F32), grid=(t // tb,),
                          in_specs=in_specs, out_specs=blk, compiler_params=_cparams(1))(*args)


def branch_bwd(y, proj, w, wt, n, dmerged, *, name):
    t = y.shape[0]
    tb = _tok_block(t)

    def body(y_ref, g_ref, w_ref, wt_ref, dm_ref, dg_ref, dup_ref, dy_ref):
        up = _dot(y_ref[...], w_ref[...])
        gate = _sigmoid(g_ref[...])
        dm = dm_ref[...]
        dg_ref[...] = dm * up * gate * (1.0 - gate)
        dup = (dm * gate).astype(BF16)
        dup_ref[...] = dup
        dy_ref[...] = _dot(dup, wt_ref[...])

    blk = pl.BlockSpec((tb, D), lambda i: (i, 0))
    wspec = pl.BlockSpec((D, D), lambda i: (0, 0))
    return pl.pallas_call(
        body, name=name,
        out_shape=[jax.ShapeDtypeStruct((t, D), F32), jax.ShapeDtypeStruct((t, D), BF16), jax.ShapeDtypeStruct((t, D), F32)],
        grid=(t // tb,),
        in_specs=[blk, pl.BlockSpec((tb, D), lambda i: (i, COL_GATE + n)), wspec, wspec, blk],
        out_specs=[blk, blk, blk], compiler_params=_cparams(1))(y, proj, w, wt, dmerged)


def swiglu_fwd(f, *, name):
    t = f.shape[0]
    tb = _tok_block(t)

    def body(g_ref, u_ref, o_ref):
        o_ref[...] = (_silu(g_ref[...]) * u_ref[...]).astype(BF16)

    return pl.pallas_call(body, name=name, out_shape=jax.ShapeDtypeStruct((t, D_FF), BF16), grid=(t // tb,),
                          in_specs=[pl.BlockSpec((tb, D_FF), lambda i: (i, 0)), pl.BlockSpec((tb, D_FF), lambda i: (i, 1))],
                          out_specs=pl.BlockSpec((tb, D_FF), lambda i: (i, 0)), compiler_params=_cparams(1))(f, f)


def swiglu_bwd(f, dact, *, name):
    t = f.shape[0]
    tb = _tok_block(t)

    def body(g_ref, u_ref, d_ref, o_ref):
        g, u, d = g_ref[...], u_ref[...], d_ref[...]
        o_ref[:, 0:D_FF] = d * u * _dsilu(g)
        o_ref[:, D_FF:2 * D_FF] = d * _silu(g)

    return pl.pallas_call(body, name=name, out_shape=jax.ShapeDtypeStruct((t, 2 * D_FF), F32), grid=(t // tb,),
                          in_specs=[pl.BlockSpec((tb, D_FF), lambda i: (i, 0)), pl.BlockSpec((tb, D_FF), lambda i: (i, 1)),
                                    pl.BlockSpec((tb, D_FF), lambda i: (i, 0))],
                          out_specs=pl.BlockSpec((tb, 2 * D_FF), lambda i: (i, 0)), compiler_params=_cparams(1))(f, f, dact)


def loss_head(h, target, *, name):
    t = h.shape[0]
    nb = t // CH

    def body(h_ref, t_ref, l_ref, d_ref):
        i = pl.program_id(0)

        @pl.when(i == 0)
        def _():
            l_ref[...] = jnp.zeros_like(l_ref)
            d_ref[...] = jnp.zeros_like(d_ref)

        @pl.when(i > 0)
        def _():
            err = h_ref[...] - t_ref[...]
            d_ref[...] = err * (1.0 / D)
            l_ref[...] += jnp.sum(err * err) * (0.5 / D)

    return pl.pallas_call(
        body, name=name, out_shape=[jax.ShapeDtypeStruct((8, 128), F32), jax.ShapeDtypeStruct((t, D), F32)], grid=(nb,),
        in_specs=[pl.BlockSpec((CH, D), lambda i: (i, 0)), pl.BlockSpec((CH, D), lambda i: (jnp.maximum(i - 1, 0), 0))],
        out_specs=[pl.BlockSpec((8, 128), lambda i: (0, 0)), pl.BlockSpec((CH, D), lambda i: (i, 0))],
        compiler_params=_cparams(1))(h, target)


def _pad_rows8(w):
    return jnp.concatenate([w, jnp.zeros((8 - w.shape[0], w.shape[1]), w.dtype)], axis=0)


def _pad_lanes(v, n=CH):
    return jnp.concatenate([v, jnp.zeros((n - v.shape[0],), v.dtype)])[None, :]


def prep_layer(p):
    w = p["w_in"]
    zeros = jnp.zeros((D, CH - SSD_HEADS), w.dtype)
    w_f = jnp.concatenate([w[:, :6144], w[:, 6160:10256], w[:, 13328:17424]], axis=1)
    w_dt = jnp.concatenate([w[:, 6144:6160], zeros], axis=1)
    w_sb = w[:, 10256:13328]
    return dict(
        w_f=w_f, w_sb=w_sb, w_dt=w_dt, w_f_t=w_f.T, w_sb_t=w_sb.T, w_dt_t=w_dt.T,
        w_br=p["w_branch"], w_br_t=jnp.swapaxes(p["w_branch"], 1, 2), w_out=p["w_out"], w_out_t=p["w_out"].T,
        w_fi=p["w_ffn_in"], w_fi_t=p["w_ffn_in"].T, w_fo=p["w_ffn_out"], w_fo_t=p["w_ffn_out"].T,
        conv_a8=_pad_rows8(p["conv_a"]), conv_s8=_pad_rows8(p["ssd_conv_w"]), conv_sb=p["ssd_conv_b"][None, :],
        dtb=_pad_lanes(p["ssd_dt_bias"]), alog=_pad_lanes(p["ssd_a_log"]), dsk=_pad_lanes(p["ssd_d"]), nw=p["ssd_norm"][None, :],
        n1=p["norm_mix_pre"][None, :], n2=p["norm_mix_post"][None, :], n3=p["norm_ffn_pre"][None, :], n4=p["norm_ffn_post"][None, :])


def layer_fwd(h0, w, cos, sin, l):
    nm = lambda s: f"l{l}_{s}"
    hn = rms_fwd(h0, w["n1"], name=nm("rms1"), out_dtype=BF16)
    proj = mm(hn, w["w_f"], name=nm("proj_f"))
    qkv = mm(hn, w["w_sb"], name=nm("proj_sb"), out_dtype=BF16)
    pdt = mm(hn, w["w_dt"], name=nm("proj_dt"))
    y_a = conv_a_fwd(proj, w["conv_a8"], name=nm("conv_a"))
    xa = ssd_conv_fwd(proj, w["conv_s8"], w["conv_sb"], name=nm("ssd_conv"))
    y_b, hs = ssd_fwd(xa, proj, pdt, w["dtb"], w["alog"], w["dsk"], w["nw"], name=nm("ssd"))
    y_c, rs = ret_fwd(proj, cos, sin, name=nm("ret"))
    y_d, sb_atot, sb_bmin = sb_fwd(qkv, name=nm("sb"))
    ys = (y_a, y_b, y_c, y_d)
    merged = None
    for n in range(4):
        merged = branch_fwd(ys[n], proj, w["w_br"][n], n, name=nm(f"branch{n}"), add=merged)
    mix = mm(merged, w["w_out"], name=nm("mix"))
    h1 = rms_fwd(mix, w["n2"], name=nm("rms2"), res=h0)
    hn2 = rms_fwd(h1, w["n3"], name=nm("rms3"), out_dtype=BF16)
    f = mm(hn2, w["w_fi"], name=nm("ffn_in"))
    act = swiglu_fwd(f, name=nm("swiglu"))
    f2 = mm(act, w["w_fo"], name=nm("ffn_out"))
    h2 = rms_fwd(f2, w["n4"], name=nm("rms4"), res=h1)
    saved = dict(h0=h0, hn=hn, proj=proj, qkv=qkv, pdt=pdt, xa=xa, hs=hs, rs=rs, ys=ys, sb_atot=sb_atot, sb_bmin=sb_bmin, merged=merged, mix=mix, h1=h1, hn2=hn2,
                 f=f, act=act, f2=f2)
    return h2, saved


def layer_bwd(dh2, s, w, cos, sin, l):
    nm = lambda t: f"l{l}_{t}"
    g = {}
    df2, g["n4"] = rms_bwd(s["f2"], w["n4"], dh2, name=nm("rms4_b"))
    g["w_fo"] = mm_tn(s["act"], df2, name=nm("ffn_out_dw"))
    dact = mm(df2, w["w_fo_t"], name=nm("ffn_out_dx"), out_dtype=BF16)
    df = swiglu_bwd(s["f"], dact, name=nm("swiglu_b"))
    g["w_fi"] = mm_tn(s["hn2"], df, name=nm("ffn_in_dw"))
    dhn2 = mm(df, w["w_fi_t"], name=nm("ffn_in_dx"))
    dh1, g["n3"] = rms_bwd(s["h1"], w["n3"], dhn2, name=nm("rms3_b"), add=dh2)
    dmix, g["n2"] = rms_bwd(s["mix"], w["n2"], dh1, name=nm("rms2_b"))
    g["w_out"] = mm_tn(s["merged"], dmix, name=nm("mix_dw"))
    dmerged = mm(dmix, w["w_out_t"], name=nm("mix_dx"))
    dgate, dys, dwb = [], [], []
    for n in range(4):
        dg_n, dup_n, dy_n = branch_bwd(s["ys"][n], s["proj"], w["w_br"][n], w["w_br_t"][n], n, dmerged, name=nm(f"branch{n}_b"))
        dgate.append(dg_n)
        dys.append(dy_n)
        dwb.append(mm_tn(s["ys"][n], dup_n, name=nm(f"branch{n}_dw")))
    g["w_br"] = jnp.stack(dwb)
    d_a, g["conv_a8"] = conv_a_bwd(s["proj"], w["conv_a8"], dys[0], name=nm("conv_a_b"))
    dz, dxa, ddt, g["dtb"], g["alog"], g["dsk"], g["nw"] = ssd_bwd(
        s["xa"], s["proj"], s["pdt"], s["hs"], dys[1], w["dtb"], w["alog"], w["dsk"], w["nw"], name=nm("ssd_b"))
    dpre, g["conv_s8"], g["conv_sb"] = ssd_conv_bwd_pre(s["proj"], w["conv_s8"], w["conv_sb"], dxa, name=nm("ssd_conv_b1"))
    dxbc = ssd_conv_bwd_in(dpre, w["conv_s8"], name=nm("ssd_conv_b2"))
    d_r = ret_bwd(s["proj"], cos, sin, s["rs"], dys[2], name=nm("ret_b"))
    dq, dk, dv = sb_bwd(s["qkv"], s["sb_atot"], s["sb_bmin"], dys[3], name=nm("sb_b"))
    t = dq.shape[0]
    d_sb = jnp.concatenate([dq, jnp.swapaxes(dk, 0, 1).reshape(t, D), jnp.swapaxes(dv, 0, 1).reshape(t, D)], axis=1)
    segs = [(d_a, 0), (dz, 3072), (dxbc, 4096), (d_r, 6144), (dgate[0], 10240), (dgate[1], 11264), (dgate[2], 12288),
            (dgate[3], 13312)]
    dws = [mm_tn(s["hn"], d, name=nm(f"proj_dw{k}")) for k, (d, _) in enumerate(segs)]
    g["w_f"] = jnp.concatenate(dws, axis=1)
    g["w_sb"] = mm_tn(s["hn"], d_sb, name=nm("proj_dw_sb"))
    g["w_dt"] = mm_tn(s["hn"], ddt, name=nm("proj_dw_dt"))
    dhn = mm(d_sb, w["w_sb_t"], name=nm("proj_dx_sb"))
    dhn = mm(ddt, w["w_dt_t"], name=nm("proj_dx_dt"), add=dhn)
    for k, (d, c0) in enumerate(segs):
        dhn = mm(d, w["w_f_t"][c0:c0 + d.shape[1]], name=nm(f"proj_dx{k}"), add=dhn)
    dh0, g["n1"] = rms_bwd(s["h0"], w["n1"], dhn, name=nm("rms1_b"), add=dh1)
    return dh0, g


def layer_grads_to_params(g):
    wf, wsb = g["w_f"], g["w_sb"]
    w_in = jnp.concatenate([wf[:, :6144], g["w_dt"][:, :SSD_HEADS], wf[:, 6144:10240], wsb, wf[:, 10240:14336]], axis=1)
    return dict(
        w_in=w_in, conv_a=g["conv_a8"][:3], ssd_conv_w=g["conv_s8"][:4], ssd_conv_b=g["conv_sb"][0],
        ssd_dt_bias=g["dtb"][0, :SSD_HEADS], ssd_a_log=g["alog"][0, :SSD_HEADS], ssd_d=g["dsk"][0, :SSD_HEADS], ssd_norm=g["nw"][0],
        w_branch=g["w_br"], w_out=g["w_out"], w_ffn_in=g["w_fi"], w_ffn_out=g["w_fo"],
        norm_mix_pre=g["n1"][0], norm_mix_post=g["n2"][0], norm_ffn_pre=g["n3"][0], norm_ffn_post=g["n4"][0])


def rope_tables(t):
    half = RET_DK // 2
    inv = ROPE_BASE ** (-jnp.arange(half, dtype=F32) / half)
    ang = jnp.arange(t).astype(F32)[:, None] * inv[None, :]
    return jnp.cos(ang), jnp.sin(ang)


def local_step(x, target, meta, layers):
    h = jnp.concatenate([jnp.zeros((N_PAD, D), F32), meta, x], axis=0)
    t = h.shape[0]
    cos, sin = rope_tables(t)
    ws = [prep_layer(p) for p in layers]
    saved = []
    for l, w in enumerate(ws):
        h, s = layer_fwd(h, w, cos, sin, l)
        saved.append(s)
    loss, dh = loss_head(h, target, name="loss_head")
    grads = [None] * len(ws)
    for l in reversed(range(len(ws))):
        dh, g = layer_bwd(dh, saved[l], ws[l], cos, sin, l)
        grads[l] = layer_grads_to_params(g)
    return loss, dh[CH:], dh[N_PAD:CH], grads


def _my_place():
    return lax.axis_index("x"), lax.axis_index("y"), lax.axis_index("c")


def _flat(px, py, pc):
    return 4 * px + 2 * py + pc


ANY = pl.BlockSpec(memory_space=pl.ANY)


def all_gather(x_shard, *, name):
    shape = x_shard.shape

    def body(x_ref, out_ref, send_sems, recv_sems, local_sem):
        x, y, c = _my_place()
        me, sibling = (x, y, c), (x, y, 1 - c)
        chips = [(1 - x, y), (x, 1 - y), (1 - x, 1 - y)]

        def rows(px, py, pc):
            return out_ref.at[_flat(px, py, pc)]

        def copy(k, block, to, src=None):
            return pltpu.make_async_remote_copy(
                src_ref=rows(*block) if src is None else src, dst_ref=rows(*block),
                send_sem=send_sems.at[k], recv_sem=recv_sems.at[k], device_id=to, device_id_type=MESH_ID)

        mine = pltpu.make_async_copy(x_ref, rows(*me), local_sem)
        mine.start()
        first = [copy(0, me, sibling, src=x_ref)]
        first += [copy(1 + j, me, (*chip, c), src=x_ref) for j, chip in enumerate(chips)]
        for cp in first:
            cp.start()
        passed = [copy(4 + j, (*chip, c), sibling) for j, chip in enumerate(chips)]
        for j, chip in enumerate(chips):
            copy(1 + j, (*chip, c), me).wait_recv()
            passed[j].start()
        copy(0, sibling, me).wait_recv()
        for j, chip in enumerate(chips):
            copy(4 + j, (*chip, 1 - c), me).wait_recv()
        for cp in first + passed:
            cp.wait_send()
        mine.wait()

    return pl.pallas_call(
        body, name=name, out_shape=jax.ShapeDtypeStruct((N_DEV,) + shape, x_shard.dtype),
        in_specs=[ANY], out_specs=ANY,
        scratch_shapes=[pltpu.SemaphoreType.DMA((7,)), pltpu.SemaphoreType.DMA((7,)), pltpu.SemaphoreType.DMA],
    )(x_shard)


def all_to_all(g, *, name):
    def body(g_ref, out_ref, send_sems, recv_sems, local_sem):
        x, y, c = _my_place()
        me = _flat(x, y, c)
        mine = pltpu.make_async_copy(g_ref.at[me], out_ref.at[me], local_sem)
        mine.start()
        peers = []
        for k in range(1, N_DEV):
            px = jnp.bitwise_xor(x, (k >> 2) & 1)
            py = jnp.bitwise_xor(y, (k >> 1) & 1)
            pc = jnp.bitwise_xor(c, k & 1)
            peers.append((px, py, pc))
        sends = []
        for k, peer in enumerate(peers):
            cp = pltpu.make_async_remote_copy(
                src_ref=g_ref.at[_flat(*peer)], dst_ref=out_ref.at[me],
                send_sem=send_sems.at[k], recv_sem=recv_sems.at[k], device_id=peer, device_id_type=MESH_ID)
            cp.start()
            sends.append(cp)
        for k, peer in enumerate(peers):
            slot = out_ref.at[_flat(*peer)]
            pltpu.make_async_remote_copy(
                src_ref=slot, dst_ref=slot, send_sem=send_sems.at[k], recv_sem=recv_sems.at[k],
                device_id=peer, device_id_type=MESH_ID).wait_recv()
        for cp in sends:
            cp.wait_send()
        mine.wait()

    return pl.pallas_call(
        body, name=name, out_shape=jax.ShapeDtypeStruct(g.shape, g.dtype), in_specs=[ANY], out_specs=ANY,
        scratch_shapes=[pltpu.SemaphoreType.DMA((7,)), pltpu.SemaphoreType.DMA((7,)), pltpu.SemaphoreType.DMA],
    )(g)


def sum_slots(a, *, name):
    def body(a_ref, o_ref):
        s = a_ref[0]
        for d in range(1, N_DEV):
            s = s + a_ref[d]
        o_ref[...] = s

    return pl.pallas_call(body, name=name, out_shape=jax.ShapeDtypeStruct(a.shape[1:], a.dtype))(a)


def _adamw_math(w, g, m, v):
    m = ADAM_B1 * m + (1.0 - ADAM_B1) * g
    v = ADAM_B2 * v + (1.0 - ADAM_B2) * (g * g)
    m_hat = m / (1.0 - ADAM_B1 ** ADAM_STEP)
    v_hat = v / (1.0 - ADAM_B2 ** ADAM_STEP)
    delta = -ADAM_LR * (m_hat / (jnp.sqrt(v_hat) + ADAM_EPS) + ADAM_WD * w)
    return delta, m, v


def adamw_big(recv, w, m, v, *, name):
    r, cols = w.shape
    tb = 128

    def body(r_ref, w_ref, m_ref, v_ref, g_ref, d_ref, nm_ref, nv_ref):
        g = r_ref[0].astype(F32)
        for d in range(1, N_DEV):
            g = g + r_ref[d].astype(F32)
        g_ref[...] = g
        d_ref[...], nm_ref[...], nv_ref[...] = _adamw_math(w_ref[...], g, m_ref[...], v_ref[...])

    blk = pl.BlockSpec((tb, cols), lambda i: (i, 0))
    out = jax.ShapeDtypeStruct((r, cols), F32)
    return pl.pallas_call(
        body, name=name, out_shape=[out] * 4, grid=(r // tb,),
        in_specs=[pl.BlockSpec((N_DEV, tb, cols), lambda i: (0, i, 0)), blk, blk, blk], out_specs=[blk] * 4,
        compiler_params=_cparams(1))(recv, w, m, v)


def adamw_small(w, g, m, v, *, name):
    def body(w_ref, g_ref, m_ref, v_ref, d_ref, nm_ref, nv_ref):
        d_ref[...], nm_ref[...], nv_ref[...] = _adamw_math(w_ref[...], g_ref[...], m_ref[...], v_ref[...])

    out = jax.ShapeDtypeStruct(w.shape, F32)
    return pl.pallas_call(body, name=name, out_shape=[out] * 3)(w, g, m, v)


BIG = ("w_in", "w_branch", "w_out", "w_ffn_in", "w_ffn_out")
BIG_SHARD = {"w_in": (DEPTH, D, 2178), "w_branch": (DEPTH, 4, 128, D), "w_out": (DEPTH, 128, D),
             "w_ffn_in": (DEPTH, D, 704), "w_ffn_out": (DEPTH, 352, D)}
BIG_FULL = {"w_in": ((1, 2, 0, 3), (DEPTH, D, 17424)), "w_branch": ((1, 2, 0, 3, 4), (DEPTH, 4, D, D)),
            "w_out": ((1, 0, 2, 3), (DEPTH, D, D)), "w_ffn_in": ((1, 2, 0, 3), (DEPTH, D, 2 * D_FF)),
            "w_ffn_out": ((1, 0, 2, 3), (DEPTH, D_FF, D))}
BIG_ROWS = {n: int(np.prod(s)) // D for n, s in BIG_SHARD.items()}
BIG_R = 7808


def pack_big(shards, dtype):
    parts = [shards[n].astype(dtype).reshape(BIG_ROWS[n], D) for n in BIG]
    parts.append(jnp.zeros((BIG_R - sum(BIG_ROWS.values()), D), dtype))
    return jnp.concatenate(parts, axis=0)


def unpack_big(flat):
    out, o = {}, 0
    for n in BIG:
        out[n] = flat[o:o + BIG_ROWS[n]].reshape(BIG_SHARD[n])
        o += BIG_ROWS[n]
    return out


def unpack_big_full(gathered):
    out, o = {}, 0
    for n in BIG:
        perm, full = BIG_FULL[n]
        out[n] = gathered[:, o:o + BIG_ROWS[n]].reshape((N_DEV,) + BIG_SHARD[n]).transpose(perm).reshape(full)
        o += BIG_ROWS[n]
    return out


def pack_big_full(full, dtype):
    parts = []
    for n in BIG:
        perm, _ = BIG_FULL[n]
        split = tuple(int(v) for v in np.array((N_DEV,) + BIG_SHARD[n])[list(perm)])
        inv = tuple(int(i) for i in np.argsort(perm))
        parts.append(full[n].astype(dtype).reshape(split).transpose(inv).reshape(N_DEV, BIG_ROWS[n], D))
    parts.append(jnp.zeros((N_DEV, BIG_R - sum(BIG_ROWS.values()), D), dtype))
    return jnp.concatenate(parts, axis=1)


def _rows128(a):
    a = a.reshape(-1)
    pad = (-a.shape[0]) % CH
    if pad:
        a = jnp.concatenate([a, jnp.zeros((pad,), a.dtype)])
    return a.reshape(-1, CH)


def _pack_rows(arrs, total):
    parts = [_rows128(a) for a in arrs]
    n = sum(p.shape[0] for p in parts)
    parts.append(jnp.zeros((total - n, CH), F32))
    return jnp.concatenate(parts, axis=0)


def _unpack_rows(flat, shapes):
    out, o = [], 0
    for s in shapes:
        size = int(np.prod(s))
        rows = -(-size // CH)
        out.append(flat[o:o + rows].reshape(-1)[:size].reshape(s))
        o += rows
    return out


SMALL_SHARDED = ("meta", "conv_a", "ssd_conv_w")
SMALL_SHARD_SHAPE = {"meta": (N_META, 128), "conv_a": (DEPTH, 3, 128), "ssd_conv_w": (DEPTH, 4, 256)}
SMALL_FULL_SHAPE = {"meta": (N_META, D), "conv_a": (DEPTH, 3, D), "ssd_conv_w": (DEPTH, 4, 2048)}
SMALL_REPL = ("ssd_conv_b", "ssd_dt_bias", "ssd_a_log", "ssd_d", "ssd_norm", "norm_mix_pre", "norm_mix_post", "norm_ffn_pre",
              "norm_ffn_post")
SMALL_REPL_SHAPE = {"ssd_conv_b": (DEPTH, 2048), "ssd_dt_bias": (DEPTH, SSD_HEADS), "ssd_a_log": (DEPTH, SSD_HEADS),
                    "ssd_d": (DEPTH, SSD_HEADS), "ssd_norm": (DEPTH, D), "norm_mix_pre": (DEPTH, D), "norm_mix_post": (DEPTH, D),
                    "norm_ffn_pre": (DEPTH, D), "norm_ffn_post": (DEPTH, D)}


def _gather_small_full(gathered, n):
    nd = gathered.ndim
    perm = tuple(range(1, nd - 1)) + (0, nd - 1)
    return gathered.transpose(perm).reshape(SMALL_FULL_SHAPE[n])


WEIGHTS = ("meta", "w_in", "conv_a", "ssd_conv_w", "ssd_conv_b", "ssd_dt_bias", "ssd_a_log", "ssd_d", "ssd_norm", "w_branch", "w_out",
           "w_ffn_in", "w_ffn_out", "norm_mix_pre", "norm_mix_post", "norm_ffn_pre", "norm_ffn_post")


def kernel(x, meta, w_in, conv_a, ssd_conv_w, ssd_conv_b, ssd_dt_bias, ssd_a_log, ssd_d, ssd_norm, w_branch, w_out, w_ffn_in, w_ffn_out, norm_mix_pre, norm_mix_post, norm_ffn_pre, norm_ffn_post, loss_target, m_meta, m_w_in, m_conv_a, m_ssd_conv_w, m_ssd_conv_b, m_ssd_dt_bias, m_ssd_a_log, m_ssd_d, m_ssd_norm, m_w_branch, m_w_out, m_w_ffn_in, m_w_ffn_out, m_norm_mix_pre, m_norm_mix_post, m_norm_ffn_pre, m_norm_ffn_post, v_meta, v_w_in, v_conv_a, v_ssd_conv_w, v_ssd_conv_b, v_ssd_dt_bias, v_ssd_a_log, v_ssd_d, v_ssd_norm, v_w_branch, v_w_out, v_w_ffn_in, v_w_ffn_out, v_norm_mix_pre, v_norm_mix_post, v_norm_ffn_pre, v_norm_ffn_post):
    w = dict(meta=meta, w_in=w_in, conv_a=conv_a, ssd_conv_w=ssd_conv_w, ssd_conv_b=ssd_conv_b, ssd_dt_bias=ssd_dt_bias,
             ssd_a_log=ssd_a_log, ssd_d=ssd_d, ssd_norm=ssd_norm, w_branch=w_branch, w_out=w_out, w_ffn_in=w_ffn_in,
             w_ffn_out=w_ffn_out, norm_mix_pre=norm_mix_pre, norm_mix_post=norm_mix_post, norm_ffn_pre=norm_ffn_pre,
             norm_ffn_post=norm_ffn_post)
    m = dict(meta=m_meta, w_in=m_w_in, conv_a=m_conv_a, ssd_conv_w=m_ssd_conv_w, ssd_conv_b=m_ssd_conv_b, ssd_dt_bias=m_ssd_dt_bias,
             ssd_a_log=m_ssd_a_log, ssd_d=m_ssd_d, ssd_norm=m_ssd_norm, w_branch=m_w_branch, w_out=m_w_out, w_ffn_in=m_w_ffn_in,
             w_ffn_out=m_w_ffn_out, norm_mix_pre=m_norm_mix_pre, norm_mix_post=m_norm_mix_post, norm_ffn_pre=m_norm_ffn_pre,
             norm_ffn_post=m_norm_ffn_post)
    v = dict(meta=v_meta, w_in=v_w_in, conv_a=v_conv_a, ssd_conv_w=v_ssd_conv_w, ssd_conv_b=v_ssd_conv_b, ssd_dt_bias=v_ssd_dt_bias,
             ssd_a_log=v_ssd_a_log, ssd_d=v_ssd_d, ssd_norm=v_ssd_norm, w_branch=v_w_branch, w_out=v_w_out, w_ffn_in=v_w_ffn_in,
             w_ffn_out=v_w_ffn_out, norm_mix_pre=v_norm_mix_pre, norm_mix_post=v_norm_mix_post, norm_ffn_pre=v_norm_ffn_pre,
             norm_ffn_post=v_norm_ffn_post)
    xi, yi, ci = _my_place()
    dev = _flat(xi, yi, ci)

    full = unpack_big_full(all_gather(pack_big(w, BF16), name="gather_big"))
    small_shard = _pack_rows([w[n] for n in SMALL_SHARDED], 40)
    small_all = all_gather(small_shard, name="gather_small")
    small_full = {}
    o = 0
    for n in SMALL_SHARDED:
        rows = int(np.prod(SMALL_SHARD_SHAPE[n])) // CH
        small_full[n] = _gather_small_full(small_all[:, o:o + rows].reshape((N_DEV,) + SMALL_SHARD_SHAPE[n]), n)
        o += rows

    layers = []
    for l in range(DEPTH):
        p = {n: full[n][l] for n in BIG}
        p["conv_a"] = small_full["conv_a"][l]
        p["ssd_conv_w"] = small_full["ssd_conv_w"][l]
        for n in SMALL_REPL:
            p[n] = w[n][l]
        layers.append(p)

    loss_blk, grad_x, gmeta, grads = local_step(x[0], loss_target[0], small_full["meta"], layers)

    gfull = {n: jnp.stack([grads[l][n] for l in range(DEPTH)]) for n in BIG}
    recv = all_to_all(pack_big_full(gfull, BF16), name="exchange_big")
    g_flat, d_flat, nm_flat, nv_flat = adamw_big(recv, pack_big(w, F32), pack_big(m, F32), pack_big(v, F32), name="adamw_big")
    out_g, out_d, out_m, out_v = unpack_big(g_flat), unpack_big(d_flat), unpack_big(nm_flat), unpack_big(nv_flat)

    small_names = SMALL_SHARDED + SMALL_REPL
    small_grads = [gmeta] + [jnp.stack([grads[l][n] for l in range(DEPTH)]) for n in small_names[1:]]
    small_shapes = [SMALL_FULL_SHAPE[n] for n in SMALL_SHARDED] + [SMALL_REPL_SHAPE[n] for n in SMALL_REPL]
    sm = _pack_rows(small_grads + [loss_blk[0:1]], 424)
    sm_sum = sum_slots(all_gather(sm, name="gather_small_grads"), name="sum_small_grads")
    summed = _unpack_rows(sm_sum, small_shapes + [(1, CH)])
    loss = summed[-1][0, 0]
    sg = dict(zip(small_names, summed[:-1]))
    for n in SMALL_SHARDED:
        width = SMALL_SHARD_SHAPE[n][-1]
        sg[n] = lax.dynamic_slice_in_dim(sg[n], dev * width, width, axis=sg[n].ndim - 1)
    pk = lambda d: _pack_rows([d[n] for n in small_names], 160)
    sd, snm, snv = adamw_small(pk(w), pk(sg), pk(m), pk(v), name="adamw_small")
    shard_shapes = [SMALL_SHARD_SHAPE[n] for n in SMALL_SHARDED] + [SMALL_REPL_SHAPE[n] for n in SMALL_REPL]
    for dst, flat in ((out_d, sd), (out_m, snm), (out_v, snv)):
        dst.update(zip(small_names, _unpack_rows(flat, shard_shapes)))
    out_g.update(sg)

    return (loss, grad_x[None], *[out_g[n] for n in WEIGHTS], *[out_d[n] for n in WEIGHTS], *[out_m[n] for n in WEIGHTS],
            *[out_v[n] for n in WEIGHTS])
```

```python
import functools
import math

import numpy as np
import jax
import jax.numpy as jnp
from jax import lax
from jax.experimental import pallas as pl
from jax.experimental.pallas import tpu as pltpu

F32, BF16 = jnp.float32, jnp.bfloat16
HI = lax.Precision.HIGHEST
MESH_ID = pl.DeviceIdType.MESH

D = 1024
CH = 128
N_META = 16
N_PAD = CH - N_META
EPS = 1e-6
N_DEV = 8
DEPTH = 2
SSD_HEADS = 16
RET_HEADS = 4
SB_HEADS = 8
D_FF = 2816
ROPE_BASE = 10000.0

NF = 14336
COL_GATE = 10

ADAM_LR, ADAM_B1, ADAM_B2, ADAM_EPS, ADAM_WD, ADAM_STEP = 0.001, 0.9, 0.999, 1e-08, 0.01, 10

VMEM_BYTES = 48 * 1024 * 1024


def _pick(n, cands):
    for c in cands:
        if n % c == 0:
            return c
    raise ValueError((n, cands))


def _tok_block(t):
    return _pick(t, (384, 128))


def _cparams(ngrid, vmem=VMEM_BYTES):
    return pltpu.CompilerParams(dimension_semantics=("arbitrary",) * ngrid, vmem_limit_bytes=vmem)


def _iota(shape, dim):
    return lax.broadcasted_iota(jnp.int32, shape, dim)


def _sigmoid(x):
    return 1.0 / (1.0 + jnp.exp(-x))


def _silu(x):
    return x * _sigmoid(x)


def _dsilu(x):
    s = _sigmoid(x)
    return s * (1.0 + x * (1.0 - s))


def _softplus(x):
    return jnp.maximum(x, 0.0) + jnp.log(1.0 + jnp.exp(-jnp.abs(x)))


def _dot(a, b):
    return jnp.dot(a.astype(BF16), b.astype(BF16), preferred_element_type=F32)


def _dot_nt(a, b):
    return lax.dot_general(a.astype(BF16), b.astype(BF16), (((1,), (1,)), ((), ())), preferred_element_type=F32)


def _dot_tn(a, b):
    return lax.dot_general(a.astype(BF16), b.astype(BF16), (((0,), (0,)), ((), ())), preferred_element_type=F32)


def _dot_hi(a, b):
    return jnp.dot(a, b, precision=HI, preferred_element_type=F32)


def mm(a, b, *, name, out_dtype=F32, add=None, tm=None, tn=None, tk=None):
    m, k = a.shape
    k2, n = b.shape
    assert k == k2
    tm = tm or _pick(m, (1376, 384, 128))
    tn = tn or _pick(n, (512, 384, 256, 128))
    tk = tk or _pick(k, (1024, 1408, 512, 384, 128))
    nk = k // tk
    has_add = add is not None

    def body(*refs):
        if has_add:
            a_ref, b_ref, c_ref, o_ref = refs[:4]
            scr = refs[4:]
        else:
            a_ref, b_ref, o_ref = refs[:3]
            c_ref = None
            scr = refs[3:]
        x = _dot(a_ref[...], b_ref[...])
        if nk == 1:
            if has_add:
                x = x + c_ref[...]
            o_ref[...] = x.astype(out_dtype)
        else:
            acc = scr[0]
            kk = pl.program_id(2)

            @pl.when(kk == 0)
            def _():
                acc[...] = x

            @pl.when(kk > 0)
            def _():
                acc[...] += x

            @pl.when(kk == nk - 1)
            def _():
                r = acc[...]
                if has_add:
                    r = r + c_ref[...]
                o_ref[...] = r.astype(out_dtype)

    in_specs = [pl.BlockSpec((tm, tk), lambda i, j, kk: (i, kk)), pl.BlockSpec((tk, tn), lambda i, j, kk: (kk, j))]
    args = [a, b]
    if has_add:
        in_specs.append(pl.BlockSpec((tm, tn), lambda i, j, kk: (i, j)))
        args.append(add)
    return pl.pallas_call(
        body, name=name, out_shape=jax.ShapeDtypeStruct((m, n), out_dtype), grid=(m // tm, n // tn, nk),
        in_specs=in_specs, out_specs=pl.BlockSpec((tm, tn), lambda i, j, kk: (i, j)),
        scratch_shapes=[pltpu.VMEM((tm, tn), F32)] if nk > 1 else [],
        compiler_params=_cparams(3))(*args)


def mm_tn(a, b, *, name, tm=None, tn=None, tk=None):
    t, m = a.shape
    t2, n = b.shape
    assert t == t2
    tm = tm or _pick(m, (1024, 1408, 512, 128))
    tn = tn or _pick(n, (512, 384, 256, 128))
    tk = tk or _pick(t, (1376, 384, 128))
    nk = t // tk

    def body(a_ref, b_ref, o_ref):
        x = _dot_tn(a_ref[...], b_ref[...])
        kk = pl.program_id(2)

        @pl.when(kk == 0)
        def _():
            o_ref[...] = x

        @pl.when(kk > 0)
        def _():
            o_ref[...] += x

    return pl.pallas_call(
        body, name=name, out_shape=jax.ShapeDtypeStruct((m, n), F32), grid=(m // tm, n // tn, nk),
        in_specs=[pl.BlockSpec((tk, tm), lambda i, j, kk: (kk, i)), pl.BlockSpec((tk, tn), lambda i, j, kk: (kk, j))],
        out_specs=pl.BlockSpec((tm, tn), lambda i, j, kk: (i, j)),
        compiler_params=_cparams(3))(a, b)


def rms_fwd(x, w, *, name, out_dtype=F32, res=None):
    t, d = x.shape
    tb = _tok_block(t)
    has_res = res is not None

    def body(*refs):
        if has_res:
            x_ref, w_ref, r_ref, o_ref = refs
        else:
            x_ref, w_ref, o_ref = refs
        xv = x_ref[...]
        y = xv * lax.rsqrt(jnp.mean(xv * xv, axis=-1, keepdims=True) + EPS) * w_ref[...]
        if has_res:
            y = y + r_ref[...]
        o_ref[...] = y.astype(out_dtype)

    blk = pl.BlockSpec((tb, d), lambda i: (i, 0))
    wspec = pl.BlockSpec((1, d), lambda i: (0, 0))
    in_specs = [blk, wspec] + ([blk] if has_res else [])
    args = [x, w] + ([res] if has_res else [])
    return pl.pallas_call(body, name=name, out_shape=jax.ShapeDtypeStruct((t, d), out_dtype), grid=(t // tb,),
                          in_specs=in_specs, out_specs=blk, compiler_params=_cparams(1))(*args)


def rms_bwd(x, w, dy, *, name, add=None):
    t, d = x.shape
    tb = _tok_block(t)
    has_add = add is not None

    def body(*refs):
        if has_add:
            x_ref, w_ref, dy_ref, a_ref, dx_ref, dw_ref = refs
        else:
            x_ref, w_ref, dy_ref, dx_ref, dw_ref = refs
        xv = x_ref[...]
        dyv = dy_ref[...]
        r = lax.rsqrt(jnp.mean(xv * xv, axis=-1, keepdims=True) + EPS)
        g = dyv * w_ref[...]
        dx = r * g - xv * (r * r * r) * jnp.mean(xv * g, axis=-1, keepdims=True)
        if has_add:
            dx = dx + a_ref[...]
        dx_ref[...] = dx
        part = jnp.sum(dyv * xv * r, axis=0, keepdims=True)

        @pl.when(pl.program_id(0) == 0)
        def _():
            dw_ref[...] = part

        @pl.when(pl.program_id(0) > 0)
        def _():
            dw_ref[...] += part

    blk = pl.BlockSpec((tb, d), lambda i: (i, 0))
    wspec = pl.BlockSpec((1, d), lambda i: (0, 0))
    in_specs = [blk, wspec, blk] + ([blk] if has_add else [])
    args = [x, w, dy] + ([add] if has_add else [])
    return pl.pallas_call(body, name=name,
                          out_shape=[jax.ShapeDtypeStruct((t, d), F32), jax.ShapeDtypeStruct((1, d), F32)],
                          grid=(t // tb,), in_specs=in_specs, out_specs=[blk, wspec], compiler_params=_cparams(1))(*args)


def _shift_down(cur, prev8, k):
    z = jnp.concatenate([prev8, cur], axis=0)
    return pltpu.roll(z, k, 0)[8:]


def _shift_up(cur, next8, k):
    n = cur.shape[0] + 8
    z = jnp.concatenate([cur, next8], axis=0)
    return pltpu.roll(z, n - k, 0)[:cur.shape[0]]


def _prev8_spec(tb, width, col):
    return pl.BlockSpec((8, width), lambda i: (jnp.maximum(i * (tb // 8) - 1, 0), col))


def _next8_spec(tb, width, col, t):
    return pl.BlockSpec((8, width), lambda i: (jnp.minimum((i + 1) * (tb // 8), t // 8 - 1), col))


def _row_valid(i, tb, n, offset=0):
    rows = i * tb + offset + _iota((n, 1), 0)
    return (rows >= N_PAD).astype(F32)


def conv_a_fwd(proj, w8, *, name):
    t = proj.shape[0]
    tb = _tok_block(t)

    def body(b_ref, c_ref, x_ref, cp_ref, xp_ref, w_ref, o_ref):
        i = pl.program_id(0)
        u = c_ref[...] * x_ref[...] * _row_valid(i, tb, tb)
        up = cp_ref[...] * xp_ref[...] * _row_valid(i, tb, 8, -8) * (i > 0).astype(F32)
        w = w_ref[...]
        conv = w[2:3] * u + w[1:2] * _shift_down(u, up, 1) + w[0:1] * _shift_down(u, up, 2)
        o_ref[...] = b_ref[...] * conv

    blk = lambda col: pl.BlockSpec((tb, D), lambda i: (i, col))
    return pl.pallas_call(
        body, name=name, out_shape=jax.ShapeDtypeStruct((t, D), F32), grid=(t // tb,),
        in_specs=[blk(0), blk(1), blk(2), _prev8_spec(tb, D, 1), _prev8_spec(tb, D, 2), pl.BlockSpec((8, D), lambda i: (0, 0))],
        out_specs=pl.BlockSpec((tb, D), lambda i: (i, 0)), compiler_params=_cparams(1))(proj, proj, proj, proj, proj, w8)


def conv_a_bwd(proj, w8, dy, *, name):
    t = proj.shape[0]
    tb = _tok_block(t)
    nblk = t // tb

    def body(b_ref, c_ref, x_ref, cp_ref, xp_ref, dy_ref, dyn_ref, bn_ref, w_ref, o_ref, dw_ref):
        i = pl.program_id(0)
        vm = _row_valid(i, tb, tb)
        cv, xv, bv, dyv = c_ref[...], x_ref[...], b_ref[...], dy_ref[...]
        u = cv * xv * vm
        up = cp_ref[...] * xp_ref[...] * _row_valid(i, tb, 8, -8) * (i > 0).astype(F32)
        w = w_ref[...]
        u1 = _shift_down(u, up, 1)
        u2 = _shift_down(u, up, 2)
        conv = w[2:3] * u + w[1:2] * u1 + w[0:1] * u2
        dconv = dyv * bv
        dconv_n = dyn_ref[...] * bn_ref[...] * (i < nblk - 1).astype(F32)
        du = w[2:3] * dconv + w[1:2] * _shift_up(dconv, dconv_n, 1) + w[0:1] * _shift_up(dconv, dconv_n, 2)
        o_ref[:, 0:D] = dyv * conv
        o_ref[:, D:2 * D] = du * xv * vm
        o_ref[:, 2 * D:3 * D] = du * cv * vm

        @pl.when(i == 0)
        def _():
            dw_ref[...] = jnp.zeros_like(dw_ref)

        dw_ref[0:1, :] += jnp.sum(dconv * u2, axis=0, keepdims=True)
        dw_ref[1:2, :] += jnp.sum(dconv * u1, axis=0, keepdims=True)
        dw_ref[2:3, :] += jnp.sum(dconv * u, axis=0, keepdims=True)

    blk = lambda col: pl.BlockSpec((tb, D), lambda i: (i, col))
    w8spec = pl.BlockSpec((8, D), lambda i: (0, 0))
    return pl.pallas_call(
        body, name=name,
        out_shape=[jax.ShapeDtypeStruct((t, 3 * D), F32), jax.ShapeDtypeStruct((8, D), F32)], grid=(nblk,),
        in_specs=[blk(0), blk(1), blk(2), _prev8_spec(tb, D, 1), _prev8_spec(tb, D, 2), blk(0),
                  _next8_spec(tb, D, 0, t), _next8_spec(tb, D, 0, t), w8spec],
        out_specs=[pl.BlockSpec((tb, 3 * D), lambda i: (i, 0)), w8spec],
        compiler_params=_cparams(1))(proj, proj, proj, proj, proj, dy, dy, proj, w8)


XBC_W = 2048


def ssd_conv_fwd(proj, w8, b, *, name):
    t = proj.shape[0]
    tb = _tok_block(t)

    def body(x_ref, xp_ref, w_ref, b_ref, o_ref):
        i = pl.program_id(0)
        xm = x_ref[...] * _row_valid(i, tb, tb)
        xmp = xp_ref[...] * _row_valid(i, tb, 8, -8) * (i > 0).astype(F32)
        w = w_ref[...]
        c = w[3:4] * xm + w[2:3] * _shift_down(xm, xmp, 1) + w[1:2] * _shift_down(xm, xmp, 2) + w[0:1] * _shift_down(xm, xmp, 3)
        o_ref[...] = _silu(c + b_ref[...])

    return pl.pallas_call(
        body, name=name, out_shape=jax.ShapeDtypeStruct((t, XBC_W), F32), grid=(t // tb,),
        in_specs=[pl.BlockSpec((tb, XBC_W), lambda i: (i, 2)), _prev8_spec(tb, XBC_W, 2),
                  pl.BlockSpec((8, XBC_W), lambda i: (0, 0)), pl.BlockSpec((1, XBC_W), lambda i: (0, 0))],
        out_specs=pl.BlockSpec((tb, XBC_W), lambda i: (i, 0)), compiler_params=_cparams(1))(proj, proj, w8, b)


def ssd_conv_bwd_pre(proj, w8, b, dxa, *, name):
    t = proj.shape[0]
    tb = _tok_block(t)

    def body(x_ref, xp_ref, w_ref, b_ref, d_ref, o_ref, dw_ref, db_ref):
        i = pl.program_id(0)
        xm = x_ref[...] * _row_valid(i, tb, tb)
        xmp = xp_ref[...] * _row_valid(i, tb, 8, -8) * (i > 0).astype(F32)
        w = w_ref[...]
        x1, x2, x3 = _shift_down(xm, xmp, 1), _shift_down(xm, xmp, 2), _shift_down(xm, xmp, 3)
        c = w[3:4] * xm + w[2:3] * x1 + w[1:2] * x2 + w[0:1] * x3 + b_ref[...]
        dpre = d_ref[...] * _dsilu(c)
        o_ref[...] = dpre

        @pl.when(i == 0)
        def _():
            dw_ref[...] = jnp.zeros_like(dw_ref)
            db_ref[...] = jnp.zeros_like(db_ref)

        dw_ref[0:1, :] += jnp.sum(dpre * x3, axis=0, keepdims=True)
        dw_ref[1:2, :] += jnp.sum(dpre * x2, axis=0, keepdims=True)
        dw_ref[2:3, :] += jnp.sum(dpre * x1, axis=0, keepdims=True)
        dw_ref[3:4, :] += jnp.sum(dpre * xm, axis=0, keepdims=True)
        db_ref[...] += jnp.sum(dpre, axis=0, keepdims=True)

    w8spec = pl.BlockSpec((8, XBC_W), lambda i: (0, 0))
    bspec = pl.BlockSpec((1, XBC_W), lambda i: (0, 0))
    return pl.pallas_call(
        body, name=name,
        out_shape=[jax.ShapeDtypeStruct((t, XBC_W), F32), jax.ShapeDtypeStruct((8, XBC_W), F32), jax.ShapeDtypeStruct((1, XBC_W), F32)],
        grid=(t // tb,),
        in_specs=[pl.BlockSpec((tb, XBC_W), lambda i: (i, 2)), _prev8_spec(tb, XBC_W, 2), w8spec, bspec,
                  pl.BlockSpec((tb, XBC_W), lambda i: (i, 0))],
        out_specs=[pl.BlockSpec((tb, XBC_W), lambda i: (i, 0)), w8spec, bspec],
        compiler_params=_cparams(1))(proj, proj, w8, b, dxa)


def ssd_conv_bwd_in(dpre, w8, *, name):
    t = dpre.shape[0]
    tb = _tok_block(t)
    nblk = t // tb

    def body(d_ref, dn_ref, w_ref, o_ref):
        i = pl.program_id(0)
        d = d_ref[...]
        dn = dn_ref[...] * (i < nblk - 1).astype(F32)
        w = w_ref[...]
        dx = w[3:4] * d + w[2:3] * _shift_up(d, dn, 1) + w[1:2] * _shift_up(d, dn, 2) + w[0:1] * _shift_up(d, dn, 3)
        o_ref[...] = dx * _row_valid(i, tb, tb)

    return pl.pallas_call(
        body, name=name, out_shape=jax.ShapeDtypeStruct((t, XBC_W), F32), grid=(nblk,),
        in_specs=[pl.BlockSpec((tb, XBC_W), lambda i: (i, 0)), _next8_spec(tb, XBC_W, 0, t), pl.BlockSpec((8, XBC_W), lambda i: (0, 0))],
        out_specs=pl.BlockSpec((tb, XBC_W), lambda i: (i, 0)), compiler_params=_cparams(1))(dpre, dpre, w8)


def _col(x, h):
    return jnp.sum(jnp.where(_iota(x.shape, 1) == h, x, 0.0), axis=1, keepdims=True)


def _row(x, h):
    return jnp.sum(jnp.where(_iota(x.shape, 0) == h, x, 0.0), axis=0, keepdims=True)


def _ssd_common(xa, dtr, dtb, alog, c):
    vm = _row_valid(c, CH, CH)
    xs = xa[:, :D] * vm
    dt = _softplus(dtr + dtb)
    a = -jnp.exp(alog) * dt
    tri = (_iota((CH, CH), 0) >= _iota((CH, CH), 1)).astype(F32)
    acs = _dot_hi(tri, a)
    return vm, xs, dt, a, acs, acs.T


def _pair_lanes(v0, v1):
    lane = _iota((1, CH), 1)
    return jnp.where(lane < 64, v0, v1)


def _ssd_pair_fwd(q, xs, xa, dt, acs, acs_t, dsk, hin, g_mat):
    g = q // 2
    h0, h1 = 2 * q, 2 * q + 1
    causal = _iota((CH, CH), 0) >= _iota((CH, CH), 1)
    bg = xa[:, D + CH * g:D + CH * (g + 1)]
    cg = xa[:, D + 512 + CH * g:D + 512 + CH * (g + 1)]
    xs_p = xs[:, CH * q:CH * (q + 1)]
    ac0, ac1 = _col(acs, h0), _col(acs, h1)
    ar0, ar1 = _row(acs_t, h0), _row(acs_t, h1)
    l0 = jnp.exp(jnp.where(causal, ac0 - ar0, -1e30))
    l1 = jnp.exp(jnp.where(causal, ac1 - ar1, -1e30))
    dt_p = _pair_lanes(_col(dt, h0), _col(dt, h1))
    x = xs_p * dt_p
    m0, m1 = g_mat * l0, g_mat * l1
    lane = _iota((CH, CH), 1)
    yd = jnp.where(lane < 64, _dot(m0, x), _dot(m1, x))
    ac_p = _pair_lanes(ac0, ac1)
    eac = jnp.exp(ac_p)
    yoff_raw = _dot_nt(cg, hin)
    al0 = jnp.sum(jnp.where(_iota((CH, 1), 0) == CH - 1, ac0, 0.0), axis=0, keepdims=True)
    al1 = jnp.sum(jnp.where(_iota((CH, 1), 0) == CH - 1, ac1, 0.0), axis=0, keepdims=True)
    dsv = jnp.exp(_pair_lanes(al0, al1) - ac_p)
    s = _dot_tn(x * dsv, bg)
    cd = jnp.where(_iota((CH, 1), 0) < 64, jnp.exp(al0), jnp.exp(al1))
    d_p = _pair_lanes(_col(dsk, h0), _col(dsk, h1))
    y = yd + yoff_raw * eac + xs_p * d_p
    return dict(bg=bg, cg=cg, xs_p=xs_p, l0=l0, l1=l1, m0=m0, m1=m1, dt_p=dt_p, x=x, eac=eac, yoff_raw=yoff_raw,
                dsv=dsv, s=s, cd=cd, d_p=d_p, y=y, al0=al0, al1=al1)


def _ssd_gate_norm(y, z, nw):
    yv = y * _silu(z)
    outs, rs = [], []
    for g in range(4):
        yg = yv[:, 256 * g:256 * (g + 1)]
        r = lax.rsqrt(jnp.mean(yg * yg, axis=-1, keepdims=True) + EPS)
        outs.append(yg * r * nw[:, 256 * g:256 * (g + 1)])
        rs.append(r)
    return yv, jnp.concatenate(outs, axis=1), rs


def ssd_fwd(xa, proj, pdt, dtb, alog, dsk, nw, *, name):
    t = xa.shape[0]
    nc = t // CH

    def body(xa_ref, dtr_ref, z_ref, dtb_ref, alog_ref, dsk_ref, nw_ref, y_ref, hs_ref, h_scr):
        c = pl.program_id(0)

        @pl.when(c == 0)
        def _():
            h_scr[...] = jnp.zeros_like(h_scr)

        xa_v = xa_ref[...]
        vm, xs, dt, a, acs, acs_t = _ssd_common(xa_v, dtr_ref[...], dtb_ref[...], alog_ref[...], c)
        dsk_v = dsk_ref[...]
        ys = []
        g_mat = None
        for q in range(8):
            if q % 2 == 0:
                g = q // 2
                g_mat = _dot_nt(xa_v[:, D + 512 + CH * g:D + 512 + CH * (g + 1)], xa_v[:, D + CH * g:D + CH * (g + 1)])
            hin = h_scr[q]
            hs_ref[0, q] = hin
            p = _ssd_pair_fwd(q, xs, xa_v, dt, acs, acs_t, dsk_v, hin, g_mat)
            h_scr[q] = hin * p["cd"] + p["s"]
            ys.append(p["y"])
        y = jnp.concatenate(ys, axis=1)
        _, out, _ = _ssd_gate_norm(y, z_ref[...], nw_ref[...])
        y_ref[...] = out

    small = pl.BlockSpec((1, CH), lambda c: (0, 0))
    return pl.pallas_call(
        body, name=name,
        out_shape=[jax.ShapeDtypeStruct((t, D), F32), jax.ShapeDtypeStruct((nc, 8, CH, CH), F32)], grid=(nc,),
        in_specs=[pl.BlockSpec((CH, XBC_W), lambda c: (c, 0)), pl.BlockSpec((CH, CH), lambda c: (c, 0)),
                  pl.BlockSpec((CH, D), lambda c: (c, 3)), small, small, small, pl.BlockSpec((1, D), lambda c: (0, 0))],
        out_specs=[pl.BlockSpec((CH, D), lambda c: (c, 0)), pl.BlockSpec((1, 8, CH, CH), lambda c: (c, 0, 0, 0))],
        scratch_shapes=[pltpu.VMEM((8, CH, CH), F32)], compiler_params=_cparams(1))(xa, pdt, proj, dtb, alog, dsk, nw)


def ssd_bwd(xa, proj, pdt, hs, dyb, dtb, alog, dsk, nw, *, name):
    t = xa.shape[0]
    nc = t // CH

    def body(xa_ref, dtr_ref, z_ref, hs_ref, dy_ref, dtb_ref, alog_ref, dsk_ref, nw_ref,
             dz_ref, dxa_ref, ddt_ref, gdtb_ref, galog_ref, gdsk_ref, gnw_ref, dh_scr):
        step = pl.program_id(0)
        c = nc - 1 - step

        @pl.when(step == 0)
        def _():
            dh_scr[...] = jnp.zeros_like(dh_scr)
            gdtb_ref[...] = jnp.zeros_like(gdtb_ref)
            galog_ref[...] = jnp.zeros_like(galog_ref)
            gdsk_ref[...] = jnp.zeros_like(gdsk_ref)
            gnw_ref[...] = jnp.zeros_like(gnw_ref)

        xa_v = xa_ref[...]
        dtr = dtr_ref[...]
        dtb_v = dtb_ref[...]
        alog_v = alog_ref[...]
        vm, xs, dt, a, acs, acs_t = _ssd_common(xa_v, dtr, dtb_v, alog_v, c)
        dsk_v = dsk_ref[...]
        z = z_ref[...]
        nw_v = nw_ref[...]
        lane1 = _iota((1, CH), 1)
        sub1 = _iota((CH, 1), 0)
        lane = _iota((CH, CH), 1)

        pairs = []
        g_mats = []
        for q in range(8):
            if q % 2 == 0:
                g = q // 2
                g_mats.append(_dot_nt(xa_v[:, D + 512 + CH * g:D + 512 + CH * (g + 1)], xa_v[:, D + CH * g:D + CH * (g + 1)]))
            pairs.append(_ssd_pair_fwd(q, xs, xa_v, dt, acs, acs_t, dsk_v, hs_ref[0, q], g_mats[q // 2]))
        y_pre = jnp.concatenate([p["y"] for p in pairs], axis=1)

        dout = dy_ref[...]
        sz = _silu(z)
        yv = y_pre * sz
        dyv_parts = []
        gnw_parts = []
        for g in range(4):
            sl = slice(256 * g, 256 * (g + 1))
            yg = yv[:, sl]
            r = lax.rsqrt(jnp.mean(yg * yg, axis=-1, keepdims=True) + EPS)
            gy = dout[:, sl] * nw_v[:, sl]
            dyv_parts.append(r * gy - yg * (r * r * r) * jnp.mean(yg * gy, axis=-1, keepdims=True))
            gnw_parts.append(jnp.sum(dout[:, sl] * yg * r, axis=0, keepdims=True))
        dyv = jnp.concatenate(dyv_parts, axis=1)
        gnw_ref[...] += jnp.concatenate(gnw_parts, axis=1)
        dz_ref[...] = dyv * y_pre * _dsilu(z)
        dy_pre = dyv * sz

        dacs_c = jnp.zeros((CH, CH), F32)
        dacs_r = jnp.zeros((CH, CH), F32)
        ddt = jnp.zeros((CH, CH), F32)
        gdsk = jnp.zeros((1, CH), F32)
        dxs_parts = []
        db_g = [None] * 4
        dc_g = [None] * 4
        dg_g = [None] * 4

        def acc(lst, g, v):
            lst[g] = v if lst[g] is None else lst[g] + v

        for q in range(8):
            p = pairs[q]
            g = q // 2
            h0, h1 = 2 * q, 2 * q + 1
            dy = dy_pre[:, CH * q:CH * (q + 1)]
            hin = hs_ref[0, q]
            dhout = dh_scr[q]
            x = p["x"]
            m_lo = lane < 64
            dxs = dy * p["d_p"]
            t_sk = dy * p["xs_p"]
            gdsk = gdsk + jnp.where(lane1 == h0, jnp.sum(jnp.where(m_lo, t_sk, 0.0)), 0.0) \
                        + jnp.where(lane1 == h1, jnp.sum(jnp.where(m_lo, 0.0, t_sk)), 0.0)
            dx = jnp.zeros((CH, CH), F32)
            for k, (hh, mk, lk, mm_) in enumerate(((h0, m_lo, p["l0"], p["m0"]), (h1, ~m_lo, p["l1"], p["m1"]))):
                dyk = jnp.where(mk, dy, 0.0)
                dm = _dot_nt(dyk, x)
                dx = dx + jnp.where(mk, _dot_tn(mm_, dy), 0.0)
                acc(dg_g, g, dm * lk)
                qm = dm * mm_
                dacs_c = dacs_c + jnp.where(lane1 == hh, jnp.sum(qm, axis=1, keepdims=True), 0.0)
                dacs_r = dacs_r - jnp.where(sub1 == hh, jnp.sum(qm, axis=0, keepdims=True), 0.0)
            dye = dy * p["eac"]
            acc(dc_g, g, _dot(dye, hin))
            t_off = dy * p["yoff_raw"] * p["eac"]
            dacs_c = dacs_c + jnp.where(lane1 == h0, jnp.sum(jnp.where(m_lo, t_off, 0.0), axis=1, keepdims=True), 0.0) \
                            + jnp.where(lane1 == h1, jnp.sum(jnp.where(m_lo, 0.0, t_off), axis=1, keepdims=True), 0.0)
            dhin = _dot_tn(dye, p["cg"]) + dhout * p["cd"]
            w1 = _dot_nt(p["bg"], dhout)
            dx = dx + p["dsv"] * w1
            t_ds = x * w1 * p["dsv"]
            dd0 = jnp.sum(jnp.where(m_lo, t_ds, 0.0), axis=1, keepdims=True)
            dd1 = jnp.sum(jnp.where(m_lo, 0.0, t_ds), axis=1, keepdims=True)
            acc(db_g, g, _dot(x * p["dsv"], dhout))
            t_cd = dhout * hin
            sub_lo = _iota((CH, CH), 0) < 64
            dcd0 = jnp.sum(jnp.where(sub_lo, t_cd, 0.0)) * jnp.exp(p["al0"])
            dcd1 = jnp.sum(jnp.where(sub_lo, 0.0, t_cd)) * jnp.exp(p["al1"])
            last = (sub1 == CH - 1)
            dacs_c = dacs_c + jnp.where(lane1 == h0, jnp.where(last, jnp.sum(dd0) + dcd0, 0.0) - dd0, 0.0) \
                            + jnp.where(lane1 == h1, jnp.where(last, jnp.sum(dd1) + dcd1, 0.0) - dd1, 0.0)
            dh_scr[q] = dhin
            dxs = dxs + dx * p["dt_p"]
            t_dt = dx * p["xs_p"]
            ddt = ddt + jnp.where(lane1 == h0, jnp.sum(jnp.where(m_lo, t_dt, 0.0), axis=1, keepdims=True), 0.0) \
                      + jnp.where(lane1 == h1, jnp.sum(jnp.where(m_lo, 0.0, t_dt), axis=1, keepdims=True), 0.0)
            dxs_parts.append(dxs)

        for g in range(4):
            bg, cg = pairs[2 * g]["bg"], pairs[2 * g]["cg"]
            dc_g[g] = dc_g[g] + _dot(dg_g[g], bg)
            db_g[g] = db_g[g] + _dot_tn(dg_g[g], cg)

        dacs = dacs_c + dacs_r.T
        rtri = (_iota((CH, CH), 1) >= _iota((CH, CH), 0)).astype(F32)
        da = _dot_hi(rtri, dacs)
        ddt = ddt - da * jnp.exp(alog_v)
        galog_ref[...] += jnp.sum(da * a, axis=0, keepdims=True)
        dpre = ddt * _sigmoid(dtr + dtb_v) * (lane1 < SSD_HEADS).astype(F32)
        ddt_ref[...] = dpre
        gdtb_ref[...] += jnp.sum(dpre, axis=0, keepdims=True)
        gdsk_ref[...] += gdsk
        dxa_ref[:, 0:D] = jnp.concatenate(dxs_parts, axis=1) * vm
        dxa_ref[:, D:D + 512] = jnp.concatenate(db_g, axis=1)
        dxa_ref[:, D + 512:D + 1024] = jnp.concatenate(dc_g, axis=1)

    small = pl.BlockSpec((1, CH), lambda s: (0, 0))
    wide = pl.BlockSpec((1, D), lambda s: (0, 0))
    rev = lambda s: nc - 1 - s
    return pl.pallas_call(
        body, name=name,
        out_shape=[jax.ShapeDtypeStruct((t, D), F32), jax.ShapeDtypeStruct((t, XBC_W), F32), jax.ShapeDtypeStruct((t, CH), F32),
                   jax.ShapeDtypeStruct((1, CH), F32), jax.ShapeDtypeStruct((1, CH), F32), jax.ShapeDtypeStruct((1, CH), F32),
                   jax.ShapeDtypeStruct((1, D), F32)],
        grid=(nc,),
        in_specs=[pl.BlockSpec((CH, XBC_W), lambda s: (rev(s), 0)), pl.BlockSpec((CH, CH), lambda s: (rev(s), 0)),
                  pl.BlockSpec((CH, D), lambda s: (rev(s), 3)), pl.BlockSpec((1, 8, CH, CH), lambda s: (rev(s), 0, 0, 0)),
                  pl.BlockSpec((CH, D), lambda s: (rev(s), 0)), small, small, small, wide],
        out_specs=[pl.BlockSpec((CH, D), lambda s: (rev(s), 0)), pl.BlockSpec((CH, XBC_W), lambda s: (rev(s), 0)),
                   pl.BlockSpec((CH, CH), lambda s: (rev(s), 0)), small, small, small, wide],
        scratch_shapes=[pltpu.VMEM((8, CH, CH), F32)], compiler_params=_cparams(1))(xa, pdt, proj, hs, dyb, dtb, alog, dsk, nw)


RET_DK = 256


def _log_gamma(h):
    return math.log(1.0 - 2.0 ** (-5.0 - h))


def _rope(x, cos, sin):
    x1, x2 = x[:, :128], x[:, 128:]
    return jnp.concatenate([x1 * cos - x2 * sin, x1 * sin + x2 * cos], axis=1)


def _unrope(d, cos, sin):
    d1, d2 = d[:, :128], d[:, 128:]
    return jnp.concatenate([d1 * cos + d2 * sin, d2 * cos - d1 * sin], axis=1)


def _ret_head_fwd(h, q, k, v, cos, sin, vm, r_in):
    lg = _log_gamma(h)
    sl = slice(RET_DK * h, RET_DK * (h + 1))
    qr = _rope(q[:, sl], cos, sin)
    kr = _rope(k[:, sl], cos, sin) * (RET_DK ** -0.5)
    vr = v[:, sl] * vm
    rel = (_iota((CH, CH), 0) - _iota((CH, CH), 1)).astype(F32)
    dmask = jnp.where(rel >= 0, jnp.exp(lg * jnp.maximum(rel, 0.0)), 0.0)
    idx = _iota((CH, 1), 0).astype(F32)
    kdec = jnp.exp(lg * (CH - 1 - idx))
    qdec = jnp.exp(lg * (idx + 1.0))
    scores = _dot_nt(qr, kr) * dmask
    y = _dot(scores, vr) + _dot(qr, r_in) * qdec
    kv = _dot_tn(kr * kdec, vr)
    return dict(qr=qr, kr=kr, vr=vr, dmask=dmask, kdec=kdec, qdec=qdec, scores=scores, y=y, kv=kv, cdec=math.exp(lg * CH))


def _group_norm(y):
    mu = jnp.mean(y, axis=-1, keepdims=True)
    yc = y - mu
    r = lax.rsqrt(jnp.mean(yc * yc, axis=-1, keepdims=True) + EPS)
    return yc * r, r


def ret_fwd(proj, cos, sin, *, name):
    t = proj.shape[0]
    nc = t // CH

    def body(q_ref, k_ref, v_ref, g_ref, cos_ref, sin_ref, y_ref, rs_ref, r_scr):
        c = pl.program_id(0)

        @pl.when(c == 0)
        def _():
            r_scr[...] = jnp.zeros_like(r_scr)

        vm = _row_valid(c, CH, CH)
        q, k, v, gt = q_ref[...], k_ref[...], v_ref[...], g_ref[...]
        cos, sin = cos_ref[...], sin_ref[...]
        for h in range(RET_HEADS):
            r_in = r_scr[h]
            rs_ref[0, h] = r_in
            p = _ret_head_fwd(h, q, k, v, cos, sin, vm, r_in)
            r_scr[h] = r_in * p["cdec"] + p["kv"]
            yn, _ = _group_norm(p["y"])
            sl = slice(RET_DK * h, RET_DK * (h + 1))
            y_ref[:, sl] = yn * _silu(gt[:, sl])

    blk = lambda col: pl.BlockSpec((CH, D), lambda c: (c, col))
    tab = pl.BlockSpec((CH, CH), lambda c: (c, 0))
    return pl.pallas_call(
        body, name=name,
        out_shape=[jax.ShapeDtypeStruct((t, D), F32), jax.ShapeDtypeStruct((nc, RET_HEADS, RET_DK, RET_DK), F32)], grid=(nc,),
        in_specs=[blk(6), blk(7), blk(8), blk(9), tab, tab],
        out_specs=[pl.BlockSpec((CH, D), lambda c: (c, 0)), pl.BlockSpec((1, RET_HEADS, RET_DK, RET_DK), lambda c: (c, 0, 0, 0))],
        scratch_shapes=[pltpu.VMEM((RET_HEADS, RET_DK, RET_DK), F32)], compiler_params=_cparams(1))(proj, proj, proj, proj, cos, sin)


def ret_bwd(proj, cos, sin, rs, dyc, *, name):
    t = proj.shape[0]
    nc = t // CH

    def body(q_ref, k_ref, v_ref, g_ref, cos_ref, sin_ref, rs_ref, dy_ref, o_ref, dr_scr):
        step = pl.program_id(0)
        c = nc - 1 - step

        @pl.when(step == 0)
        def _():
            dr_scr[...] = jnp.zeros_like(dr_scr)

        vm = _row_valid(c, CH, CH)
        q, k, v, gt = q_ref[...], k_ref[...], v_ref[...], g_ref[...]
        cos, sin = cos_ref[...], sin_ref[...]
        dout = dy_ref[...]
        for h in range(RET_HEADS):
            sl = slice(RET_DK * h, RET_DK * (h + 1))
            r_in = rs_ref[0, h]
            p = _ret_head_fwd(h, q, k, v, cos, sin, vm, r_in)
            yn, r = _group_norm(p["y"])
            gh = gt[:, sl]
            do = dout[:, sl]
            dg = do * yn * _dsilu(gh)
            dyn = do * _silu(gh)
            dy = r * (dyn - jnp.mean(dyn, axis=-1, keepdims=True) - yn * jnp.mean(dyn * yn, axis=-1, keepdims=True))
            dr_out = dr_scr[h]
            dyc_ = dy * p["qdec"]
            dqr = _dot_nt(dyc_, r_in)
            dr_scr[h] = dr_out * p["cdec"] + _dot_tn(p["qr"], dyc_)
            dkr = _dot_nt(p["vr"], dr_out) * p["kdec"]
            dv = _dot(p["kr"] * p["kdec"], dr_out)
            ds = _dot_nt(dy, p["vr"]) * p["dmask"]
            dqr = dqr + _dot(ds, p["kr"])
            dkr = dkr + _dot_tn(ds, p["qr"])
            dv = dv + _dot_tn(p["scores"], dy)
            o_ref[:, RET_DK * h:RET_DK * (h + 1)] = _unrope(dqr, cos, sin)
            o_ref[:, D + RET_DK * h:D + RET_DK * (h + 1)] = _unrope(dkr, cos, sin) * (RET_DK ** -0.5)
            o_ref[:, 2 * D + RET_DK * h:2 * D + RET_DK * (h + 1)] = dv * vm
            o_ref[:, 3 * D + RET_DK * h:3 * D + RET_DK * (h + 1)] = dg

    rev = lambda s: nc - 1 - s
    blk = lambda col: pl.BlockSpec((CH, D), lambda s: (rev(s), col))
    tab = pl.BlockSpec((CH, CH), lambda s: (rev(s), 0))
    return pl.pallas_call(
        body, name=name, out_shape=jax.ShapeDtypeStruct((t, 4 * D), F32), grid=(nc,),
        in_specs=[blk(6), blk(7), blk(8), blk(9), tab, tab,
                  pl.BlockSpec((1, RET_HEADS, RET_DK, RET_DK), lambda s: (rev(s), 0, 0, 0)), pl.BlockSpec((CH, D), lambda s: (rev(s), 0))],
        out_specs=pl.BlockSpec((CH, 4 * D), lambda s: (rev(s), 0)),
        scratch_shapes=[pltpu.VMEM((RET_HEADS, RET_DK, RET_DK), F32)], compiler_params=_cparams(1))(proj, proj, proj, proj, cos, sin, rs, dyc)


SB_D = 128
SB_SCALE = SB_D ** -0.5
SB_CUTOFF = 104.0


def _split_hi_lo(x):
    hi = x.astype(BF16)
    lo = (x - hi.astype(F32)).astype(BF16)
    return hi, lo


def _sum_matrix(kind):
    a, b = _iota((128, 128), 0), _iota((128, 128), 1)
    tri = ((b > a) if kind == "after" else (b < a)).astype(BF16)
    return jnp.concatenate([tri, tri], axis=1)


def _key_sums(x, mat2):
    hi, lo = _split_hi_lo(x)
    return jnp.dot(mat2, jnp.concatenate([hi, lo], axis=0), preferred_element_type=F32)


def _sb_mask(d_kq, key_idx, first_key, q_minus_k):
    return (d_kq < q_minus_k) & (key_idx >= N_PAD - first_key)


def _sb_log_sigmoid(z):
    return jnp.minimum(z, 0.0) - jnp.log(1.0 + jnp.exp(-jnp.abs(z)))


def sb_fwd(qkv, *, name):
    t = qkv.shape[0]
    tq = _tok_block(t)
    nq = t // tq
    per = tq // 128

    def body(q_ref, k_ref, v_ref, o_ref, at_ref, bmin_ref):
        h = pl.program_id(0)
        i = pl.program_id(1)
        top = (i + 1) * per - 1
        mat_after = _sum_matrix("after")
        qs = [q_ref[pl.ds(128 * r, 128), :] for r in range(per)]
        d_kq = _iota((128, 128), 0) - _iota((128, 128), 1)
        key_idx = _iota((128, 128), 0)

        def mask(r, b):
            return _sb_mask(d_kq, key_idx, b * 128, (i * per + r - b) * 128)

        def step(carry):
            b, _, a_runs, accs = carry
            off = pl.multiple_of(b * 128, 128)
            kb = k_ref[pl.ds(off, 128), :]
            vb = v_ref[pl.ds(off, 128), :]
            tiles = range(per)
            zs = [_dot_nt(kb, qs[r]) * SB_SCALE for r in tiles]
            ms = [mask(r, b) for r in tiles]
            lss = [_sb_log_sigmoid(zs[r]) for r in tiles]
            lnegs = [jnp.where(ms[r], lss[r] - zs[r], 0.0) for r in tiles]
            sufs = [_key_sums(lnegs[r], mat_after) for r in tiles]
            ws = [jnp.where(ms[r], jnp.exp(lss[r] + a_runs[r] + sufs[r]), 0.0) for r in tiles]
            a_new = [a_runs[r] + sufs[r][0:1, :] + lnegs[r][0:1, :] for r in tiles]
            acc_new = [accs[r] + _dot_tn(ws[r], vb) for r in tiles]
            a_max = jnp.max(functools.reduce(jnp.maximum, a_new))
            return b - 1, a_max >= -SB_CUTOFF, tuple(a_new), tuple(acc_new)

        zeros = tuple(qs[r].astype(F32) * 0.0 for r in range(per))
        zrow = tuple(z[0:1, :] for z in zeros)
        b_end, _, a_runs, accs = lax.while_loop(lambda c: jnp.logical_and(c[0] >= 0, c[1]), step, (top, top >= 0, zrow, zeros))
        bmin_ref[h, i] = b_end + 1
        at_ref[...] = jnp.zeros_like(at_ref)
        for r in range(per):
            o_ref[pl.ds(128 * r, 128), :] = accs[r]
            at_ref[0, 0, r:r + 1, :] = a_runs[r]

    blk = pl.BlockSpec((tq, 128), lambda h, i: (i, h))
    return pl.pallas_call(
        body, name=name,
        out_shape=[jax.ShapeDtypeStruct((t, D), F32), jax.ShapeDtypeStruct((SB_HEADS, nq, 8, 128), F32),
                   jax.ShapeDtypeStruct((SB_HEADS, nq), jnp.int32)],
        grid=(SB_HEADS, nq),
        in_specs=[blk, pl.BlockSpec((t, 128), lambda h, i: (0, SB_HEADS + h)),
                  pl.BlockSpec((t, 128), lambda h, i: (0, 2 * SB_HEADS + h))],
        out_specs=[blk, pl.BlockSpec((1, 1, 8, 128), lambda h, i: (h, i, 0, 0)), pl.BlockSpec(memory_space=pltpu.SMEM)],
        compiler_params=_cparams(2))(qkv, qkv, qkv)


def sb_bwd(qkv, atot, bmin, dout, *, name):
    t = qkv.shape[0]
    tq = _tok_block(t)
    nq = t // tq
    per = tq // 128

    nblk = t // 128

    def body(bmin_ref, q_ref, k_ref, v_ref, at_ref, do_ref, dq_ref, dk_ref, dv_ref):
        i = pl.program_id(1)
        top = (i + 1) * per - 1
        b_first = bmin_ref[pl.program_id(0), i]

        @pl.when(i == 0)
        def _():
            dk_ref[...] = jnp.zeros_like(dk_ref)
            dv_ref[...] = jnp.zeros_like(dv_ref)

        qs = [q_ref[pl.ds(128 * r, 128), :] for r in range(per)]
        dos = [do_ref[pl.ds(128 * r, 128), :].astype(BF16) for r in range(per)]
        q_all = q_ref[...]
        do_all = do_ref[...].astype(BF16)
        a_tots = [at_ref[0, 0, r:r + 1, :] for r in range(per)]
        mat_after = _sum_matrix("after")
        mat_before = _sum_matrix("before")
        d_kq = _iota((128, 128), 0) - _iota((128, 128), 1)
        key_idx = _iota((128, 128), 0)

        def offset(b):
            return pl.multiple_of(jnp.clip(b, 0, nblk - 1) * 128, 128)

        def mask(r, b):
            return _sb_mask(d_kq, key_idx, b * 128, (i * per + r - b) * 128)

        def step(b, carry):
            lss, dws, sufs, tots, p_runs, e_runs, dqs = carry
            off = offset(b)
            kb = k_ref[pl.ds(off, 128), :]
            vb = v_ref[pl.ds(off, 128), :]
            zs = [_dot_nt(kb, qs[r]) * SB_SCALE for r in range(per)]
            dw_new = tuple(_dot_nt(vb, dos[r]) for r in range(per))
            off1 = offset(b - 1)
            k1 = k_ref[pl.ds(off1, 128), :]
            ws, es, epres, sigs, p_new = [], [], [], [], []
            for r in range(per):
                p = p_runs[r] + tots[r]
                w = jnp.where(mask(r, b - 1), jnp.exp(lss[r] + (a_tots[r] - p) + sufs[r]), 0.0)
                e = w * dws[r]
                p_new.append(p)
                ws.append(w.astype(BF16))
                es.append(e)
                epres.append(_key_sums(e, mat_before))
                sigs.append(jnp.exp(lss[r]))
            dv_ref[pl.ds(off1, 128), :] += _dot(jnp.concatenate(ws, axis=1), do_all)
            ls_new, suf_new, tot_new = [], [], []
            for r in range(per):
                ls = _sb_log_sigmoid(zs[r])
                lneg = jnp.where(mask(r, b), ls - zs[r], 0.0)
                suf = _key_sums(lneg, mat_after)
                ls_new.append(ls)
                suf_new.append(suf)
                tot_new.append(suf[0:1, :] + lneg[0:1, :])
            dzs, e_new, dq_new = [], [], []
            for r in range(per):
                t2 = jnp.where(mask(r, b - 1), (e_runs[r] + epres[r]) * sigs[r], 0.0)
                dz = ((es[r] * (1.0 - sigs[r]) - t2) * SB_SCALE).astype(BF16)
                e_new.append(e_runs[r] + epres[r][127:128, :] + es[r][127:128, :])
                dq_new.append(dqs[r] + _dot_tn(dz, k1))
                dzs.append(dz)
            dk_ref[pl.ds(off1, 128), :] += _dot(jnp.concatenate(dzs, axis=1), q_all)
            return tuple(ls_new), dw_new, tuple(suf_new), tuple(tot_new), tuple(p_new), tuple(e_new), tuple(dq_new)

        zeros = tuple(qs[r].astype(F32) * 0.0 for r in range(per))
        zrow = tuple(z[0:1, :] for z in zeros)
        carry = lax.fori_loop(b_first, top + 2, step, (zeros, zeros, zeros, zrow, zrow, zrow, zeros))
        for r in range(per):
            dq_ref[pl.ds(128 * r, 128), :] = carry[6][r]

    head_blk = pl.BlockSpec((t, 128), lambda h, i: (0, h))
    return pl.pallas_call(
        body, name=name, out_shape=[jax.ShapeDtypeStruct((t, D), F32)] * 3, grid=(SB_HEADS, nq),
        in_specs=[pl.BlockSpec(memory_space=pltpu.SMEM),
                  pl.BlockSpec((tq, 128), lambda h, i: (i, h)), pl.BlockSpec((t, 128), lambda h, i: (0, SB_HEADS + h)),
                  pl.BlockSpec((t, 128), lambda h, i: (0, 2 * SB_HEADS + h)),
                  pl.BlockSpec((1, 1, 8, 128), lambda h, i: (h, i, 0, 0)), pl.BlockSpec((tq, 128), lambda h, i: (i, h))],
        out_specs=[pl.BlockSpec((tq, 128), lambda h, i: (i, h)), head_blk, head_blk],
        compiler_params=_cparams(2, 60 * 1024 * 1024))(bmin, qkv, qkv, qkv, atot, dout)


def branch_fwd(y, proj, w, n, *, name, add=None):
    t = y.shape[0]
    tb = _tok_block(t)
    has_add = add is not None

    def body(*refs):
        if has_add:
            y_ref, g_ref, w_ref, a_ref, o_ref = refs
        else:
            y_ref, g_ref, w_ref, o_ref = refs
        r = _sigmoid(g_ref[...]) * _dot(y_ref[...], w_ref[...])
        if has_add:
            r = r + a_ref[...]
        o_ref[...] = r

    blk = pl.BlockSpec((tb, D), lambda i: (i, 0))
    in_specs = [blk, pl.BlockSpec((tb, D), lambda i: (i, COL_GATE + n)), pl.BlockSpec((D, D), lambda i: (0, 0))] + ([blk] if has_add else [])
    args = [y, proj, w] + ([add] if has_add else [])
    return pl.pallas_call(body, name=name, out_shape=jax.ShapeDtypeStruct((t, D), F32), grid=(t // tb,),
                          in_specs=in_specs, out_specs=blk, compiler_params=_cparams(1))(*args)


def branch_bwd(y, proj, w, wt, n, dmerged, *, name):
    t = y.shape[0]
    tb = _tok_block(t)

    def body(y_ref, g_ref, w_ref, wt_ref, dm_ref, dg_ref, dup_ref, dy_ref):
        up = _dot(y_ref[...], w_ref[...])
        gate = _sigmoid(g_ref[...])
        dm = dm_ref[...]
        dg_ref[...] = dm * up * gate * (1.0 - gate)
        dup = (dm * gate).astype(BF16)
        dup_ref[...] = dup
        dy_ref[...] = _dot(dup, wt_ref[...])

    blk = pl.BlockSpec((tb, D), lambda i: (i, 0))
    wspec = pl.BlockSpec((D, D), lambda i: (0, 0))
    return pl.pallas_call(
        body, name=name,
        out_shape=[jax.ShapeDtypeStruct((t, D), F32), jax.ShapeDtypeStruct((t, D), BF16), jax.ShapeDtypeStruct((t, D), F32)],
        grid=(t // tb,),
        in_specs=[blk, pl.BlockSpec((tb, D), lambda i: (i, COL_GATE + n)), wspec, wspec, blk],
        out_specs=[blk, blk, blk], compiler_params=_cparams(1))(y, proj, w, wt, dmerged)


def swiglu_fwd(f, *, name):
    t = f.shape[0]
    tb = _tok_block(t)

    def body(g_ref, u_ref, o_ref):
        o_ref[...] = (_silu(g_ref[...]) * u_ref[...]).astype(BF16)

    return pl.pallas_call(body, name=name, out_shape=jax.ShapeDtypeStruct((t, D_FF), BF16), grid=(t // tb,),
                          in_specs=[pl.BlockSpec((tb, D_FF), lambda i: (i, 0)), pl.BlockSpec((tb, D_FF), lambda i: (i, 1))],
                          out_specs=pl.BlockSpec((tb, D_FF), lambda i: (i, 0)), compiler_params=_cparams(1))(f, f)


def swiglu_bwd(f, dact, *, name):
    t = f.shape[0]
    tb = _tok_block(t)

    def body(g_ref, u_ref, d_ref, o_ref):
        g, u, d = g_ref[...], u_ref[...], d_ref[...]
        o_ref[:, 0:D_FF] = d * u * _dsilu(g)
        o_ref[:, D_FF:2 * D_FF] = d * _silu(g)

    return pl.pallas_call(body, name=name, out_shape=jax.ShapeDtypeStruct((t, 2 * D_FF), F32), grid=(t // tb,),
                          in_specs=[pl.BlockSpec((tb, D_FF), lambda i: (i, 0)), pl.BlockSpec((tb, D_FF), lambda i: (i, 1)),
                                    pl.BlockSpec((tb, D_FF), lambda i: (i, 0))],
                          out_specs=pl.BlockSpec((tb, 2 * D_FF), lambda i: (i, 0)), compiler_params=_cparams(1))(f, f, dact)


def loss_head(h, target, *, name):
    t = h.shape[0]
    nb = t // CH

    def body(h_ref, t_ref, l_ref, d_ref):
        i = pl.program_id(0)

        @pl.when(i == 0)
        def _():
            l_ref[...] = jnp.zeros_like(l_ref)
            d_ref[...] = jnp.zeros_like(d_ref)

        @pl.when(i > 0)
        def _():
            err = h_ref[...] - t_ref[...]
            d_ref[...] = err * (1.0 / D)
            l_ref[...] += jnp.sum(err * err) * (0.5 / D)

    return pl.pallas_call(
        body, name=name, out_shape=[jax.ShapeDtypeStruct((8, 128), F32), jax.ShapeDtypeStruct((t, D), F32)], grid=(nb,),
        in_specs=[pl.BlockSpec((CH, D), lambda i: (i, 0)), pl.BlockSpec((CH, D), lambda i: (jnp.maximum(i - 1, 0), 0))],
        out_specs=[pl.BlockSpec((8, 128), lambda i: (0, 0)), pl.BlockSpec((CH, D), lambda i: (i, 0))],
        compiler_params=_cparams(1))(h, target)


def _pad_rows8(w):
    return jnp.concatenate([w, jnp.zeros((8 - w.shape[0], w.shape[1]), w.dtype)], axis=0)


def _pad_lanes(v, n=CH):
    return jnp.concatenate([v, jnp.zeros((n - v.shape[0],), v.dtype)])[None, :]


def prep_layer(p):
    w = p["w_in"]
    zeros = jnp.zeros((D, CH - SSD_HEADS), w.dtype)
    w_f = jnp.concatenate([w[:, :6144], w[:, 6160:10256], w[:, 13328:17424]], axis=1)
    w_dt = jnp.concatenate([w[:, 6144:6160], zeros], axis=1)
    w_sb = w[:, 10256:13328]
    return dict(
        w_f=w_f, w_sb=w_sb, w_dt=w_dt, w_f_t=w_f.T, w_sb_t=w_sb.T, w_dt_t=w_dt.T,
        w_br=p["w_branch"], w_br_t=jnp.swapaxes(p["w_branch"], 1, 2), w_out=p["w_out"], w_out_t=p["w_out"].T,
        w_fi=p["w_ffn_in"], w_fi_t=p["w_ffn_in"].T, w_fo=p["w_ffn_out"], w_fo_t=p["w_ffn_out"].T,
        conv_a8=_pad_rows8(p["conv_a"]), conv_s8=_pad_rows8(p["ssd_conv_w"]), conv_sb=p["ssd_conv_b"][None, :],
        dtb=_pad_lanes(p["ssd_dt_bias"]), alog=_pad_lanes(p["ssd_a_log"]), dsk=_pad_lanes(p["ssd_d"]), nw=p["ssd_norm"][None, :],
        n1=p["norm_mix_pre"][None, :], n2=p["norm_mix_post"][None, :], n3=p["norm_ffn_pre"][None, :], n4=p["norm_ffn_post"][None, :])


def layer_fwd(h0, w, cos, sin, l):
    nm = lambda s: f"l{l}_{s}"
    hn = rms_fwd(h0, w["n1"], name=nm("rms1"), out_dtype=BF16)
    proj = mm(hn, w["w_f"], name=nm("proj_f"))
    qkv = mm(hn, w["w_sb"], name=nm("proj_sb"), out_dtype=BF16)
    pdt = mm(hn, w["w_dt"], name=nm("proj_dt"))
    y_a = conv_a_fwd(proj, w["conv_a8"], name=nm("conv_a"))
    xa = ssd_conv_fwd(proj, w["conv_s8"], w["conv_sb"], name=nm("ssd_conv"))
    y_b, hs = ssd_fwd(xa, proj, pdt, w["dtb"], w["alog"], w["dsk"], w["nw"], name=nm("ssd"))
    y_c, rs = ret_fwd(proj, cos, sin, name=nm("ret"))
    y_d, sb_atot, sb_bmin = sb_fwd(qkv, name=nm("sb"))
    ys = (y_a, y_b, y_c, y_d)
    merged = None
    for n in range(4):
        merged = branch_fwd(ys[n], proj, w["w_br"][n], n, name=nm(f"branch{n}"), add=merged)
    mix = mm(merged, w["w_out"], name=nm("mix"))
    h1 = rms_fwd(mix, w["n2"], name=nm("rms2"), res=h0)
    hn2 = rms_fwd(h1, w["n3"], name=nm("rms3"), out_dtype=BF16)
    f = mm(hn2, w["w_fi"], name=nm("ffn_in"))
    act = swiglu_fwd(f, name=nm("swiglu"))
    f2 = mm(act, w["w_fo"], name=nm("ffn_out"))
    h2 = rms_fwd(f2, w["n4"], name=nm("rms4"), res=h1)
    saved = dict(h0=h0, hn=hn, proj=proj, qkv=qkv, pdt=pdt, xa=xa, hs=hs, rs=rs, ys=ys, sb_atot=sb_atot, sb_bmin=sb_bmin, merged=merged, mix=mix, h1=h1, hn2=hn2,
                 f=f, act=act, f2=f2)
    return h2, saved


def layer_bwd(dh2, s, w, cos, sin, l):
    nm = lambda t: f"l{l}_{t}"
    g = {}
    df2, g["n4"] = rms_bwd(s["f2"], w["n4"], dh2, name=nm("rms4_b"))
    g["w_fo"] = mm_tn(s["act"], df2, name=nm("ffn_out_dw"))
    dact = mm(df2, w["w_fo_t"], name=nm("ffn_out_dx"), out_dtype=BF16)
    df = swiglu_bwd(s["f"], dact, name=nm("swiglu_b"))
    g["w_fi"] = mm_tn(s["hn2"], df, name=nm("ffn_in_dw"))
    dhn2 = mm(df, w["w_fi_t"], name=nm("ffn_in_dx"))
    dh1, g["n3"] = rms_bwd(s["h1"], w["n3"], dhn2, name=nm("rms3_b"), add=dh2)
    dmix, g["n2"] = rms_bwd(s["mix"], w["n2"], dh1, name=nm("rms2_b"))
    g["w_out"] = mm_tn(s["merged"], dmix, name=nm("mix_dw"))
    dmerged = mm(dmix, w["w_out_t"], name=nm("mix_dx"))
    dgate, dys, dwb = [], [], []
    for n in range(4):
        dg_n, dup_n, dy_n = branch_bwd(s["ys"][n], s["proj"], w["w_br"][n], w["w_br_t"][n], n, dmerged, name=nm(f"branch{n}_b"))
        dgate.append(dg_n)
        dys.append(dy_n)
        dwb.append(mm_tn(s["ys"][n], dup_n, name=nm(f"branch{n}_dw")))
    g["w_br"] = jnp.stack(dwb)
    d_a, g["conv_a8"] = conv_a_bwd(s["proj"], w["conv_a8"], dys[0], name=nm("conv_a_b"))
    dz, dxa, ddt, g["dtb"], g["alog"], g["dsk"], g["nw"] = ssd_bwd(
        s["xa"], s["proj"], s["pdt"], s["hs"], dys[1], w["dtb"], w["alog"], w["dsk"], w["nw"], name=nm("ssd_b"))
    dpre, g["conv_s8"], g["conv_sb"] = ssd_conv_bwd_pre(s["proj"], w["conv_s8"], w["conv_sb"], dxa, name=nm("ssd_conv_b1"))
    dxbc = ssd_conv_bwd_in(dpre, w["conv_s8"], name=nm("ssd_conv_b2"))
    d_r = ret_bwd(s["proj"], cos, sin, s["rs"], dys[2], name=nm("ret_b"))
    d_sb = sb_bwd(s["qkv"], s["sb_atot"], s["sb_bmin"], dys[3], name=nm("sb_b"))
    segs = [(d_a, 0), (dz, 3072), (dxbc, 4096), (d_r, 6144), (dgate[0], 10240), (dgate[1], 11264), (dgate[2], 12288),
            (dgate[3], 13312)]
    dws = [mm_tn(s["hn"], d, name=nm(f"proj_dw{k}")) for k, (d, _) in enumerate(segs)]
    g["w_f"] = jnp.concatenate(dws, axis=1)
    g["w_sb"] = jnp.concatenate([mm_tn(s["hn"], d, name=nm(f"proj_dw_sb{k}")) for k, d in enumerate(d_sb)], axis=1)
    g["w_dt"] = mm_tn(s["hn"], ddt, name=nm("proj_dw_dt"))
    dhn = mm(ddt, w["w_dt_t"], name=nm("proj_dx_dt"))
    for k, d in enumerate(d_sb):
        dhn = mm(d, w["w_sb_t"][k * D:(k + 1) * D], name=nm(f"proj_dx_sb{k}"), add=dhn)
    for k, (d, c0) in enumerate(segs):
        dhn = mm(d, w["w_f_t"][c0:c0 + d.shape[1]], name=nm(f"proj_dx{k}"), add=dhn)
    dh0, g["n1"] = rms_bwd(s["h0"], w["n1"], dhn, name=nm("rms1_b"), add=dh1)
    return dh0, g


def layer_grads_to_params(g):
    wf, wsb = g["w_f"], g["w_sb"]
    w_in = jnp.concatenate([wf[:, :6144], g["w_dt"][:, :SSD_HEADS], wf[:, 6144:10240], wsb, wf[:, 10240:14336]], axis=1)
    return dict(
        w_in=w_in, conv_a=g["conv_a8"][:3], ssd_conv_w=g["conv_s8"][:4], ssd_conv_b=g["conv_sb"][0],
        ssd_dt_bias=g["dtb"][0, :SSD_HEADS], ssd_a_log=g["alog"][0, :SSD_HEADS], ssd_d=g["dsk"][0, :SSD_HEADS], ssd_norm=g["nw"][0],
        w_branch=g["w_br"], w_out=g["w_out"], w_ffn_in=g["w_fi"], w_ffn_out=g["w_fo"],
        norm_mix_pre=g["n1"][0], norm_mix_post=g["n2"][0], norm_ffn_pre=g["n3"][0], norm_ffn_post=g["n4"][0])


def rope_tables(t):
    half = RET_DK // 2
    inv = ROPE_BASE ** (-jnp.arange(half, dtype=F32) / half)
    ang = jnp.arange(t).astype(F32)[:, None] * inv[None, :]
    return jnp.cos(ang), jnp.sin(ang)


def local_step(x, target, meta, layers):
    h = jnp.concatenate([jnp.zeros((N_PAD, D), F32), meta, x], axis=0)
    t = h.shape[0]
    cos, sin = rope_tables(t)
    ws = [prep_layer(p) for p in layers]
    saved = []
    for l, w in enumerate(ws):
        h, s = layer_fwd(h, w, cos, sin, l)
        saved.append(s)
    loss, dh = loss_head(h, target, name="loss_head")
    grads = [None] * len(ws)
    for l in reversed(range(len(ws))):
        dh, g = layer_bwd(dh, saved[l], ws[l], cos, sin, l)
        grads[l] = layer_grads_to_params(g)
    return loss, dh[CH:], dh[N_PAD:CH], grads


def _my_place():
    return lax.axis_index("x"), lax.axis_index("y"), lax.axis_index("c")


def _flat(px, py, pc):
    return 4 * px + 2 * py + pc


ANY = pl.BlockSpec(memory_space=pl.ANY)


def all_gather(x_shard, *, name):
    shape = x_shard.shape

    def body(x_ref, out_ref, send_sems, recv_sems, local_sem):
        x, y, c = _my_place()
        me, sibling = (x, y, c), (x, y, 1 - c)
        chips = [(1 - x, y), (x, 1 - y), (1 - x, 1 - y)]

        def rows(px, py, pc):
            return out_ref.at[_flat(px, py, pc)]

        def copy(k, block, to, src=None):
            return pltpu.make_async_remote_copy(
                src_ref=rows(*block) if src is None else src, dst_ref=rows(*block),
                send_sem=send_sems.at[k], recv_sem=recv_sems.at[k], device_id=to, device_id_type=MESH_ID)

        mine = pltpu.make_async_copy(x_ref, rows(*me), local_sem)
        mine.start()
        first = [copy(0, me, sibling, src=x_ref)]
        first += [copy(1 + j, me, (*chip, c), src=x_ref) for j, chip in enumerate(chips)]
        for cp in first:
            cp.start()
        passed = [copy(4 + j, (*chip, c), sibling) for j, chip in enumerate(chips)]
        for j, chip in enumerate(chips):
            copy(1 + j, (*chip, c), me).wait_recv()
            passed[j].start()
        copy(0, sibling, me).wait_recv()
        for j, chip in enumerate(chips):
            copy(4 + j, (*chip, 1 - c), me).wait_recv()
        for cp in first + passed:
            cp.wait_send()
        mine.wait()

    return pl.pallas_call(
        body, name=name, out_shape=jax.ShapeDtypeStruct((N_DEV,) + shape, x_shard.dtype),
        in_specs=[ANY], out_specs=ANY,
        scratch_shapes=[pltpu.SemaphoreType.DMA((7,)), pltpu.SemaphoreType.DMA((7,)), pltpu.SemaphoreType.DMA],
    )(x_shard)


def all_to_all(g, *, name):
    def body(g_ref, out_ref, send_sems, recv_sems, local_sem):
        x, y, c = _my_place()
        me = _flat(x, y, c)
        mine = pltpu.make_async_copy(g_ref.at[me], out_ref.at[me], local_sem)
        mine.start()
        peers = []
        for k in range(1, N_DEV):
            px = jnp.bitwise_xor(x, (k >> 2) & 1)
            py = jnp.bitwise_xor(y, (k >> 1) & 1)
            pc = jnp.bitwise_xor(c, k & 1)
            peers.append((px, py, pc))
        sends = []
        for k, peer in enumerate(peers):
            cp = pltpu.make_async_remote_copy(
                src_ref=g_ref.at[_flat(*peer)], dst_ref=out_ref.at[me],
                send_sem=send_sems.at[k], recv_sem=recv_sems.at[k], device_id=peer, device_id_type=MESH_ID)
            cp.start()
            sends.append(cp)
        for k, peer in enumerate(peers):
            slot = out_ref.at[_flat(*peer)]
            pltpu.make_async_remote_copy(
                src_ref=slot, dst_ref=slot, send_sem=send_sems.at[k], recv_sem=recv_sems.at[k],
                device_id=peer, device_id_type=MESH_ID).wait_recv()
        for cp in sends:
            cp.wait_send()
        mine.wait()

    return pl.pallas_call(
        body, name=name, out_shape=jax.ShapeDtypeStruct(g.shape, g.dtype), in_specs=[ANY], out_specs=ANY,
        scratch_shapes=[pltpu.SemaphoreType.DMA((7,)), pltpu.SemaphoreType.DMA((7,)), pltpu.SemaphoreType.DMA],
    )(g)


def sum_slots(a, *, name):
    def body(a_ref, o_ref):
        s = a_ref[0]
        for d in range(1, N_DEV):
            s = s + a_ref[d]
        o_ref[...] = s

    return pl.pallas_call(body, name=name, out_shape=jax.ShapeDtypeStruct(a.shape[1:], a.dtype))(a)


def _adamw_math(w, g, m, v):
    m = ADAM_B1 * m + (1.0 - ADAM_B1) * g
    v = ADAM_B2 * v + (1.0 - ADAM_B2) * (g * g)
    m_hat = m / (1.0 - ADAM_B1 ** ADAM_STEP)
    v_hat = v / (1.0 - ADAM_B2 ** ADAM_STEP)
    delta = -ADAM_LR * (m_hat / (jnp.sqrt(v_hat) + ADAM_EPS) + ADAM_WD * w)
    return delta, m, v


def adamw_big(recv, w, m, v, *, name):
    r, cols = w.shape
    tb = 128

    def body(r_ref, w_ref, m_ref, v_ref, g_ref, d_ref, nm_ref, nv_ref):
        g = r_ref[0].astype(F32)
        for d in range(1, N_DEV):
            g = g + r_ref[d].astype(F32)
        g_ref[...] = g
        d_ref[...], nm_ref[...], nv_ref[...] = _adamw_math(w_ref[...], g, m_ref[...], v_ref[...])

    blk = pl.BlockSpec((tb, cols), lambda i: (i, 0))
    out = jax.ShapeDtypeStruct((r, cols), F32)
    return pl.pallas_call(
        body, name=name, out_shape=[out] * 4, grid=(r // tb,),
        in_specs=[pl.BlockSpec((N_DEV, tb, cols), lambda i: (0, i, 0)), blk, blk, blk], out_specs=[blk] * 4,
        compiler_params=_cparams(1))(recv, w, m, v)


def adamw_small(w, g, m, v, *, name):
    def body(w_ref, g_ref, m_ref, v_ref, d_ref, nm_ref, nv_ref):
        d_ref[...], nm_ref[...], nv_ref[...] = _adamw_math(w_ref[...], g_ref[...], m_ref[...], v_ref[...])

    out = jax.ShapeDtypeStruct(w.shape, F32)
    return pl.pallas_call(body, name=name, out_shape=[out] * 3)(w, g, m, v)


BIG = ("w_in", "w_branch", "w_out", "w_ffn_in", "w_ffn_out")
BIG_SHARD = {"w_in": (DEPTH, D, 2178), "w_branch": (DEPTH, 4, 128, D), "w_out": (DEPTH, 128, D),
             "w_ffn_in": (DEPTH, D, 704), "w_ffn_out": (DEPTH, 352, D)}
BIG_FULL = {"w_in": ((1, 2, 0, 3), (DEPTH, D, 17424)), "w_branch": ((1, 2, 0, 3, 4), (DEPTH, 4, D, D)),
            "w_out": ((1, 0, 2, 3), (DEPTH, D, D)), "w_ffn_in": ((1, 2, 0, 3), (DEPTH, D, 2 * D_FF)),
            "w_ffn_out": ((1, 0, 2, 3), (DEPTH, D_FF, D))}
BIG_ROWS = {n: int(np.prod(s)) // D for n, s in BIG_SHARD.items()}
BIG_R = 7808


def pack_big(shards, dtype):
    parts = [shards[n].astype(dtype).reshape(BIG_ROWS[n], D) for n in BIG]
    parts.append(jnp.zeros((BIG_R - sum(BIG_ROWS.values()), D), dtype))
    return jnp.concatenate(parts, axis=0)


def unpack_big(flat):
    out, o = {}, 0
    for n in BIG:
        out[n] = flat[o:o + BIG_ROWS[n]].reshape(BIG_SHARD[n])
        o += BIG_ROWS[n]
    return out


def unpack_big_full(gathered):
    out, o = {}, 0
    for n in BIG:
        perm, full = BIG_FULL[n]
        out[n] = gathered[:, o:o + BIG_ROWS[n]].reshape((N_DEV,) + BIG_SHARD[n]).transpose(perm).reshape(full)
        o += BIG_ROWS[n]
    return out


def pack_big_full(full, dtype):
    parts = []
    for n in BIG:
        perm, _ = BIG_FULL[n]
        split = tuple(int(v) for v in np.array((N_DEV,) + BIG_SHARD[n])[list(perm)])
        inv = tuple(int(i) for i in np.argsort(perm))
        parts.append(full[n].astype(dtype).reshape(split).transpose(inv).reshape(N_DEV, BIG_ROWS[n], D))
    parts.append(jnp.zeros((N_DEV, BIG_R - sum(BIG_ROWS.values()), D), dtype))
    return jnp.concatenate(parts, axis=1)


def _rows128(a):
    a = a.reshape(-1)
    pad = (-a.shape[0]) % CH
    if pad:
        a = jnp.concatenate([a, jnp.zeros((pad,), a.dtype)])
    return a.reshape(-1, CH)


def _pack_rows(arrs, total):
    parts = [_rows128(a) for a in arrs]
    n = sum(p.shape[0] for p in parts)
    parts.append(jnp.zeros((total - n, CH), F32))
    return jnp.concatenate(parts, axis=0)


def _unpack_rows(flat, shapes):
    out, o = [], 0
    for s in shapes:
        size = int(np.prod(s))
        rows = -(-size // CH)
        out.append(flat[o:o + rows].reshape(-1)[:size].reshape(s))
        o += rows
    return out


SMALL_SHARDED = ("meta", "conv_a", "ssd_conv_w")
SMALL_SHARD_SHAPE = {"meta": (N_META, 128), "conv_a": (DEPTH, 3, 128), "ssd_conv_w": (DEPTH, 4, 256)}
SMALL_FULL_SHAPE = {"meta": (N_META, D), "conv_a": (DEPTH, 3, D), "ssd_conv_w": (DEPTH, 4, 2048)}
SMALL_REPL = ("ssd_conv_b", "ssd_dt_bias", "ssd_a_log", "ssd_d", "ssd_norm", "norm_mix_pre", "norm_mix_post", "norm_ffn_pre",
              "norm_ffn_post")
SMALL_REPL_SHAPE = {"ssd_conv_b": (DEPTH, 2048), "ssd_dt_bias": (DEPTH, SSD_HEADS), "ssd_a_log": (DEPTH, SSD_HEADS),
                    "ssd_d": (DEPTH, SSD_HEADS), "ssd_norm": (DEPTH, D), "norm_mix_pre": (DEPTH, D), "norm_mix_post": (DEPTH, D),
                    "norm_ffn_pre": (DEPTH, D), "norm_ffn_post": (DEPTH, D)}


def _gather_small_full(gathered, n):
    nd = gathered.ndim
    perm = tuple(range(1, nd - 1)) + (0, nd - 1)
    return gathered.transpose(perm).reshape(SMALL_FULL_SHAPE[n])


WEIGHTS = ("meta", "w_in", "conv_a", "ssd_conv_w", "ssd_conv_b", "ssd_dt_bias", "ssd_a_log", "ssd_d", "ssd_norm", "w_branch", "w_out",
           "w_ffn_in", "w_ffn_out", "norm_mix_pre", "norm_mix_post", "norm_ffn_pre", "norm_ffn_post")


def kernel(x, meta, w_in, conv_a, ssd_conv_w, ssd_conv_b, ssd_dt_bias, ssd_a_log, ssd_d, ssd_norm, w_branch, w_out, w_ffn_in, w_ffn_out, norm_mix_pre, norm_mix_post, norm_ffn_pre, norm_ffn_post, loss_target, m_meta, m_w_in, m_conv_a, m_ssd_conv_w, m_ssd_conv_b, m_ssd_dt_bias, m_ssd_a_log, m_ssd_d, m_ssd_norm, m_w_branch, m_w_out, m_w_ffn_in, m_w_ffn_out, m_norm_mix_pre, m_norm_mix_post, m_norm_ffn_pre, m_norm_ffn_post, v_meta, v_w_in, v_conv_a, v_ssd_conv_w, v_ssd_conv_b, v_ssd_dt_bias, v_ssd_a_log, v_ssd_d, v_ssd_norm, v_w_branch, v_w_out, v_w_ffn_in, v_w_ffn_out, v_norm_mix_pre, v_norm_mix_post, v_norm_ffn_pre, v_norm_ffn_post):
    w = dict(meta=meta, w_in=w_in, conv_a=conv_a, ssd_conv_w=ssd_conv_w, ssd_conv_b=ssd_conv_b, ssd_dt_bias=ssd_dt_bias,
             ssd_a_log=ssd_a_log, ssd_d=ssd_d, ssd_norm=ssd_norm, w_branch=w_branch, w_out=w_out, w_ffn_in=w_ffn_in,
             w_ffn_out=w_ffn_out, norm_mix_pre=norm_mix_pre, norm_mix_post=norm_mix_post, norm_ffn_pre=norm_ffn_pre,
             norm_ffn_post=norm_ffn_post)
    m = dict(meta=m_meta, w_in=m_w_in, conv_a=m_conv_a, ssd_conv_w=m_ssd_conv_w, ssd_conv_b=m_ssd_conv_b, ssd_dt_bias=m_ssd_dt_bias,
             ssd_a_log=m_ssd_a_log, ssd_d=m_ssd_d, ssd_norm=m_ssd_norm, w_branch=m_w_branch, w_out=m_w_out, w_ffn_in=m_w_ffn_in,
             w_ffn_out=m_w_ffn_out, norm_mix_pre=m_norm_mix_pre, norm_mix_post=m_norm_mix_post, norm_ffn_pre=m_norm_ffn_pre,
             norm_ffn_post=m_norm_ffn_post)
    v = dict(meta=v_meta, w_in=v_w_in, conv_a=v_conv_a, ssd_conv_w=v_ssd_conv_w, ssd_conv_b=v_ssd_conv_b, ssd_dt_bias=v_ssd_dt_bias,
             ssd_a_log=v_ssd_a_log, ssd_d=v_ssd_d, ssd_norm=v_ssd_norm, w_branch=v_w_branch, w_out=v_w_out, w_ffn_in=v_w_ffn_in,
             w_ffn_out=v_w_ffn_out, norm_mix_pre=v_norm_mix_pre, norm_mix_post=v_norm_mix_post, norm_ffn_pre=v_norm_ffn_pre,
             norm_ffn_post=v_norm_ffn_post)
    xi, yi, ci = _my_place()
    dev = _flat(xi, yi, ci)

    full = unpack_big_full(all_gather(pack_big(w, BF16), name="gather_big"))
    small_shard = _pack_rows([w[n] for n in SMALL_SHARDED], 40)
    small_all = all_gather(small_shard, name="gather_small")
    small_full = {}
    o = 0
    for n in SMALL_SHARDED:
        rows = int(np.prod(SMALL_SHARD_SHAPE[n])) // CH
        small_full[n] = _gather_small_full(small_all[:, o:o + rows].reshape((N_DEV,) + SMALL_SHARD_SHAPE[n]), n)
        o += rows

    layers = []
    for l in range(DEPTH):
        p = {n: full[n][l] for n in BIG}
        p["conv_a"] = small_full["conv_a"][l]
        p["ssd_conv_w"] = small_full["ssd_conv_w"][l]
        for n in SMALL_REPL:
            p[n] = w[n][l]
        layers.append(p)

    loss_blk, grad_x, gmeta, grads = local_step(x[0], loss_target[0], small_full["meta"], layers)

    gfull = {n: jnp.stack([grads[l][n] for l in range(DEPTH)]) for n in BIG}
    recv = all_to_all(pack_big_full(gfull, BF16), name="exchange_big")
    g_flat, d_flat, nm_flat, nv_flat = adamw_big(recv, pack_big(w, F32), pack_big(m, F32), pack_big(v, F32), name="adamw_big")
    out_g, out_d, out_m, out_v = unpack_big(g_flat), unpack_big(d_flat), unpack_big(nm_flat), unpack_big(nv_flat)

    small_names = SMALL_SHARDED + SMALL_REPL
    small_grads = [gmeta] + [jnp.stack([grads[l][n] for l in range(DEPTH)]) for n in small_names[1:]]
    small_shapes = [SMALL_FULL_SHAPE[n] for n in SMALL_SHARDED] + [SMALL_REPL_SHAPE[n] for n in SMALL_REPL]
    sm = _pack_rows(small_grads + [loss_blk[0:1]], 424)
    sm_sum = sum_slots(all_gather(sm, name="gather_small_grads"), name="sum_small_grads")
    summed = _unpack_rows(sm_sum, small_shapes + [(1, CH)])
    loss = summed[-1][0, 0]
    sg = dict(zip(small_names, summed[:-1]))
    for n in SMALL_SHARDED:
        width = SMALL_SHARD_SHAPE[n][-1]
        sg[n] = lax.dynamic_slice_in_dim(sg[n], dev * width, width, axis=sg[n].ndim - 1)
    pk = lambda d: _pack_rows([d[n] for n in small_names], 160)
    sd, snm, snv = adamw_small(pk(w), pk(sg), pk(m), pk(v), name="adamw_small")
    shard_shapes = [SMALL_SHARD_SHAPE[n] for n in SMALL_SHARDED] + [SMALL_REPL_SHAPE[n] for n in SMALL_REPL]
    for dst, flat in ((out_d, sd), (out_m, snm), (out_v, snv)):
        dst.update(zip(small_names, _unpack_rows(flat, shard_shapes)))
    out_g.update(sg)

    return (loss, grad_x[None], *[out_g[n] for n in WEIGHTS], *[out_d[n] for n in WEIGHTS], *[out_m[n] for n in WEIGHTS],
            *[out_v[n] for n in WEIGHTS])
```

```python
import functools
import math

import numpy as np
import jax
import jax.numpy as jnp
from jax import lax
from jax.experimental import pallas as pl
from jax.experimental.pallas import tpu as pltpu

F32, BF16 = jnp.float32, jnp.bfloat16
HI = lax.Precision.HIGHEST
MESH_ID = pl.DeviceIdType.MESH

D = 1024
CH = 128
N_META = 16
N_PAD = CH - N_META
EPS = 1e-6
N_DEV = 8
DEPTH = 2
SSD_HEADS = 16
RET_HEADS = 4
SB_HEADS = 8
D_FF = 2816
ROPE_BASE = 10000.0

NF = 14336
COL_GATE = 10

ADAM_LR, ADAM_B1, ADAM_B2, ADAM_EPS, ADAM_WD, ADAM_STEP = 0.001, 0.9, 0.999, 1e-08, 0.01, 10

VMEM_BYTES = 48 * 1024 * 1024


def _pick(n, cands):
    for c in cands:
        if n % c == 0:
            return c
    raise ValueError((n, cands))


def _tok_block(t):
    return _pick(t, (384, 128))


def _cparams(ngrid, vmem=VMEM_BYTES):
    return pltpu.CompilerParams(dimension_semantics=("arbitrary",) * ngrid, vmem_limit_bytes=vmem)


def _iota(shape, dim):
    return lax.broadcasted_iota(jnp.int32, shape, dim)


def _sigmoid(x):
    return 1.0 / (1.0 + jnp.exp(-x))


def _silu(x):
    return x * _sigmoid(x)


def _dsilu(x):
    s = _sigmoid(x)
    return s * (1.0 + x * (1.0 - s))


def _softplus(x):
    return jnp.maximum(x, 0.0) + jnp.log(1.0 + jnp.exp(-jnp.abs(x)))


def _dot(a, b):
    return jnp.dot(a.astype(BF16), b.astype(BF16), preferred_element_type=F32)


def _dot_nt(a, b):
    return lax.dot_general(a.astype(BF16), b.astype(BF16), (((1,), (1,)), ((), ())), preferred_element_type=F32)


def _dot_tn(a, b):
    return lax.dot_general(a.astype(BF16), b.astype(BF16), (((0,), (0,)), ((), ())), preferred_element_type=F32)


def _dot_hi(a, b):
    return jnp.dot(a, b, precision=HI, preferred_element_type=F32)


def mm(a, b, *, name, out_dtype=F32, add=None, tm=None, tn=None, tk=None):
    m, k = a.shape
    k2, n = b.shape
    assert k == k2
    tm = tm or _pick(m, (1376, 384, 128))
    tn = tn or _pick(n, (512, 384, 256, 128))
    tk = tk or _pick(k, (1024, 1408, 512, 384, 128))
    nk = k // tk
    has_add = add is not None

    def body(*refs):
        if has_add:
            a_ref, b_ref, c_ref, o_ref = refs[:4]
            scr = refs[4:]
        else:
            a_ref, b_ref, o_ref = refs[:3]
            c_ref = None
            scr = refs[3:]
        x = _dot(a_ref[...], b_ref[...])
        if nk == 1:
            if has_add:
                x = x + c_ref[...]
            o_ref[...] = x.astype(out_dtype)
        else:
            acc = scr[0]
            kk = pl.program_id(2)

            @pl.when(kk == 0)
            def _():
                acc[...] = x

            @pl.when(kk > 0)
            def _():
                acc[...] += x

            @pl.when(kk == nk - 1)
            def _():
                r = acc[...]
                if has_add:
                    r = r + c_ref[...]
                o_ref[...] = r.astype(out_dtype)

    in_specs = [pl.BlockSpec((tm, tk), lambda i, j, kk: (i, kk)), pl.BlockSpec((tk, tn), lambda i, j, kk: (kk, j))]
    args = [a, b]
    if has_add:
        in_specs.append(pl.BlockSpec((tm, tn), lambda i, j, kk: (i, j)))
        args.append(add)
    return pl.pallas_call(
        body, name=name, out_shape=jax.ShapeDtypeStruct((m, n), out_dtype), grid=(m // tm, n // tn, nk),
        in_specs=in_specs, out_specs=pl.BlockSpec((tm, tn), lambda i, j, kk: (i, j)),
        scratch_shapes=[pltpu.VMEM((tm, tn), F32)] if nk > 1 else [],
        compiler_params=_cparams(3))(*args)


def mm_tn(a, b, *, name, tm=None, tn=None, tk=None):
    t, m = a.shape
    t2, n = b.shape
    assert t == t2
    tm = tm or _pick(m, (1024, 1408, 512, 128))
    tn = tn or _pick(n, (512, 384, 256, 128))
    tk = tk or _pick(t, (1376, 384, 128))
    nk = t // tk

    def body(a_ref, b_ref, o_ref):
        x = _dot_tn(a_ref[...], b_ref[...])
        kk = pl.program_id(2)

        @pl.when(kk == 0)
        def _():
            o_ref[...] = x

        @pl.when(kk > 0)
        def _():
            o_ref[...] += x

    return pl.pallas_call(
        body, name=name, out_shape=jax.ShapeDtypeStruct((m, n), F32), grid=(m // tm, n // tn, nk),
        in_specs=[pl.BlockSpec((tk, tm), lambda i, j, kk: (kk, i)), pl.BlockSpec((tk, tn), lambda i, j, kk: (kk, j))],
        out_specs=pl.BlockSpec((tm, tn), lambda i, j, kk: (i, j)),
        compiler_params=_cparams(3))(a, b)


def rms_fwd(x, w, *, name, out_dtype=F32, res=None):
    t, d = x.shape
    tb = _tok_block(t)
    has_res = res is not None

    def body(*refs):
        if has_res:
            x_ref, w_ref, r_ref, o_ref = refs
        else:
            x_ref, w_ref, o_ref = refs
        xv = x_ref[...]
        y = xv * lax.rsqrt(jnp.mean(xv * xv, axis=-1, keepdims=True) + EPS) * w_ref[...]
        if has_res:
            y = y + r_ref[...]
        o_ref[...] = y.astype(out_dtype)

    blk = pl.BlockSpec((tb, d), lambda i: (i, 0))
    wspec = pl.BlockSpec((1, d), lambda i: (0, 0))
    in_specs = [blk, wspec] + ([blk] if has_res else [])
    args = [x, w] + ([res] if has_res else [])
    return pl.pallas_call(body, name=name, out_shape=jax.ShapeDtypeStruct((t, d), out_dtype), grid=(t // tb,),
                          in_specs=in_specs, out_specs=blk, compiler_params=_cparams(1))(*args)


def rms_bwd(x, w, dy, *, name, add=None):
    t, d = x.shape
    tb = _tok_block(t)
    has_add = add is not None

    def body(*refs):
        if has_add:
            x_ref, w_ref, dy_ref, a_ref, dx_ref, dw_ref = refs
        else:
            x_ref, w_ref, dy_ref, dx_ref, dw_ref = refs
        xv = x_ref[...]
        dyv = dy_ref[...]
        r = lax.rsqrt(jnp.mean(xv * xv, axis=-1, keepdims=True) + EPS)
        g = dyv * w_ref[...]
        dx = r * g - xv * (r * r * r) * jnp.mean(xv * g, axis=-1, keepdims=True)
        if has_add:
            dx = dx + a_ref[...]
        dx_ref[...] = dx
        part = jnp.sum(dyv * xv * r, axis=0, keepdims=True)

        @pl.when(pl.program_id(0) == 0)
        def _():
            dw_ref[...] = part

        @pl.when(pl.program_id(0) > 0)
        def _():
            dw_ref[...] += part

    blk = pl.BlockSpec((tb, d), lambda i: (i, 0))
    wspec = pl.BlockSpec((1, d), lambda i: (0, 0))
    in_specs = [blk, wspec, blk] + ([blk] if has_add else [])
    args = [x, w, dy] + ([add] if has_add else [])
    return pl.pallas_call(body, name=name,
                          out_shape=[jax.ShapeDtypeStruct((t, d), F32), jax.ShapeDtypeStruct((1, d), F32)],
                          grid=(t // tb,), in_specs=in_specs, out_specs=[blk, wspec], compiler_params=_cparams(1))(*args)


def _shift_down(cur, prev8, k):
    z = jnp.concatenate([prev8, cur], axis=0)
    return pltpu.roll(z, k, 0)[8:]


def _shift_up(cur, next8, k):
    n = cur.shape[0] + 8
    z = jnp.concatenate([cur, next8], axis=0)
    return pltpu.roll(z, n - k, 0)[:cur.shape[0]]


def _prev8_spec(tb, width, col):
    return pl.BlockSpec((8, width), lambda i: (jnp.maximum(i * (tb // 8) - 1, 0), col))


def _next8_spec(tb, width, col, t):
    return pl.BlockSpec((8, width), lambda i: (jnp.minimum((i + 1) * (tb // 8), t // 8 - 1), col))


def _row_valid(i, tb, n, offset=0):
    rows = i * tb + offset + _iota((n, 1), 0)
    return (rows >= N_PAD).astype(F32)


def conv_a_fwd(proj, w8, *, name):
    t = proj.shape[0]
    tb = _tok_block(t)

    def body(b_ref, c_ref, x_ref, cp_ref, xp_ref, w_ref, o_ref):
        i = pl.program_id(0)
        u = c_ref[...] * x_ref[...] * _row_valid(i, tb, tb)
        up = cp_ref[...] * xp_ref[...] * _row_valid(i, tb, 8, -8) * (i > 0).astype(F32)
        w = w_ref[...]
        conv = w[2:3] * u + w[1:2] * _shift_down(u, up, 1) + w[0:1] * _shift_down(u, up, 2)
        o_ref[...] = b_ref[...] * conv

    blk = lambda col: pl.BlockSpec((tb, D), lambda i: (i, col))
    return pl.pallas_call(
        body, name=name, out_shape=jax.ShapeDtypeStruct((t, D), F32), grid=(t // tb,),
        in_specs=[blk(0), blk(1), blk(2), _prev8_spec(tb, D, 1), _prev8_spec(tb, D, 2), pl.BlockSpec((8, D), lambda i: (0, 0))],
        out_specs=pl.BlockSpec((tb, D), lambda i: (i, 0)), compiler_params=_cparams(1))(proj, proj, proj, proj, proj, w8)


def conv_a_bwd(proj, w8, dy, *, name):
    t = proj.shape[0]
    tb = _tok_block(t)
    nblk = t // tb

    def body(b_ref, c_ref, x_ref, cp_ref, xp_ref, dy_ref, dyn_ref, bn_ref, w_ref, o_ref, dw_ref):
        i = pl.program_id(0)
        vm = _row_valid(i, tb, tb)
        cv, xv, bv, dyv = c_ref[...], x_ref[...], b_ref[...], dy_ref[...]
        u = cv * xv * vm
        up = cp_ref[...] * xp_ref[...] * _row_valid(i, tb, 8, -8) * (i > 0).astype(F32)
        w = w_ref[...]
        u1 = _shift_down(u, up, 1)
        u2 = _shift_down(u, up, 2)
        conv = w[2:3] * u + w[1:2] * u1 + w[0:1] * u2
        dconv = dyv * bv
        dconv_n = dyn_ref[...] * bn_ref[...] * (i < nblk - 1).astype(F32)
        du = w[2:3] * dconv + w[1:2] * _shift_up(dconv, dconv_n, 1) + w[0:1] * _shift_up(dconv, dconv_n, 2)
        o_ref[:, 0:D] = dyv * conv
        o_ref[:, D:2 * D] = du * xv * vm
        o_ref[:, 2 * D:3 * D] = du * cv * vm

        @pl.when(i == 0)
        def _():
            dw_ref[...] = jnp.zeros_like(dw_ref)

        dw_ref[0:1, :] += jnp.sum(dconv * u2, axis=0, keepdims=True)
        dw_ref[1:2, :] += jnp.sum(dconv * u1, axis=0, keepdims=True)
        dw_ref[2:3, :] += jnp.sum(dconv * u, axis=0, keepdims=True)

    blk = lambda col: pl.BlockSpec((tb, D), lambda i: (i, col))
    w8spec = pl.BlockSpec((8, D), lambda i: (0, 0))
    return pl.pallas_call(
        body, name=name,
        out_shape=[jax.ShapeDtypeStruct((t, 3 * D), F32), jax.ShapeDtypeStruct((8, D), F32)], grid=(nblk,),
        in_specs=[blk(0), blk(1), blk(2), _prev8_spec(tb, D, 1), _prev8_spec(tb, D, 2), blk(0),
                  _next8_spec(tb, D, 0, t), _next8_spec(tb, D, 0, t), w8spec],
        out_specs=[pl.BlockSpec((tb, 3 * D), lambda i: (i, 0)), w8spec],
        compiler_params=_cparams(1))(proj, proj, proj, proj, proj, dy, dy, proj, w8)


XBC_W = 2048


def ssd_conv_fwd(proj, w8, b, *, name):
    t = proj.shape[0]
    tb = _tok_block(t)

    def body(x_ref, xp_ref, w_ref, b_ref, o_ref):
        i = pl.program_id(0)
        xm = x_ref[...] * _row_valid(i, tb, tb)
        xmp = xp_ref[...] * _row_valid(i, tb, 8, -8) * (i > 0).astype(F32)
        w = w_ref[...]
        c = w[3:4] * xm + w[2:3] * _shift_down(xm, xmp, 1) + w[1:2] * _shift_down(xm, xmp, 2) + w[0:1] * _shift_down(xm, xmp, 3)
        o_ref[...] = _silu(c + b_ref[...])

    return pl.pallas_call(
        body, name=name, out_shape=jax.ShapeDtypeStruct((t, XBC_W), F32), grid=(t // tb,),
        in_specs=[pl.BlockSpec((tb, XBC_W), lambda i: (i, 2)), _prev8_spec(tb, XBC_W, 2),
                  pl.BlockSpec((8, XBC_W), lambda i: (0, 0)), pl.BlockSpec((1, XBC_W), lambda i: (0, 0))],
        out_specs=pl.BlockSpec((tb, XBC_W), lambda i: (i, 0)), compiler_params=_cparams(1))(proj, proj, w8, b)


def ssd_conv_bwd_pre(proj, w8, b, dxa, *, name):
    t = proj.shape[0]
    tb = _tok_block(t)

    def body(x_ref, xp_ref, w_ref, b_ref, d_ref, o_ref, dw_ref, db_ref):
        i = pl.program_id(0)
        xm = x_ref[...] * _row_valid(i, tb, tb)
        xmp = xp_ref[...] * _row_valid(i, tb, 8, -8) * (i > 0).astype(F32)
        w = w_ref[...]
        x1, x2, x3 = _shift_down(xm, xmp, 1), _shift_down(xm, xmp, 2), _shift_down(xm, xmp, 3)
        c = w[3:4] * xm + w[2:3] * x1 + w[1:2] * x2 + w[0:1] * x3 + b_ref[...]
        dpre = d_ref[...] * _dsilu(c)
        o_ref[...] = dpre

        @pl.when(i == 0)
        def _():
            dw_ref[...] = jnp.zeros_like(dw_ref)
            db_ref[...] = jnp.zeros_like(db_ref)

        dw_ref[0:1, :] += jnp.sum(dpre * x3, axis=0, keepdims=True)
        dw_ref[1:2, :] += jnp.sum(dpre * x2, axis=0, keepdims=True)
        dw_ref[2:3, :] += jnp.sum(dpre * x1, axis=0, keepdims=True)
        dw_ref[3:4, :] += jnp.sum(dpre * xm, axis=0, keepdims=True)
        db_ref[...] += jnp.sum(dpre, axis=0, keepdims=True)

    w8spec = pl.BlockSpec((8, XBC_W), lambda i: (0, 0))
    bspec = pl.BlockSpec((1, XBC_W), lambda i: (0, 0))
    return pl.pallas_call(
        body, name=name,
        out_shape=[jax.ShapeDtypeStruct((t, XBC_W), F32), jax.ShapeDtypeStruct((8, XBC_W), F32), jax.ShapeDtypeStruct((1, XBC_W), F32)],
        grid=(t // tb,),
        in_specs=[pl.BlockSpec((tb, XBC_W), lambda i: (i, 2)), _prev8_spec(tb, XBC_W, 2), w8spec, bspec,
                  pl.BlockSpec((tb, XBC_W), lambda i: (i, 0))],
        out_specs=[pl.BlockSpec((tb, XBC_W), lambda i: (i, 0)), w8spec, bspec],
        compiler_params=_cparams(1))(proj, proj, w8, b, dxa)


def ssd_conv_bwd_in(dpre, w8, *, name):
    t = dpre.shape[0]
    tb = _tok_block(t)
    nblk = t // tb

    def body(d_ref, dn_ref, w_ref, o_ref):
        i = pl.program_id(0)
        d = d_ref[...]
        dn = dn_ref[...] * (i < nblk - 1).astype(F32)
        w = w_ref[...]
        dx = w[3:4] * d + w[2:3] * _shift_up(d, dn, 1) + w[1:2] * _shift_up(d, dn, 2) + w[0:1] * _shift_up(d, dn, 3)
        o_ref[...] = dx * _row_valid(i, tb, tb)

    return pl.pallas_call(
        body, name=name, out_shape=jax.ShapeDtypeStruct((t, XBC_W), F32), grid=(nblk,),
        in_specs=[pl.BlockSpec((tb, XBC_W), lambda i: (i, 0)), _next8_spec(tb, XBC_W, 0, t), pl.BlockSpec((8, XBC_W), lambda i: (0, 0))],
        out_specs=pl.BlockSpec((tb, XBC_W), lambda i: (i, 0)), compiler_params=_cparams(1))(dpre, dpre, w8)


def _col(x, h):
    return jnp.sum(jnp.where(_iota(x.shape, 1) == h, x, 0.0), axis=1, keepdims=True)


def _row(x, h):
    return jnp.sum(jnp.where(_iota(x.shape, 0) == h, x, 0.0), axis=0, keepdims=True)


def _ssd_common(xa, dtr, dtb, alog, c):
    vm = _row_valid(c, CH, CH)
    xs = xa[:, :D] * vm
    dt = _softplus(dtr + dtb)
    a = -jnp.exp(alog) * dt
    tri = (_iota((CH, CH), 0) >= _iota((CH, CH), 1)).astype(F32)
    acs = _dot_hi(tri, a)
    return vm, xs, dt, a, acs, acs.T


def _pair_lanes(v0, v1):
    lane = _iota((1, CH), 1)
    return jnp.where(lane < 64, v0, v1)


def _ssd_pairs_fwd(xs, xa, dt, acs, acs_t, dsk, hins):
    causal = _iota((CH, CH), 0) >= _iota((CH, CH), 1)
    lane = _iota((CH, CH), 1)
    last = _iota((CH, 1), 0) == CH - 1
    bgs = [xa[:, D + CH * g:D + CH * (g + 1)] for g in range(4)]
    cgs = [xa[:, D + 512 + CH * g:D + 512 + CH * (g + 1)] for g in range(4)]
    g_mats = [_dot_nt(cgs[g], bgs[g]) for g in range(4)]
    ps = []
    for q in range(8):
        h0, h1 = 2 * q, 2 * q + 1
        xs_p = xs[:, CH * q:CH * (q + 1)]
        ac0, ac1 = _col(acs, h0), _col(acs, h1)
        ar0, ar1 = _row(acs_t, h0), _row(acs_t, h1)
        l0 = jnp.exp(jnp.where(causal, ac0 - ar0, -1e30))
        l1 = jnp.exp(jnp.where(causal, ac1 - ar1, -1e30))
        dt_p = _pair_lanes(_col(dt, h0), _col(dt, h1))
        ac_p = _pair_lanes(ac0, ac1)
        al0 = jnp.sum(jnp.where(last, ac0, 0.0), axis=0, keepdims=True)
        al1 = jnp.sum(jnp.where(last, ac1, 0.0), axis=0, keepdims=True)
        ps.append(dict(bg=bgs[q // 2], cg=cgs[q // 2], xs_p=xs_p, l0=l0, l1=l1, dt_p=dt_p, x=xs_p * dt_p, eac=jnp.exp(ac_p),
                       dsv=jnp.exp(_pair_lanes(al0, al1) - ac_p), al0=al0, al1=al1,
                       cd=jnp.where(_iota((CH, 1), 0) < 64, jnp.exp(al0), jnp.exp(al1)),
                       d_p=_pair_lanes(_col(dsk, h0), _col(dsk, h1))))
    for q, p in enumerate(ps):
        p["m0"], p["m1"] = g_mats[q // 2] * p["l0"], g_mats[q // 2] * p["l1"]
    for q, p in enumerate(ps):
        p["yd0"], p["yd1"] = _dot(p["m0"], p["x"]), _dot(p["m1"], p["x"])
        p["yoff_raw"] = _dot_nt(p["cg"], hins[q])
        p["s"] = _dot_tn(p["x"] * p["dsv"], p["bg"])
    for p in ps:
        p["y"] = jnp.where(lane < 64, p["yd0"], p["yd1"]) + p["yoff_raw"] * p["eac"] + p["xs_p"] * p["d_p"]
    return ps


def _ssd_gate_norm(y, z, nw):
    yv = y * _silu(z)
    outs, rs = [], []
    for g in range(4):
        yg = yv[:, 256 * g:256 * (g + 1)]
        r = lax.rsqrt(jnp.mean(yg * yg, axis=-1, keepdims=True) + EPS)
        outs.append(yg * r * nw[:, 256 * g:256 * (g + 1)])
        rs.append(r)
    return yv, jnp.concatenate(outs, axis=1), rs


def ssd_fwd(xa, proj, pdt, dtb, alog, dsk, nw, *, name):
    t = xa.shape[0]
    nc = t // CH

    def body(xa_ref, dtr_ref, z_ref, dtb_ref, alog_ref, dsk_ref, nw_ref, y_ref, hs_ref, h_scr):
        c = pl.program_id(0)

        @pl.when(c == 0)
        def _():
            h_scr[...] = jnp.zeros_like(h_scr)

        xa_v = xa_ref[...]
        vm, xs, dt, a, acs, acs_t = _ssd_common(xa_v, dtr_ref[...], dtb_ref[...], alog_ref[...], c)
        dsk_v = dsk_ref[...]
        hins = [h_scr[q] for q in range(8)]
        ps = _ssd_pairs_fwd(xs, xa_v, dt, acs, acs_t, dsk_v, hins)
        for q, p in enumerate(ps):
            hs_ref[0, q] = hins[q]
            h_scr[q] = hins[q] * p["cd"] + p["s"]
        y = jnp.concatenate([p["y"] for p in ps], axis=1)
        _, out, _ = _ssd_gate_norm(y, z_ref[...], nw_ref[...])
        y_ref[...] = out

    small = pl.BlockSpec((1, CH), lambda c: (0, 0))
    return pl.pallas_call(
        body, name=name,
        out_shape=[jax.ShapeDtypeStruct((t, D), F32), jax.ShapeDtypeStruct((nc, 8, CH, CH), F32)], grid=(nc,),
        in_specs=[pl.BlockSpec((CH, XBC_W), lambda c: (c, 0)), pl.BlockSpec((CH, CH), lambda c: (c, 0)),
                  pl.BlockSpec((CH, D), lambda c: (c, 3)), small, small, small, pl.BlockSpec((1, D), lambda c: (0, 0))],
        out_specs=[pl.BlockSpec((CH, D), lambda c: (c, 0)), pl.BlockSpec((1, 8, CH, CH), lambda c: (c, 0, 0, 0))],
        scratch_shapes=[pltpu.VMEM((8, CH, CH), F32)], compiler_params=_cparams(1))(xa, pdt, proj, dtb, alog, dsk, nw)


def ssd_bwd(xa, proj, pdt, hs, dyb, dtb, alog, dsk, nw, *, name):
    t = xa.shape[0]
    nc = t // CH

    def body(xa_ref, dtr_ref, z_ref, hs_ref, dy_ref, dtb_ref, alog_ref, dsk_ref, nw_ref,
             dz_ref, dxa_ref, ddt_ref, gdtb_ref, galog_ref, gdsk_ref, gnw_ref, dh_scr):
        step = pl.program_id(0)
        c = nc - 1 - step

        @pl.when(step == 0)
        def _():
            dh_scr[...] = jnp.zeros_like(dh_scr)
            gdtb_ref[...] = jnp.zeros_like(gdtb_ref)
            galog_ref[...] = jnp.zeros_like(galog_ref)
            gdsk_ref[...] = jnp.zeros_like(gdsk_ref)
            gnw_ref[...] = jnp.zeros_like(gnw_ref)

        xa_v = xa_ref[...]
        dtr = dtr_ref[...]
        dtb_v = dtb_ref[...]
        alog_v = alog_ref[...]
        vm, xs, dt, a, acs, acs_t = _ssd_common(xa_v, dtr, dtb_v, alog_v, c)
        dsk_v = dsk_ref[...]
        z = z_ref[...]
        nw_v = nw_ref[...]
        lane1 = _iota((1, CH), 1)
        sub1 = _iota((CH, 1), 0)
        lane = _iota((CH, CH), 1)

        hins = [hs_ref[0, q] for q in range(8)]
        pairs = _ssd_pairs_fwd(xs, xa_v, dt, acs, acs_t, dsk_v, hins)
        y_pre = jnp.concatenate([p["y"] for p in pairs], axis=1)

        dout = dy_ref[...]
        sz = _silu(z)
        yv = y_pre * sz
        dyv_parts = []
        gnw_parts = []
        for g in range(4):
            sl = slice(256 * g, 256 * (g + 1))
            yg = yv[:, sl]
            r = lax.rsqrt(jnp.mean(yg * yg, axis=-1, keepdims=True) + EPS)
            gy = dout[:, sl] * nw_v[:, sl]
            dyv_parts.append(r * gy - yg * (r * r * r) * jnp.mean(yg * gy, axis=-1, keepdims=True))
            gnw_parts.append(jnp.sum(dout[:, sl] * yg * r, axis=0, keepdims=True))
        dyv = jnp.concatenate(dyv_parts, axis=1)
        gnw_ref[...] += jnp.concatenate(gnw_parts, axis=1)
        dz_ref[...] = dyv * y_pre * _dsilu(z)
        dy_pre = dyv * sz

        dacs_c = jnp.zeros((CH, CH), F32)
        dacs_r = jnp.zeros((CH, CH), F32)
        ddt = jnp.zeros((CH, CH), F32)
        gdsk = jnp.zeros((1, CH), F32)
        dxs_parts = []
        db_g = [None] * 4
        dc_g = [None] * 4
        dg_g = [None] * 4

        def acc(lst, g, v):
            lst[g] = v if lst[g] is None else lst[g] + v

        m_lo = lane < 64
        dhouts = [dh_scr[q] for q in range(8)]
        mats = []
        for q in range(8):
            p = pairs[q]
            dy = dy_pre[:, CH * q:CH * (q + 1)]
            dye = dy * p["eac"]
            mats.append(dict(
                dy=dy, dye=dye,
                dm0=_dot_nt(jnp.where(m_lo, dy, 0.0), p["x"]), dm1=_dot_nt(jnp.where(m_lo, 0.0, dy), p["x"]),
                dx0=_dot_tn(p["m0"], dy), dx1=_dot_tn(p["m1"], dy),
                dc=_dot(dye, hins[q]), dhin=_dot_tn(dye, p["cg"]),
                w1=_dot_nt(p["bg"], dhouts[q]), db=_dot(p["x"] * p["dsv"], dhouts[q])))

        for q in range(8):
            p = pairs[q]
            mt = mats[q]
            g = q // 2
            h0, h1 = 2 * q, 2 * q + 1
            dy = mt["dy"]
            hin = hins[q]
            dhout = dhouts[q]
            x = p["x"]
            dxs = dy * p["d_p"]
            t_sk = dy * p["xs_p"]
            gdsk = gdsk + jnp.where(lane1 == h0, jnp.sum(jnp.where(m_lo, t_sk, 0.0)), 0.0) \
                        + jnp.where(lane1 == h1, jnp.sum(jnp.where(m_lo, 0.0, t_sk)), 0.0)
            dx = jnp.where(m_lo, mt["dx0"], mt["dx1"])
            for hh, lk, mm_, dm in ((h0, p["l0"], p["m0"], mt["dm0"]), (h1, p["l1"], p["m1"], mt["dm1"])):
                acc(dg_g, g, dm * lk)
                qm = dm * mm_
                dacs_c = dacs_c + jnp.where(lane1 == hh, jnp.sum(qm, axis=1, keepdims=True), 0.0)
                dacs_r = dacs_r - jnp.where(sub1 == hh, jnp.sum(qm, axis=0, keepdims=True), 0.0)
            acc(dc_g, g, mt["dc"])
            t_off = dy * p["yoff_raw"] * p["eac"]
            dacs_c = dacs_c + jnp.where(lane1 == h0, jnp.sum(jnp.where(m_lo, t_off, 0.0), axis=1, keepdims=True), 0.0) \
                            + jnp.where(lane1 == h1, jnp.sum(jnp.where(m_lo, 0.0, t_off), axis=1, keepdims=True), 0.0)
            dhin = mt["dhin"] + dhout * p["cd"]
            w1 = mt["w1"]
            dx = dx + p["dsv"] * w1
            t_ds = x * w1 * p["dsv"]
            dd0 = jnp.sum(jnp.where(m_lo, t_ds, 0.0), axis=1, keepdims=True)
            dd1 = jnp.sum(jnp.where(m_lo, 0.0, t_ds), axis=1, keepdims=True)
            acc(db_g, g, mt["db"])
            t_cd = dhout * hin
            sub_lo = _iota((CH, CH), 0) < 64
            dcd0 = jnp.sum(jnp.where(sub_lo, t_cd, 0.0)) * jnp.exp(p["al0"])
            dcd1 = jnp.sum(jnp.where(sub_lo, 0.0, t_cd)) * jnp.exp(p["al1"])
            last = (sub1 == CH - 1)
            dacs_c = dacs_c + jnp.where(lane1 == h0, jnp.where(last, jnp.sum(dd0) + dcd0, 0.0) - dd0, 0.0) \
                            + jnp.where(lane1 == h1, jnp.where(last, jnp.sum(dd1) + dcd1, 0.0) - dd1, 0.0)
            dh_scr[q] = dhin
            dxs = dxs + dx * p["dt_p"]
            t_dt = dx * p["xs_p"]
            ddt = ddt + jnp.where(lane1 == h0, jnp.sum(jnp.where(m_lo, t_dt, 0.0), axis=1, keepdims=True), 0.0) \
                      + jnp.where(lane1 == h1, jnp.sum(jnp.where(m_lo, 0.0, t_dt), axis=1, keepdims=True), 0.0)
            dxs_parts.append(dxs)

        for g in range(4):
            bg, cg = pairs[2 * g]["bg"], pairs[2 * g]["cg"]
            dc_g[g] = dc_g[g] + _dot(dg_g[g], bg)
            db_g[g] = db_g[g] + _dot_tn(dg_g[g], cg)

        dacs = dacs_c + dacs_r.T
        rtri = (_iota((CH, CH), 1) >= _iota((CH, CH), 0)).astype(F32)
        da = _dot_hi(rtri, dacs)
        ddt = ddt - da * jnp.exp(alog_v)
        galog_ref[...] += jnp.sum(da * a, axis=0, keepdims=True)
        dpre = ddt * _sigmoid(dtr + dtb_v) * (lane1 < SSD_HEADS).astype(F32)
        ddt_ref[...] = dpre
        gdtb_ref[...] += jnp.sum(dpre, axis=0, keepdims=True)
        gdsk_ref[...] += gdsk
        dxa_ref[:, 0:D] = jnp.concatenate(dxs_parts, axis=1) * vm
        dxa_ref[:, D:D + 512] = jnp.concatenate(db_g, axis=1)
        dxa_ref[:, D + 512:D + 1024] = jnp.concatenate(dc_g, axis=1)

    small = pl.BlockSpec((1, CH), lambda s: (0, 0))
    wide = pl.BlockSpec((1, D), lambda s: (0, 0))
    rev = lambda s: nc - 1 - s
    return pl.pallas_call(
        body, name=name,
        out_shape=[jax.ShapeDtypeStruct((t, D), F32), jax.ShapeDtypeStruct((t, XBC_W), F32), jax.ShapeDtypeStruct((t, CH), F32),
                   jax.ShapeDtypeStruct((1, CH), F32), jax.ShapeDtypeStruct((1, CH), F32), jax.ShapeDtypeStruct((1, CH), F32),
                   jax.ShapeDtypeStruct((1, D), F32)],
        grid=(nc,),
        in_specs=[pl.BlockSpec((CH, XBC_W), lambda s: (rev(s), 0)), pl.BlockSpec((CH, CH), lambda s: (rev(s), 0)),
                  pl.BlockSpec((CH, D), lambda s: (rev(s), 3)), pl.BlockSpec((1, 8, CH, CH), lambda s: (rev(s), 0, 0, 0)),
                  pl.BlockSpec((CH, D), lambda s: (rev(s), 0)), small, small, small, wide],
        out_specs=[pl.BlockSpec((CH, D), lambda s: (rev(s), 0)), pl.BlockSpec((CH, XBC_W), lambda s: (rev(s), 0)),
                   pl.BlockSpec((CH, CH), lambda s: (rev(s), 0)), small, small, small, wide],
        scratch_shapes=[pltpu.VMEM((8, CH, CH), F32)], compiler_params=_cparams(1))(xa, pdt, proj, hs, dyb, dtb, alog, dsk, nw)


RET_DK = 256


def _log_gamma(h):
    return math.log(1.0 - 2.0 ** (-5.0 - h))


def _rope(x, cos, sin):
    x1, x2 = x[:, :128], x[:, 128:]
    return jnp.concatenate([x1 * cos - x2 * sin, x1 * sin + x2 * cos], axis=1)


def _unrope(d, cos, sin):
    d1, d2 = d[:, :128], d[:, 128:]
    return jnp.concatenate([d1 * cos + d2 * sin, d2 * cos - d1 * sin], axis=1)


def _ret_heads_fwd(q, k, v, cos, sin, vm, r_ins):
    hs = range(RET_HEADS)
    lgs = [_log_gamma(h) for h in hs]
    sls = [slice(RET_DK * h, RET_DK * (h + 1)) for h in hs]
    qr = [_rope(q[:, sls[h]], cos, sin) for h in hs]
    kr = [_rope(k[:, sls[h]], cos, sin) * (RET_DK ** -0.5) for h in hs]
    vr = [v[:, sls[h]] * vm for h in hs]
    rel = (_iota((CH, CH), 0) - _iota((CH, CH), 1)).astype(F32)
    idx = _iota((CH, 1), 0).astype(F32)
    dmask = [jnp.where(rel >= 0, jnp.exp(lgs[h] * jnp.maximum(rel, 0.0)), 0.0) for h in hs]
    kdec = [jnp.exp(lgs[h] * (CH - 1 - idx)) for h in hs]
    qdec = [jnp.exp(lgs[h] * (idx + 1.0)) for h in hs]
    raw = [_dot_nt(qr[h], kr[h]) for h in hs]
    cross = [_dot(qr[h], r_ins[h]) for h in hs]
    kv = [_dot_tn(kr[h] * kdec[h], vr[h]) for h in hs]
    scores = [raw[h] * dmask[h] for h in hs]
    y = [_dot(scores[h], vr[h]) + cross[h] * qdec[h] for h in hs]
    return [dict(qr=qr[h], kr=kr[h], vr=vr[h], dmask=dmask[h], kdec=kdec[h], qdec=qdec[h], scores=scores[h], y=y[h], kv=kv[h],
                 cdec=math.exp(lgs[h] * CH)) for h in hs]


def _group_norm(y):
    mu = jnp.mean(y, axis=-1, keepdims=True)
    yc = y - mu
    r = lax.rsqrt(jnp.mean(yc * yc, axis=-1, keepdims=True) + EPS)
    return yc * r, r


def ret_fwd(proj, cos, sin, *, name):
    t = proj.shape[0]
    nc = t // CH

    def body(q_ref, k_ref, v_ref, g_ref, cos_ref, sin_ref, y_ref, rs_ref, r_scr):
        c = pl.program_id(0)

        @pl.when(c == 0)
        def _():
            r_scr[...] = jnp.zeros_like(r_scr)

        vm = _row_valid(c, CH, CH)
        q, k, v, gt = q_ref[...], k_ref[...], v_ref[...], g_ref[...]
        cos, sin = cos_ref[...], sin_ref[...]
        r_ins = [r_scr[h] for h in range(RET_HEADS)]
        ps = _ret_heads_fwd(q, k, v, cos, sin, vm, r_ins)
        for h, p in enumerate(ps):
            rs_ref[0, h] = r_ins[h]
            r_scr[h] = r_ins[h] * p["cdec"] + p["kv"]
            yn, _ = _group_norm(p["y"])
            sl = slice(RET_DK * h, RET_DK * (h + 1))
            y_ref[:, sl] = yn * _silu(gt[:, sl])

    blk = lambda col: pl.BlockSpec((CH, D), lambda c: (c, col))
    tab = pl.BlockSpec((CH, CH), lambda c: (c, 0))
    return pl.pallas_call(
        body, name=name,
        out_shape=[jax.ShapeDtypeStruct((t, D), F32), jax.ShapeDtypeStruct((nc, RET_HEADS, RET_DK, RET_DK), F32)], grid=(nc,),
        in_specs=[blk(6), blk(7), blk(8), blk(9), tab, tab],
        out_specs=[pl.BlockSpec((CH, D), lambda c: (c, 0)), pl.BlockSpec((1, RET_HEADS, RET_DK, RET_DK), lambda c: (c, 0, 0, 0))],
        scratch_shapes=[pltpu.VMEM((RET_HEADS, RET_DK, RET_DK), F32)], compiler_params=_cparams(1))(proj, proj, proj, proj, cos, sin)


def ret_bwd(proj, cos, sin, rs, dyc, *, name):
    t = proj.shape[0]
    nc = t // CH

    def body(q_ref, k_ref, v_ref, g_ref, cos_ref, sin_ref, rs_ref, dy_ref, o_ref, dr_scr):
        step = pl.program_id(0)
        c = nc - 1 - step

        @pl.when(step == 0)
        def _():
            dr_scr[...] = jnp.zeros_like(dr_scr)

        vm = _row_valid(c, CH, CH)
        q, k, v, gt = q_ref[...], k_ref[...], v_ref[...], g_ref[...]
        cos, sin = cos_ref[...], sin_ref[...]
        dout = dy_ref[...]
        hs = range(RET_HEADS)
        sls = [slice(RET_DK * h, RET_DK * (h + 1)) for h in hs]
        r_ins = [rs_ref[0, h] for h in hs]
        dr_outs = [dr_scr[h] for h in hs]
        ps = _ret_heads_fwd(q, k, v, cos, sin, vm, r_ins)
        dys, dgs = [], []
        for h in hs:
            yn, r = _group_norm(ps[h]["y"])
            gh = gt[:, sls[h]]
            do = dout[:, sls[h]]
            dgs.append(do * yn * _dsilu(gh))
            dyn = do * _silu(gh)
            dys.append(r * (dyn - jnp.mean(dyn, axis=-1, keepdims=True) - yn * jnp.mean(dyn * yn, axis=-1, keepdims=True)))
        dycs = [dys[h] * ps[h]["qdec"] for h in hs]
        dqr = [_dot_nt(dycs[h], r_ins[h]) for h in hs]
        dr_new = [_dot_tn(ps[h]["qr"], dycs[h]) for h in hs]
        dkr = [_dot_nt(ps[h]["vr"], dr_outs[h]) * ps[h]["kdec"] for h in hs]
        dv = [_dot(ps[h]["kr"] * ps[h]["kdec"], dr_outs[h]) + _dot_tn(ps[h]["scores"], dys[h]) for h in hs]
        ds = [_dot_nt(dys[h], ps[h]["vr"]) * ps[h]["dmask"] for h in hs]
        dqr = [dqr[h] + _dot(ds[h], ps[h]["kr"]) for h in hs]
        dkr = [dkr[h] + _dot_tn(ds[h], ps[h]["qr"]) for h in hs]
        for h in hs:
            dr_scr[h] = dr_outs[h] * ps[h]["cdec"] + dr_new[h]
            o_ref[:, RET_DK * h:RET_DK * (h + 1)] = _unrope(dqr[h], cos, sin)
            o_ref[:, D + RET_DK * h:D + RET_DK * (h + 1)] = _unrope(dkr[h], cos, sin) * (RET_DK ** -0.5)
            o_ref[:, 2 * D + RET_DK * h:2 * D + RET_DK * (h + 1)] = dv[h] * vm
            o_ref[:, 3 * D + RET_DK * h:3 * D + RET_DK * (h + 1)] = dgs[h]

    rev = lambda s: nc - 1 - s
    blk = lambda col: pl.BlockSpec((CH, D), lambda s: (rev(s), col))
    tab = pl.BlockSpec((CH, CH), lambda s: (rev(s), 0))
    return pl.pallas_call(
        body, name=name, out_shape=jax.ShapeDtypeStruct((t, 4 * D), F32), grid=(nc,),
        in_specs=[blk(6), blk(7), blk(8), blk(9), tab, tab,
                  pl.BlockSpec((1, RET_HEADS, RET_DK, RET_DK), lambda s: (rev(s), 0, 0, 0)), pl.BlockSpec((CH, D), lambda s: (rev(s), 0))],
        out_specs=pl.BlockSpec((CH, 4 * D), lambda s: (rev(s), 0)),
        scratch_shapes=[pltpu.VMEM((RET_HEADS, RET_DK, RET_DK), F32)], compiler_params=_cparams(1))(proj, proj, proj, proj, cos, sin, rs, dyc)


SB_D = 128
SB_SCALE = SB_D ** -0.5
SB_CUTOFF = 104.0


def _split_hi_lo(x):
    hi = x.astype(BF16)
    lo = (x - hi.astype(F32)).astype(BF16)
    return hi, lo


def _sum_matrix(kind):
    a, b = _iota((128, 128), 0), _iota((128, 128), 1)
    tri = ((b > a) if kind == "after" else (b < a)).astype(BF16)
    return jnp.concatenate([tri, tri], axis=1)


def _key_sums(x, mat2):
    hi, lo = _split_hi_lo(x)
    return jnp.dot(mat2, jnp.concatenate([hi, lo], axis=0), preferred_element_type=F32)


def _sb_mask(d_kq, key_idx, first_key, q_minus_k):
    return (d_kq < q_minus_k) & (key_idx >= N_PAD - first_key)


def _sb_log_sigmoid(z):
    return jnp.minimum(z, 0.0) - jnp.log(1.0 + jnp.exp(-jnp.abs(z)))


def sb_fwd(qkv, *, name):
    t = qkv.shape[0]
    tq = _tok_block(t)
    nq = t // tq
    per = tq // 128

    def body(q_ref, k_ref, v_ref, o_ref, at_ref, bmin_ref):
        h = pl.program_id(0)
        i = pl.program_id(1)
        top = (i + 1) * per - 1
        mat_after = _sum_matrix("after")
        qs = [q_ref[pl.ds(128 * r, 128), :] for r in range(per)]
        d_kq = _iota((128, 128), 0) - _iota((128, 128), 1)
        key_idx = _iota((128, 128), 0)

        def mask(r, b):
            return _sb_mask(d_kq, key_idx, b * 128, (i * per + r - b) * 128)

        def step(carry):
            b, _, a_runs, accs = carry
            off = pl.multiple_of(b * 128, 128)
            kb = k_ref[pl.ds(off, 128), :]
            vb = v_ref[pl.ds(off, 128), :]
            tiles = range(per)
            zs = [_dot_nt(kb, qs[r]) * SB_SCALE for r in tiles]
            ms = [mask(r, b) for r in tiles]
            lss = [_sb_log_sigmoid(zs[r]) for r in tiles]
            lnegs = [jnp.where(ms[r], lss[r] - zs[r], 0.0) for r in tiles]
            sufs = [_key_sums(lnegs[r], mat_after) for r in tiles]
            ws = [jnp.where(ms[r], jnp.exp(lss[r] + a_runs[r] + sufs[r]), 0.0) for r in tiles]
            a_new = [a_runs[r] + sufs[r][0:1, :] + lnegs[r][0:1, :] for r in tiles]
            acc_new = [accs[r] + _dot_tn(ws[r], vb) for r in tiles]
            a_max = jnp.max(functools.reduce(jnp.maximum, a_new))
            return b - 1, a_max >= -SB_CUTOFF, tuple(a_new), tuple(acc_new)

        zeros = tuple(qs[r].astype(F32) * 0.0 for r in range(per))
        zrow = tuple(z[0:1, :] for z in zeros)
        b_end, _, a_runs, accs = lax.while_loop(lambda c: jnp.logical_and(c[0] >= 0, c[1]), step, (top, top >= 0, zrow, zeros))
        bmin_ref[h, i] = b_end + 1
        at_ref[...] = jnp.zeros_like(at_ref)
        for r in range(per):
            o_ref[pl.ds(128 * r, 128), :] = accs[r]
            at_ref[0, 0, r:r + 1, :] = a_runs[r]

    blk = pl.BlockSpec((tq, 128), lambda h, i: (i, h))
    return pl.pallas_call(
        body, name=name,
        out_shape=[jax.ShapeDtypeStruct((t, D), F32), jax.ShapeDtypeStruct((SB_HEADS, nq, 8, 128), F32),
                   jax.ShapeDtypeStruct((SB_HEADS, nq), jnp.int32)],
        grid=(SB_HEADS, nq),
        in_specs=[blk, pl.BlockSpec((t, 128), lambda h, i: (0, SB_HEADS + h)),
                  pl.BlockSpec((t, 128), lambda h, i: (0, 2 * SB_HEADS + h))],
        out_specs=[blk, pl.BlockSpec((1, 1, 8, 128), lambda h, i: (h, i, 0, 0)), pl.BlockSpec(memory_space=pltpu.SMEM)],
        compiler_params=_cparams(2))(qkv, qkv, qkv)


def sb_bwd(qkv, atot, bmin, dout, *, name):
    t = qkv.shape[0]
    tq = _tok_block(t)
    nq = t // tq
    per = tq // 128

    nblk = t // 128

    def body(bmin_ref, q_ref, k_ref, v_ref, at_ref, do_ref, dq_ref, dk_ref, dv_ref):
        i = pl.program_id(1)
        top = (i + 1) * per - 1
        b_first = bmin_ref[pl.program_id(0), i]

        @pl.when(i == 0)
        def _():
            dk_ref[...] = jnp.zeros_like(dk_ref)
            dv_ref[...] = jnp.zeros_like(dv_ref)

        qs = [q_ref[pl.ds(128 * r, 128), :] for r in range(per)]
        dos = [do_ref[pl.ds(128 * r, 128), :].astype(BF16) for r in range(per)]
        q_all = q_ref[...]
        do_all = do_ref[...].astype(BF16)
        a_tots = [at_ref[0, 0, r:r + 1, :] for r in range(per)]
        mat_after = _sum_matrix("after")
        mat_before = _sum_matrix("before")
        d_kq = _iota((128, 128), 0) - _iota((128, 128), 1)
        key_idx = _iota((128, 128), 0)

        def offset(b):
            return pl.multiple_of(jnp.clip(b, 0, nblk - 1) * 128, 128)

        def mask(r, b):
            return _sb_mask(d_kq, key_idx, b * 128, (i * per + r - b) * 128)

        def step(b, carry):
            lss, dws, sufs, tots, p_runs, e_runs, dqs = carry
            off = offset(b)
            kb = k_ref[pl.ds(off, 128), :]
            vb = v_ref[pl.ds(off, 128), :]
            zs = [_dot_nt(kb, qs[r]) * SB_SCALE for r in range(per)]
            dw_new = tuple(_dot_nt(vb, dos[r]) for r in range(per))
            off1 = offset(b - 1)
            k1 = k_ref[pl.ds(off1, 128), :]
            ws, es, epres, sigs, p_new = [], [], [], [], []
            for r in range(per):
                p = p_runs[r] + tots[r]
                w = jnp.where(mask(r, b - 1), jnp.exp(lss[r] + (a_tots[r] - p) + sufs[r]), 0.0)
                e = w * dws[r]
                p_new.append(p)
                ws.append(w.astype(BF16))
                es.append(e)
                epres.append(_key_sums(e, mat_before))
                sigs.append(jnp.exp(lss[r]))
            dv_ref[pl.ds(off1, 128), :] += _dot(jnp.concatenate(ws, axis=1), do_all)
            ls_new, suf_new, tot_new = [], [], []
            for r in range(per):
                ls = _sb_log_sigmoid(zs[r])
                lneg = jnp.where(mask(r, b), ls - zs[r], 0.0)
                suf = _key_sums(lneg, mat_after)
                ls_new.append(ls)
                suf_new.append(suf)
                tot_new.append(suf[0:1, :] + lneg[0:1, :])
            dzs, e_new, dq_new = [], [], []
            for r in range(per):
                t2 = jnp.where(mask(r, b - 1), (e_runs[r] + epres[r]) * sigs[r], 0.0)
                dz = ((es[r] * (1.0 - sigs[r]) - t2) * SB_SCALE).astype(BF16)
                e_new.append(e_runs[r] + epres[r][127:128, :] + es[r][127:128, :])
                dq_new.append(dqs[r] + _dot_tn(dz, k1))
                dzs.append(dz)
            dk_ref[pl.ds(off1, 128), :] += _dot(jnp.concatenate(dzs, axis=1), q_all)
            return tuple(ls_new), dw_new, tuple(suf_new), tuple(tot_new), tuple(p_new), tuple(e_new), tuple(dq_new)

        zeros = tuple(qs[r].astype(F32) * 0.0 for r in range(per))
        zrow = tuple(z[0:1, :] for z in zeros)
        carry = lax.fori_loop(b_first, top + 2, step, (zeros, zeros, zeros, zrow, zrow, zrow, zeros))
        for r in range(per):
            dq_ref[pl.ds(128 * r, 128), :] = carry[6][r]

    head_blk = pl.BlockSpec((t, 128), lambda h, i: (0, h))
    return pl.pallas_call(
        body, name=name, out_shape=[jax.ShapeDtypeStruct((t, D), F32)] * 3, grid=(SB_HEADS, nq),
        in_specs=[pl.BlockSpec(memory_space=pltpu.SMEM),
                  pl.BlockSpec((tq, 128), lambda h, i: (i, h)), pl.BlockSpec((t, 128), lambda h, i: (0, SB_HEADS + h)),
                  pl.BlockSpec((t, 128), lambda h, i: (0, 2 * SB_HEADS + h)),
                  pl.BlockSpec((1, 1, 8, 128), lambda h, i: (h, i, 0, 0)), pl.BlockSpec((tq, 128), lambda h, i: (i, h))],
        out_specs=[pl.BlockSpec((tq, 128), lambda h, i: (i, h)), head_blk, head_blk],
        compiler_params=_cparams(2, 60 * 1024 * 1024))(bmin, qkv, qkv, qkv, atot, dout)


def branch_fwd(y, proj, w, n, *, name, add=None):
    t = y.shape[0]
    tb = _tok_block(t)
    has_add = add is not None

    def body(*refs):
        if has_add:
            y_ref, g_ref, w_ref, a_ref, o_ref = refs
        else:
            y_ref, g_ref, w_ref, o_ref = refs
        r = _sigmoid(g_ref[...]) * _dot(y_ref[...], w_ref[...])
        if has_add:
            r = r + a_ref[...]
        o_ref[...] = r

    blk = pl.BlockSpec((tb, D), lambda i: (i, 0))
    in_specs = [blk, pl.BlockSpec((tb, D), lambda i: (i, COL_GATE + n)), pl.BlockSpec((D, D), lambda i: (0, 0))] + ([blk] if has_add else [])
    args = [y, proj, w] + ([add] if has_add else [])
    return pl.pallas_call(body, name=name, out_shape=jax.ShapeDtypeStruct((t, D), F32), grid=(t // tb,),
                          in_specs=in_specs, out_specs=blk, compiler_params=_cparams(1))(*args)


def branch_bwd(y, proj, w, wt, n, dmerged, *, name):
    t = y.shape[0]
    tb = _tok_block(t)

    def body(y_ref, g_ref, w_ref, wt_ref, dm_ref, dg_ref, dup_ref, dy_ref):
        up = _dot(y_ref[...], w_ref[...])
        gate = _sigmoid(g_ref[...])
        dm = dm_ref[...]
        dg_ref[...] = dm * up * gate * (1.0 - gate)
        dup = (dm * gate).astype(BF16)
        dup_ref[...] = dup
        dy_ref[...] = _dot(dup, wt_ref[...])

    blk = pl.BlockSpec((tb, D), lambda i: (i, 0))
    wspec = pl.BlockSpec((D, D), lambda i: (0, 0))
    return pl.pallas_call(
        body, name=name,
        out_shape=[jax.ShapeDtypeStruct((t, D), F32), jax.ShapeDtypeStruct((t, D), BF16), jax.ShapeDtypeStruct((t, D), F32)],
        grid=(t // tb,),
        in_specs=[blk, pl.BlockSpec((tb, D), lambda i: (i, COL_GATE + n)), wspec, wspec, blk],
        out_specs=[blk, blk, blk], compiler_params=_cparams(1))(y, proj, w, wt, dmerged)


def swiglu_fwd(f, *, name):
    t = f.shape[0]
    tb = _tok_block(t)

    def body(g_ref, u_ref, o_ref):
        o_ref[...] = (_silu(g_ref[...]) * u_ref[...]).astype(BF16)

    return pl.pallas_call(body, name=name, out_shape=jax.ShapeDtypeStruct((t, D_FF), BF16), grid=(t // tb,),
                          in_specs=[pl.BlockSpec((tb, D_FF), lambda i: (i, 0)), pl.BlockSpec((tb, D_FF), lambda i: (i, 1))],
                          out_specs=pl.BlockSpec((tb, D_FF), lambda i: (i, 0)), compiler_params=_cparams(1))(f, f)


def swiglu_bwd(f, dact, *, name):
    t = f.shape[0]
    tb = _tok_block(t)

    def body(g_ref, u_ref, d_ref, o_ref):
        g, u, d = g_ref[...], u_ref[...], d_ref[...]
        o_ref[:, 0:D_FF] = d * u * _dsilu(g)
        o_ref[:, D_FF:2 * D_FF] = d * _silu(g)

    return pl.pallas_call(body, name=name, out_shape=jax.ShapeDtypeStruct((t, 2 * D_FF), F32), grid=(t // tb,),
                          in_specs=[pl.BlockSpec((tb, D_FF), lambda i: (i, 0)), pl.BlockSpec((tb, D_FF), lambda i: (i, 1)),
                                    pl.BlockSpec((tb, D_FF), lambda i: (i, 0))],
                          out_specs=pl.BlockSpec((tb, 2 * D_FF), lambda i: (i, 0)), compiler_params=_cparams(1))(f, f, dact)


def loss_head(h, target, *, name):
    t = h.shape[0]
    nb = t // CH

    def body(h_ref, t_ref, l_ref, d_ref):
        i = pl.program_id(0)

        @pl.when(i == 0)
        def _():
            l_ref[...] = jnp.zeros_like(l_ref)
            d_ref[...] = jnp.zeros_like(d_ref)

        @pl.when(i > 0)
        def _():
            err = h_ref[...] - t_ref[...]
            d_ref[...] = err * (1.0 / D)
            l_ref[...] += jnp.sum(err * err) * (0.5 / D)

    return pl.pallas_call(
        body, name=name, out_shape=[jax.ShapeDtypeStruct((8, 128), F32), jax.ShapeDtypeStruct((t, D), F32)], grid=(nb,),
        in_specs=[pl.BlockSpec((CH, D), lambda i: (i, 0)), pl.BlockSpec((CH, D), lambda i: (jnp.maximum(i - 1, 0), 0))],
        out_specs=[pl.BlockSpec((8, 128), lambda i: (0, 0)), pl.BlockSpec((CH, D), lambda i: (i, 0))],
        compiler_params=_cparams(1))(h, target)


def _pad_rows8(w):
    return jnp.concatenate([w, jnp.zeros((8 - w.shape[0], w.shape[1]), w.dtype)], axis=0)


def _pad_lanes(v, n=CH):
    return jnp.concatenate([v, jnp.zeros((n - v.shape[0],), v.dtype)])[None, :]


def prep_layer(p):
    w = p["w_in"]
    zeros = jnp.zeros((D, CH - SSD_HEADS), w.dtype)
    w_f = jnp.concatenate([w[:, :6144], w[:, 6160:10256], w[:, 13328:17424]], axis=1)
    w_dt = jnp.concatenate([w[:, 6144:6160], zeros], axis=1)
    w_sb = w[:, 10256:13328]
    return dict(
        w_f=w_f, w_sb=w_sb, w_dt=w_dt, w_f_t=w_f.T, w_sb_t=w_sb.T, w_dt_t=w_dt.T,
        w_br=p["w_branch"], w_br_t=jnp.swapaxes(p["w_branch"], 1, 2), w_out=p["w_out"], w_out_t=p["w_out"].T,
        w_fi=p["w_ffn_in"], w_fi_t=p["w_ffn_in"].T, w_fo=p["w_ffn_out"], w_fo_t=p["w_ffn_out"].T,
        conv_a8=_pad_rows8(p["conv_a"]), conv_s8=_pad_rows8(p["ssd_conv_w"]), conv_sb=p["ssd_conv_b"][None, :],
        dtb=_pad_lanes(p["ssd_dt_bias"]), alog=_pad_lanes(p["ssd_a_log"]), dsk=_pad_lanes(p["ssd_d"]), nw=p["ssd_norm"][None, :],
        n1=p["norm_mix_pre"][None, :], n2=p["norm_mix_post"][None, :], n3=p["norm_ffn_pre"][None, :], n4=p["norm_ffn_post"][None, :])


def layer_fwd(h0, w, cos, sin, l):
    nm = lambda s: f"l{l}_{s}"
    hn = rms_fwd(h0, w["n1"], name=nm("rms1"), out_dtype=BF16)
    proj = mm(hn, w["w_f"], name=nm("proj_f"))
    qkv = mm(hn, w["w_sb"], name=nm("proj_sb"), out_dtype=BF16)
    pdt = mm(hn, w["w_dt"], name=nm("proj_dt"))
    y_a = conv_a_fwd(proj, w["conv_a8"], name=nm("conv_a"))
    xa = ssd_conv_fwd(proj, w["conv_s8"], w["conv_sb"], name=nm("ssd_conv"))
    y_b, hs = ssd_fwd(xa, proj, pdt, w["dtb"], w["alog"], w["dsk"], w["nw"], name=nm("ssd"))
    y_c, rs = ret_fwd(proj, cos, sin, name=nm("ret"))
    y_d, sb_atot, sb_bmin = sb_fwd(qkv, name=nm("sb"))
    ys = (y_a, y_b, y_c, y_d)
    merged = None
    for n in range(4):
        merged = branch_fwd(ys[n], proj, w["w_br"][n], n, name=nm(f"branch{n}"), add=merged)
    mix = mm(merged, w["w_out"], name=nm("mix"))
    h1 = rms_fwd(mix, w["n2"], name=nm("rms2"), res=h0)
    hn2 = rms_fwd(h1, w["n3"], name=nm("rms3"), out_dtype=BF16)
    f = mm(hn2, w["w_fi"], name=nm("ffn_in"))
    act = swiglu_fwd(f, name=nm("swiglu"))
    f2 = mm(act, w["w_fo"], name=nm("ffn_out"))
    h2 = rms_fwd(f2, w["n4"], name=nm("rms4"), res=h1)
    saved = dict(h0=h0, hn=hn, proj=proj, qkv=qkv, pdt=pdt, xa=xa, hs=hs, rs=rs, ys=ys, sb_atot=sb_atot, sb_bmin=sb_bmin, merged=merged, mix=mix, h1=h1, hn2=hn2,
                 f=f, act=act, f2=f2)
    return h2, saved


def layer_bwd(dh2, s, w, cos, sin, l):
    nm = lambda t: f"l{l}_{t}"
    g = {}
    df2, g["n4"] = rms_bwd(s["f2"], w["n4"], dh2, name=nm("rms4_b"))
    g["w_fo"] = mm_tn(s["act"], df2, name=nm("ffn_out_dw"))
    dact = mm(df2, w["w_fo_t"], name=nm("ffn_out_dx"), out_dtype=BF16)
    df = swiglu_bwd(s["f"], dact, name=nm("swiglu_b"))
    g["w_fi"] = mm_tn(s["hn2"], df, name=nm("ffn_in_dw"))
    dhn2 = mm(df, w["w_fi_t"], name=nm("ffn_in_dx"))
    dh1, g["n3"] = rms_bwd(s["h1"], w["n3"], dhn2, name=nm("rms3_b"), add=dh2)
    dmix, g["n2"] = rms_bwd(s["mix"], w["n2"], dh1, name=nm("rms2_b"))
    g["w_out"] = mm_tn(s["merged"], dmix, name=nm("mix_dw"))
    dmerged = mm(dmix, w["w_out_t"], name=nm("mix_dx"))
    dgate, dys, dwb = [], [], []
    for n in range(4):
        dg_n, dup_n, dy_n = branch_bwd(s["ys"][n], s["proj"], w["w_br"][n], w["w_br_t"][n], n, dmerged, name=nm(f"branch{n}_b"))
        dgate.append(dg_n)
        dys.append(dy_n)
        dwb.append(mm_tn(s["ys"][n], dup_n, name=nm(f"branch{n}_dw")))
    g["w_br"] = jnp.stack(dwb)
    d_a, g["conv_a8"] = conv_a_bwd(s["proj"], w["conv_a8"], dys[0], name=nm("conv_a_b"))
    dz, dxa, ddt, g["dtb"], g["alog"], g["dsk"], g["nw"] = ssd_bwd(
        s["xa"], s["proj"], s["pdt"], s["hs"], dys[1], w["dtb"], w["alog"], w["dsk"], w["nw"], name=nm("ssd_b"))
    dpre, g["conv_s8"], g["conv_sb"] = ssd_conv_bwd_pre(s["proj"], w["conv_s8"], w["conv_sb"], dxa, name=nm("ssd_conv_b1"))
    dxbc = ssd_conv_bwd_in(dpre, w["conv_s8"], name=nm("ssd_conv_b2"))
    d_r = ret_bwd(s["proj"], cos, sin, s["rs"], dys[2], name=nm("ret_b"))
    d_sb = sb_bwd(s["qkv"], s["sb_atot"], s["sb_bmin"], dys[3], name=nm("sb_b"))
    segs = [(d_a, 0), (dz, 3072), (dxbc, 4096), (d_r, 6144), (dgate[0], 10240), (dgate[1], 11264), (dgate[2], 12288),
            (dgate[3], 13312)]
    dws = [mm_tn(s["hn"], d, name=nm(f"proj_dw{k}")) for k, (d, _) in enumerate(segs)]
    g["w_f"] = jnp.concatenate(dws, axis=1)
    g["w_sb"] = jnp.concatenate([mm_tn(s["hn"], d, name=nm(f"proj_dw_sb{k}")) for k, d in enumerate(d_sb)], axis=1)
    g["w_dt"] = mm_tn(s["hn"], ddt, name=nm("proj_dw_dt"))
    dhn = mm(ddt, w["w_dt_t"], name=nm("proj_dx_dt"))
    for k, d in enumerate(d_sb):
        dhn = mm(d, w["w_sb_t"][k * D:(k + 1) * D], name=nm(f"proj_dx_sb{k}"), add=dhn)
    for k, (d, c0) in enumerate(segs):
        dhn = mm(d, w["w_f_t"][c0:c0 + d.shape[1]], name=nm(f"proj_dx{k}"), add=dhn)
    dh0, g["n1"] = rms_bwd(s["h0"], w["n1"], dhn, name=nm("rms1_b"), add=dh1)
    return dh0, g


def layer_grads_to_params(g):
    wf, wsb = g["w_f"], g["w_sb"]
    w_in = jnp.concatenate([wf[:, :6144], g["w_dt"][:, :SSD_HEADS], wf[:, 6144:10240], wsb, wf[:, 10240:14336]], axis=1)
    return dict(
        w_in=w_in, conv_a=g["conv_a8"][:3], ssd_conv_w=g["conv_s8"][:4], ssd_conv_b=g["conv_sb"][0],
        ssd_dt_bias=g["dtb"][0, :SSD_HEADS], ssd_a_log=g["alog"][0, :SSD_HEADS], ssd_d=g["dsk"][0, :SSD_HEADS], ssd_norm=g["nw"][0],
        w_branch=g["w_br"], w_out=g["w_out"], w_ffn_in=g["w_fi"], w_ffn_out=g["w_fo"],
        norm_mix_pre=g["n1"][0], norm_mix_post=g["n2"][0], norm_ffn_pre=g["n3"][0], norm_ffn_post=g["n4"][0])


def rope_tables(t):
    half = RET_DK // 2
    inv = ROPE_BASE ** (-jnp.arange(half, dtype=F32) / half)
    ang = jnp.arange(t).astype(F32)[:, None] * inv[None, :]
    return jnp.cos(ang), jnp.sin(ang)


def local_step(x, target, meta, layers):
    h = jnp.concatenate([jnp.zeros((N_PAD, D), F32), meta, x], axis=0)
    t = h.shape[0]
    cos, sin = rope_tables(t)
    ws = [prep_layer(p) for p in layers]
    saved = []
    for l, w in enumerate(ws):
        h, s = layer_fwd(h, w, cos, sin, l)
        saved.append(s)
    loss, dh = loss_head(h, target, name="loss_head")
    grads = [None] * len(ws)
    for l in reversed(range(len(ws))):
        dh, g = layer_bwd(dh, saved[l], ws[l], cos, sin, l)
        grads[l] = layer_grads_to_params(g)
    return loss, dh[CH:], dh[N_PAD:CH], grads


def _my_place():
    return lax.axis_index("x"), lax.axis_index("y"), lax.axis_index("c")


def _flat(px, py, pc):
    return 4 * px + 2 * py + pc


ANY = pl.BlockSpec(memory_space=pl.ANY)


def all_gather(x_shard, *, name):
    shape = x_shard.shape

    def body(x_ref, out_ref, send_sems, recv_sems, local_sem):
        x, y, c = _my_place()
        me, sibling = (x, y, c), (x, y, 1 - c)
        chips = [(1 - x, y), (x, 1 - y), (1 - x, 1 - y)]

        def rows(px, py, pc):
            return out_ref.at[_flat(px, py, pc)]

        def copy(k, block, to, src=None):
            return pltpu.make_async_remote_copy(
                src_ref=rows(*block) if src is None else src, dst_ref=rows(*block),
                send_sem=send_sems.at[k], recv_sem=recv_sems.at[k], device_id=to, device_id_type=MESH_ID)

        mine = pltpu.make_async_copy(x_ref, rows(*me), local_sem)
        mine.start()
        first = [copy(0, me, sibling, src=x_ref)]
        first += [copy(1 + j, me, (*chip, c), src=x_ref) for j, chip in enumerate(chips)]
        for cp in first:
            cp.start()
        passed = [copy(4 + j, (*chip, c), sibling) for j, chip in enumerate(chips)]
        for j, chip in enumerate(chips):
            copy(1 + j, (*chip, c), me).wait_recv()
            passed[j].start()
        copy(0, sibling, me).wait_recv()
        for j, chip in enumerate(chips):
            copy(4 + j, (*chip, 1 - c), me).wait_recv()
        for cp in first + passed:
            cp.wait_send()
        mine.wait()

    return pl.pallas_call(
        body, name=name, out_shape=jax.ShapeDtypeStruct((N_DEV,) + shape, x_shard.dtype),
        in_specs=[ANY], out_specs=ANY,
        scratch_shapes=[pltpu.SemaphoreType.DMA((7,)), pltpu.SemaphoreType.DMA((7,)), pltpu.SemaphoreType.DMA],
    )(x_shard)


N_CHIP = 4


def pair_exchange(g, *, name):
    shape = (N_CHIP,) + g.shape[1:]

    def body(g_ref, recv_ref, own_ref, send_sems, recv_sems, local_sems):
        x, y, c = _my_place()
        sibling = (x, y, 1 - c)
        keeps = [pltpu.make_async_copy(g_ref.at[2 * j + c], own_ref.at[j], local_sems.at[j]) for j in range(N_CHIP)]
        sends = [pltpu.make_async_remote_copy(
            src_ref=g_ref.at[2 * j + 1 - c], dst_ref=recv_ref.at[j], send_sem=send_sems.at[j], recv_sem=recv_sems.at[j],
            device_id=sibling, device_id_type=MESH_ID) for j in range(N_CHIP)]
        for cp in keeps + sends:
            cp.start()
        for cp in sends:
            cp.wait_recv()
        for cp in sends:
            cp.wait_send()
        for cp in keeps:
            cp.wait()

    out = jax.ShapeDtypeStruct(shape, g.dtype)
    return pl.pallas_call(
        body, name=name, out_shape=[out, out], in_specs=[ANY], out_specs=[ANY, ANY],
        scratch_shapes=[pltpu.SemaphoreType.DMA((N_CHIP,)), pltpu.SemaphoreType.DMA((N_CHIP,)), pltpu.SemaphoreType.DMA((N_CHIP,))],
    )(g)


def pair_sum(a, b, *, name):
    n, r, cols = a.shape
    tb = _pick(r, (512, 256, 128))

    def body(a_ref, b_ref, o_ref):
        o_ref[...] = (a_ref[...].astype(F32) + b_ref[...].astype(F32)).astype(o_ref.dtype)

    blk = pl.BlockSpec((1, tb, cols), lambda j, i: (j, i, 0))
    return pl.pallas_call(body, name=name, out_shape=jax.ShapeDtypeStruct(a.shape, a.dtype), grid=(n, r // tb),
                          in_specs=[blk, blk], out_specs=blk, compiler_params=_cparams(2))(a, b)


def chip_exchange(s, *, name):
    def body(s_ref, out_ref, send_sems, recv_sems, local_sem):
        x, y, c = _my_place()
        me = 2 * x + y
        mine = pltpu.make_async_copy(s_ref.at[me], out_ref.at[me], local_sem)
        mine.start()
        peers = [(jnp.bitwise_xor(x, k >> 1), jnp.bitwise_xor(y, k & 1)) for k in range(1, N_CHIP)]
        sends = []
        for k, (px, py) in enumerate(peers):
            cp = pltpu.make_async_remote_copy(
                src_ref=s_ref.at[2 * px + py], dst_ref=out_ref.at[me], send_sem=send_sems.at[k], recv_sem=recv_sems.at[k],
                device_id=(px, py, c), device_id_type=MESH_ID)
            cp.start()
            sends.append(cp)
        for k, (px, py) in enumerate(peers):
            slot = out_ref.at[2 * px + py]
            pltpu.make_async_remote_copy(
                src_ref=slot, dst_ref=slot, send_sem=send_sems.at[k], recv_sem=recv_sems.at[k],
                device_id=(px, py, c), device_id_type=MESH_ID).wait_recv()
        for cp in sends:
            cp.wait_send()
        mine.wait()

    return pl.pallas_call(
        body, name=name, out_shape=jax.ShapeDtypeStruct(s.shape, s.dtype), in_specs=[ANY], out_specs=ANY,
        scratch_shapes=[pltpu.SemaphoreType.DMA((N_CHIP - 1,)), pltpu.SemaphoreType.DMA((N_CHIP - 1,)), pltpu.SemaphoreType.DMA],
    )(s)


def sum_slots(a, *, name):
    def body(a_ref, o_ref):
        s = a_ref[0]
        for d in range(1, N_DEV):
            s = s + a_ref[d]
        o_ref[...] = s

    return pl.pallas_call(body, name=name, out_shape=jax.ShapeDtypeStruct(a.shape[1:], a.dtype))(a)


def _adamw_math(w, g, m, v):
    m = ADAM_B1 * m + (1.0 - ADAM_B1) * g
    v = ADAM_B2 * v + (1.0 - ADAM_B2) * (g * g)
    m_hat = m / (1.0 - ADAM_B1 ** ADAM_STEP)
    v_hat = v / (1.0 - ADAM_B2 ** ADAM_STEP)
    delta = -ADAM_LR * (m_hat / (jnp.sqrt(v_hat) + ADAM_EPS) + ADAM_WD * w)
    return delta, m, v


def adamw_big(recv, w, m, v, *, name):
    r, cols = w.shape
    tb = 128
    n = recv.shape[0]

    def body(r_ref, w_ref, m_ref, v_ref, g_ref, d_ref, nm_ref, nv_ref):
        g = r_ref[0].astype(F32)
        for d in range(1, n):
            g = g + r_ref[d].astype(F32)
        g_ref[...] = g
        d_ref[...], nm_ref[...], nv_ref[...] = _adamw_math(w_ref[...], g, m_ref[...], v_ref[...])

    blk = pl.BlockSpec((tb, cols), lambda i: (i, 0))
    out = jax.ShapeDtypeStruct((r, cols), F32)
    return pl.pallas_call(
        body, name=name, out_shape=[out] * 4, grid=(r // tb,),
        in_specs=[pl.BlockSpec((n, tb, cols), lambda i: (0, i, 0)), blk, blk, blk], out_specs=[blk] * 4,
        compiler_params=_cparams(1))(recv, w, m, v)


def adamw_small(w, g, m, v, *, name):
    def body(w_ref, g_ref, m_ref, v_ref, d_ref, nm_ref, nv_ref):
        d_ref[...], nm_ref[...], nv_ref[...] = _adamw_math(w_ref[...], g_ref[...], m_ref[...], v_ref[...])

    out = jax.ShapeDtypeStruct(w.shape, F32)
    return pl.pallas_call(body, name=name, out_shape=[out] * 3)(w, g, m, v)


BIG = ("w_in", "w_branch", "w_out", "w_ffn_in", "w_ffn_out")
BIG_SHARD = {"w_in": (DEPTH, D, 2178), "w_branch": (DEPTH, 4, 128, D), "w_out": (DEPTH, 128, D),
             "w_ffn_in": (DEPTH, D, 704), "w_ffn_out": (DEPTH, 352, D)}
BIG_FULL = {"w_in": ((1, 2, 0, 3), (DEPTH, D, 17424)), "w_branch": ((1, 2, 0, 3, 4), (DEPTH, 4, D, D)),
            "w_out": ((1, 0, 2, 3), (DEPTH, D, D)), "w_ffn_in": ((1, 2, 0, 3), (DEPTH, D, 2 * D_FF)),
            "w_ffn_out": ((1, 0, 2, 3), (DEPTH, D_FF, D))}
BIG_ROWS = {n: int(np.prod(s)) // D for n, s in BIG_SHARD.items()}
BIG_R = 7808


def pack_big(shards, dtype):
    parts = [shards[n].astype(dtype).reshape(BIG_ROWS[n], D) for n in BIG]
    parts.append(jnp.zeros((BIG_R - sum(BIG_ROWS.values()), D), dtype))
    return jnp.concatenate(parts, axis=0)


def unpack_big(flat):
    out, o = {}, 0
    for n in BIG:
        out[n] = flat[o:o + BIG_ROWS[n]].reshape(BIG_SHARD[n])
        o += BIG_ROWS[n]
    return out


def unpack_big_full(gathered):
    out, o = {}, 0
    for n in BIG:
        perm, full = BIG_FULL[n]
        out[n] = gathered[:, o:o + BIG_ROWS[n]].reshape((N_DEV,) + BIG_SHARD[n]).transpose(perm).reshape(full)
        o += BIG_ROWS[n]
    return out


def pack_big_full(full, dtype):
    parts = []
    for n in BIG:
        perm, _ = BIG_FULL[n]
        split = tuple(int(v) for v in np.array((N_DEV,) + BIG_SHARD[n])[list(perm)])
        inv = tuple(int(i) for i in np.argsort(perm))
        parts.append(full[n].astype(dtype).reshape(split).transpose(inv).reshape(N_DEV, BIG_ROWS[n], D))
    parts.append(jnp.zeros((N_DEV, BIG_R - sum(BIG_ROWS.values()), D), dtype))
    return jnp.concatenate(parts, axis=1)


def _rows128(a):
    a = a.reshape(-1)
    pad = (-a.shape[0]) % CH
    if pad:
        a = jnp.concatenate([a, jnp.zeros((pad,), a.dtype)])
    return a.reshape(-1, CH)


def _pack_rows(arrs, total):
    parts = [_rows128(a) for a in arrs]
    n = sum(p.shape[0] for p in parts)
    parts.append(jnp.zeros((total - n, CH), F32))
    return jnp.concatenate(parts, axis=0)


def _unpack_rows(flat, shapes):
    out, o = [], 0
    for s in shapes:
        size = int(np.prod(s))
        rows = -(-size // CH)
        out.append(flat[o:o + rows].reshape(-1)[:size].reshape(s))
        o += rows
    return out


SMALL_SHARDED = ("meta", "conv_a", "ssd_conv_w")
SMALL_SHARD_SHAPE = {"meta": (N_META, 128), "conv_a": (DEPTH, 3, 128), "ssd_conv_w": (DEPTH, 4, 256)}
SMALL_FULL_SHAPE = {"meta": (N_META, D), "conv_a": (DEPTH, 3, D), "ssd_conv_w": (DEPTH, 4, 2048)}
SMALL_REPL = ("ssd_conv_b", "ssd_dt_bias", "ssd_a_log", "ssd_d", "ssd_norm", "norm_mix_pre", "norm_mix_post", "norm_ffn_pre",
              "norm_ffn_post")
SMALL_REPL_SHAPE = {"ssd_conv_b": (DEPTH, 2048), "ssd_dt_bias": (DEPTH, SSD_HEADS), "ssd_a_log": (DEPTH, SSD_HEADS),
                    "ssd_d": (DEPTH, SSD_HEADS), "ssd_norm": (DEPTH, D), "norm_mix_pre": (DEPTH, D), "norm_mix_post": (DEPTH, D),
                    "norm_ffn_pre": (DEPTH, D), "norm_ffn_post": (DEPTH, D)}


def _gather_small_full(gathered, n):
    nd = gathered.ndim
    perm = tuple(range(1, nd - 1)) + (0, nd - 1)
    return gathered.transpose(perm).reshape(SMALL_FULL_SHAPE[n])


WEIGHTS = ("meta", "w_in", "conv_a", "ssd_conv_w", "ssd_conv_b", "ssd_dt_bias", "ssd_a_log", "ssd_d", "ssd_norm", "w_branch", "w_out",
           "w_ffn_in", "w_ffn_out", "norm_mix_pre", "norm_mix_post", "norm_ffn_pre", "norm_ffn_post")


def kernel(x, meta, w_in, conv_a, ssd_conv_w, ssd_conv_b, ssd_dt_bias, ssd_a_log, ssd_d, ssd_norm, w_branch, w_out, w_ffn_in, w_ffn_out, norm_mix_pre, norm_mix_post, norm_ffn_pre, norm_ffn_post, loss_target, m_meta, m_w_in, m_conv_a, m_ssd_conv_w, m_ssd_conv_b, m_ssd_dt_bias, m_ssd_a_log, m_ssd_d, m_ssd_norm, m_w_branch, m_w_out, m_w_ffn_in, m_w_ffn_out, m_norm_mix_pre, m_norm_mix_post, m_norm_ffn_pre, m_norm_ffn_post, v_meta, v_w_in, v_conv_a, v_ssd_conv_w, v_ssd_conv_b, v_ssd_dt_bias, v_ssd_a_log, v_ssd_d, v_ssd_norm, v_w_branch, v_w_out, v_w_ffn_in, v_w_ffn_out, v_norm_mix_pre, v_norm_mix_post, v_norm_ffn_pre, v_norm_ffn_post):
    w = dict(meta=meta, w_in=w_in, conv_a=conv_a, ssd_conv_w=ssd_conv_w, ssd_conv_b=ssd_conv_b, ssd_dt_bias=ssd_dt_bias,
             ssd_a_log=ssd_a_log, ssd_d=ssd_d, ssd_norm=ssd_norm, w_branch=w_branch, w_out=w_out, w_ffn_in=w_ffn_in,
             w_ffn_out=w_ffn_out, norm_mix_pre=norm_mix_pre, norm_mix_post=norm_mix_post, norm_ffn_pre=norm_ffn_pre,
             norm_ffn_post=norm_ffn_post)
    m = dict(meta=m_meta, w_in=m_w_in, conv_a=m_conv_a, ssd_conv_w=m_ssd_conv_w, ssd_conv_b=m_ssd_conv_b, ssd_dt_bias=m_ssd_dt_bias,
             ssd_a_log=m_ssd_a_log, ssd_d=m_ssd_d, ssd_norm=m_ssd_norm, w_branch=m_w_branch, w_out=m_w_out, w_ffn_in=m_w_ffn_in,
             w_ffn_out=m_w_ffn_out, norm_mix_pre=m_norm_mix_pre, norm_mix_post=m_norm_mix_post, norm_ffn_pre=m_norm_ffn_pre,
             norm_ffn_post=m_norm_ffn_post)
    v = dict(meta=v_meta, w_in=v_w_in, conv_a=v_conv_a, ssd_conv_w=v_ssd_conv_w, ssd_conv_b=v_ssd_conv_b, ssd_dt_bias=v_ssd_dt_bias,
             ssd_a_log=v_ssd_a_log, ssd_d=v_ssd_d, ssd_norm=v_ssd_norm, w_branch=v_w_branch, w_out=v_w_out, w_ffn_in=v_w_ffn_in,
             w_ffn_out=v_w_ffn_out, norm_mix_pre=v_norm_mix_pre, norm_mix_post=v_norm_mix_post, norm_ffn_pre=v_norm_ffn_pre,
             norm_ffn_post=v_norm_ffn_post)
    xi, yi, ci = _my_place()
    dev = _flat(xi, yi, ci)

    full = unpack_big_full(all_gather(pack_big(w, BF16), name="gather_big"))
    small_shard = _pack_rows([w[n] for n in SMALL_SHARDED], 40)
    small_all = all_gather(small_shard, name="gather_small")
    small_full = {}
    o = 0
    for n in SMALL_SHARDED:
        rows = int(np.prod(SMALL_SHARD_SHAPE[n])) // CH
        small_full[n] = _gather_small_full(small_all[:, o:o + rows].reshape((N_DEV,) + SMALL_SHARD_SHAPE[n]), n)
        o += rows

    layers = []
    for l in range(DEPTH):
        p = {n: full[n][l] for n in BIG}
        p["conv_a"] = small_full["conv_a"][l]
        p["ssd_conv_w"] = small_full["ssd_conv_w"][l]
        for n in SMALL_REPL:
            p[n] = w[n][l]
        layers.append(p)

    loss_blk, grad_x, gmeta, grads = local_step(x[0], loss_target[0], small_full["meta"], layers)

    gfull = {n: jnp.stack([grads[l][n] for l in range(DEPTH)]) for n in BIG}
    from_sibling, own = pair_exchange(pack_big_full(gfull, BF16), name="exchange_pair")
    recv = chip_exchange(pair_sum(own, from_sibling, name="sum_pair"), name="exchange_chip")
    g_flat, d_flat, nm_flat, nv_flat = adamw_big(recv, pack_big(w, F32), pack_big(m, F32), pack_big(v, F32), name="adamw_big")
    out_g, out_d, out_m, out_v = unpack_big(g_flat), unpack_big(d_flat), unpack_big(nm_flat), unpack_big(nv_flat)

    small_names = SMALL_SHARDED + SMALL_REPL
    small_grads = [gmeta] + [jnp.stack([grads[l][n] for l in range(DEPTH)]) for n in small_names[1:]]
    small_shapes = [SMALL_FULL_SHAPE[n] for n in SMALL_SHARDED] + [SMALL_REPL_SHAPE[n] for n in SMALL_REPL]
    sm = _pack_rows(small_grads + [loss_blk[0:1]], 424)
    sm_sum = sum_slots(all_gather(sm, name="gather_small_grads"), name="sum_small_grads")
    summed = _unpack_rows(sm_sum, small_shapes + [(1, CH)])
    loss = summed[-1][0, 0]
    sg = dict(zip(small_names, summed[:-1]))
    for n in SMALL_SHARDED:
        width = SMALL_SHARD_SHAPE[n][-1]
        sg[n] = lax.dynamic_slice_in_dim(sg[n], dev * width, width, axis=sg[n].ndim - 1)
    pk = lambda d: _pack_rows([d[n] for n in small_names], 160)
    sd, snm, snv = adamw_small(pk(w), pk(sg), pk(m), pk(v), name="adamw_small")
    shard_shapes = [SMALL_SHARD_SHAPE[n] for n in SMALL_SHARDED] + [SMALL_REPL_SHAPE[n] for n in SMALL_REPL]
    for dst, flat in ((out_d, sd), (out_m, snm), (out_v, snv)):
        dst.update(zip(small_names, _unpack_rows(flat, shard_shapes)))
    out_g.update(sg)

    return (loss, grad_x[None], *[out_g[n] for n in WEIGHTS], *[out_d[n] for n in WEIGHTS], *[out_m[n] for n in WEIGHTS],
            *[out_v[n] for n in WEIGHTS])
```

```python
import functools
import math

import numpy as np
import jax
import jax.numpy as jnp
from jax import lax
from jax.experimental import pallas as pl
from jax.experimental.pallas import tpu as pltpu

F32, BF16 = jnp.float32, jnp.bfloat16
HI = lax.Precision.HIGHEST
MESH_ID = pl.DeviceIdType.MESH

D = 1024
CH = 128
N_META = 16
N_PAD = CH - N_META
EPS = 1e-6
N_DEV = 8
DEPTH = 2
SSD_HEADS = 16
RET_HEADS = 4
SB_HEADS = 8
D_FF = 2816
ROPE_BASE = 10000.0

NF = 14336
COL_GATE = 10

ADAM_LR, ADAM_B1, ADAM_B2, ADAM_EPS, ADAM_WD, ADAM_STEP = 0.001, 0.9, 0.999, 1e-08, 0.01, 10

VMEM_BYTES = 48 * 1024 * 1024


def _pick(n, cands):
    for c in cands:
        if n % c == 0:
            return c
    raise ValueError((n, cands))


def _tok_block(t):
    return _pick(t, (384, 128))


def _cparams(ngrid, vmem=VMEM_BYTES):
    return pltpu.CompilerParams(dimension_semantics=("arbitrary",) * ngrid, vmem_limit_bytes=vmem)


def _iota(shape, dim):
    return lax.broadcasted_iota(jnp.int32, shape, dim)


def _sigmoid(x):
    return 1.0 / (1.0 + jnp.exp(-x))


def _silu(x):
    return x * _sigmoid(x)


def _dsilu(x):
    s = _sigmoid(x)
    return s * (1.0 + x * (1.0 - s))


def _softplus(x):
    return jnp.maximum(x, 0.0) + jnp.log(1.0 + jnp.exp(-jnp.abs(x)))


def _dot(a, b):
    return jnp.dot(a.astype(BF16), b.astype(BF16), preferred_element_type=F32)


def _dot_nt(a, b):
    return lax.dot_general(a.astype(BF16), b.astype(BF16), (((1,), (1,)), ((), ())), preferred_element_type=F32)


def _dot_tn(a, b):
    return lax.dot_general(a.astype(BF16), b.astype(BF16), (((0,), (0,)), ((), ())), preferred_element_type=F32)


def _dot_hi(a, b):
    return jnp.dot(a, b, precision=HI, preferred_element_type=F32)


def mm(a, b, *, name, out_dtype=F32, add=None, tm=None, tn=None, tk=None):
    m, k = a.shape
    k2, n = b.shape
    assert k == k2
    tm = tm or _pick(m, (1376, 384, 128))
    tn = tn or _pick(n, (512, 384, 256, 128))
    tk = tk or _pick(k, (1024, 1408, 512, 384, 128))
    nk = k // tk
    has_add = add is not None

    def body(*refs):
        if has_add:
            a_ref, b_ref, c_ref, o_ref = refs[:4]
            scr = refs[4:]
        else:
            a_ref, b_ref, o_ref = refs[:3]
            c_ref = None
            scr = refs[3:]
        x = _dot(a_ref[...], b_ref[...])
        if nk == 1:
            if has_add:
                x = x + c_ref[...]
            o_ref[...] = x.astype(out_dtype)
        else:
            acc = scr[0]
            kk = pl.program_id(2)

            @pl.when(kk == 0)
            def _():
                acc[...] = x

            @pl.when(kk > 0)
            def _():
                acc[...] += x

            @pl.when(kk == nk - 1)
            def _():
                r = acc[...]
                if has_add:
                    r = r + c_ref[...]
                o_ref[...] = r.astype(out_dtype)

    in_specs = [pl.BlockSpec((tm, tk), lambda i, j, kk: (i, kk)), pl.BlockSpec((tk, tn), lambda i, j, kk: (kk, j))]
    args = [a, b]
    if has_add:
        in_specs.append(pl.BlockSpec((tm, tn), lambda i, j, kk: (i, j)))
        args.append(add)
    return pl.pallas_call(
        body, name=name, out_shape=jax.ShapeDtypeStruct((m, n), out_dtype), grid=(m // tm, n // tn, nk),
        in_specs=in_specs, out_specs=pl.BlockSpec((tm, tn), lambda i, j, kk: (i, j)),
        scratch_shapes=[pltpu.VMEM((tm, tn), F32)] if nk > 1 else [],
        compiler_params=_cparams(3))(*args)


def mm_tn(a, b, *, name, tm=None, tn=None, tk=None):
    t, m = a.shape
    t2, n = b.shape
    assert t == t2
    tm = tm or _pick(m, (1024, 1408, 512, 128))
    tn = tn or _pick(n, (512, 384, 256, 128))
    tk = tk or _pick(t, (1376, 384, 128))
    nk = t // tk

    def body(a_ref, b_ref, o_ref):
        x = _dot_tn(a_ref[...], b_ref[...])
        kk = pl.program_id(2)

        @pl.when(kk == 0)
        def _():
            o_ref[...] = x

        @pl.when(kk > 0)
        def _():
            o_ref[...] += x

    return pl.pallas_call(
        body, name=name, out_shape=jax.ShapeDtypeStruct((m, n), F32), grid=(m // tm, n // tn, nk),
        in_specs=[pl.BlockSpec((tk, tm), lambda i, j, kk: (kk, i)), pl.BlockSpec((tk, tn), lambda i, j, kk: (kk, j))],
        out_specs=pl.BlockSpec((tm, tn), lambda i, j, kk: (i, j)),
        compiler_params=_cparams(3))(a, b)


def rms_fwd(x, w, *, name, out_dtype=F32, res=None):
    t, d = x.shape
    tb = _tok_block(t)
    has_res = res is not None

    def body(*refs):
        if has_res:
            x_ref, w_ref, r_ref, o_ref = refs
        else:
            x_ref, w_ref, o_ref = refs
        xv = x_ref[...]
        y = xv * lax.rsqrt(jnp.mean(xv * xv, axis=-1, keepdims=True) + EPS) * w_ref[...]
        if has_res:
            y = y + r_ref[...]
        o_ref[...] = y.astype(out_dtype)

    blk = pl.BlockSpec((tb, d), lambda i: (i, 0))
    wspec = pl.BlockSpec((1, d), lambda i: (0, 0))
    in_specs = [blk, wspec] + ([blk] if has_res else [])
    args = [x, w] + ([res] if has_res else [])
    return pl.pallas_call(body, name=name, out_shape=jax.ShapeDtypeStruct((t, d), out_dtype), grid=(t // tb,),
                          in_specs=in_specs, out_specs=blk, compiler_params=_cparams(1))(*args)


def rms_bwd(x, w, dy, *, name, add=None):
    t, d = x.shape
    tb = _tok_block(t)
    has_add = add is not None

    def body(*refs):
        if has_add:
            x_ref, w_ref, dy_ref, a_ref, dx_ref, dw_ref = refs
        else:
            x_ref, w_ref, dy_ref, dx_ref, dw_ref = refs
        xv = x_ref[...]
        dyv = dy_ref[...]
        r = lax.rsqrt(jnp.mean(xv * xv, axis=-1, keepdims=True) + EPS)
        g = dyv * w_ref[...]
        dx = r * g - xv * (r * r * r) * jnp.mean(xv * g, axis=-1, keepdims=True)
        if has_add:
            dx = dx + a_ref[...]
        dx_ref[...] = dx
        part = jnp.sum(dyv * xv * r, axis=0, keepdims=True)

        @pl.when(pl.program_id(0) == 0)
        def _():
            dw_ref[...] = part

        @pl.when(pl.program_id(0) > 0)
        def _():
            dw_ref[...] += part

    blk = pl.BlockSpec((tb, d), lambda i: (i, 0))
    wspec = pl.BlockSpec((1, d), lambda i: (0, 0))
    in_specs = [blk, wspec, blk] + ([blk] if has_add else [])
    args = [x, w, dy] + ([add] if has_add else [])
    return pl.pallas_call(body, name=name,
                          out_shape=[jax.ShapeDtypeStruct((t, d), F32), jax.ShapeDtypeStruct((1, d), F32)],
                          grid=(t // tb,), in_specs=in_specs, out_specs=[blk, wspec], compiler_params=_cparams(1))(*args)


def _shift_down(cur, prev8, k):
    z = jnp.concatenate([prev8, cur], axis=0)
    return pltpu.roll(z, k, 0)[8:]


def _shift_up(cur, next8, k):
    n = cur.shape[0] + 8
    z = jnp.concatenate([cur, next8], axis=0)
    return pltpu.roll(z, n - k, 0)[:cur.shape[0]]


def _prev8_spec(tb, width, col):
    return pl.BlockSpec((8, width), lambda i: (jnp.maximum(i * (tb // 8) - 1, 0), col))


def _next8_spec(tb, width, col, t):
    return pl.BlockSpec((8, width), lambda i: (jnp.minimum((i + 1) * (tb // 8), t // 8 - 1), col))


def _row_valid(i, tb, n, offset=0):
    rows = i * tb + offset + _iota((n, 1), 0)
    return (rows >= N_PAD).astype(F32)


def conv_a_fwd(proj, w8, *, name):
    t = proj.shape[0]
    tb = _tok_block(t)

    def body(b_ref, c_ref, x_ref, cp_ref, xp_ref, w_ref, o_ref):
        i = pl.program_id(0)
        u = c_ref[...] * x_ref[...] * _row_valid(i, tb, tb)
        up = cp_ref[...] * xp_ref[...] * _row_valid(i, tb, 8, -8) * (i > 0).astype(F32)
        w = w_ref[...]
        conv = w[2:3] * u + w[1:2] * _shift_down(u, up, 1) + w[0:1] * _shift_down(u, up, 2)
        o_ref[...] = b_ref[...] * conv

    blk = lambda col: pl.BlockSpec((tb, D), lambda i: (i, col))
    return pl.pallas_call(
        body, name=name, out_shape=jax.ShapeDtypeStruct((t, D), F32), grid=(t // tb,),
        in_specs=[blk(0), blk(1), blk(2), _prev8_spec(tb, D, 1), _prev8_spec(tb, D, 2), pl.BlockSpec((8, D), lambda i: (0, 0))],
        out_specs=pl.BlockSpec((tb, D), lambda i: (i, 0)), compiler_params=_cparams(1))(proj, proj, proj, proj, proj, w8)


def conv_a_bwd(proj, w8, dy, *, name):
    t = proj.shape[0]
    tb = _tok_block(t)
    nblk = t // tb

    def body(b_ref, c_ref, x_ref, cp_ref, xp_ref, dy_ref, dyn_ref, bn_ref, w_ref, o_ref, dw_ref):
        i = pl.program_id(0)
        vm = _row_valid(i, tb, tb)
        cv, xv, bv, dyv = c_ref[...], x_ref[...], b_ref[...], dy_ref[...]
        u = cv * xv * vm
        up = cp_ref[...] * xp_ref[...] * _row_valid(i, tb, 8, -8) * (i > 0).astype(F32)
        w = w_ref[...]
        u1 = _shift_down(u, up, 1)
        u2 = _shift_down(u, up, 2)
        conv = w[2:3] * u + w[1:2] * u1 + w[0:1] * u2
        dconv = dyv * bv
        dconv_n = dyn_ref[...] * bn_ref[...] * (i < nblk - 1).astype(F32)
        du = w[2:3] * dconv + w[1:2] * _shift_up(dconv, dconv_n, 1) + w[0:1] * _shift_up(dconv, dconv_n, 2)
        o_ref[:, 0:D] = dyv * conv
        o_ref[:, D:2 * D] = du * xv * vm
        o_ref[:, 2 * D:3 * D] = du * cv * vm

        @pl.when(i == 0)
        def _():
            dw_ref[...] = jnp.zeros_like(dw_ref)

        dw_ref[0:1, :] += jnp.sum(dconv * u2, axis=0, keepdims=True)
        dw_ref[1:2, :] += jnp.sum(dconv * u1, axis=0, keepdims=True)
        dw_ref[2:3, :] += jnp.sum(dconv * u, axis=0, keepdims=True)

    blk = lambda col: pl.BlockSpec((tb, D), lambda i: (i, col))
    w8spec = pl.BlockSpec((8, D), lambda i: (0, 0))
    return pl.pallas_call(
        body, name=name,
        out_shape=[jax.ShapeDtypeStruct((t, 3 * D), F32), jax.ShapeDtypeStruct((8, D), F32)], grid=(nblk,),
        in_specs=[blk(0), blk(1), blk(2), _prev8_spec(tb, D, 1), _prev8_spec(tb, D, 2), blk(0),
                  _next8_spec(tb, D, 0, t), _next8_spec(tb, D, 0, t), w8spec],
        out_specs=[pl.BlockSpec((tb, 3 * D), lambda i: (i, 0)), w8spec],
        compiler_params=_cparams(1))(proj, proj, proj, proj, proj, dy, dy, proj, w8)


XBC_W = 2048


def ssd_conv_fwd(proj, w8, b, *, name):
    t = proj.shape[0]
    tb = _tok_block(t)

    def body(x_ref, xp_ref, w_ref, b_ref, o_ref):
        i = pl.program_id(0)
        xm = x_ref[...] * _row_valid(i, tb, tb)
        xmp = xp_ref[...] * _row_valid(i, tb, 8, -8) * (i > 0).astype(F32)
        w = w_ref[...]
        c = w[3:4] * xm + w[2:3] * _shift_down(xm, xmp, 1) + w[1:2] * _shift_down(xm, xmp, 2) + w[0:1] * _shift_down(xm, xmp, 3)
        o_ref[...] = _silu(c + b_ref[...])

    return pl.pallas_call(
        body, name=name, out_shape=jax.ShapeDtypeStruct((t, XBC_W), F32), grid=(t // tb,),
        in_specs=[pl.BlockSpec((tb, XBC_W), lambda i: (i, 2)), _prev8_spec(tb, XBC_W, 2),
                  pl.BlockSpec((8, XBC_W), lambda i: (0, 0)), pl.BlockSpec((1, XBC_W), lambda i: (0, 0))],
        out_specs=pl.BlockSpec((tb, XBC_W), lambda i: (i, 0)), compiler_params=_cparams(1))(proj, proj, w8, b)


def ssd_conv_bwd_pre(proj, w8, b, dxa, *, name):
    t = proj.shape[0]
    tb = _tok_block(t)

    def body(x_ref, xp_ref, w_ref, b_ref, d_ref, o_ref, dw_ref, db_ref):
        i = pl.program_id(0)
        xm = x_ref[...] * _row_valid(i, tb, tb)
        xmp = xp_ref[...] * _row_valid(i, tb, 8, -8) * (i > 0).astype(F32)
        w = w_ref[...]
        x1, x2, x3 = _shift_down(xm, xmp, 1), _shift_down(xm, xmp, 2), _shift_down(xm, xmp, 3)
        c = w[3:4] * xm + w[2:3] * x1 + w[1:2] * x2 + w[0:1] * x3 + b_ref[...]
        dpre = d_ref[...] * _dsilu(c)
        o_ref[...] = dpre

        @pl.when(i == 0)
        def _():
            dw_ref[...] = jnp.zeros_like(dw_ref)
            db_ref[...] = jnp.zeros_like(db_ref)

        dw_ref[0:1, :] += jnp.sum(dpre * x3, axis=0, keepdims=True)
        dw_ref[1:2, :] += jnp.sum(dpre * x2, axis=0, keepdims=True)
        dw_ref[2:3, :] += jnp.sum(dpre * x1, axis=0, keepdims=True)
        dw_ref[3:4, :] += jnp.sum(dpre * xm, axis=0, keepdims=True)
        db_ref[...] += jnp.sum(dpre, axis=0, keepdims=True)

    w8spec = pl.BlockSpec((8, XBC_W), lambda i: (0, 0))
    bspec = pl.BlockSpec((1, XBC_W), lambda i: (0, 0))
    return pl.pallas_call(
        body, name=name,
        out_shape=[jax.ShapeDtypeStruct((t, XBC_W), F32), jax.ShapeDtypeStruct((8, XBC_W), F32), jax.ShapeDtypeStruct((1, XBC_W), F32)],
        grid=(t // tb,),
        in_specs=[pl.BlockSpec((tb, XBC_W), lambda i: (i, 2)), _prev8_spec(tb, XBC_W, 2), w8spec, bspec,
                  pl.BlockSpec((tb, XBC_W), lambda i: (i, 0))],
        out_specs=[pl.BlockSpec((tb, XBC_W), lambda i: (i, 0)), w8spec, bspec],
        compiler_params=_cparams(1))(proj, proj, w8, b, dxa)


def ssd_conv_bwd_in(dpre, w8, *, name):
    t = dpre.shape[0]
    tb = _tok_block(t)
    nblk = t // tb

    def body(d_ref, dn_ref, w_ref, o_ref):
        i = pl.program_id(0)
        d = d_ref[...]
        dn = dn_ref[...] * (i < nblk - 1).astype(F32)
        w = w_ref[...]
        dx = w[3:4] * d + w[2:3] * _shift_up(d, dn, 1) + w[1:2] * _shift_up(d, dn, 2) + w[0:1] * _shift_up(d, dn, 3)
        o_ref[...] = dx * _row_valid(i, tb, tb)

    return pl.pallas_call(
        body, name=name, out_shape=jax.ShapeDtypeStruct((t, XBC_W), F32), grid=(nblk,),
        in_specs=[pl.BlockSpec((tb, XBC_W), lambda i: (i, 0)), _next8_spec(tb, XBC_W, 0, t), pl.BlockSpec((8, XBC_W), lambda i: (0, 0))],
        out_specs=pl.BlockSpec((tb, XBC_W), lambda i: (i, 0)), compiler_params=_cparams(1))(dpre, dpre, w8)


def _col(x, h):
    return jnp.sum(jnp.where(_iota(x.shape, 1) == h, x, 0.0), axis=1, keepdims=True)


def _row(x, h):
    return jnp.sum(jnp.where(_iota(x.shape, 0) == h, x, 0.0), axis=0, keepdims=True)


def _ssd_common(xa, dtr, dtb, alog, c):
    vm = _row_valid(c, CH, CH)
    xs = xa[:, :D] * vm
    dt = _softplus(dtr + dtb)
    a = -jnp.exp(alog) * dt
    tri = (_iota((CH, CH), 0) >= _iota((CH, CH), 1)).astype(F32)
    acs = _dot_hi(tri, a)
    return vm, xs, dt, a, acs, acs.T


def _pair_lanes(v0, v1):
    lane = _iota((1, CH), 1)
    return jnp.where(lane < 64, v0, v1)


def _ssd_pairs_fwd(xs, xa, dt, acs, acs_t, dsk, hins):
    causal = _iota((CH, CH), 0) >= _iota((CH, CH), 1)
    lane = _iota((CH, CH), 1)
    last = _iota((CH, 1), 0) == CH - 1
    bgs = [xa[:, D + CH * g:D + CH * (g + 1)] for g in range(4)]
    cgs = [xa[:, D + 512 + CH * g:D + 512 + CH * (g + 1)] for g in range(4)]
    g_mats = [_dot_nt(cgs[g], bgs[g]) for g in range(4)]
    ps = []
    for q in range(8):
        h0, h1 = 2 * q, 2 * q + 1
        xs_p = xs[:, CH * q:CH * (q + 1)]
        ac0, ac1 = _col(acs, h0), _col(acs, h1)
        ar0, ar1 = _row(acs_t, h0), _row(acs_t, h1)
        l0 = jnp.exp(jnp.where(causal, ac0 - ar0, -1e30))
        l1 = jnp.exp(jnp.where(causal, ac1 - ar1, -1e30))
        dt_p = _pair_lanes(_col(dt, h0), _col(dt, h1))
        ac_p = _pair_lanes(ac0, ac1)
        al0 = jnp.sum(jnp.where(last, ac0, 0.0), axis=0, keepdims=True)
        al1 = jnp.sum(jnp.where(last, ac1, 0.0), axis=0, keepdims=True)
        ps.append(dict(bg=bgs[q // 2], cg=cgs[q // 2], xs_p=xs_p, l0=l0, l1=l1, dt_p=dt_p, x=xs_p * dt_p, eac=jnp.exp(ac_p),
                       dsv=jnp.exp(_pair_lanes(al0, al1) - ac_p), al0=al0, al1=al1,
                       cd=jnp.where(_iota((CH, 1), 0) < 64, jnp.exp(al0), jnp.exp(al1)),
                       d_p=_pair_lanes(_col(dsk, h0), _col(dsk, h1))))
    for q, p in enumerate(ps):
        p["m0"], p["m1"] = g_mats[q // 2] * p["l0"], g_mats[q // 2] * p["l1"]
    for q, p in enumerate(ps):
        p["yd0"], p["yd1"] = _dot(p["m0"], p["x"]), _dot(p["m1"], p["x"])
        p["yoff_raw"] = _dot_nt(p["cg"], hins[q])
        p["s"] = _dot_tn(p["x"] * p["dsv"], p["bg"])
    for p in ps:
        p["y"] = jnp.where(lane < 64, p["yd0"], p["yd1"]) + p["yoff_raw"] * p["eac"] + p["xs_p"] * p["d_p"]
    return ps


def _ssd_gate_norm(y, z, nw):
    yv = y * _silu(z)
    outs, rs = [], []
    for g in range(4):
        yg = yv[:, 256 * g:256 * (g + 1)]
        r = lax.rsqrt(jnp.mean(yg * yg, axis=-1, keepdims=True) + EPS)
        outs.append(yg * r * nw[:, 256 * g:256 * (g + 1)])
        rs.append(r)
    return yv, jnp.concatenate(outs, axis=1), rs


def ssd_fwd(xa, proj, pdt, dtb, alog, dsk, nw, *, name):
    t = xa.shape[0]
    nc = t // CH

    def body(xa_ref, dtr_ref, z_ref, dtb_ref, alog_ref, dsk_ref, nw_ref, y_ref, hs_ref, h_scr):
        c = pl.program_id(0)

        @pl.when(c == 0)
        def _():
            h_scr[...] = jnp.zeros_like(h_scr)

        xa_v = xa_ref[...]
        vm, xs, dt, a, acs, acs_t = _ssd_common(xa_v, dtr_ref[...], dtb_ref[...], alog_ref[...], c)
        dsk_v = dsk_ref[...]
        hins = [h_scr[q] for q in range(8)]
        ps = _ssd_pairs_fwd(xs, xa_v, dt, acs, acs_t, dsk_v, hins)
        for q, p in enumerate(ps):
            hs_ref[0, q] = hins[q]
            h_scr[q] = hins[q] * p["cd"] + p["s"]
        y = jnp.concatenate([p["y"] for p in ps], axis=1)
        _, out, _ = _ssd_gate_norm(y, z_ref[...], nw_ref[...])
        y_ref[...] = out

    small = pl.BlockSpec((1, CH), lambda c: (0, 0))
    return pl.pallas_call(
        body, name=name,
        out_shape=[jax.ShapeDtypeStruct((t, D), F32), jax.ShapeDtypeStruct((nc, 8, CH, CH), F32)], grid=(nc,),
        in_specs=[pl.BlockSpec((CH, XBC_W), lambda c: (c, 0)), pl.BlockSpec((CH, CH), lambda c: (c, 0)),
                  pl.BlockSpec((CH, D), lambda c: (c, 3)), small, small, small, pl.BlockSpec((1, D), lambda c: (0, 0))],
        out_specs=[pl.BlockSpec((CH, D), lambda c: (c, 0)), pl.BlockSpec((1, 8, CH, CH), lambda c: (c, 0, 0, 0))],
        scratch_shapes=[pltpu.VMEM((8, CH, CH), F32)], compiler_params=_cparams(1))(xa, pdt, proj, dtb, alog, dsk, nw)


def ssd_bwd(xa, proj, pdt, hs, dyb, dtb, alog, dsk, nw, *, name):
    t = xa.shape[0]
    nc = t // CH

    def body(xa_ref, dtr_ref, z_ref, hs_ref, dy_ref, dtb_ref, alog_ref, dsk_ref, nw_ref,
             dz_ref, dxa_ref, ddt_ref, gdtb_ref, galog_ref, gdsk_ref, gnw_ref, dh_scr):
        step = pl.program_id(0)
        c = nc - 1 - step

        @pl.when(step == 0)
        def _():
            dh_scr[...] = jnp.zeros_like(dh_scr)
            gdtb_ref[...] = jnp.zeros_like(gdtb_ref)
            galog_ref[...] = jnp.zeros_like(galog_ref)
            gdsk_ref[...] = jnp.zeros_like(gdsk_ref)
            gnw_ref[...] = jnp.zeros_like(gnw_ref)

        xa_v = xa_ref[...]
        dtr = dtr_ref[...]
        dtb_v = dtb_ref[...]
        alog_v = alog_ref[...]
        vm, xs, dt, a, acs, acs_t = _ssd_common(xa_v, dtr, dtb_v, alog_v, c)
        dsk_v = dsk_ref[...]
        z = z_ref[...]
        nw_v = nw_ref[...]
        lane1 = _iota((1, CH), 1)
        sub1 = _iota((CH, 1), 0)
        lane = _iota((CH, CH), 1)

        hins = [hs_ref[0, q] for q in range(8)]
        pairs = _ssd_pairs_fwd(xs, xa_v, dt, acs, acs_t, dsk_v, hins)
        y_pre = jnp.concatenate([p["y"] for p in pairs], axis=1)

        dout = dy_ref[...]
        sz = _silu(z)
        yv = y_pre * sz
        dyv_parts = []
        gnw_parts = []
        for g in range(4):
            sl = slice(256 * g, 256 * (g + 1))
            yg = yv[:, sl]
            r = lax.rsqrt(jnp.mean(yg * yg, axis=-1, keepdims=True) + EPS)
            gy = dout[:, sl] * nw_v[:, sl]
            dyv_parts.append(r * gy - yg * (r * r * r) * jnp.mean(yg * gy, axis=-1, keepdims=True))
            gnw_parts.append(jnp.sum(dout[:, sl] * yg * r, axis=0, keepdims=True))
        dyv = jnp.concatenate(dyv_parts, axis=1)
        gnw_ref[...] += jnp.concatenate(gnw_parts, axis=1)
        dz_ref[...] = dyv * y_pre * _dsilu(z)
        dy_pre = dyv * sz

        dacs_c = jnp.zeros((CH, CH), F32)
        dacs_r = jnp.zeros((CH, CH), F32)
        ddt = jnp.zeros((CH, CH), F32)
        gdsk = jnp.zeros((1, CH), F32)
        dxs_parts = []
        db_g = [None] * 4
        dc_g = [None] * 4
        dg_g = [None] * 4

        def acc(lst, g, v):
            lst[g] = v if lst[g] is None else lst[g] + v

        m_lo = lane < 64
        dhouts = [dh_scr[q] for q in range(8)]
        mats = []
        for q in range(8):
            p = pairs[q]
            dy = dy_pre[:, CH * q:CH * (q + 1)]
            dye = dy * p["eac"]
            mats.append(dict(
                dy=dy, dye=dye,
                dm0=_dot_nt(jnp.where(m_lo, dy, 0.0), p["x"]), dm1=_dot_nt(jnp.where(m_lo, 0.0, dy), p["x"]),
                dx0=_dot_tn(p["m0"], dy), dx1=_dot_tn(p["m1"], dy),
                dc=_dot(dye, hins[q]), dhin=_dot_tn(dye, p["cg"]),
                w1=_dot_nt(p["bg"], dhouts[q]), db=_dot(p["x"] * p["dsv"], dhouts[q])))

        for q in range(8):
            p = pairs[q]
            mt = mats[q]
            g = q // 2
            h0, h1 = 2 * q, 2 * q + 1
            dy = mt["dy"]
            hin = hins[q]
            dhout = dhouts[q]
            x = p["x"]
            dxs = dy * p["d_p"]
            t_sk = dy * p["xs_p"]
            gdsk = gdsk + jnp.where(lane1 == h0, jnp.sum(jnp.where(m_lo, t_sk, 0.0)), 0.0) \
                        + jnp.where(lane1 == h1, jnp.sum(jnp.where(m_lo, 0.0, t_sk)), 0.0)
            dx = jnp.where(m_lo, mt["dx0"], mt["dx1"])
            for hh, lk, mm_, dm in ((h0, p["l0"], p["m0"], mt["dm0"]), (h1, p["l1"], p["m1"], mt["dm1"])):
                acc(dg_g, g, dm * lk)
                qm = dm * mm_
                dacs_c = dacs_c + jnp.where(lane1 == hh, jnp.sum(qm, axis=1, keepdims=True), 0.0)
                dacs_r = dacs_r - jnp.where(sub1 == hh, jnp.sum(qm, axis=0, keepdims=True), 0.0)
            acc(dc_g, g, mt["dc"])
            t_off = dy * p["yoff_raw"] * p["eac"]
            dacs_c = dacs_c + jnp.where(lane1 == h0, jnp.sum(jnp.where(m_lo, t_off, 0.0), axis=1, keepdims=True), 0.0) \
                            + jnp.where(lane1 == h1, jnp.sum(jnp.where(m_lo, 0.0, t_off), axis=1, keepdims=True), 0.0)
            dhin = mt["dhin"] + dhout * p["cd"]
            w1 = mt["w1"]
            dx = dx + p["dsv"] * w1
            t_ds = x * w1 * p["dsv"]
            dd0 = jnp.sum(jnp.where(m_lo, t_ds, 0.0), axis=1, keepdims=True)
            dd1 = jnp.sum(jnp.where(m_lo, 0.0, t_ds), axis=1, keepdims=True)
            acc(db_g, g, mt["db"])
            t_cd = dhout * hin
            sub_lo = _iota((CH, CH), 0) < 64
            dcd0 = jnp.sum(jnp.where(sub_lo, t_cd, 0.0)) * jnp.exp(p["al0"])
            dcd1 = jnp.sum(jnp.where(sub_lo, 0.0, t_cd)) * jnp.exp(p["al1"])
            last = (sub1 == CH - 1)
            dacs_c = dacs_c + jnp.where(lane1 == h0, jnp.where(last, jnp.sum(dd0) + dcd0, 0.0) - dd0, 0.0) \
                            + jnp.where(lane1 == h1, jnp.where(last, jnp.sum(dd1) + dcd1, 0.0) - dd1, 0.0)
            dh_scr[q] = dhin
            dxs = dxs + dx * p["dt_p"]
            t_dt = dx * p["xs_p"]
            ddt = ddt + jnp.where(lane1 == h0, jnp.sum(jnp.where(m_lo, t_dt, 0.0), axis=1, keepdims=True), 0.0) \
                      + jnp.where(lane1 == h1, jnp.sum(jnp.where(m_lo, 0.0, t_dt), axis=1, keepdims=True), 0.0)
            dxs_parts.append(dxs)

        for g in range(4):
            bg, cg = pairs[2 * g]["bg"], pairs[2 * g]["cg"]
            dc_g[g] = dc_g[g] + _dot(dg_g[g], bg)
            db_g[g] = db_g[g] + _dot_tn(dg_g[g], cg)

        dacs = dacs_c + dacs_r.T
        rtri = (_iota((CH, CH), 1) >= _iota((CH, CH), 0)).astype(F32)
        da = _dot_hi(rtri, dacs)
        ddt = ddt - da * jnp.exp(alog_v)
        galog_ref[...] += jnp.sum(da * a, axis=0, keepdims=True)
        dpre = ddt * _sigmoid(dtr + dtb_v) * (lane1 < SSD_HEADS).astype(F32)
        ddt_ref[...] = dpre
        gdtb_ref[...] += jnp.sum(dpre, axis=0, keepdims=True)
        gdsk_ref[...] += gdsk
        dxa_ref[:, 0:D] = jnp.concatenate(dxs_parts, axis=1) * vm
        dxa_ref[:, D:D + 512] = jnp.concatenate(db_g, axis=1)
        dxa_ref[:, D + 512:D + 1024] = jnp.concatenate(dc_g, axis=1)

    small = pl.BlockSpec((1, CH), lambda s: (0, 0))
    wide = pl.BlockSpec((1, D), lambda s: (0, 0))
    rev = lambda s: nc - 1 - s
    return pl.pallas_call(
        body, name=name,
        out_shape=[jax.ShapeDtypeStruct((t, D), F32), jax.ShapeDtypeStruct((t, XBC_W), F32), jax.ShapeDtypeStruct((t, CH), F32),
                   jax.ShapeDtypeStruct((1, CH), F32), jax.ShapeDtypeStruct((1, CH), F32), jax.ShapeDtypeStruct((1, CH), F32),
                   jax.ShapeDtypeStruct((1, D), F32)],
        grid=(nc,),
        in_specs=[pl.BlockSpec((CH, XBC_W), lambda s: (rev(s), 0)), pl.BlockSpec((CH, CH), lambda s: (rev(s), 0)),
                  pl.BlockSpec((CH, D), lambda s: (rev(s), 3)), pl.BlockSpec((1, 8, CH, CH), lambda s: (rev(s), 0, 0, 0)),
                  pl.BlockSpec((CH, D), lambda s: (rev(s), 0)), small, small, small, wide],
        out_specs=[pl.BlockSpec((CH, D), lambda s: (rev(s), 0)), pl.BlockSpec((CH, XBC_W), lambda s: (rev(s), 0)),
                   pl.BlockSpec((CH, CH), lambda s: (rev(s), 0)), small, small, small, wide],
        scratch_shapes=[pltpu.VMEM((8, CH, CH), F32)], compiler_params=_cparams(1))(xa, pdt, proj, hs, dyb, dtb, alog, dsk, nw)


RET_DK = 256


def _log_gamma(h):
    return math.log(1.0 - 2.0 ** (-5.0 - h))


def _rope(x, cos, sin):
    x1, x2 = x[:, :128], x[:, 128:]
    return jnp.concatenate([x1 * cos - x2 * sin, x1 * sin + x2 * cos], axis=1)


def _unrope(d, cos, sin):
    d1, d2 = d[:, :128], d[:, 128:]
    return jnp.concatenate([d1 * cos + d2 * sin, d2 * cos - d1 * sin], axis=1)


def _ret_heads_fwd(q, k, v, cos, sin, vm, r_ins):
    hs = range(RET_HEADS)
    lgs = [_log_gamma(h) for h in hs]
    sls = [slice(RET_DK * h, RET_DK * (h + 1)) for h in hs]
    qr = [_rope(q[:, sls[h]], cos, sin) for h in hs]
    kr = [_rope(k[:, sls[h]], cos, sin) * (RET_DK ** -0.5) for h in hs]
    vr = [v[:, sls[h]] * vm for h in hs]
    rel = (_iota((CH, CH), 0) - _iota((CH, CH), 1)).astype(F32)
    idx = _iota((CH, 1), 0).astype(F32)
    dmask = [jnp.where(rel >= 0, jnp.exp(lgs[h] * jnp.maximum(rel, 0.0)), 0.0) for h in hs]
    kdec = [jnp.exp(lgs[h] * (CH - 1 - idx)) for h in hs]
    qdec = [jnp.exp(lgs[h] * (idx + 1.0)) for h in hs]
    raw = [_dot_nt(qr[h], kr[h]) for h in hs]
    cross = [_dot(qr[h], r_ins[h]) for h in hs]
    kv = [_dot_tn(kr[h] * kdec[h], vr[h]) for h in hs]
    scores = [raw[h] * dmask[h] for h in hs]
    y = [_dot(scores[h], vr[h]) + cross[h] * qdec[h] for h in hs]
    return [dict(qr=qr[h], kr=kr[h], vr=vr[h], dmask=dmask[h], kdec=kdec[h], qdec=qdec[h], scores=scores[h], y=y[h], kv=kv[h],
                 cdec=math.exp(lgs[h] * CH)) for h in hs]


def _group_norm(y):
    mu = jnp.mean(y, axis=-1, keepdims=True)
    yc = y - mu
    r = lax.rsqrt(jnp.mean(yc * yc, axis=-1, keepdims=True) + EPS)
    return yc * r, r


def ret_fwd(proj, cos, sin, *, name):
    t = proj.shape[0]
    nc = t // CH

    def body(q_ref, k_ref, v_ref, g_ref, cos_ref, sin_ref, y_ref, rs_ref, r_scr):
        c = pl.program_id(0)

        @pl.when(c == 0)
        def _():
            r_scr[...] = jnp.zeros_like(r_scr)

        vm = _row_valid(c, CH, CH)
        q, k, v, gt = q_ref[...], k_ref[...], v_ref[...], g_ref[...]
        cos, sin = cos_ref[...], sin_ref[...]
        r_ins = [r_scr[h] for h in range(RET_HEADS)]
        ps = _ret_heads_fwd(q, k, v, cos, sin, vm, r_ins)
        for h, p in enumerate(ps):
            rs_ref[0, h] = r_ins[h]
            r_scr[h] = r_ins[h] * p["cdec"] + p["kv"]
            yn, _ = _group_norm(p["y"])
            sl = slice(RET_DK * h, RET_DK * (h + 1))
            y_ref[:, sl] = yn * _silu(gt[:, sl])

    blk = lambda col: pl.BlockSpec((CH, D), lambda c: (c, col))
    tab = pl.BlockSpec((CH, CH), lambda c: (c, 0))
    return pl.pallas_call(
        body, name=name,
        out_shape=[jax.ShapeDtypeStruct((t, D), F32), jax.ShapeDtypeStruct((nc, RET_HEADS, RET_DK, RET_DK), F32)], grid=(nc,),
        in_specs=[blk(6), blk(7), blk(8), blk(9), tab, tab],
        out_specs=[pl.BlockSpec((CH, D), lambda c: (c, 0)), pl.BlockSpec((1, RET_HEADS, RET_DK, RET_DK), lambda c: (c, 0, 0, 0))],
        scratch_shapes=[pltpu.VMEM((RET_HEADS, RET_DK, RET_DK), F32)], compiler_params=_cparams(1))(proj, proj, proj, proj, cos, sin)


def ret_bwd(proj, cos, sin, rs, dyc, *, name):
    t = proj.shape[0]
    nc = t // CH

    def body(q_ref, k_ref, v_ref, g_ref, cos_ref, sin_ref, rs_ref, dy_ref, o_ref, dr_scr):
        step = pl.program_id(0)
        c = nc - 1 - step

        @pl.when(step == 0)
        def _():
            dr_scr[...] = jnp.zeros_like(dr_scr)

        vm = _row_valid(c, CH, CH)
        q, k, v, gt = q_ref[...], k_ref[...], v_ref[...], g_ref[...]
        cos, sin = cos_ref[...], sin_ref[...]
        dout = dy_ref[...]
        hs = range(RET_HEADS)
        sls = [slice(RET_DK * h, RET_DK * (h + 1)) for h in hs]
        r_ins = [rs_ref[0, h] for h in hs]
        dr_outs = [dr_scr[h] for h in hs]
        ps = _ret_heads_fwd(q, k, v, cos, sin, vm, r_ins)
        dys, dgs = [], []
        for h in hs:
            yn, r = _group_norm(ps[h]["y"])
            gh = gt[:, sls[h]]
            do = dout[:, sls[h]]
            dgs.append(do * yn * _dsilu(gh))
            dyn = do * _silu(gh)
            dys.append(r * (dyn - jnp.mean(dyn, axis=-1, keepdims=True) - yn * jnp.mean(dyn * yn, axis=-1, keepdims=True)))
        dycs = [dys[h] * ps[h]["qdec"] for h in hs]
        dqr = [_dot_nt(dycs[h], r_ins[h]) for h in hs]
        dr_new = [_dot_tn(ps[h]["qr"], dycs[h]) for h in hs]
        dkr = [_dot_nt(ps[h]["vr"], dr_outs[h]) * ps[h]["kdec"] for h in hs]
        dv = [_dot(ps[h]["kr"] * ps[h]["kdec"], dr_outs[h]) + _dot_tn(ps[h]["scores"], dys[h]) for h in hs]
        ds = [_dot_nt(dys[h], ps[h]["vr"]) * ps[h]["dmask"] for h in hs]
        dqr = [dqr[h] + _dot(ds[h], ps[h]["kr"]) for h in hs]
        dkr = [dkr[h] + _dot_tn(ds[h], ps[h]["qr"]) for h in hs]
        for h in hs:
            dr_scr[h] = dr_outs[h] * ps[h]["cdec"] + dr_new[h]
            o_ref[:, RET_DK * h:RET_DK * (h + 1)] = _unrope(dqr[h], cos, sin)
            o_ref[:, D + RET_DK * h:D + RET_DK * (h + 1)] = _unrope(dkr[h], cos, sin) * (RET_DK ** -0.5)
            o_ref[:, 2 * D + RET_DK * h:2 * D + RET_DK * (h + 1)] = dv[h] * vm
            o_ref[:, 3 * D + RET_DK * h:3 * D + RET_DK * (h + 1)] = dgs[h]

    rev = lambda s: nc - 1 - s
    blk = lambda col: pl.BlockSpec((CH, D), lambda s: (rev(s), col))
    tab = pl.BlockSpec((CH, CH), lambda s: (rev(s), 0))
    return pl.pallas_call(
        body, name=name, out_shape=jax.ShapeDtypeStruct((t, 4 * D), F32), grid=(nc,),
        in_specs=[blk(6), blk(7), blk(8), blk(9), tab, tab,
                  pl.BlockSpec((1, RET_HEADS, RET_DK, RET_DK), lambda s: (rev(s), 0, 0, 0)), pl.BlockSpec((CH, D), lambda s: (rev(s), 0))],
        out_specs=pl.BlockSpec((CH, 4 * D), lambda s: (rev(s), 0)),
        scratch_shapes=[pltpu.VMEM((RET_HEADS, RET_DK, RET_DK), F32)], compiler_params=_cparams(1))(proj, proj, proj, proj, cos, sin, rs, dyc)


SB_D = 128
SB_SCALE = SB_D ** -0.5
SB_CUTOFF = 104.0


def _split_hi_lo(x):
    hi = x.astype(BF16)
    lo = (x - hi.astype(F32)).astype(BF16)
    return hi, lo


def _sum_matrix(kind):
    a, b = _iota((128, 128), 0), _iota((128, 128), 1)
    tri = ((b > a) if kind == "after" else (b < a)).astype(BF16)
    return jnp.concatenate([tri, tri], axis=1)


def _key_sums(x, mat2):
    hi, lo = _split_hi_lo(x)
    return jnp.dot(mat2, jnp.concatenate([hi, lo], axis=0), preferred_element_type=F32)


def _sb_mask(d_kq, key_idx, first_key, q_minus_k):
    return (d_kq < q_minus_k) & (key_idx >= N_PAD - first_key)


def _sb_log_sigmoid(z):
    return jnp.minimum(z, 0.0) - jnp.log(1.0 + jnp.exp(-jnp.abs(z)))


def sb_fwd(qkv, *, name):
    t = qkv.shape[0]
    tq = _tok_block(t)
    nq = t // tq
    per = tq // 128

    def body(q_ref, k_ref, v_ref, o_ref, at_ref, bmin_ref):
        h = pl.program_id(0)
        i = pl.program_id(1)
        top = (i + 1) * per - 1
        mat_after = _sum_matrix("after")
        qs = [q_ref[pl.ds(128 * r, 128), :] for r in range(per)]
        d_kq = _iota((128, 128), 0) - _iota((128, 128), 1)
        key_idx = _iota((128, 128), 0)

        def mask(r, b):
            return _sb_mask(d_kq, key_idx, b * 128, (i * per + r - b) * 128)

        def step(carry):
            b, _, a_runs, accs = carry
            off = pl.multiple_of(b * 128, 128)
            kb = k_ref[pl.ds(off, 128), :]
            vb = v_ref[pl.ds(off, 128), :]
            tiles = range(per)
            zs = [_dot_nt(kb, qs[r]) * SB_SCALE for r in tiles]
            ms = [mask(r, b) for r in tiles]
            lss = [_sb_log_sigmoid(zs[r]) for r in tiles]
            lnegs = [jnp.where(ms[r], lss[r] - zs[r], 0.0) for r in tiles]
            sufs = [_key_sums(lnegs[r], mat_after) for r in tiles]
            ws = [jnp.where(ms[r], jnp.exp(lss[r] + a_runs[r] + sufs[r]), 0.0) for r in tiles]
            a_new = [a_runs[r] + sufs[r][0:1, :] + lnegs[r][0:1, :] for r in tiles]
            acc_new = [accs[r] + _dot_tn(ws[r], vb) for r in tiles]
            a_max = jnp.max(functools.reduce(jnp.maximum, a_new))
            return b - 1, a_max >= -SB_CUTOFF, tuple(a_new), tuple(acc_new)

        zeros = tuple(qs[r].astype(F32) * 0.0 for r in range(per))
        zrow = tuple(z[0:1, :] for z in zeros)
        b_end, _, a_runs, accs = lax.while_loop(lambda c: jnp.logical_and(c[0] >= 0, c[1]), step, (top, top >= 0, zrow, zeros))
        bmin_ref[h, i] = b_end + 1
        at_ref[...] = jnp.zeros_like(at_ref)
        for r in range(per):
            o_ref[pl.ds(128 * r, 128), :] = accs[r]
            at_ref[0, 0, r:r + 1, :] = a_runs[r]

    blk = pl.BlockSpec((tq, 128), lambda h, i: (i, h))
    return pl.pallas_call(
        body, name=name,
        out_shape=[jax.ShapeDtypeStruct((t, D), F32), jax.ShapeDtypeStruct((SB_HEADS, nq, 8, 128), F32),
                   jax.ShapeDtypeStruct((SB_HEADS, nq), jnp.int32)],
        grid=(SB_HEADS, nq),
        in_specs=[blk, pl.BlockSpec((t, 128), lambda h, i: (0, SB_HEADS + h)),
                  pl.BlockSpec((t, 128), lambda h, i: (0, 2 * SB_HEADS + h))],
        out_specs=[blk, pl.BlockSpec((1, 1, 8, 128), lambda h, i: (h, i, 0, 0)), pl.BlockSpec(memory_space=pltpu.SMEM)],
        compiler_params=_cparams(2))(qkv, qkv, qkv)


def sb_bwd(qkv, atot, bmin, dout, *, name):
    t = qkv.shape[0]
    tq = _tok_block(t)
    nq = t // tq
    per = tq // 128

    nblk = t // 128

    def body(bmin_ref, q_ref, k_ref, v_ref, at_ref, do_ref, dq_ref, dk_ref, dv_ref):
        i = pl.program_id(1)
        top = (i + 1) * per - 1
        b_first = bmin_ref[pl.program_id(0), i]

        @pl.when(i == 0)
        def _():
            dk_ref[...] = jnp.zeros_like(dk_ref)
            dv_ref[...] = jnp.zeros_like(dv_ref)

        qs = [q_ref[pl.ds(128 * r, 128), :] for r in range(per)]
        dos = [do_ref[pl.ds(128 * r, 128), :].astype(BF16) for r in range(per)]
        q_all = q_ref[...]
        do_all = do_ref[...].astype(BF16)
        a_tots = [at_ref[0, 0, r:r + 1, :] for r in range(per)]
        mat_after = _sum_matrix("after")
        mat_before = _sum_matrix("before")
        d_kq = _iota((128, 128), 0) - _iota((128, 128), 1)
        key_idx = _iota((128, 128), 0)

        def offset(b):
            return pl.multiple_of(jnp.clip(b, 0, nblk - 1) * 128, 128)

        def mask(r, b):
            return _sb_mask(d_kq, key_idx, b * 128, (i * per + r - b) * 128)

        def step(b, carry):
            lss, dws, sufs, tots, p_runs, e_runs, dqs = carry
            off = offset(b)
            kb = k_ref[pl.ds(off, 128), :]
            vb = v_ref[pl.ds(off, 128), :]
            zs = [_dot_nt(kb, qs[r]) * SB_SCALE for r in range(per)]
            dw_new = tuple(_dot_nt(vb, dos[r]) for r in range(per))
            off1 = offset(b - 1)
            k1 = k_ref[pl.ds(off1, 128), :]
            ws, es, epres, sigs, p_new = [], [], [], [], []
            for r in range(per):
                p = p_runs[r] + tots[r]
                w = jnp.where(mask(r, b - 1), jnp.exp(lss[r] + (a_tots[r] - p) + sufs[r]), 0.0)
                e = w * dws[r]
                p_new.append(p)
                ws.append(w.astype(BF16))
                es.append(e)
                epres.append(_key_sums(e, mat_before))
                sigs.append(jnp.exp(lss[r]))
            dv_ref[pl.ds(off1, 128), :] += _dot(jnp.concatenate(ws, axis=1), do_all)
            ls_new, suf_new, tot_new = [], [], []
            for r in range(per):
                ls = _sb_log_sigmoid(zs[r])
                lneg = jnp.where(mask(r, b), ls - zs[r], 0.0)
                suf = _key_sums(lneg, mat_after)
                ls_new.append(ls)
                suf_new.append(suf)
                tot_new.append(suf[0:1, :] + lneg[0:1, :])
            dzs, e_new, dq_new = [], [], []
            for r in range(per):
                t2 = jnp.where(mask(r, b - 1), (e_runs[r] + epres[r]) * sigs[r], 0.0)
                dz = ((es[r] * (1.0 - sigs[r]) - t2) * SB_SCALE).astype(BF16)
                e_new.append(e_runs[r] + epres[r][127:128, :] + es[r][127:128, :])
                dq_new.append(dqs[r] + _dot_tn(dz, k1))
                dzs.append(dz)
            dk_ref[pl.ds(off1, 128), :] += _dot(jnp.concatenate(dzs, axis=1), q_all)
            return tuple(ls_new), dw_new, tuple(suf_new), tuple(tot_new), tuple(p_new), tuple(e_new), tuple(dq_new)

        zeros = tuple(qs[r].astype(F32) * 0.0 for r in range(per))
        zrow = tuple(z[0:1, :] for z in zeros)
        carry = lax.fori_loop(b_first, top + 2, step, (zeros, zeros, zeros, zrow, zrow, zrow, zeros))
        for r in range(per):
            dq_ref[pl.ds(128 * r, 128), :] = carry[6][r]

    head_blk = pl.BlockSpec((t, 128), lambda h, i: (0, h))
    return pl.pallas_call(
        body, name=name, out_shape=[jax.ShapeDtypeStruct((t, D), F32)] * 3, grid=(SB_HEADS, nq),
        in_specs=[pl.BlockSpec(memory_space=pltpu.SMEM),
                  pl.BlockSpec((tq, 128), lambda h, i: (i, h)), pl.BlockSpec((t, 128), lambda h, i: (0, SB_HEADS + h)),
                  pl.BlockSpec((t, 128), lambda h, i: (0, 2 * SB_HEADS + h)),
                  pl.BlockSpec((1, 1, 8, 128), lambda h, i: (h, i, 0, 0)), pl.BlockSpec((tq, 128), lambda h, i: (i, h))],
        out_specs=[pl.BlockSpec((tq, 128), lambda h, i: (i, h)), head_blk, head_blk],
        compiler_params=_cparams(2, 60 * 1024 * 1024))(bmin, qkv, qkv, qkv, atot, dout)


def branch_fwd(y, proj, w, n, *, name, add=None):
    t = y.shape[0]
    tb = _tok_block(t)
    has_add = add is not None

    def body(*refs):
        if has_add:
            y_ref, g_ref, w_ref, a_ref, o_ref = refs
        else:
            y_ref, g_ref, w_ref, o_ref = refs
        r = _sigmoid(g_ref[...]) * _dot(y_ref[...], w_ref[...])
        if has_add:
            r = r + a_ref[...]
        o_ref[...] = r

    blk = pl.BlockSpec((tb, D), lambda i: (i, 0))
    in_specs = [blk, pl.BlockSpec((tb, D), lambda i: (i, COL_GATE + n)), pl.BlockSpec((D, D), lambda i: (0, 0))] + ([blk] if has_add else [])
    args = [y, proj, w] + ([add] if has_add else [])
    return pl.pallas_call(body, name=name, out_shape=jax.ShapeDtypeStruct((t, D), F32), grid=(t // tb,),
                          in_specs=in_specs, out_specs=blk, compiler_params=_cparams(1))(*args)


def branch_bwd(y, proj, w, wt, n, dmerged, *, name):
    t = y.shape[0]
    tb = _tok_block(t)

    def body(y_ref, g_ref, w_ref, wt_ref, dm_ref, dg_ref, dup_ref, dy_ref):
        up = _dot(y_ref[...], w_ref[...])
        gate = _sigmoid(g_ref[...])
        dm = dm_ref[...]
        dg_ref[...] = dm * up * gate * (1.0 - gate)
        dup = (dm * gate).astype(BF16)
        dup_ref[...] = dup
        dy_ref[...] = _dot(dup, wt_ref[...])

    blk = pl.BlockSpec((tb, D), lambda i: (i, 0))
    wspec = pl.BlockSpec((D, D), lambda i: (0, 0))
    return pl.pallas_call(
        body, name=name,
        out_shape=[jax.ShapeDtypeStruct((t, D), F32), jax.ShapeDtypeStruct((t, D), BF16), jax.ShapeDtypeStruct((t, D), F32)],
        grid=(t // tb,),
        in_specs=[blk, pl.BlockSpec((tb, D), lambda i: (i, COL_GATE + n)), wspec, wspec, blk],
        out_specs=[blk, blk, blk], compiler_params=_cparams(1))(y, proj, w, wt, dmerged)


def swiglu_fwd(f, *, name):
    t = f.shape[0]
    tb = _tok_block(t)

    def body(g_ref, u_ref, o_ref):
        o_ref[...] = (_silu(g_ref[...]) * u_ref[...]).astype(BF16)

    return pl.pallas_call(body, name=name, out_shape=jax.ShapeDtypeStruct((t, D_FF), BF16), grid=(t // tb,),
                          in_specs=[pl.BlockSpec((tb, D_FF), lambda i: (i, 0)), pl.BlockSpec((tb, D_FF), lambda i: (i, 1))],
                          out_specs=pl.BlockSpec((tb, D_FF), lambda i: (i, 0)), compiler_params=_cparams(1))(f, f)


def swiglu_bwd(f, dact, *, name):
    t = f.shape[0]
    tb = _tok_block(t)

    def body(g_ref, u_ref, d_ref, o_ref):
        g, u, d = g_ref[...], u_ref[...], d_ref[...]
        o_ref[:, 0:D_FF] = d * u * _dsilu(g)
        o_ref[:, D_FF:2 * D_FF] = d * _silu(g)

    return pl.pallas_call(body, name=name, out_shape=jax.ShapeDtypeStruct((t, 2 * D_FF), F32), grid=(t // tb,),
                          in_specs=[pl.BlockSpec((tb, D_FF), lambda i: (i, 0)), pl.BlockSpec((tb, D_FF), lambda i: (i, 1)),
                                    pl.BlockSpec((tb, D_FF), lambda i: (i, 0))],
                          out_specs=pl.BlockSpec((tb, 2 * D_FF), lambda i: (i, 0)), compiler_params=_cparams(1))(f, f, dact)


def loss_head(h, target, *, name):
    t = h.shape[0]
    nb = t // CH

    def body(h_ref, t_ref, l_ref, d_ref):
        i = pl.program_id(0)

        @pl.when(i == 0)
        def _():
            l_ref[...] = jnp.zeros_like(l_ref)
            d_ref[...] = jnp.zeros_like(d_ref)

        @pl.when(i > 0)
        def _():
            err = h_ref[...] - t_ref[...]
            d_ref[...] = err * (1.0 / D)
            l_ref[...] += jnp.sum(err * err) * (0.5 / D)

    return pl.pallas_call(
        body, name=name, out_shape=[jax.ShapeDtypeStruct((8, 128), F32), jax.ShapeDtypeStruct((t, D), F32)], grid=(nb,),
        in_specs=[pl.BlockSpec((CH, D), lambda i: (i, 0)), pl.BlockSpec((CH, D), lambda i: (jnp.maximum(i - 1, 0), 0))],
        out_specs=[pl.BlockSpec((8, 128), lambda i: (0, 0)), pl.BlockSpec((CH, D), lambda i: (i, 0))],
        compiler_params=_cparams(1))(h, target)


def _pad_rows8(w):
    return jnp.concatenate([w, jnp.zeros((8 - w.shape[0], w.shape[1]), w.dtype)], axis=0)


def _pad_lanes(v, n=CH):
    return jnp.concatenate([v, jnp.zeros((n - v.shape[0],), v.dtype)])[None, :]


def prep_layer(p):
    w = p["w_in"]
    zeros = jnp.zeros((D, CH - SSD_HEADS), w.dtype)
    w_f = jnp.concatenate([w[:, :6144], w[:, 6160:10256], w[:, 13328:17424]], axis=1)
    w_dt = jnp.concatenate([w[:, 6144:6160], zeros], axis=1)
    w_sb = w[:, 10256:13328]
    return dict(
        w_f=w_f, w_sb=w_sb, w_dt=w_dt, w_f_t=w_f.T, w_sb_t=w_sb.T, w_dt_t=w_dt.T,
        w_br=p["w_branch"], w_br_t=jnp.swapaxes(p["w_branch"], 1, 2), w_out=p["w_out"], w_out_t=p["w_out"].T,
        w_fi=p["w_ffn_in"], w_fi_t=p["w_ffn_in"].T, w_fo=p["w_ffn_out"], w_fo_t=p["w_ffn_out"].T,
        conv_a8=_pad_rows8(p["conv_a"]), conv_s8=_pad_rows8(p["ssd_conv_w"]), conv_sb=p["ssd_conv_b"][None, :],
        dtb=_pad_lanes(p["ssd_dt_bias"]), alog=_pad_lanes(p["ssd_a_log"]), dsk=_pad_lanes(p["ssd_d"]), nw=p["ssd_norm"][None, :],
        n1=p["norm_mix_pre"][None, :], n2=p["norm_mix_post"][None, :], n3=p["norm_ffn_pre"][None, :], n4=p["norm_ffn_post"][None, :])


def layer_fwd(h0, w, cos, sin, l):
    nm = lambda s: f"l{l}_{s}"
    hn = rms_fwd(h0, w["n1"], name=nm("rms1"), out_dtype=BF16)
    proj = mm(hn, w["w_f"], name=nm("proj_f"))
    qkv = mm(hn, w["w_sb"], name=nm("proj_sb"), out_dtype=BF16)
    pdt = mm(hn, w["w_dt"], name=nm("proj_dt"))
    y_a = conv_a_fwd(proj, w["conv_a8"], name=nm("conv_a"))
    xa = ssd_conv_fwd(proj, w["conv_s8"], w["conv_sb"], name=nm("ssd_conv"))
    y_b, hs = ssd_fwd(xa, proj, pdt, w["dtb"], w["alog"], w["dsk"], w["nw"], name=nm("ssd"))
    y_c, rs = ret_fwd(proj, cos, sin, name=nm("ret"))
    y_d, sb_atot, sb_bmin = sb_fwd(qkv, name=nm("sb"))
    ys = (y_a, y_b, y_c, y_d)
    merged = None
    for n in range(4):
        merged = branch_fwd(ys[n], proj, w["w_br"][n], n, name=nm(f"branch{n}"), add=merged)
    mix = mm(merged, w["w_out"], name=nm("mix"))
    h1 = rms_fwd(mix, w["n2"], name=nm("rms2"), res=h0)
    hn2 = rms_fwd(h1, w["n3"], name=nm("rms3"), out_dtype=BF16)
    f = mm(hn2, w["w_fi"], name=nm("ffn_in"))
    act = swiglu_fwd(f, name=nm("swiglu"))
    f2 = mm(act, w["w_fo"], name=nm("ffn_out"))
    h2 = rms_fwd(f2, w["n4"], name=nm("rms4"), res=h1)
    saved = dict(h0=h0, hn=hn, proj=proj, qkv=qkv, pdt=pdt, xa=xa, hs=hs, rs=rs, ys=ys, sb_atot=sb_atot, sb_bmin=sb_bmin, merged=merged, mix=mix, h1=h1, hn2=hn2,
                 f=f, act=act, f2=f2)
    return h2, saved


def layer_bwd(dh2, s, w, cos, sin, l):
    nm = lambda t: f"l{l}_{t}"
    g = {}
    df2, g["n4"] = rms_bwd(s["f2"], w["n4"], dh2, name=nm("rms4_b"))
    g["w_fo"] = mm_tn(s["act"], df2, name=nm("ffn_out_dw"))
    dact = mm(df2, w["w_fo_t"], name=nm("ffn_out_dx"), out_dtype=BF16)
    df = swiglu_bwd(s["f"], dact, name=nm("swiglu_b"))
    g["w_fi"] = mm_tn(s["hn2"], df, name=nm("ffn_in_dw"))
    dhn2 = mm(df, w["w_fi_t"], name=nm("ffn_in_dx"))
    dh1, g["n3"] = rms_bwd(s["h1"], w["n3"], dhn2, name=nm("rms3_b"), add=dh2)
    dmix, g["n2"] = rms_bwd(s["mix"], w["n2"], dh1, name=nm("rms2_b"))
    g["w_out"] = mm_tn(s["merged"], dmix, name=nm("mix_dw"))
    dmerged = mm(dmix, w["w_out_t"], name=nm("mix_dx"))
    dgate, dys, dwb = [], [], []
    for n in range(4):
        dg_n, dup_n, dy_n = branch_bwd(s["ys"][n], s["proj"], w["w_br"][n], w["w_br_t"][n], n, dmerged, name=nm(f"branch{n}_b"))
        dgate.append(dg_n)
        dys.append(dy_n)
        dwb.append(mm_tn(s["ys"][n], dup_n, name=nm(f"branch{n}_dw")))
    g["w_br"] = jnp.stack(dwb)
    d_a, g["conv_a8"] = conv_a_bwd(s["proj"], w["conv_a8"], dys[0], name=nm("conv_a_b"))
    dz, dxa, ddt, g["dtb"], g["alog"], g["dsk"], g["nw"] = ssd_bwd(
        s["xa"], s["proj"], s["pdt"], s["hs"], dys[1], w["dtb"], w["alog"], w["dsk"], w["nw"], name=nm("ssd_b"))
    dpre, g["conv_s8"], g["conv_sb"] = ssd_conv_bwd_pre(s["proj"], w["conv_s8"], w["conv_sb"], dxa, name=nm("ssd_conv_b1"))
    dxbc = ssd_conv_bwd_in(dpre, w["conv_s8"], name=nm("ssd_conv_b2"))
    d_r = ret_bwd(s["proj"], cos, sin, s["rs"], dys[2], name=nm("ret_b"))
    d_sb = sb_bwd(s["qkv"], s["sb_atot"], s["sb_bmin"], dys[3], name=nm("sb_b"))
    segs = [(d_a, 0), (dz, 3072), (dxbc, 4096), (d_r, 6144), (dgate[0], 10240), (dgate[1], 11264), (dgate[2], 12288),
            (dgate[3], 13312)]
    dws = [mm_tn(s["hn"], d, name=nm(f"proj_dw{k}")) for k, (d, _) in enumerate(segs)]
    g["w_f"] = jnp.concatenate(dws, axis=1)
    g["w_sb"] = jnp.concatenate([mm_tn(s["hn"], d, name=nm(f"proj_dw_sb{k}")) for k, d in enumerate(d_sb)], axis=1)
    g["w_dt"] = mm_tn(s["hn"], ddt, name=nm("proj_dw_dt"))
    dhn = mm(ddt, w["w_dt_t"], name=nm("proj_dx_dt"))
    for k, d in enumerate(d_sb):
        dhn = mm(d, w["w_sb_t"][k * D:(k + 1) * D], name=nm(f"proj_dx_sb{k}"), add=dhn)
    for k, (d, c0) in enumerate(segs):
        dhn = mm(d, w["w_f_t"][c0:c0 + d.shape[1]], name=nm(f"proj_dx{k}"), add=dhn)
    dh0, g["n1"] = rms_bwd(s["h0"], w["n1"], dhn, name=nm("rms1_b"), add=dh1)
    return dh0, g


def layer_grads_to_params(g):
    wf, wsb = g["w_f"], g["w_sb"]
    w_in = jnp.concatenate([wf[:, :6144], g["w_dt"][:, :SSD_HEADS], wf[:, 6144:10240], wsb, wf[:, 10240:14336]], axis=1)
    return dict(
        w_in=w_in, conv_a=g["conv_a8"][:3], ssd_conv_w=g["conv_s8"][:4], ssd_conv_b=g["conv_sb"][0],
        ssd_dt_bias=g["dtb"][0, :SSD_HEADS], ssd_a_log=g["alog"][0, :SSD_HEADS], ssd_d=g["dsk"][0, :SSD_HEADS], ssd_norm=g["nw"][0],
        w_branch=g["w_br"], w_out=g["w_out"], w_ffn_in=g["w_fi"], w_ffn_out=g["w_fo"],
        norm_mix_pre=g["n1"][0], norm_mix_post=g["n2"][0], norm_ffn_pre=g["n3"][0], norm_ffn_post=g["n4"][0])


def rope_tables(t):
    half = RET_DK // 2
    inv = ROPE_BASE ** (-jnp.arange(half, dtype=F32) / half)
    ang = jnp.arange(t).astype(F32)[:, None] * inv[None, :]
    return jnp.cos(ang), jnp.sin(ang)


def local_step(x, target, meta, layers):
    h = jnp.concatenate([jnp.zeros((N_PAD, D), F32), meta, x], axis=0)
    t = h.shape[0]
    cos, sin = rope_tables(t)
    ws = [prep_layer(p) for p in layers]
    saved = []
    for l, w in enumerate(ws):
        h, s = layer_fwd(h, w, cos, sin, l)
        saved.append(s)
    loss, dh = loss_head(h, target, name="loss_head")
    grads = [None] * len(ws)
    for l in reversed(range(len(ws))):
        dh, g = layer_bwd(dh, saved[l], ws[l], cos, sin, l)
        grads[l] = layer_grads_to_params(g)
    return loss, dh[CH:], dh[N_PAD:CH], grads


def _my_place():
    return lax.axis_index("x"), lax.axis_index("y"), lax.axis_index("c")


def _flat(px, py, pc):
    return 4 * px + 2 * py + pc


ANY = pl.BlockSpec(memory_space=pl.ANY)


def all_gather(x_shard, *, name):
    shape = x_shard.shape

    def body(x_ref, out_ref, send_sems, recv_sems, local_sem):
        x, y, c = _my_place()
        me, sibling = (x, y, c), (x, y, 1 - c)
        chips = [(1 - x, y), (x, 1 - y), (1 - x, 1 - y)]

        def rows(px, py, pc):
            return out_ref.at[_flat(px, py, pc)]

        def copy(k, block, to, src=None):
            return pltpu.make_async_remote_copy(
                src_ref=rows(*block) if src is None else src, dst_ref=rows(*block),
                send_sem=send_sems.at[k], recv_sem=recv_sems.at[k], device_id=to, device_id_type=MESH_ID)

        mine = pltpu.make_async_copy(x_ref, rows(*me), local_sem)
        mine.start()
        first = [copy(0, me, sibling, src=x_ref)]
        first += [copy(1 + j, me, (*chip, c), src=x_ref) for j, chip in enumerate(chips)]
        for cp in first:
            cp.start()
        passed = [copy(4 + j, (*chip, c), sibling) for j, chip in enumerate(chips)]
        for j, chip in enumerate(chips):
            copy(1 + j, (*chip, c), me).wait_recv()
            passed[j].start()
        copy(0, sibling, me).wait_recv()
        for j, chip in enumerate(chips):
            copy(4 + j, (*chip, 1 - c), me).wait_recv()
        for cp in first + passed:
            cp.wait_send()
        mine.wait()

    return pl.pallas_call(
        body, name=name, out_shape=jax.ShapeDtypeStruct((N_DEV,) + shape, x_shard.dtype),
        in_specs=[ANY], out_specs=ANY,
        scratch_shapes=[pltpu.SemaphoreType.DMA((7,)), pltpu.SemaphoreType.DMA((7,)), pltpu.SemaphoreType.DMA],
    )(x_shard)


N_CHIP = 4


def pair_exchange(g, *, name):
    shape = (N_CHIP,) + g.shape[1:]

    def body(g_ref, recv_ref, send_sems, recv_sems):
        x, y, c = _my_place()
        sibling = (x, y, 1 - c)
        sends = [pltpu.make_async_remote_copy(
            src_ref=g_ref.at[2 * j + 1 - c], dst_ref=recv_ref.at[j], send_sem=send_sems.at[j], recv_sem=recv_sems.at[j],
            device_id=sibling, device_id_type=MESH_ID) for j in range(N_CHIP)]
        for cp in sends:
            cp.start()
        for cp in sends:
            cp.wait_recv()
        for cp in sends:
            cp.wait_send()

    return pl.pallas_call(
        body, name=name, out_shape=jax.ShapeDtypeStruct(shape, g.dtype), in_specs=[ANY], out_specs=ANY,
        scratch_shapes=[pltpu.SemaphoreType.DMA((N_CHIP,)), pltpu.SemaphoreType.DMA((N_CHIP,))],
    )(g)


def pair_sum(g, from_sibling, core, *, name):
    n, r, cols = from_sibling.shape
    tb = _pick(r, (512, 256, 128))

    def body(core_ref, a_ref, b_ref, o_ref):
        o_ref[...] = (a_ref[0].astype(F32) + b_ref[...].astype(F32)).astype(o_ref.dtype)

    blk = pl.BlockSpec((1, tb, cols), lambda j, i, core_ref: (j, i, 0))
    grid_spec = pltpu.PrefetchScalarGridSpec(
        num_scalar_prefetch=1, grid=(n, r // tb),
        in_specs=[pl.BlockSpec((1, 1, tb, cols), lambda j, i, core_ref: (j, core_ref[0], i, 0)), blk], out_specs=blk)
    return pl.pallas_call(body, name=name, out_shape=jax.ShapeDtypeStruct(from_sibling.shape, from_sibling.dtype),
                          grid_spec=grid_spec, compiler_params=_cparams(2))(core, g.reshape(n, 2, r, cols), from_sibling)


def chip_exchange(s, *, name):
    def body(s_ref, out_ref, send_sems, recv_sems, local_sem):
        x, y, c = _my_place()
        me = 2 * x + y
        mine = pltpu.make_async_copy(s_ref.at[me], out_ref.at[me], local_sem)
        mine.start()
        peers = [(jnp.bitwise_xor(x, k >> 1), jnp.bitwise_xor(y, k & 1)) for k in range(1, N_CHIP)]
        sends = []
        for k, (px, py) in enumerate(peers):
            cp = pltpu.make_async_remote_copy(
                src_ref=s_ref.at[2 * px + py], dst_ref=out_ref.at[me], send_sem=send_sems.at[k], recv_sem=recv_sems.at[k],
                device_id=(px, py, c), device_id_type=MESH_ID)
            cp.start()
            sends.append(cp)
        for k, (px, py) in enumerate(peers):
            slot = out_ref.at[2 * px + py]
            pltpu.make_async_remote_copy(
                src_ref=slot, dst_ref=slot, send_sem=send_sems.at[k], recv_sem=recv_sems.at[k],
                device_id=(px, py, c), device_id_type=MESH_ID).wait_recv()
        for cp in sends:
            cp.wait_send()
        mine.wait()

    return pl.pallas_call(
        body, name=name, out_shape=jax.ShapeDtypeStruct(s.shape, s.dtype), in_specs=[ANY], out_specs=ANY,
        scratch_shapes=[pltpu.SemaphoreType.DMA((N_CHIP - 1,)), pltpu.SemaphoreType.DMA((N_CHIP - 1,)), pltpu.SemaphoreType.DMA],
    )(s)


def sum_slots(a, *, name):
    def body(a_ref, o_ref):
        s = a_ref[0]
        for d in range(1, N_DEV):
            s = s + a_ref[d]
        o_ref[...] = s

    return pl.pallas_call(body, name=name, out_shape=jax.ShapeDtypeStruct(a.shape[1:], a.dtype))(a)


def _adamw_math(w, g, m, v):
    m = ADAM_B1 * m + (1.0 - ADAM_B1) * g
    v = ADAM_B2 * v + (1.0 - ADAM_B2) * (g * g)
    m_hat = m / (1.0 - ADAM_B1 ** ADAM_STEP)
    v_hat = v / (1.0 - ADAM_B2 ** ADAM_STEP)
    delta = -ADAM_LR * (m_hat / (jnp.sqrt(v_hat) + ADAM_EPS) + ADAM_WD * w)
    return delta, m, v


def adamw_big(recv, w, m, v, *, name):
    r, cols = w.shape
    tb = 128
    n = recv.shape[0]

    def body(r_ref, w_ref, m_ref, v_ref, g_ref, d_ref, nm_ref, nv_ref):
        g = r_ref[0].astype(F32)
        for d in range(1, n):
            g = g + r_ref[d].astype(F32)
        g_ref[...] = g
        d_ref[...], nm_ref[...], nv_ref[...] = _adamw_math(w_ref[...], g, m_ref[...], v_ref[...])

    blk = pl.BlockSpec((tb, cols), lambda i: (i, 0))
    out = jax.ShapeDtypeStruct((r, cols), F32)
    return pl.pallas_call(
        body, name=name, out_shape=[out] * 4, grid=(r // tb,),
        in_specs=[pl.BlockSpec((n, tb, cols), lambda i: (0, i, 0)), blk, blk, blk], out_specs=[blk] * 4,
        compiler_params=_cparams(1))(recv, w, m, v)


def adamw_small(w, g, m, v, *, name):
    def body(w_ref, g_ref, m_ref, v_ref, d_ref, nm_ref, nv_ref):
        d_ref[...], nm_ref[...], nv_ref[...] = _adamw_math(w_ref[...], g_ref[...], m_ref[...], v_ref[...])

    out = jax.ShapeDtypeStruct(w.shape, F32)
    return pl.pallas_call(body, name=name, out_shape=[out] * 3)(w, g, m, v)


BIG = ("w_in", "w_branch", "w_out", "w_ffn_in", "w_ffn_out")
BIG_SHARD = {"w_in": (DEPTH, D, 2178), "w_branch": (DEPTH, 4, 128, D), "w_out": (DEPTH, 128, D),
             "w_ffn_in": (DEPTH, D, 704), "w_ffn_out": (DEPTH, 352, D)}
BIG_FULL = {"w_in": ((1, 2, 0, 3), (DEPTH, D, 17424)), "w_branch": ((1, 2, 0, 3, 4), (DEPTH, 4, D, D)),
            "w_out": ((1, 0, 2, 3), (DEPTH, D, D)), "w_ffn_in": ((1, 2, 0, 3), (DEPTH, D, 2 * D_FF)),
            "w_ffn_out": ((1, 0, 2, 3), (DEPTH, D_FF, D))}
BIG_ROWS = {n: int(np.prod(s)) // D for n, s in BIG_SHARD.items()}
BIG_R = 7808


def pack_big(shards, dtype):
    parts = [shards[n].astype(dtype).reshape(BIG_ROWS[n], D) for n in BIG]
    parts.append(jnp.zeros((BIG_R - sum(BIG_ROWS.values()), D), dtype))
    return jnp.concatenate(parts, axis=0)


def unpack_big(flat):
    out, o = {}, 0
    for n in BIG:
        out[n] = flat[o:o + BIG_ROWS[n]].reshape(BIG_SHARD[n])
        o += BIG_ROWS[n]
    return out


def unpack_big_full(gathered):
    out, o = {}, 0
    for n in BIG:
        perm, full = BIG_FULL[n]
        out[n] = gathered[:, o:o + BIG_ROWS[n]].reshape((N_DEV,) + BIG_SHARD[n]).transpose(perm).reshape(full)
        o += BIG_ROWS[n]
    return out


def pack_big_full(full, dtype):
    parts = []
    for n in BIG:
        perm, _ = BIG_FULL[n]
        split = tuple(int(v) for v in np.array((N_DEV,) + BIG_SHARD[n])[list(perm)])
        inv = tuple(int(i) for i in np.argsort(perm))
        parts.append(full[n].astype(dtype).reshape(split).transpose(inv).reshape(N_DEV, BIG_ROWS[n], D))
    parts.append(jnp.zeros((N_DEV, BIG_R - sum(BIG_ROWS.values()), D), dtype))
    return jnp.concatenate(parts, axis=1)


def _rows128(a):
    a = a.reshape(-1)
    pad = (-a.shape[0]) % CH
    if pad:
        a = jnp.concatenate([a, jnp.zeros((pad,), a.dtype)])
    return a.reshape(-1, CH)


def _pack_rows(arrs, total):
    parts = [_rows128(a) for a in arrs]
    n = sum(p.shape[0] for p in parts)
    parts.append(jnp.zeros((total - n, CH), F32))
    return jnp.concatenate(parts, axis=0)


def _unpack_rows(flat, shapes):
    out, o = [], 0
    for s in shapes:
        size = int(np.prod(s))
        rows = -(-size // CH)
        out.append(flat[o:o + rows].reshape(-1)[:size].reshape(s))
        o += rows
    return out


SMALL_SHARDED = ("meta", "conv_a", "ssd_conv_w")
SMALL_SHARD_SHAPE = {"meta": (N_META, 128), "conv_a": (DEPTH, 3, 128), "ssd_conv_w": (DEPTH, 4, 256)}
SMALL_FULL_SHAPE = {"meta": (N_META, D), "conv_a": (DEPTH, 3, D), "ssd_conv_w": (DEPTH, 4, 2048)}
SMALL_REPL = ("ssd_conv_b", "ssd_dt_bias", "ssd_a_log", "ssd_d", "ssd_norm", "norm_mix_pre", "norm_mix_post", "norm_ffn_pre",
              "norm_ffn_post")
SMALL_REPL_SHAPE = {"ssd_conv_b": (DEPTH, 2048), "ssd_dt_bias": (DEPTH, SSD_HEADS), "ssd_a_log": (DEPTH, SSD_HEADS),
                    "ssd_d": (DEPTH, SSD_HEADS), "ssd_norm": (DEPTH, D), "norm_mix_pre": (DEPTH, D), "norm_mix_post": (DEPTH, D),
                    "norm_ffn_pre": (DEPTH, D), "norm_ffn_post": (DEPTH, D)}


def _gather_small_full(gathered, n):
    nd = gathered.ndim
    perm = tuple(range(1, nd - 1)) + (0, nd - 1)
    return gathered.transpose(perm).reshape(SMALL_FULL_SHAPE[n])


WEIGHTS = ("meta", "w_in", "conv_a", "ssd_conv_w", "ssd_conv_b", "ssd_dt_bias", "ssd_a_log", "ssd_d", "ssd_norm", "w_branch", "w_out",
           "w_ffn_in", "w_ffn_out", "norm_mix_pre", "norm_mix_post", "norm_ffn_pre", "norm_ffn_post")


def kernel(x, meta, w_in, conv_a, ssd_conv_w, ssd_conv_b, ssd_dt_bias, ssd_a_log, ssd_d, ssd_norm, w_branch, w_out, w_ffn_in, w_ffn_out, norm_mix_pre, norm_mix_post, norm_ffn_pre, norm_ffn_post, loss_target, m_meta, m_w_in, m_conv_a, m_ssd_conv_w, m_ssd_conv_b, m_ssd_dt_bias, m_ssd_a_log, m_ssd_d, m_ssd_norm, m_w_branch, m_w_out, m_w_ffn_in, m_w_ffn_out, m_norm_mix_pre, m_norm_mix_post, m_norm_ffn_pre, m_norm_ffn_post, v_meta, v_w_in, v_conv_a, v_ssd_conv_w, v_ssd_conv_b, v_ssd_dt_bias, v_ssd_a_log, v_ssd_d, v_ssd_norm, v_w_branch, v_w_out, v_w_ffn_in, v_w_ffn_out, v_norm_mix_pre, v_norm_mix_post, v_norm_ffn_pre, v_norm_ffn_post):
    w = dict(meta=meta, w_in=w_in, conv_a=conv_a, ssd_conv_w=ssd_conv_w, ssd_conv_b=ssd_conv_b, ssd_dt_bias=ssd_dt_bias,
             ssd_a_log=ssd_a_log, ssd_d=ssd_d, ssd_norm=ssd_norm, w_branch=w_branch, w_out=w_out, w_ffn_in=w_ffn_in,
             w_ffn_out=w_ffn_out, norm_mix_pre=norm_mix_pre, norm_mix_post=norm_mix_post, norm_ffn_pre=norm_ffn_pre,
             norm_ffn_post=norm_ffn_post)
    m = dict(meta=m_meta, w_in=m_w_in, conv_a=m_conv_a, ssd_conv_w=m_ssd_conv_w, ssd_conv_b=m_ssd_conv_b, ssd_dt_bias=m_ssd_dt_bias,
             ssd_a_log=m_ssd_a_log, ssd_d=m_ssd_d, ssd_norm=m_ssd_norm, w_branch=m_w_branch, w_out=m_w_out, w_ffn_in=m_w_ffn_in,
             w_ffn_out=m_w_ffn_out, norm_mix_pre=m_norm_mix_pre, norm_mix_post=m_norm_mix_post, norm_ffn_pre=m_norm_ffn_pre,
             norm_ffn_post=m_norm_ffn_post)
    v = dict(meta=v_meta, w_in=v_w_in, conv_a=v_conv_a, ssd_conv_w=v_ssd_conv_w, ssd_conv_b=v_ssd_conv_b, ssd_dt_bias=v_ssd_dt_bias,
             ssd_a_log=v_ssd_a_log, ssd_d=v_ssd_d, ssd_norm=v_ssd_norm, w_branch=v_w_branch, w_out=v_w_out, w_ffn_in=v_w_ffn_in,
             w_ffn_out=v_w_ffn_out, norm_mix_pre=v_norm_mix_pre, norm_mix_post=v_norm_mix_post, norm_ffn_pre=v_norm_ffn_pre,
             norm_ffn_post=v_norm_ffn_post)
    xi, yi, ci = _my_place()
    dev = _flat(xi, yi, ci)

    full = unpack_big_full(all_gather(pack_big(w, BF16), name="gather_big"))
    small_shard = _pack_rows([w[n] for n in SMALL_SHARDED], 40)
    small_all = all_gather(small_shard, name="gather_small")
    small_full = {}
    o = 0
    for n in SMALL_SHARDED:
        rows = int(np.prod(SMALL_SHARD_SHAPE[n])) // CH
        small_full[n] = _gather_small_full(small_all[:, o:o + rows].reshape((N_DEV,) + SMALL_SHARD_SHAPE[n]), n)
        o += rows

    layers = []
    for l in range(DEPTH):
        p = {n: full[n][l] for n in BIG}
        p["conv_a"] = small_full["conv_a"][l]
        p["ssd_conv_w"] = small_full["ssd_conv_w"][l]
        for n in SMALL_REPL:
            p[n] = w[n][l]
        layers.append(p)

    loss_blk, grad_x, gmeta, grads = local_step(x[0], loss_target[0], small_full["meta"], layers)

    gfull = {n: jnp.stack([grads[l][n] for l in range(DEPTH)]) for n in BIG}
    partials = pack_big_full(gfull, BF16)
    from_sibling = pair_exchange(partials, name="exchange_pair")
    core = jnp.reshape(ci, (1,)).astype(jnp.int32)
    recv = chip_exchange(pair_sum(partials, from_sibling, core, name="sum_pair"), name="exchange_chip")
    g_flat, d_flat, nm_flat, nv_flat = adamw_big(recv, pack_big(w, F32), pack_big(m, F32), pack_big(v, F32), name="adamw_big")
    out_g, out_d, out_m, out_v = unpack_big(g_flat), unpack_big(d_flat), unpack_big(nm_flat), unpack_big(nv_flat)

    small_names = SMALL_SHARDED + SMALL_REPL
    small_grads = [gmeta] + [jnp.stack([grads[l][n] for l in range(DEPTH)]) for n in small_names[1:]]
    small_shapes = [SMALL_FULL_SHAPE[n] for n in SMALL_SHARDED] + [SMALL_REPL_SHAPE[n] for n in SMALL_REPL]
    sm = _pack_rows(small_grads + [loss_blk[0:1]], 424)
    sm_sum = sum_slots(all_gather(sm, name="gather_small_grads"), name="sum_small_grads")
    summed = _unpack_rows(sm_sum, small_shapes + [(1, CH)])
    loss = summed[-1][0, 0]
    sg = dict(zip(small_names, summed[:-1]))
    for n in SMALL_SHARDED:
        width = SMALL_SHARD_SHAPE[n][-1]
        sg[n] = lax.dynamic_slice_in_dim(sg[n], dev * width, width, axis=sg[n].ndim - 1)
    pk = lambda d: _pack_rows([d[n] for n in small_names], 160)
    sd, snm, snv = adamw_small(pk(w), pk(sg), pk(m), pk(v), name="adamw_small")
    shard_shapes = [SMALL_SHARD_SHAPE[n] for n in SMALL_SHARDED] + [SMALL_REPL_SHAPE[n] for n in SMALL_REPL]
    for dst, flat in ((out_d, sd), (out_m, snm), (out_v, snv)):
        dst.update(zip(small_names, _unpack_rows(flat, shard_shapes)))
    out_g.update(sg)

    return (loss, grad_x[None], *[out_g[n] for n in WEIGHTS], *[out_d[n] for n in WEIGHTS], *[out_m[n] for n in WEIGHTS],
            *[out_v[n] for n in WEIGHTS])
```

```python
import functools
import math

import numpy as np
import jax
import jax.numpy as jnp
from jax import lax
from jax.experimental import pallas as pl
from jax.experimental.pallas import tpu as pltpu

F32, BF16 = jnp.float32, jnp.bfloat16
HI = lax.Precision.HIGHEST
MESH_ID = pl.DeviceIdType.MESH

D = 1024
CH = 128
N_META = 16
N_PAD = CH - N_META
EPS = 1e-6
N_DEV = 8
DEPTH = 2
SSD_HEADS = 16
RET_HEADS = 4
SB_HEADS = 8
D_FF = 2816
ROPE_BASE = 10000.0

NF = 14336
COL_GATE = 10

ADAM_LR, ADAM_B1, ADAM_B2, ADAM_EPS, ADAM_WD, ADAM_STEP = 0.001, 0.9, 0.999, 1e-08, 0.01, 10

VMEM_BYTES = 48 * 1024 * 1024


def _pick(n, cands):
    for c in cands:
        if n % c == 0:
            return c
    raise ValueError((n, cands))


def _tok_block(t):
    return _pick(t, (384, 128))


def _cparams(ngrid, vmem=VMEM_BYTES):
    return pltpu.CompilerParams(dimension_semantics=("arbitrary",) * ngrid, vmem_limit_bytes=vmem)


def _iota(shape, dim):
    return lax.broadcasted_iota(jnp.int32, shape, dim)


def _sigmoid(x):
    return 1.0 / (1.0 + jnp.exp(-x))


def _silu(x):
    return x * _sigmoid(x)


def _dsilu(x):
    s = _sigmoid(x)
    return s * (1.0 + x * (1.0 - s))


def _softplus(x):
    return jnp.maximum(x, 0.0) + jnp.log(1.0 + jnp.exp(-jnp.abs(x)))


def _dot(a, b):
    return jnp.dot(a.astype(BF16), b.astype(BF16), preferred_element_type=F32)


def _dot_nt(a, b):
    return lax.dot_general(a.astype(BF16), b.astype(BF16), (((1,), (1,)), ((), ())), preferred_element_type=F32)


def _dot_tn(a, b):
    return lax.dot_general(a.astype(BF16), b.astype(BF16), (((0,), (0,)), ((), ())), preferred_element_type=F32)


def _dot_hi(a, b):
    return jnp.dot(a, b, precision=HI, preferred_element_type=F32)


def mm(a, b, *, name, out_dtype=F32, add=None, tm=None, tn=None, tk=None):
    m, k = a.shape
    k2, n = b.shape
    assert k == k2
    tm = tm or _pick(m, (1376, 384, 128))
    tn = tn or _pick(n, (512, 384, 256, 128))
    tk = tk or _pick(k, (1024, 1408, 512, 384, 128))
    nk = k // tk
    has_add = add is not None

    def body(*refs):
        if has_add:
            a_ref, b_ref, c_ref, o_ref = refs[:4]
            scr = refs[4:]
        else:
            a_ref, b_ref, o_ref = refs[:3]
            c_ref = None
            scr = refs[3:]
        x = _dot(a_ref[...], b_ref[...])
        if nk == 1:
            if has_add:
                x = x + c_ref[...]
            o_ref[...] = x.astype(out_dtype)
        else:
            acc = scr[0]
            kk = pl.program_id(2)

            @pl.when(kk == 0)
            def _():
                acc[...] = x

            @pl.when(kk > 0)
            def _():
                acc[...] += x

            @pl.when(kk == nk - 1)
            def _():
                r = acc[...]
                if has_add:
                    r = r + c_ref[...]
                o_ref[...] = r.astype(out_dtype)

    in_specs = [pl.BlockSpec((tm, tk), lambda i, j, kk: (i, kk)), pl.BlockSpec((tk, tn), lambda i, j, kk: (kk, j))]
    args = [a, b]
    if has_add:
        in_specs.append(pl.BlockSpec((tm, tn), lambda i, j, kk: (i, j)))
        args.append(add)
    return pl.pallas_call(
        body, name=name, out_shape=jax.ShapeDtypeStruct((m, n), out_dtype), grid=(m // tm, n // tn, nk),
        in_specs=in_specs, out_specs=pl.BlockSpec((tm, tn), lambda i, j, kk: (i, j)),
        scratch_shapes=[pltpu.VMEM((tm, tn), F32)] if nk > 1 else [],
        compiler_params=_cparams(3))(*args)


def mm_tn(a, b, *, name, tm=None, tn=None, tk=None):
    t, m = a.shape
    t2, n = b.shape
    assert t == t2
    tm = tm or _pick(m, (1024, 1408, 512, 128))
    tn = tn or _pick(n, (512, 384, 256, 128))
    tk = tk or _pick(t, (1376, 384, 128))
    nk = t // tk

    def body(a_ref, b_ref, o_ref):
        x = _dot_tn(a_ref[...], b_ref[...])
        kk = pl.program_id(2)

        @pl.when(kk == 0)
        def _():
            o_ref[...] = x

        @pl.when(kk > 0)
        def _():
            o_ref[...] += x

    return pl.pallas_call(
        body, name=name, out_shape=jax.ShapeDtypeStruct((m, n), F32), grid=(m // tm, n // tn, nk),
        in_specs=[pl.BlockSpec((tk, tm), lambda i, j, kk: (kk, i)), pl.BlockSpec((tk, tn), lambda i, j, kk: (kk, j))],
        out_specs=pl.BlockSpec((tm, tn), lambda i, j, kk: (i, j)),
        compiler_params=_cparams(3))(a, b)


def rms_fwd(x, w, *, name, out_dtype=F32, res=None):
    t, d = x.shape
    tb = _tok_block(t)
    has_res = res is not None

    def body(*refs):
        if has_res:
            x_ref, w_ref, r_ref, o_ref = refs
        else:
            x_ref, w_ref, o_ref = refs
        xv = x_ref[...]
        y = xv * lax.rsqrt(jnp.mean(xv * xv, axis=-1, keepdims=True) + EPS) * w_ref[...]
        if has_res:
            y = y + r_ref[...]
        o_ref[...] = y.astype(out_dtype)

    blk = pl.BlockSpec((tb, d), lambda i: (i, 0))
    wspec = pl.BlockSpec((1, d), lambda i: (0, 0))
    in_specs = [blk, wspec] + ([blk] if has_res else [])
    args = [x, w] + ([res] if has_res else [])
    return pl.pallas_call(body, name=name, out_shape=jax.ShapeDtypeStruct((t, d), out_dtype), grid=(t // tb,),
                          in_specs=in_specs, out_specs=blk, compiler_params=_cparams(1))(*args)


def rms_bwd(x, w, dy, *, name, add=None):
    t, d = x.shape
    tb = _tok_block(t)
    has_add = add is not None

    def body(*refs):
        if has_add:
            x_ref, w_ref, dy_ref, a_ref, dx_ref, dw_ref = refs
        else:
            x_ref, w_ref, dy_ref, dx_ref, dw_ref = refs
        xv = x_ref[...]
        dyv = dy_ref[...]
        r = lax.rsqrt(jnp.mean(xv * xv, axis=-1, keepdims=True) + EPS)
        g = dyv * w_ref[...]
        dx = r * g - xv * (r * r * r) * jnp.mean(xv * g, axis=-1, keepdims=True)
        if has_add:
            dx = dx + a_ref[...]
        dx_ref[...] = dx
        part = jnp.sum(dyv * xv * r, axis=0, keepdims=True)

        @pl.when(pl.program_id(0) == 0)
        def _():
            dw_ref[...] = part

        @pl.when(pl.program_id(0) > 0)
        def _():
            dw_ref[...] += part

    blk = pl.BlockSpec((tb, d), lambda i: (i, 0))
    wspec = pl.BlockSpec((1, d), lambda i: (0, 0))
    in_specs = [blk, wspec, blk] + ([blk] if has_add else [])
    args = [x, w, dy] + ([add] if has_add else [])
    return pl.pallas_call(body, name=name,
                          out_shape=[jax.ShapeDtypeStruct((t, d), F32), jax.ShapeDtypeStruct((1, d), F32)],
                          grid=(t // tb,), in_specs=in_specs, out_specs=[blk, wspec], compiler_params=_cparams(1))(*args)


def _shift_down(cur, prev8, k):
    z = jnp.concatenate([prev8, cur], axis=0)
    return pltpu.roll(z, k, 0)[8:]


def _shift_up(cur, next8, k):
    n = cur.shape[0] + 8
    z = jnp.concatenate([cur, next8], axis=0)
    return pltpu.roll(z, n - k, 0)[:cur.shape[0]]


def _prev8_spec(tb, width, col):
    return pl.BlockSpec((8, width), lambda i: (jnp.maximum(i * (tb // 8) - 1, 0), col))


def _next8_spec(tb, width, col, t):
    return pl.BlockSpec((8, width), lambda i: (jnp.minimum((i + 1) * (tb // 8), t // 8 - 1), col))


def _row_valid(i, tb, n, offset=0):
    rows = i * tb + offset + _iota((n, 1), 0)
    return (rows >= N_PAD).astype(F32)


def conv_a_fwd(proj, w8, *, name):
    t = proj.shape[0]
    tb = _tok_block(t)

    def body(b_ref, c_ref, x_ref, cp_ref, xp_ref, w_ref, o_ref):
        i = pl.program_id(0)
        u = c_ref[...] * x_ref[...] * _row_valid(i, tb, tb)
        up = cp_ref[...] * xp_ref[...] * _row_valid(i, tb, 8, -8) * (i > 0).astype(F32)
        w = w_ref[...]
        conv = w[2:3] * u + w[1:2] * _shift_down(u, up, 1) + w[0:1] * _shift_down(u, up, 2)
        o_ref[...] = b_ref[...] * conv

    blk = lambda col: pl.BlockSpec((tb, D), lambda i: (i, col))
    return pl.pallas_call(
        body, name=name, out_shape=jax.ShapeDtypeStruct((t, D), F32), grid=(t // tb,),
        in_specs=[blk(0), blk(1), blk(2), _prev8_spec(tb, D, 1), _prev8_spec(tb, D, 2), pl.BlockSpec((8, D), lambda i: (0, 0))],
        out_specs=pl.BlockSpec((tb, D), lambda i: (i, 0)), compiler_params=_cparams(1))(proj, proj, proj, proj, proj, w8)


def conv_a_bwd(proj, w8, dy, *, name):
    t = proj.shape[0]
    tb = _tok_block(t)
    nblk = t // tb

    def body(b_ref, c_ref, x_ref, cp_ref, xp_ref, dy_ref, dyn_ref, bn_ref, w_ref, o_ref, dw_ref):
        i = pl.program_id(0)
        vm = _row_valid(i, tb, tb)
        cv, xv, bv, dyv = c_ref[...], x_ref[...], b_ref[...], dy_ref[...]
        u = cv * xv * vm
        up = cp_ref[...] * xp_ref[...] * _row_valid(i, tb, 8, -8) * (i > 0).astype(F32)
        w = w_ref[...]
        u1 = _shift_down(u, up, 1)
        u2 = _shift_down(u, up, 2)
        conv = w[2:3] * u + w[1:2] * u1 + w[0:1] * u2
        dconv = dyv * bv
        dconv_n = dyn_ref[...] * bn_ref[...] * (i < nblk - 1).astype(F32)
        du = w[2:3] * dconv + w[1:2] * _shift_up(dconv, dconv_n, 1) + w[0:1] * _shift_up(dconv, dconv_n, 2)
        o_ref[:, 0:D] = dyv * conv
        o_ref[:, D:2 * D] = du * xv * vm
        o_ref[:, 2 * D:3 * D] = du * cv * vm

        @pl.when(i == 0)
        def _():
            dw_ref[...] = jnp.zeros_like(dw_ref)

        dw_ref[0:1, :] += jnp.sum(dconv * u2, axis=0, keepdims=True)
        dw_ref[1:2, :] += jnp.sum(dconv * u1, axis=0, keepdims=True)
        dw_ref[2:3, :] += jnp.sum(dconv * u, axis=0, keepdims=True)

    blk = lambda col: pl.BlockSpec((tb, D), lambda i: (i, col))
    w8spec = pl.BlockSpec((8, D), lambda i: (0, 0))
    return pl.pallas_call(
        body, name=name,
        out_shape=[jax.ShapeDtypeStruct((t, 3 * D), F32), jax.ShapeDtypeStruct((8, D), F32)], grid=(nblk,),
        in_specs=[blk(0), blk(1), blk(2), _prev8_spec(tb, D, 1), _prev8_spec(tb, D, 2), blk(0),
                  _next8_spec(tb, D, 0, t), _next8_spec(tb, D, 0, t), w8spec],
        out_specs=[pl.BlockSpec((tb, 3 * D), lambda i: (i, 0)), w8spec],
        compiler_params=_cparams(1))(proj, proj, proj, proj, proj, dy, dy, proj, w8)


XBC_W = 2048


def ssd_conv_fwd(proj, w8, b, *, name):
    t = proj.shape[0]
    tb = _tok_block(t)

    def body(x_ref, xp_ref, w_ref, b_ref, o_ref):
        i = pl.program_id(0)
        xm = x_ref[...] * _row_valid(i, tb, tb)
        xmp = xp_ref[...] * _row_valid(i, tb, 8, -8) * (i > 0).astype(F32)
        w = w_ref[...]
        c = w[3:4] * xm + w[2:3] * _shift_down(xm, xmp, 1) + w[1:2] * _shift_down(xm, xmp, 2) + w[0:1] * _shift_down(xm, xmp, 3)
        o_ref[...] = _silu(c + b_ref[...])

    return pl.pallas_call(
        body, name=name, out_shape=jax.ShapeDtypeStruct((t, XBC_W), F32), grid=(t // tb,),
        in_specs=[pl.BlockSpec((tb, XBC_W), lambda i: (i, 2)), _prev8_spec(tb, XBC_W, 2),
                  pl.BlockSpec((8, XBC_W), lambda i: (0, 0)), pl.BlockSpec((1, XBC_W), lambda i: (0, 0))],
        out_specs=pl.BlockSpec((tb, XBC_W), lambda i: (i, 0)), compiler_params=_cparams(1))(proj, proj, w8, b)


def ssd_conv_bwd_pre(proj, w8, b, dxa, *, name):
    t = proj.shape[0]
    tb = _tok_block(t)

    def body(x_ref, xp_ref, w_ref, b_ref, d_ref, o_ref, dw_ref, db_ref):
        i = pl.program_id(0)
        xm = x_ref[...] * _row_valid(i, tb, tb)
        xmp = xp_ref[...] * _row_valid(i, tb, 8, -8) * (i > 0).astype(F32)
        w = w_ref[...]
        x1, x2, x3 = _shift_down(xm, xmp, 1), _shift_down(xm, xmp, 2), _shift_down(xm, xmp, 3)
        c = w[3:4] * xm + w[2:3] * x1 + w[1:2] * x2 + w[0:1] * x3 + b_ref[...]
        dpre = d_ref[...] * _dsilu(c)
        o_ref[...] = dpre

        @pl.when(i == 0)
        def _():
            dw_ref[...] = jnp.zeros_like(dw_ref)
            db_ref[...] = jnp.zeros_like(db_ref)

        dw_ref[0:1, :] += jnp.sum(dpre * x3, axis=0, keepdims=True)
        dw_ref[1:2, :] += jnp.sum(dpre * x2, axis=0, keepdims=True)
        dw_ref[2:3, :] += jnp.sum(dpre * x1, axis=0, keepdims=True)
        dw_ref[3:4, :] += jnp.sum(dpre * xm, axis=0, keepdims=True)
        db_ref[...] += jnp.sum(dpre, axis=0, keepdims=True)

    w8spec = pl.BlockSpec((8, XBC_W), lambda i: (0, 0))
    bspec = pl.BlockSpec((1, XBC_W), lambda i: (0, 0))
    return pl.pallas_call(
        body, name=name,
        out_shape=[jax.ShapeDtypeStruct((t, XBC_W), F32), jax.ShapeDtypeStruct((8, XBC_W), F32), jax.ShapeDtypeStruct((1, XBC_W), F32)],
        grid=(t // tb,),
        in_specs=[pl.BlockSpec((tb, XBC_W), lambda i: (i, 2)), _prev8_spec(tb, XBC_W, 2), w8spec, bspec,
                  pl.BlockSpec((tb, XBC_W), lambda i: (i, 0))],
        out_specs=[pl.BlockSpec((tb, XBC_W), lambda i: (i, 0)), w8spec, bspec],
        compiler_params=_cparams(1))(proj, proj, w8, b, dxa)


def ssd_conv_bwd_in(dpre, w8, *, name):
    t = dpre.shape[0]
    tb = _tok_block(t)
    nblk = t // tb

    def body(d_ref, dn_ref, w_ref, o_ref):
        i = pl.program_id(0)
        d = d_ref[...]
        dn = dn_ref[...] * (i < nblk - 1).astype(F32)
        w = w_ref[...]
        dx = w[3:4] * d + w[2:3] * _shift_up(d, dn, 1) + w[1:2] * _shift_up(d, dn, 2) + w[0:1] * _shift_up(d, dn, 3)
        o_ref[...] = dx * _row_valid(i, tb, tb)

    return pl.pallas_call(
        body, name=name, out_shape=jax.ShapeDtypeStruct((t, XBC_W), F32), grid=(nblk,),
        in_specs=[pl.BlockSpec((tb, XBC_W), lambda i: (i, 0)), _next8_spec(tb, XBC_W, 0, t), pl.BlockSpec((8, XBC_W), lambda i: (0, 0))],
        out_specs=pl.BlockSpec((tb, XBC_W), lambda i: (i, 0)), compiler_params=_cparams(1))(dpre, dpre, w8)


def _col(x, h):
    return jnp.sum(jnp.where(_iota(x.shape, 1) == h, x, 0.0), axis=1, keepdims=True)


def _row(x, h):
    return jnp.sum(jnp.where(_iota(x.shape, 0) == h, x, 0.0), axis=0, keepdims=True)


def _ssd_common(xa, dtr, dtb, alog, c):
    vm = _row_valid(c, CH, CH)
    xs = xa[:, :D] * vm
    dt = _softplus(dtr + dtb)
    a = -jnp.exp(alog) * dt
    tri = (_iota((CH, CH), 0) >= _iota((CH, CH), 1)).astype(F32)
    acs = _dot_hi(tri, a)
    return vm, xs, dt, a, acs, acs.T


def _pair_lanes(v0, v1):
    lane = _iota((1, CH), 1)
    return jnp.where(lane < 64, v0, v1)


def _ssd_pairs_fwd(xs, xa, dt, acs, acs_t, dsk, hins):
    causal = _iota((CH, CH), 0) >= _iota((CH, CH), 1)
    lane = _iota((CH, CH), 1)
    last = _iota((CH, 1), 0) == CH - 1
    bgs = [xa[:, D + CH * g:D + CH * (g + 1)] for g in range(4)]
    cgs = [xa[:, D + 512 + CH * g:D + 512 + CH * (g + 1)] for g in range(4)]
    g_mats = [_dot_nt(cgs[g], bgs[g]) for g in range(4)]
    ps = []
    for q in range(8):
        h0, h1 = 2 * q, 2 * q + 1
        xs_p = xs[:, CH * q:CH * (q + 1)]
        ac0, ac1 = _col(acs, h0), _col(acs, h1)
        ar0, ar1 = _row(acs_t, h0), _row(acs_t, h1)
        l0 = jnp.exp(jnp.where(causal, ac0 - ar0, -1e30))
        l1 = jnp.exp(jnp.where(causal, ac1 - ar1, -1e30))
        dt_p = _pair_lanes(_col(dt, h0), _col(dt, h1))
        ac_p = _pair_lanes(ac0, ac1)
        al0 = jnp.sum(jnp.where(last, ac0, 0.0), axis=0, keepdims=True)
        al1 = jnp.sum(jnp.where(last, ac1, 0.0), axis=0, keepdims=True)
        ps.append(dict(bg=bgs[q // 2], cg=cgs[q // 2], xs_p=xs_p, l0=l0, l1=l1, dt_p=dt_p, x=xs_p * dt_p, eac=jnp.exp(ac_p),
                       dsv=jnp.exp(_pair_lanes(al0, al1) - ac_p), al0=al0, al1=al1,
                       cd=jnp.where(_iota((CH, 1), 0) < 64, jnp.exp(al0), jnp.exp(al1)),
                       d_p=_pair_lanes(_col(dsk, h0), _col(dsk, h1))))
    for q, p in enumerate(ps):
        p["m0"], p["m1"] = g_mats[q // 2] * p["l0"], g_mats[q // 2] * p["l1"]
    for q, p in enumerate(ps):
        p["yd0"], p["yd1"] = _dot(p["m0"], p["x"]), _dot(p["m1"], p["x"])
        p["yoff_raw"] = _dot_nt(p["cg"], hins[q])
        p["s"] = _dot_tn(p["x"] * p["dsv"], p["bg"])
    for p in ps:
        p["y"] = jnp.where(lane < 64, p["yd0"], p["yd1"]) + p["yoff_raw"] * p["eac"] + p["xs_p"] * p["d_p"]
    return ps


def _ssd_gate_norm(y, z, nw):
    yv = y * _silu(z)
    outs, rs = [], []
    for g in range(4):
        yg = yv[:, 256 * g:256 * (g + 1)]
        r = lax.rsqrt(jnp.mean(yg * yg, axis=-1, keepdims=True) + EPS)
        outs.append(yg * r * nw[:, 256 * g:256 * (g + 1)])
        rs.append(r)
    return yv, jnp.concatenate(outs, axis=1), rs


def ssd_fwd(xa, proj, pdt, dtb, alog, dsk, nw, *, name):
    t = xa.shape[0]
    nc = t // CH

    def body(xa_ref, dtr_ref, z_ref, dtb_ref, alog_ref, dsk_ref, nw_ref, y_ref, hs_ref, h_scr):
        c = pl.program_id(0)

        @pl.when(c == 0)
        def _():
            h_scr[...] = jnp.zeros_like(h_scr)

        xa_v = xa_ref[...]
        vm, xs, dt, a, acs, acs_t = _ssd_common(xa_v, dtr_ref[...], dtb_ref[...], alog_ref[...], c)
        dsk_v = dsk_ref[...]
        hins = [h_scr[q] for q in range(8)]
        ps = _ssd_pairs_fwd(xs, xa_v, dt, acs, acs_t, dsk_v, hins)
        for q, p in enumerate(ps):
            hs_ref[0, q] = hins[q]
            h_scr[q] = hins[q] * p["cd"] + p["s"]
        y = jnp.concatenate([p["y"] for p in ps], axis=1)
        _, out, _ = _ssd_gate_norm(y, z_ref[...], nw_ref[...])
        y_ref[...] = out

    small = pl.BlockSpec((1, CH), lambda c: (0, 0))
    return pl.pallas_call(
        body, name=name,
        out_shape=[jax.ShapeDtypeStruct((t, D), F32), jax.ShapeDtypeStruct((nc, 8, CH, CH), F32)], grid=(nc,),
        in_specs=[pl.BlockSpec((CH, XBC_W), lambda c: (c, 0)), pl.BlockSpec((CH, CH), lambda c: (c, 0)),
                  pl.BlockSpec((CH, D), lambda c: (c, 3)), small, small, small, pl.BlockSpec((1, D), lambda c: (0, 0))],
        out_specs=[pl.BlockSpec((CH, D), lambda c: (c, 0)), pl.BlockSpec((1, 8, CH, CH), lambda c: (c, 0, 0, 0))],
        scratch_shapes=[pltpu.VMEM((8, CH, CH), F32)], compiler_params=_cparams(1))(xa, pdt, proj, dtb, alog, dsk, nw)


def ssd_bwd(xa, proj, pdt, hs, dyb, dtb, alog, dsk, nw, *, name):
    t = xa.shape[0]
    nc = t // CH

    def body(xa_ref, dtr_ref, z_ref, hs_ref, dy_ref, dtb_ref, alog_ref, dsk_ref, nw_ref,
             dz_ref, dxa_ref, ddt_ref, gdtb_ref, galog_ref, gdsk_ref, gnw_ref, dh_scr):
        step = pl.program_id(0)
        c = nc - 1 - step

        @pl.when(step == 0)
        def _():
            dh_scr[...] = jnp.zeros_like(dh_scr)
            gdtb_ref[...] = jnp.zeros_like(gdtb_ref)
            galog_ref[...] = jnp.zeros_like(galog_ref)
            gdsk_ref[...] = jnp.zeros_like(gdsk_ref)
            gnw_ref[...] = jnp.zeros_like(gnw_ref)

        xa_v = xa_ref[...]
        dtr = dtr_ref[...]
        dtb_v = dtb_ref[...]
        alog_v = alog_ref[...]
        vm, xs, dt, a, acs, acs_t = _ssd_common(xa_v, dtr, dtb_v, alog_v, c)
        dsk_v = dsk_ref[...]
        z = z_ref[...]
        nw_v = nw_ref[...]
        lane1 = _iota((1, CH), 1)
        sub1 = _iota((CH, 1), 0)
        lane = _iota((CH, CH), 1)

        hins = [hs_ref[0, q] for q in range(8)]
        pairs = _ssd_pairs_fwd(xs, xa_v, dt, acs, acs_t, dsk_v, hins)
        y_pre = jnp.concatenate([p["y"] for p in pairs], axis=1)

        dout = dy_ref[...]
        sz = _silu(z)
        yv = y_pre * sz
        dyv_parts = []
        gnw_parts = []
        for g in range(4):
            sl = slice(256 * g, 256 * (g + 1))
            yg = yv[:, sl]
            r = lax.rsqrt(jnp.mean(yg * yg, axis=-1, keepdims=True) + EPS)
            gy = dout[:, sl] * nw_v[:, sl]
            dyv_parts.append(r * gy - yg * (r * r * r) * jnp.mean(yg * gy, axis=-1, keepdims=True))
            gnw_parts.append(jnp.sum(dout[:, sl] * yg * r, axis=0, keepdims=True))
        dyv = jnp.concatenate(dyv_parts, axis=1)
        gnw_ref[...] += jnp.concatenate(gnw_parts, axis=1)
        dz_ref[...] = dyv * y_pre * _dsilu(z)
        dy_pre = dyv * sz

        dacs_c = jnp.zeros((CH, CH), F32)
        dacs_r = jnp.zeros((CH, CH), F32)
        ddt = jnp.zeros((CH, CH), F32)
        gdsk = jnp.zeros((1, CH), F32)
        dxs_parts = []
        db_g = [None] * 4
        dc_g = [None] * 4
        dg_g = [None] * 4

        def acc(lst, g, v):
            lst[g] = v if lst[g] is None else lst[g] + v

        m_lo = lane < 64
        dhouts = [dh_scr[q] for q in range(8)]
        mats = []
        for q in range(8):
            p = pairs[q]
            dy = dy_pre[:, CH * q:CH * (q + 1)]
            dye = dy * p["eac"]
            mats.append(dict(
                dy=dy, dye=dye,
                dm0=_dot_nt(jnp.where(m_lo, dy, 0.0), p["x"]), dm1=_dot_nt(jnp.where(m_lo, 0.0, dy), p["x"]),
                dx0=_dot_tn(p["m0"], dy), dx1=_dot_tn(p["m1"], dy),
                dc=_dot(dye, hins[q]), dhin=_dot_tn(dye, p["cg"]),
                w1=_dot_nt(p["bg"], dhouts[q]), db=_dot(p["x"] * p["dsv"], dhouts[q])))

        for q in range(8):
            p = pairs[q]
            mt = mats[q]
            g = q // 2
            h0, h1 = 2 * q, 2 * q + 1
            dy = mt["dy"]
            hin = hins[q]
            dhout = dhouts[q]
            x = p["x"]
            dxs = dy * p["d_p"]
            t_sk = dy * p["xs_p"]
            gdsk = gdsk + jnp.where(lane1 == h0, jnp.sum(jnp.where(m_lo, t_sk, 0.0)), 0.0) \
                        + jnp.where(lane1 == h1, jnp.sum(jnp.where(m_lo, 0.0, t_sk)), 0.0)
            dx = jnp.where(m_lo, mt["dx0"], mt["dx1"])
            for hh, lk, mm_, dm in ((h0, p["l0"], p["m0"], mt["dm0"]), (h1, p["l1"], p["m1"], mt["dm1"])):
                acc(dg_g, g, dm * lk)
                qm = dm * mm_
                dacs_c = dacs_c + jnp.where(lane1 == hh, jnp.sum(qm, axis=1, keepdims=True), 0.0)
                dacs_r = dacs_r - jnp.where(sub1 == hh, jnp.sum(qm, axis=0, keepdims=True), 0.0)
            acc(dc_g, g, mt["dc"])
            t_off = dy * p["yoff_raw"] * p["eac"]
            dacs_c = dacs_c + jnp.where(lane1 == h0, jnp.sum(jnp.where(m_lo, t_off, 0.0), axis=1, keepdims=True), 0.0) \
                            + jnp.where(lane1 == h1, jnp.sum(jnp.where(m_lo, 0.0, t_off), axis=1, keepdims=True), 0.0)
            dhin = mt["dhin"] + dhout * p["cd"]
            w1 = mt["w1"]
            dx = dx + p["dsv"] * w1
            t_ds = x * w1 * p["dsv"]
            dd0 = jnp.sum(jnp.where(m_lo, t_ds, 0.0), axis=1, keepdims=True)
            dd1 = jnp.sum(jnp.where(m_lo, 0.0, t_ds), axis=1, keepdims=True)
            acc(db_g, g, mt["db"])
            t_cd = dhout * hin
            sub_lo = _iota((CH, CH), 0) < 64
            dcd0 = jnp.sum(jnp.where(sub_lo, t_cd, 0.0)) * jnp.exp(p["al0"])
            dcd1 = jnp.sum(jnp.where(sub_lo, 0.0, t_cd)) * jnp.exp(p["al1"])
            last = (sub1 == CH - 1)
            dacs_c = dacs_c + jnp.where(lane1 == h0, jnp.where(last, jnp.sum(dd0) + dcd0, 0.0) - dd0, 0.0) \
                            + jnp.where(lane1 == h1, jnp.where(last, jnp.sum(dd1) + dcd1, 0.0) - dd1, 0.0)
            dh_scr[q] = dhin
            dxs = dxs + dx * p["dt_p"]
            t_dt = dx * p["xs_p"]
            ddt = ddt + jnp.where(lane1 == h0, jnp.sum(jnp.where(m_lo, t_dt, 0.0), axis=1, keepdims=True), 0.0) \
                      + jnp.where(lane1 == h1, jnp.sum(jnp.where(m_lo, 0.0, t_dt), axis=1, keepdims=True), 0.0)
            dxs_parts.append(dxs)

        for g in range(4):
            bg, cg = pairs[2 * g]["bg"], pairs[2 * g]["cg"]
            dc_g[g] = dc_g[g] + _dot(dg_g[g], bg)
            db_g[g] = db_g[g] + _dot_tn(dg_g[g], cg)

        dacs = dacs_c + dacs_r.T
        rtri = (_iota((CH, CH), 1) >= _iota((CH, CH), 0)).astype(F32)
        da = _dot_hi(rtri, dacs)
        ddt = ddt - da * jnp.exp(alog_v)
        galog_ref[...] += jnp.sum(da * a, axis=0, keepdims=True)
        dpre = ddt * _sigmoid(dtr + dtb_v) * (lane1 < SSD_HEADS).astype(F32)
        ddt_ref[...] = dpre
        gdtb_ref[...] += jnp.sum(dpre, axis=0, keepdims=True)
        gdsk_ref[...] += gdsk
        dxa_ref[:, 0:D] = jnp.concatenate(dxs_parts, axis=1) * vm
        dxa_ref[:, D:D + 512] = jnp.concatenate(db_g, axis=1)
        dxa_ref[:, D + 512:D + 1024] = jnp.concatenate(dc_g, axis=1)

    small = pl.BlockSpec((1, CH), lambda s: (0, 0))
    wide = pl.BlockSpec((1, D), lambda s: (0, 0))
    rev = lambda s: nc - 1 - s
    return pl.pallas_call(
        body, name=name,
        out_shape=[jax.ShapeDtypeStruct((t, D), F32), jax.ShapeDtypeStruct((t, XBC_W), F32), jax.ShapeDtypeStruct((t, CH), F32),
                   jax.ShapeDtypeStruct((1, CH), F32), jax.ShapeDtypeStruct((1, CH), F32), jax.ShapeDtypeStruct((1, CH), F32),
                   jax.ShapeDtypeStruct((1, D), F32)],
        grid=(nc,),
        in_specs=[pl.BlockSpec((CH, XBC_W), lambda s: (rev(s), 0)), pl.BlockSpec((CH, CH), lambda s: (rev(s), 0)),
                  pl.BlockSpec((CH, D), lambda s: (rev(s), 3)), pl.BlockSpec((1, 8, CH, CH), lambda s: (rev(s), 0, 0, 0)),
                  pl.BlockSpec((CH, D), lambda s: (rev(s), 0)), small, small, small, wide],
        out_specs=[pl.BlockSpec((CH, D), lambda s: (rev(s), 0)), pl.BlockSpec((CH, XBC_W), lambda s: (rev(s), 0)),
                   pl.BlockSpec((CH, CH), lambda s: (rev(s), 0)), small, small, small, wide],
        scratch_shapes=[pltpu.VMEM((8, CH, CH), F32)], compiler_params=_cparams(1))(xa, pdt, proj, hs, dyb, dtb, alog, dsk, nw)


RET_DK = 256


def _log_gamma(h):
    return math.log(1.0 - 2.0 ** (-5.0 - h))


def _rope(x, cos, sin):
    x1, x2 = x[:, :128], x[:, 128:]
    return jnp.concatenate([x1 * cos - x2 * sin, x1 * sin + x2 * cos], axis=1)


def _unrope(d, cos, sin):
    d1, d2 = d[:, :128], d[:, 128:]
    return jnp.concatenate([d1 * cos + d2 * sin, d2 * cos - d1 * sin], axis=1)


def _ret_heads_fwd(q, k, v, cos, sin, vm, r_ins):
    hs = range(RET_HEADS)
    lgs = [_log_gamma(h) for h in hs]
    sls = [slice(RET_DK * h, RET_DK * (h + 1)) for h in hs]
    qr = [_rope(q[:, sls[h]], cos, sin) for h in hs]
    kr = [_rope(k[:, sls[h]], cos, sin) * (RET_DK ** -0.5) for h in hs]
    vr = [v[:, sls[h]] * vm for h in hs]
    rel = (_iota((CH, CH), 0) - _iota((CH, CH), 1)).astype(F32)
    idx = _iota((CH, 1), 0).astype(F32)
    dmask = [jnp.where(rel >= 0, jnp.exp(lgs[h] * jnp.maximum(rel, 0.0)), 0.0) for h in hs]
    kdec = [jnp.exp(lgs[h] * (CH - 1 - idx)) for h in hs]
    qdec = [jnp.exp(lgs[h] * (idx + 1.0)) for h in hs]
    raw = [_dot_nt(qr[h], kr[h]) for h in hs]
    cross = [_dot(qr[h], r_ins[h]) for h in hs]
    kv = [_dot_tn(kr[h] * kdec[h], vr[h]) for h in hs]
    scores = [raw[h] * dmask[h] for h in hs]
    y = [_dot(scores[h], vr[h]) + cross[h] * qdec[h] for h in hs]
    return [dict(qr=qr[h], kr=kr[h], vr=vr[h], dmask=dmask[h], kdec=kdec[h], qdec=qdec[h], scores=scores[h], y=y[h], kv=kv[h],
                 cdec=math.exp(lgs[h] * CH)) for h in hs]


def _group_norm(y):
    mu = jnp.mean(y, axis=-1, keepdims=True)
    yc = y - mu
    r = lax.rsqrt(jnp.mean(yc * yc, axis=-1, keepdims=True) + EPS)
    return yc * r, r


def ret_fwd(proj, cos, sin, *, name):
    t = proj.shape[0]
    nc = t // CH

    def body(q_ref, k_ref, v_ref, g_ref, cos_ref, sin_ref, y_ref, rs_ref, r_scr):
        c = pl.program_id(0)

        @pl.when(c == 0)
        def _():
            r_scr[...] = jnp.zeros_like(r_scr)

        vm = _row_valid(c, CH, CH)
        q, k, v, gt = q_ref[...], k_ref[...], v_ref[...], g_ref[...]
        cos, sin = cos_ref[...], sin_ref[...]
        r_ins = [r_scr[h] for h in range(RET_HEADS)]
        ps = _ret_heads_fwd(q, k, v, cos, sin, vm, r_ins)
        for h, p in enumerate(ps):
            rs_ref[0, h] = r_ins[h]
            r_scr[h] = r_ins[h] * p["cdec"] + p["kv"]
            yn, _ = _group_norm(p["y"])
            sl = slice(RET_DK * h, RET_DK * (h + 1))
            y_ref[:, sl] = yn * _silu(gt[:, sl])

    blk = lambda col: pl.BlockSpec((CH, D), lambda c: (c, col))
    tab = pl.BlockSpec((CH, CH), lambda c: (c, 0))
    return pl.pallas_call(
        body, name=name,
        out_shape=[jax.ShapeDtypeStruct((t, D), F32), jax.ShapeDtypeStruct((nc, RET_HEADS, RET_DK, RET_DK), F32)], grid=(nc,),
        in_specs=[blk(6), blk(7), blk(8), blk(9), tab, tab],
        out_specs=[pl.BlockSpec((CH, D), lambda c: (c, 0)), pl.BlockSpec((1, RET_HEADS, RET_DK, RET_DK), lambda c: (c, 0, 0, 0))],
        scratch_shapes=[pltpu.VMEM((RET_HEADS, RET_DK, RET_DK), F32)], compiler_params=_cparams(1))(proj, proj, proj, proj, cos, sin)


def ret_bwd(proj, cos, sin, rs, dyc, *, name):
    t = proj.shape[0]
    nc = t // CH

    def body(q_ref, k_ref, v_ref, g_ref, cos_ref, sin_ref, rs_ref, dy_ref, o_ref, dr_scr):
        step = pl.program_id(0)
        c = nc - 1 - step

        @pl.when(step == 0)
        def _():
            dr_scr[...] = jnp.zeros_like(dr_scr)

        vm = _row_valid(c, CH, CH)
        q, k, v, gt = q_ref[...], k_ref[...], v_ref[...], g_ref[...]
        cos, sin = cos_ref[...], sin_ref[...]
        dout = dy_ref[...]
        hs = range(RET_HEADS)
        sls = [slice(RET_DK * h, RET_DK * (h + 1)) for h in hs]
        r_ins = [rs_ref[0, h] for h in hs]
        dr_outs = [dr_scr[h] for h in hs]
        ps = _ret_heads_fwd(q, k, v, cos, sin, vm, r_ins)
        dys, dgs = [], []
        for h in hs:
            yn, r = _group_norm(ps[h]["y"])
            gh = gt[:, sls[h]]
            do = dout[:, sls[h]]
            dgs.append(do * yn * _dsilu(gh))
            dyn = do * _silu(gh)
            dys.append(r * (dyn - jnp.mean(dyn, axis=-1, keepdims=True) - yn * jnp.mean(dyn * yn, axis=-1, keepdims=True)))
        dycs = [dys[h] * ps[h]["qdec"] for h in hs]
        dqr = [_dot_nt(dycs[h], r_ins[h]) for h in hs]
        dr_new = [_dot_tn(ps[h]["qr"], dycs[h]) for h in hs]
        dkr = [_dot_nt(ps[h]["vr"], dr_outs[h]) * ps[h]["kdec"] for h in hs]
        dv = [_dot(ps[h]["kr"] * ps[h]["kdec"], dr_outs[h]) + _dot_tn(ps[h]["scores"], dys[h]) for h in hs]
        ds = [_dot_nt(dys[h], ps[h]["vr"]) * ps[h]["dmask"] for h in hs]
        dqr = [dqr[h] + _dot(ds[h], ps[h]["kr"]) for h in hs]
        dkr = [dkr[h] + _dot_tn(ds[h], ps[h]["qr"]) for h in hs]
        for h in hs:
            dr_scr[h] = dr_outs[h] * ps[h]["cdec"] + dr_new[h]
            o_ref[:, RET_DK * h:RET_DK * (h + 1)] = _unrope(dqr[h], cos, sin)
            o_ref[:, D + RET_DK * h:D + RET_DK * (h + 1)] = _unrope(dkr[h], cos, sin) * (RET_DK ** -0.5)
            o_ref[:, 2 * D + RET_DK * h:2 * D + RET_DK * (h + 1)] = dv[h] * vm
            o_ref[:, 3 * D + RET_DK * h:3 * D + RET_DK * (h + 1)] = dgs[h]

    rev = lambda s: nc - 1 - s
    blk = lambda col: pl.BlockSpec((CH, D), lambda s: (rev(s), col))
    tab = pl.BlockSpec((CH, CH), lambda s: (rev(s), 0))
    return pl.pallas_call(
        body, name=name, out_shape=jax.ShapeDtypeStruct((t, 4 * D), F32), grid=(nc,),
        in_specs=[blk(6), blk(7), blk(8), blk(9), tab, tab,
                  pl.BlockSpec((1, RET_HEADS, RET_DK, RET_DK), lambda s: (rev(s), 0, 0, 0)), pl.BlockSpec((CH, D), lambda s: (rev(s), 0))],
        out_specs=pl.BlockSpec((CH, 4 * D), lambda s: (rev(s), 0)),
        scratch_shapes=[pltpu.VMEM((RET_HEADS, RET_DK, RET_DK), F32)], compiler_params=_cparams(1))(proj, proj, proj, proj, cos, sin, rs, dyc)


SB_D = 128
SB_SCALE = SB_D ** -0.5
SB_CUTOFF = 104.0


def _split_hi_lo(x):
    hi = x.astype(BF16)
    lo = (x - hi.astype(F32)).astype(BF16)
    return hi, lo


def _sum_matrix(kind):
    a, b = _iota((128, 128), 0), _iota((128, 128), 1)
    tri = ((b > a) if kind == "after" else (b < a)).astype(BF16)
    return jnp.concatenate([tri, tri], axis=1)


def _key_sums(x, mat2):
    hi, lo = _split_hi_lo(x)
    return jnp.dot(mat2, jnp.concatenate([hi, lo], axis=0), preferred_element_type=F32)


def _sb_mask(d_kq, key_idx, first_key, q_minus_k):
    return (d_kq < q_minus_k) & (key_idx >= N_PAD - first_key)


def _sb_log_sigmoid(z):
    return jnp.minimum(z, 0.0) - jnp.log(1.0 + jnp.exp(-jnp.abs(z)))


def sb_fwd(qkv, *, name):
    t = qkv.shape[0]
    tq = _tok_block(t)
    nq = t // tq
    per = tq // 128

    def body(q_ref, k_ref, v_ref, o_ref, at_ref, bmin_ref):
        h = pl.program_id(0)
        i = pl.program_id(1)
        top = (i + 1) * per - 1
        mat_after = _sum_matrix("after")
        qs = [q_ref[pl.ds(128 * r, 128), :] for r in range(per)]
        d_kq = _iota((128, 128), 0) - _iota((128, 128), 1)
        key_idx = _iota((128, 128), 0)

        def mask(r, b):
            return _sb_mask(d_kq, key_idx, b * 128, (i * per + r - b) * 128)

        def step(carry):
            b, _, a_runs, accs = carry
            off = pl.multiple_of(b * 128, 128)
            kb = k_ref[pl.ds(off, 128), :]
            vb = v_ref[pl.ds(off, 128), :]
            tiles = range(per)
            zs = [_dot_nt(kb, qs[r]) * SB_SCALE for r in tiles]
            ms = [mask(r, b) for r in tiles]
            lss = [_sb_log_sigmoid(zs[r]) for r in tiles]
            lnegs = [jnp.where(ms[r], lss[r] - zs[r], 0.0) for r in tiles]
            sufs = [_key_sums(lnegs[r], mat_after) for r in tiles]
            ws = [jnp.where(ms[r], jnp.exp(lss[r] + a_runs[r] + sufs[r]), 0.0) for r in tiles]
            a_new = [a_runs[r] + sufs[r][0:1, :] + lnegs[r][0:1, :] for r in tiles]
            acc_new = [accs[r] + _dot_tn(ws[r], vb) for r in tiles]
            a_max = jnp.max(functools.reduce(jnp.maximum, a_new))
            return b - 1, a_max >= -SB_CUTOFF, tuple(a_new), tuple(acc_new)

        zeros = tuple(qs[r].astype(F32) * 0.0 for r in range(per))
        zrow = tuple(z[0:1, :] for z in zeros)
        b_end, _, a_runs, accs = lax.while_loop(lambda c: jnp.logical_and(c[0] >= 0, c[1]), step, (top, top >= 0, zrow, zeros))
        bmin_ref[h, i] = b_end + 1
        at_ref[...] = jnp.zeros_like(at_ref)
        for r in range(per):
            o_ref[pl.ds(128 * r, 128), :] = accs[r]
            at_ref[0, 0, r:r + 1, :] = a_runs[r]

    blk = pl.BlockSpec((tq, 128), lambda h, i: (i, h))
    return pl.pallas_call(
        body, name=name,
        out_shape=[jax.ShapeDtypeStruct((t, D), F32), jax.ShapeDtypeStruct((SB_HEADS, nq, 8, 128), F32),
                   jax.ShapeDtypeStruct((SB_HEADS, nq), jnp.int32)],
        grid=(SB_HEADS, nq),
        in_specs=[blk, pl.BlockSpec((t, 128), lambda h, i: (0, SB_HEADS + h)),
                  pl.BlockSpec((t, 128), lambda h, i: (0, 2 * SB_HEADS + h))],
        out_specs=[blk, pl.BlockSpec((1, 1, 8, 128), lambda h, i: (h, i, 0, 0)), pl.BlockSpec(memory_space=pltpu.SMEM)],
        compiler_params=_cparams(2))(qkv, qkv, qkv)


def sb_bwd(qkv, atot, bmin, dout, *, name):
    t = qkv.shape[0]
    tq = _tok_block(t)
    nq = t // tq
    per = tq // 128

    nblk = t // 128

    def body(bmin_ref, q_ref, k_ref, v_ref, at_ref, do_ref, dq_ref, dk_ref, dv_ref):
        i = pl.program_id(1)
        top = (i + 1) * per - 1
        b_first = bmin_ref[pl.program_id(0), i]

        @pl.when(i == 0)
        def _():
            dk_ref[...] = jnp.zeros_like(dk_ref)
            dv_ref[...] = jnp.zeros_like(dv_ref)

        qs = [q_ref[pl.ds(128 * r, 128), :] for r in range(per)]
        dos = [do_ref[pl.ds(128 * r, 128), :].astype(BF16) for r in range(per)]
        q_all = q_ref[...]
        do_all = do_ref[...].astype(BF16)
        a_tots = [at_ref[0, 0, r:r + 1, :] for r in range(per)]
        mat_after = _sum_matrix("after")
        mat_before = _sum_matrix("before")
        d_kq = _iota((128, 128), 0) - _iota((128, 128), 1)
        key_idx = _iota((128, 128), 0)

        def offset(b):
            return pl.multiple_of(jnp.clip(b, 0, nblk - 1) * 128, 128)

        def mask(r, b):
            return _sb_mask(d_kq, key_idx, b * 128, (i * per + r - b) * 128)

        def step(b, carry):
            lss, dws, sufs, tots, p_runs, e_runs, dqs = carry
            off = offset(b)
            kb = k_ref[pl.ds(off, 128), :]
            vb = v_ref[pl.ds(off, 128), :]
            zs = [_dot_nt(kb, qs[r]) * SB_SCALE for r in range(per)]
            dw_new = tuple(_dot_nt(vb, dos[r]) for r in range(per))
            off1 = offset(b - 1)
            k1 = k_ref[pl.ds(off1, 128), :]
            ws, es, epres, sigs, p_new = [], [], [], [], []
            for r in range(per):
                p = p_runs[r] + tots[r]
                w = jnp.where(mask(r, b - 1), jnp.exp(lss[r] + (a_tots[r] - p) + sufs[r]), 0.0)
                e = w * dws[r]
                p_new.append(p)
                ws.append(w.astype(BF16))
                es.append(e)
                epres.append(_key_sums(e, mat_before))
                sigs.append(jnp.exp(lss[r]))
            dv_ref[pl.ds(off1, 128), :] += _dot(jnp.concatenate(ws, axis=1), do_all)
            ls_new, suf_new, tot_new = [], [], []
            for r in range(per):
                ls = _sb_log_sigmoid(zs[r])
                lneg = jnp.where(mask(r, b), ls - zs[r], 0.0)
                suf = _key_sums(lneg, mat_after)
                ls_new.append(ls)
                suf_new.append(suf)
                tot_new.append(suf[0:1, :] + lneg[0:1, :])
            dzs, e_new, dq_new = [], [], []
            for r in range(per):
                t2 = jnp.where(mask(r, b - 1), (e_runs[r] + epres[r]) * sigs[r], 0.0)
                dz = ((es[r] * (1.0 - sigs[r]) - t2) * SB_SCALE).astype(BF16)
                e_new.append(e_runs[r] + epres[r][127:128, :] + es[r][127:128, :])
                dq_new.append(dqs[r] + _dot_tn(dz, k1))
                dzs.append(dz)
            dk_ref[pl.ds(off1, 128), :] += _dot(jnp.concatenate(dzs, axis=1), q_all)
            return tuple(ls_new), dw_new, tuple(suf_new), tuple(tot_new), tuple(p_new), tuple(e_new), tuple(dq_new)

        zeros = tuple(qs[r].astype(F32) * 0.0 for r in range(per))
        zrow = tuple(z[0:1, :] for z in zeros)
        carry = lax.fori_loop(b_first, top + 2, step, (zeros, zeros, zeros, zrow, zrow, zrow, zeros))
        for r in range(per):
            dq_ref[pl.ds(128 * r, 128), :] = carry[6][r]

    head_blk = pl.BlockSpec((t, 128), lambda h, i: (0, h))
    return pl.pallas_call(
        body, name=name, out_shape=[jax.ShapeDtypeStruct((t, D), F32)] * 3, grid=(SB_HEADS, nq),
        in_specs=[pl.BlockSpec(memory_space=pltpu.SMEM),
                  pl.BlockSpec((tq, 128), lambda h, i: (i, h)), pl.BlockSpec((t, 128), lambda h, i: (0, SB_HEADS + h)),
                  pl.BlockSpec((t, 128), lambda h, i: (0, 2 * SB_HEADS + h)),
                  pl.BlockSpec((1, 1, 8, 128), lambda h, i: (h, i, 0, 0)), pl.BlockSpec((tq, 128), lambda h, i: (i, h))],
        out_specs=[pl.BlockSpec((tq, 128), lambda h, i: (i, h)), head_blk, head_blk],
        compiler_params=_cparams(2, 60 * 1024 * 1024))(bmin, qkv, qkv, qkv, atot, dout)


def branch_fwd(y, proj, w, n, *, name, add=None):
    t = y.shape[0]
    tb = _tok_block(t)
    has_add = add is not None

    def body(*refs):
        if has_add:
            y_ref, g_ref, w_ref, a_ref, o_ref = refs
        else:
            y_ref, g_ref, w_ref, o_ref = refs
        r = _sigmoid(g_ref[...]) * _dot(y_ref[...], w_ref[...])
        if has_add:
            r = r + a_ref[...]
        o_ref[...] = r

    blk = pl.BlockSpec((tb, D), lambda i: (i, 0))
    in_specs = [blk, pl.BlockSpec((tb, D), lambda i: (i, COL_GATE + n)), pl.BlockSpec((D, D), lambda i: (0, 0))] + ([blk] if has_add else [])
    args = [y, proj, w] + ([add] if has_add else [])
    return pl.pallas_call(body, name=name, out_shape=jax.ShapeDtypeStruct((t, D), F32), grid=(t // tb,),
                          in_specs=in_specs, out_specs=blk, compiler_params=_cparams(1))(*args)


def branch_bwd(y, proj, w, wt, n, dmerged, *, name):
    t = y.shape[0]
    tb = _tok_block(t)

    def body(y_ref, g_ref, w_ref, wt_ref, dm_ref, dg_ref, dup_ref, dy_ref):
        up = _dot(y_ref[...], w_ref[...])
        gate = _sigmoid(g_ref[...])
        dm = dm_ref[...]
        dg_ref[...] = dm * up * gate * (1.0 - gate)
        dup = (dm * gate).astype(BF16)
        dup_ref[...] = dup
        dy_ref[...] = _dot(dup, wt_ref[...])

    blk = pl.BlockSpec((tb, D), lambda i: (i, 0))
    wspec = pl.BlockSpec((D, D), lambda i: (0, 0))
    return pl.pallas_call(
        body, name=name,
        out_shape=[jax.ShapeDtypeStruct((t, D), F32), jax.ShapeDtypeStruct((t, D), BF16), jax.ShapeDtypeStruct((t, D), F32)],
        grid=(t // tb,),
        in_specs=[blk, pl.BlockSpec((tb, D), lambda i: (i, COL_GATE + n)), wspec, wspec, blk],
        out_specs=[blk, blk, blk], compiler_params=_cparams(1))(y, proj, w, wt, dmerged)


def swiglu_fwd(f, *, name):
    t = f.shape[0]
    tb = _tok_block(t)

    def body(g_ref, u_ref, o_ref):
        o_ref[...] = (_silu(g_ref[...]) * u_ref[...]).astype(BF16)

    return pl.pallas_call(body, name=name, out_shape=jax.ShapeDtypeStruct((t, D_FF), BF16), grid=(t // tb,),
                          in_specs=[pl.BlockSpec((tb, D_FF), lambda i: (i, 0)), pl.BlockSpec((tb, D_FF), lambda i: (i, 1))],
                          out_specs=pl.BlockSpec((tb, D_FF), lambda i: (i, 0)), compiler_params=_cparams(1))(f, f)


def swiglu_bwd(f, dact, *, name):
    t = f.shape[0]
    tb = _tok_block(t)

    def body(g_ref, u_ref, d_ref, o_ref):
        g, u, d = g_ref[...], u_ref[...], d_ref[...]
        o_ref[:, 0:D_FF] = d * u * _dsilu(g)
        o_ref[:, D_FF:2 * D_FF] = d * _silu(g)

    return pl.pallas_call(body, name=name, out_shape=jax.ShapeDtypeStruct((t, 2 * D_FF), F32), grid=(t // tb,),
                          in_specs=[pl.BlockSpec((tb, D_FF), lambda i: (i, 0)), pl.BlockSpec((tb, D_FF), lambda i: (i, 1)),
                                    pl.BlockSpec((tb, D_FF), lambda i: (i, 0))],
                          out_specs=pl.BlockSpec((tb, 2 * D_FF), lambda i: (i, 0)), compiler_params=_cparams(1))(f, f, dact)


def loss_head(h, target, *, name):
    t = h.shape[0]
    nb = t // CH

    def body(h_ref, t_ref, l_ref, d_ref):
        i = pl.program_id(0)

        @pl.when(i == 0)
        def _():
            l_ref[...] = jnp.zeros_like(l_ref)
            d_ref[...] = jnp.zeros_like(d_ref)

        @pl.when(i > 0)
        def _():
            err = h_ref[...] - t_ref[...]
            d_ref[...] = err * (1.0 / D)
            l_ref[...] += jnp.sum(err * err) * (0.5 / D)

    return pl.pallas_call(
        body, name=name, out_shape=[jax.ShapeDtypeStruct((8, 128), F32), jax.ShapeDtypeStruct((t, D), F32)], grid=(nb,),
        in_specs=[pl.BlockSpec((CH, D), lambda i: (i, 0)), pl.BlockSpec((CH, D), lambda i: (jnp.maximum(i - 1, 0), 0))],
        out_specs=[pl.BlockSpec((8, 128), lambda i: (0, 0)), pl.BlockSpec((CH, D), lambda i: (i, 0))],
        compiler_params=_cparams(1))(h, target)


def _pad_rows8(w):
    return jnp.concatenate([w, jnp.zeros((8 - w.shape[0], w.shape[1]), w.dtype)], axis=0)


def _pad_lanes(v, n=CH):
    return jnp.concatenate([v, jnp.zeros((n - v.shape[0],), v.dtype)])[None, :]


def prep_layer(p):
    w = p["w_in"]
    zeros = jnp.zeros((D, CH - SSD_HEADS), w.dtype)
    w_f = jnp.concatenate([w[:, :6144], w[:, 6160:10256], w[:, 13328:17424]], axis=1)
    w_dt = jnp.concatenate([w[:, 6144:6160], zeros], axis=1)
    w_sb = w[:, 10256:13328]
    return dict(
        w_f=w_f, w_sb=w_sb, w_dt=w_dt, w_f_t=w_f.T, w_sb_t=w_sb.T, w_dt_t=w_dt.T,
        w_br=p["w_branch"], w_br_t=jnp.swapaxes(p["w_branch"], 1, 2), w_out=p["w_out"], w_out_t=p["w_out"].T,
        w_fi=p["w_ffn_in"], w_fi_t=p["w_ffn_in"].T, w_fo=p["w_ffn_out"], w_fo_t=p["w_ffn_out"].T,
        conv_a8=_pad_rows8(p["conv_a"]), conv_s8=_pad_rows8(p["ssd_conv_w"]), conv_sb=p["ssd_conv_b"][None, :],
        dtb=_pad_lanes(p["ssd_dt_bias"]), alog=_pad_lanes(p["ssd_a_log"]), dsk=_pad_lanes(p["ssd_d"]), nw=p["ssd_norm"][None, :],
        n1=p["norm_mix_pre"][None, :], n2=p["norm_mix_post"][None, :], n3=p["norm_ffn_pre"][None, :], n4=p["norm_ffn_post"][None, :])


def layer_fwd(h0, w, cos, sin, l):
    nm = lambda s: f"l{l}_{s}"
    hn = rms_fwd(h0, w["n1"], name=nm("rms1"), out_dtype=BF16)
    proj = mm(hn, w["w_f"], name=nm("proj_f"))
    qkv = mm(hn, w["w_sb"], name=nm("proj_sb"), out_dtype=BF16)
    pdt = mm(hn, w["w_dt"], name=nm("proj_dt"))
    y_a = conv_a_fwd(proj, w["conv_a8"], name=nm("conv_a"))
    xa = ssd_conv_fwd(proj, w["conv_s8"], w["conv_sb"], name=nm("ssd_conv"))
    y_b, hs = ssd_fwd(xa, proj, pdt, w["dtb"], w["alog"], w["dsk"], w["nw"], name=nm("ssd"))
    y_c, rs = ret_fwd(proj, cos, sin, name=nm("ret"))
    y_d, sb_atot, sb_bmin = sb_fwd(qkv, name=nm("sb"))
    ys = (y_a, y_b, y_c, y_d)
    merged = None
    for n in range(4):
        merged = branch_fwd(ys[n], proj, w["w_br"][n], n, name=nm(f"branch{n}"), add=merged)
    mix = mm(merged, w["w_out"], name=nm("mix"))
    h1 = rms_fwd(mix, w["n2"], name=nm("rms2"), res=h0)
    hn2 = rms_fwd(h1, w["n3"], name=nm("rms3"), out_dtype=BF16)
    f = mm(hn2, w["w_fi"], name=nm("ffn_in"))
    act = swiglu_fwd(f, name=nm("swiglu"))
    f2 = mm(act, w["w_fo"], name=nm("ffn_out"))
    h2 = rms_fwd(f2, w["n4"], name=nm("rms4"), res=h1)
    saved = dict(h0=h0, hn=hn, proj=proj, qkv=qkv, pdt=pdt, xa=xa, hs=hs, rs=rs, ys=ys, sb_atot=sb_atot, sb_bmin=sb_bmin, merged=merged, mix=mix, h1=h1, hn2=hn2,
                 f=f, act=act, f2=f2)
    return h2, saved


def layer_bwd(dh2, s, w, cos, sin, l):
    nm = lambda t: f"l{l}_{t}"
    g = {}
    df2, g["n4"] = rms_bwd(s["f2"], w["n4"], dh2, name=nm("rms4_b"))
    g["w_fo"] = mm_tn(s["act"], df2, name=nm("ffn_out_dw"))
    dact = mm(df2, w["w_fo_t"], name=nm("ffn_out_dx"), out_dtype=BF16)
    df = swiglu_bwd(s["f"], dact, name=nm("swiglu_b"))
    g["w_fi"] = mm_tn(s["hn2"], df, name=nm("ffn_in_dw"))
    dhn2 = mm(df, w["w_fi_t"], name=nm("ffn_in_dx"))
    dh1, g["n3"] = rms_bwd(s["h1"], w["n3"], dhn2, name=nm("rms3_b"), add=dh2)
    dmix, g["n2"] = rms_bwd(s["mix"], w["n2"], dh1, name=nm("rms2_b"))
    g["w_out"] = mm_tn(s["merged"], dmix, name=nm("mix_dw"))
    dmerged = mm(dmix, w["w_out_t"], name=nm("mix_dx"))
    dgate, dys, dwb = [], [], []
    for n in range(4):
        dg_n, dup_n, dy_n = branch_bwd(s["ys"][n], s["proj"], w["w_br"][n], w["w_br_t"][n], n, dmerged, name=nm(f"branch{n}_b"))
        dgate.append(dg_n)
        dys.append(dy_n)
        dwb.append(mm_tn(s["ys"][n], dup_n, name=nm(f"branch{n}_dw")))
    g["w_br"] = jnp.stack(dwb)
    d_a, g["conv_a8"] = conv_a_bwd(s["proj"], w["conv_a8"], dys[0], name=nm("conv_a_b"))
    dz, dxa, ddt, g["dtb"], g["alog"], g["dsk"], g["nw"] = ssd_bwd(
        s["xa"], s["proj"], s["pdt"], s["hs"], dys[1], w["dtb"], w["alog"], w["dsk"], w["nw"], name=nm("ssd_b"))
    dpre, g["conv_s8"], g["conv_sb"] = ssd_conv_bwd_pre(s["proj"], w["conv_s8"], w["conv_sb"], dxa, name=nm("ssd_conv_b1"))
    dxbc = ssd_conv_bwd_in(dpre, w["conv_s8"], name=nm("ssd_conv_b2"))
    d_r = ret_bwd(s["proj"], cos, sin, s["rs"], dys[2], name=nm("ret_b"))
    d_sb = sb_bwd(s["qkv"], s["sb_atot"], s["sb_bmin"], dys[3], name=nm("sb_b"))
    segs = [(d_a, 0), (dz, 3072), (dxbc, 4096), (d_r, 6144), (dgate[0], 10240), (dgate[1], 11264), (dgate[2], 12288),
            (dgate[3], 13312)]
    dws = [mm_tn(s["hn"], d, name=nm(f"proj_dw{k}")) for k, (d, _) in enumerate(segs)]
    g["w_f"] = jnp.concatenate(dws, axis=1)
    g["w_sb"] = jnp.concatenate([mm_tn(s["hn"], d, name=nm(f"proj_dw_sb{k}")) for k, d in enumerate(d_sb)], axis=1)
    g["w_dt"] = mm_tn(s["hn"], ddt, name=nm("proj_dw_dt"))
    dhn = mm(ddt, w["w_dt_t"], name=nm("proj_dx_dt"))
    for k, d in enumerate(d_sb):
        dhn = mm(d, w["w_sb_t"][k * D:(k + 1) * D], name=nm(f"proj_dx_sb{k}"), add=dhn)
    for k, (d, c0) in enumerate(segs):
        dhn = mm(d, w["w_f_t"][c0:c0 + d.shape[1]], name=nm(f"proj_dx{k}"), add=dhn)
    dh0, g["n1"] = rms_bwd(s["h0"], w["n1"], dhn, name=nm("rms1_b"), add=dh1)
    return dh0, g


def layer_grads_to_params(g):
    wf, wsb = g["w_f"], g["w_sb"]
    w_in = jnp.concatenate([wf[:, :6144], g["w_dt"][:, :SSD_HEADS], wf[:, 6144:10240], wsb, wf[:, 10240:14336]], axis=1)
    return dict(
        w_in=w_in, conv_a=g["conv_a8"][:3], ssd_conv_w=g["conv_s8"][:4], ssd_conv_b=g["conv_sb"][0],
        ssd_dt_bias=g["dtb"][0, :SSD_HEADS], ssd_a_log=g["alog"][0, :SSD_HEADS], ssd_d=g["dsk"][0, :SSD_HEADS], ssd_norm=g["nw"][0],
        w_branch=g["w_br"], w_out=g["w_out"], w_ffn_in=g["w_fi"], w_ffn_out=g["w_fo"],
        norm_mix_pre=g["n1"][0], norm_mix_post=g["n2"][0], norm_ffn_pre=g["n3"][0], norm_ffn_post=g["n4"][0])


def rope_tables(t):
    half = RET_DK // 2
    inv = ROPE_BASE ** (-jnp.arange(half, dtype=F32) / half)
    ang = jnp.arange(t).astype(F32)[:, None] * inv[None, :]
    return jnp.cos(ang), jnp.sin(ang)


def local_step(x, target, meta, layers):
    h = jnp.concatenate([jnp.zeros((N_PAD, D), F32), meta, x], axis=0)
    t = h.shape[0]
    cos, sin = rope_tables(t)
    ws = [prep_layer(p) for p in layers]
    saved = []
    for l, w in enumerate(ws):
        h, s = layer_fwd(h, w, cos, sin, l)
        saved.append(s)
    loss, dh = loss_head(h, target, name="loss_head")
    grads = [None] * len(ws)
    for l in reversed(range(len(ws))):
        dh, g = layer_bwd(dh, saved[l], ws[l], cos, sin, l)
        grads[l] = layer_grads_to_params(g)
    return loss, dh[CH:], dh[N_PAD:CH], grads


def _my_place():
    return lax.axis_index("x"), lax.axis_index("y"), lax.axis_index("c")


def _flat(px, py, pc):
    return 4 * px + 2 * py + pc


ANY = pl.BlockSpec(memory_space=pl.ANY)


def all_gather(x_shard, *, name):
    shape = x_shard.shape

    def body(x_ref, out_ref, send_sems, recv_sems, local_sem):
        x, y, c = _my_place()
        me, sibling = (x, y, c), (x, y, 1 - c)
        chips = [(1 - x, y), (x, 1 - y), (1 - x, 1 - y)]

        def rows(px, py, pc):
            return out_ref.at[_flat(px, py, pc)]

        def copy(k, block, to, src=None):
            return pltpu.make_async_remote_copy(
                src_ref=rows(*block) if src is None else src, dst_ref=rows(*block),
                send_sem=send_sems.at[k], recv_sem=recv_sems.at[k], device_id=to, device_id_type=MESH_ID)

        mine = pltpu.make_async_copy(x_ref, rows(*me), local_sem)
        mine.start()
        first = [copy(0, me, sibling, src=x_ref)]
        first += [copy(1 + j, me, (*chip, c), src=x_ref) for j, chip in enumerate(chips)]
        for cp in first:
            cp.start()
        passed = [copy(4 + j, (*chip, c), sibling) for j, chip in enumerate(chips)]
        for j, chip in enumerate(chips):
            copy(1 + j, (*chip, c), me).wait_recv()
            passed[j].start()
        copy(0, sibling, me).wait_recv()
        for j, chip in enumerate(chips):
            copy(4 + j, (*chip, 1 - c), me).wait_recv()
        for cp in first + passed:
            cp.wait_send()
        mine.wait()

    return pl.pallas_call(
        body, name=name, out_shape=jax.ShapeDtypeStruct((N_DEV,) + shape, x_shard.dtype),
        in_specs=[ANY], out_specs=ANY,
        scratch_shapes=[pltpu.SemaphoreType.DMA((7,)), pltpu.SemaphoreType.DMA((7,)), pltpu.SemaphoreType.DMA],
    )(x_shard)


N_CHIP = 4


def pair_exchange(g, *, name):
    shape = (N_CHIP,) + g.shape[1:]

    def body(g_ref, recv_ref, send_sems, recv_sems):
        x, y, c = _my_place()
        sibling = (x, y, 1 - c)
        sends = [pltpu.make_async_remote_copy(
            src_ref=g_ref.at[2 * j + 1 - c], dst_ref=recv_ref.at[j], send_sem=send_sems.at[j], recv_sem=recv_sems.at[j],
            device_id=sibling, device_id_type=MESH_ID) for j in range(N_CHIP)]
        for cp in sends:
            cp.start()
        for cp in sends:
            cp.wait_recv()
        for cp in sends:
            cp.wait_send()

    return pl.pallas_call(
        body, name=name, out_shape=jax.ShapeDtypeStruct(shape, g.dtype), in_specs=[ANY], out_specs=ANY,
        scratch_shapes=[pltpu.SemaphoreType.DMA((N_CHIP,)), pltpu.SemaphoreType.DMA((N_CHIP,))],
    )(g)


def pair_sum(g, from_sibling, core, *, name):
    n, r, cols = from_sibling.shape
    tb = _pick(r, (512, 256, 128))

    def body(core_ref, a_ref, b_ref, o_ref):
        o_ref[...] = (a_ref[...].astype(F32) + b_ref[...].astype(F32)).astype(o_ref.dtype)

    blk = pl.BlockSpec((1, tb, cols), lambda j, i, core_ref: (j, i, 0))
    grid_spec = pltpu.PrefetchScalarGridSpec(
        num_scalar_prefetch=1, grid=(n, r // tb),
        in_specs=[pl.BlockSpec((1, tb, cols), lambda j, i, core_ref: (2 * j + core_ref[0], i, 0)), blk], out_specs=blk)
    return pl.pallas_call(body, name=name, out_shape=jax.ShapeDtypeStruct(from_sibling.shape, from_sibling.dtype),
                          grid_spec=grid_spec, compiler_params=_cparams(2))(core, g, from_sibling)


def chip_exchange(s, *, name):
    def body(s_ref, out_ref, send_sems, recv_sems, local_sem):
        x, y, c = _my_place()
        me = 2 * x + y
        mine = pltpu.make_async_copy(s_ref.at[me], out_ref.at[me], local_sem)
        mine.start()
        peers = [(jnp.bitwise_xor(x, k >> 1), jnp.bitwise_xor(y, k & 1)) for k in range(1, N_CHIP)]
        sends = []
        for k, (px, py) in enumerate(peers):
            cp = pltpu.make_async_remote_copy(
                src_ref=s_ref.at[2 * px + py], dst_ref=out_ref.at[me], send_sem=send_sems.at[k], recv_sem=recv_sems.at[k],
                device_id=(px, py, c), device_id_type=MESH_ID)
            cp.start()
            sends.append(cp)
        for k, (px, py) in enumerate(peers):
            slot = out_ref.at[2 * px + py]
            pltpu.make_async_remote_copy(
                src_ref=slot, dst_ref=slot, send_sem=send_sems.at[k], recv_sem=recv_sems.at[k],
                device_id=(px, py, c), device_id_type=MESH_ID).wait_recv()
        for cp in sends:
            cp.wait_send()
        mine.wait()

    return pl.pallas_call(
        body, name=name, out_shape=jax.ShapeDtypeStruct(s.shape, s.dtype), in_specs=[ANY], out_specs=ANY,
        scratch_shapes=[pltpu.SemaphoreType.DMA((N_CHIP - 1,)), pltpu.SemaphoreType.DMA((N_CHIP - 1,)), pltpu.SemaphoreType.DMA],
    )(s)


def sum_slots(a, *, name):
    def body(a_ref, o_ref):
        s = a_ref[0]
        for d in range(1, N_DEV):
            s = s + a_ref[d]
        o_ref[...] = s

    return pl.pallas_call(body, name=name, out_shape=jax.ShapeDtypeStruct(a.shape[1:], a.dtype))(a)


def _adamw_math(w, g, m, v):
    m = ADAM_B1 * m + (1.0 - ADAM_B1) * g
    v = ADAM_B2 * v + (1.0 - ADAM_B2) * (g * g)
    m_hat = m / (1.0 - ADAM_B1 ** ADAM_STEP)
    v_hat = v / (1.0 - ADAM_B2 ** ADAM_STEP)
    delta = -ADAM_LR * (m_hat / (jnp.sqrt(v_hat) + ADAM_EPS) + ADAM_WD * w)
    return delta, m, v


def adamw_big(recv, w, m, v, *, name):
    r, cols = w.shape
    tb = 128
    n = recv.shape[0]

    def body(r_ref, w_ref, m_ref, v_ref, g_ref, d_ref, nm_ref, nv_ref):
        g = r_ref[0].astype(F32)
        for d in range(1, n):
            g = g + r_ref[d].astype(F32)
        g_ref[...] = g
        d_ref[...], nm_ref[...], nv_ref[...] = _adamw_math(w_ref[...], g, m_ref[...], v_ref[...])

    blk = pl.BlockSpec((tb, cols), lambda i: (i, 0))
    out = jax.ShapeDtypeStruct((r, cols), F32)
    return pl.pallas_call(
        body, name=name, out_shape=[out] * 4, grid=(r // tb,),
        in_specs=[pl.BlockSpec((n, tb, cols), lambda i: (0, i, 0)), blk, blk, blk], out_specs=[blk] * 4,
        compiler_params=_cparams(1))(recv, w, m, v)


def adamw_small(w, g, m, v, *, name):
    def body(w_ref, g_ref, m_ref, v_ref, d_ref, nm_ref, nv_ref):
        d_ref[...], nm_ref[...], nv_ref[...] = _adamw_math(w_ref[...], g_ref[...], m_ref[...], v_ref[...])

    out = jax.ShapeDtypeStruct(w.shape, F32)
    return pl.pallas_call(body, name=name, out_shape=[out] * 3)(w, g, m, v)


BIG = ("w_in", "w_branch", "w_out", "w_ffn_in", "w_ffn_out")
BIG_SHARD = {"w_in": (DEPTH, D, 2178), "w_branch": (DEPTH, 4, 128, D), "w_out": (DEPTH, 128, D),
             "w_ffn_in": (DEPTH, D, 704), "w_ffn_out": (DEPTH, 352, D)}
BIG_FULL = {"w_in": ((1, 2, 0, 3), (DEPTH, D, 17424)), "w_branch": ((1, 2, 0, 3, 4), (DEPTH, 4, D, D)),
            "w_out": ((1, 0, 2, 3), (DEPTH, D, D)), "w_ffn_in": ((1, 2, 0, 3), (DEPTH, D, 2 * D_FF)),
            "w_ffn_out": ((1, 0, 2, 3), (DEPTH, D_FF, D))}
BIG_ROWS = {n: int(np.prod(s)) // D for n, s in BIG_SHARD.items()}
BIG_R = 7808


def pack_big(shards, dtype):
    parts = [shards[n].astype(dtype).reshape(BIG_ROWS[n], D) for n in BIG]
    parts.append(jnp.zeros((BIG_R - sum(BIG_ROWS.values()), D), dtype))
    return jnp.concatenate(parts, axis=0)


def unpack_big(flat):
    out, o = {}, 0
    for n in BIG:
        out[n] = flat[o:o + BIG_ROWS[n]].reshape(BIG_SHARD[n])
        o += BIG_ROWS[n]
    return out


def unpack_big_full(gathered):
    out, o = {}, 0
    for n in BIG:
        perm, full = BIG_FULL[n]
        out[n] = gathered[:, o:o + BIG_ROWS[n]].reshape((N_DEV,) + BIG_SHARD[n]).transpose(perm).reshape(full)
        o += BIG_ROWS[n]
    return out


def pack_big_full(full, dtype):
    parts = []
    for n in BIG:
        perm, _ = BIG_FULL[n]
        split = tuple(int(v) for v in np.array((N_DEV,) + BIG_SHARD[n])[list(perm)])
        inv = tuple(int(i) for i in np.argsort(perm))
        parts.append(full[n].astype(dtype).reshape(split).transpose(inv).reshape(N_DEV, BIG_ROWS[n], D))
    parts.append(jnp.zeros((N_DEV, BIG_R - sum(BIG_ROWS.values()), D), dtype))
    return jnp.concatenate(parts, axis=1)


def _rows128(a):
    a = a.reshape(-1)
    pad = (-a.shape[0]) % CH
    if pad:
        a = jnp.concatenate([a, jnp.zeros((pad,), a.dtype)])
    return a.reshape(-1, CH)


def _pack_rows(arrs, total):
    parts = [_rows128(a) for a in arrs]
    n = sum(p.shape[0] for p in parts)
    parts.append(jnp.zeros((total - n, CH), F32))
    return jnp.concatenate(parts, axis=0)


def _unpack_rows(flat, shapes):
    out, o = [], 0
    for s in shapes:
        size = int(np.prod(s))
        rows = -(-size // CH)
        out.append(flat[o:o + rows].reshape(-1)[:size].reshape(s))
        o += rows
    return out


SMALL_SHARDED = ("meta", "conv_a", "ssd_conv_w")
SMALL_SHARD_SHAPE = {"meta": (N_META, 128), "conv_a": (DEPTH, 3, 128), "ssd_conv_w": (DEPTH, 4, 256)}
SMALL_FULL_SHAPE = {"meta": (N_META, D), "conv_a": (DEPTH, 3, D), "ssd_conv_w": (DEPTH, 4, 2048)}
SMALL_REPL = ("ssd_conv_b", "ssd_dt_bias", "ssd_a_log", "ssd_d", "ssd_norm", "norm_mix_pre", "norm_mix_post", "norm_ffn_pre",
              "norm_ffn_post")
SMALL_REPL_SHAPE = {"ssd_conv_b": (DEPTH, 2048), "ssd_dt_bias": (DEPTH, SSD_HEADS), "ssd_a_log": (DEPTH, SSD_HEADS),
                    "ssd_d": (DEPTH, SSD_HEADS), "ssd_norm": (DEPTH, D), "norm_mix_pre": (DEPTH, D), "norm_mix_post": (DEPTH, D),
                    "norm_ffn_pre": (DEPTH, D), "norm_ffn_post": (DEPTH, D)}


def _gather_small_full(gathered, n):
    nd = gathered.ndim
    perm = tuple(range(1, nd - 1)) + (0, nd - 1)
    return gathered.transpose(perm).reshape(SMALL_FULL_SHAPE[n])


WEIGHTS = ("meta", "w_in", "conv_a", "ssd_conv_w", "ssd_conv_b", "ssd_dt_bias", "ssd_a_log", "ssd_d", "ssd_norm", "w_branch", "w_out",
           "w_ffn_in", "w_ffn_out", "norm_mix_pre", "norm_mix_post", "norm_ffn_pre", "norm_ffn_post")


def kernel(x, meta, w_in, conv_a, ssd_conv_w, ssd_conv_b, ssd_dt_bias, ssd_a_log, ssd_d, ssd_norm, w_branch, w_out, w_ffn_in, w_ffn_out, norm_mix_pre, norm_mix_post, norm_ffn_pre, norm_ffn_post, loss_target, m_meta, m_w_in, m_conv_a, m_ssd_conv_w, m_ssd_conv_b, m_ssd_dt_bias, m_ssd_a_log, m_ssd_d, m_ssd_norm, m_w_branch, m_w_out, m_w_ffn_in, m_w_ffn_out, m_norm_mix_pre, m_norm_mix_post, m_norm_ffn_pre, m_norm_ffn_post, v_meta, v_w_in, v_conv_a, v_ssd_conv_w, v_ssd_conv_b, v_ssd_dt_bias, v_ssd_a_log, v_ssd_d, v_ssd_norm, v_w_branch, v_w_out, v_w_ffn_in, v_w_ffn_out, v_norm_mix_pre, v_norm_mix_post, v_norm_ffn_pre, v_norm_ffn_post):
    w = dict(meta=meta, w_in=w_in, conv_a=conv_a, ssd_conv_w=ssd_conv_w, ssd_conv_b=ssd_conv_b, ssd_dt_bias=ssd_dt_bias,
             ssd_a_log=ssd_a_log, ssd_d=ssd_d, ssd_norm=ssd_norm, w_branch=w_branch, w_out=w_out, w_ffn_in=w_ffn_in,
             w_ffn_out=w_ffn_out, norm_mix_pre=norm_mix_pre, norm_mix_post=norm_mix_post, norm_ffn_pre=norm_ffn_pre,
             norm_ffn_post=norm_ffn_post)
    m = dict(meta=m_meta, w_in=m_w_in, conv_a=m_conv_a, ssd_conv_w=m_ssd_conv_w, ssd_conv_b=m_ssd_conv_b, ssd_dt_bias=m_ssd_dt_bias,
             ssd_a_log=m_ssd_a_log, ssd_d=m_ssd_d, ssd_norm=m_ssd_norm, w_branch=m_w_branch, w_out=m_w_out, w_ffn_in=m_w_ffn_in,
             w_ffn_out=m_w_ffn_out, norm_mix_pre=m_norm_mix_pre, norm_mix_post=m_norm_mix_post, norm_ffn_pre=m_norm_ffn_pre,
             norm_ffn_post=m_norm_ffn_post)
    v = dict(meta=v_meta, w_in=v_w_in, conv_a=v_conv_a, ssd_conv_w=v_ssd_conv_w, ssd_conv_b=v_ssd_conv_b, ssd_dt_bias=v_ssd_dt_bias,
             ssd_a_log=v_ssd_a_log, ssd_d=v_ssd_d, ssd_norm=v_ssd_norm, w_branch=v_w_branch, w_out=v_w_out, w_ffn_in=v_w_ffn_in,
             w_ffn_out=v_w_ffn_out, norm_mix_pre=v_norm_mix_pre, norm_mix_post=v_norm_mix_post, norm_ffn_pre=v_norm_ffn_pre,
             norm_ffn_post=v_norm_ffn_post)
    xi, yi, ci = _my_place()
    dev = _flat(xi, yi, ci)

    full = unpack_big_full(all_gather(pack_big(w, BF16), name="gather_big"))
    small_shard = _pack_rows([w[n] for n in SMALL_SHARDED], 40)
    small_all = all_gather(small_shard, name="gather_small")
    small_full = {}
    o = 0
    for n in SMALL_SHARDED:
        rows = int(np.prod(SMALL_SHARD_SHAPE[n])) // CH
        small_full[n] = _gather_small_full(small_all[:, o:o + rows].reshape((N_DEV,) + SMALL_SHARD_SHAPE[n]), n)
        o += rows

    layers = []
    for l in range(DEPTH):
        p = {n: full[n][l] for n in BIG}
        p["conv_a"] = small_full["conv_a"][l]
        p["ssd_conv_w"] = small_full["ssd_conv_w"][l]
        for n in SMALL_REPL:
            p[n] = w[n][l]
        layers.append(p)

    loss_blk, grad_x, gmeta, grads = local_step(x[0], loss_target[0], small_full["meta"], layers)

    gfull = {n: jnp.stack([grads[l][n] for l in range(DEPTH)]) for n in BIG}
    partials = pack_big_full(gfull, BF16)
    from_sibling = pair_exchange(partials, name="exchange_pair")
    core = jnp.reshape(ci, (1,)).astype(jnp.int32)
    recv = chip_exchange(pair_sum(partials, from_sibling, core, name="sum_pair"), name="exchange_chip")
    g_flat, d_flat, nm_flat, nv_flat = adamw_big(recv, pack_big(w, F32), pack_big(m, F32), pack_big(v, F32), name="adamw_big")
    out_g, out_d, out_m, out_v = unpack_big(g_flat), unpack_big(d_flat), unpack_big(nm_flat), unpack_big(nv_flat)

    small_names = SMALL_SHARDED + SMALL_REPL
    small_grads = [gmeta] + [jnp.stack([grads[l][n] for l in range(DEPTH)]) for n in small_names[1:]]
    small_shapes = [SMALL_FULL_SHAPE[n] for n in SMALL_SHARDED] + [SMALL_REPL_SHAPE[n] for n in SMALL_REPL]
    sm = _pack_rows(small_grads + [loss_blk[0:1]], 424)
    sm_sum = sum_slots(all_gather(sm, name="gather_small_grads"), name="sum_small_grads")
    summed = _unpack_rows(sm_sum, small_shapes + [(1, CH)])
    loss = summed[-1][0, 0]
    sg = dict(zip(small_names, summed[:-1]))
    for n in SMALL_SHARDED:
        width = SMALL_SHARD_SHAPE[n][-1]
        sg[n] = lax.dynamic_slice_in_dim(sg[n], dev * width, width, axis=sg[n].ndim - 1)
    pk = lambda d: _pack_rows([d[n] for n in small_names], 160)
    sd, snm, snv = adamw_small(pk(w), pk(sg), pk(m), pk(v), name="adamw_small")
    shard_shapes = [SMALL_SHARD_SHAPE[n] for n in SMALL_SHARDED] + [SMALL_REPL_SHAPE[n] for n in SMALL_REPL]
    for dst, flat in ((out_d, sd), (out_m, snm), (out_v, snv)):
        dst.update(zip(small_names, _unpack_rows(flat, shard_shapes)))
    out_g.update(sg)

    return (loss, grad_x[None], *[out_g[n] for n in WEIGHTS], *[out_d[n] for n in WEIGHTS], *[out_m[n] for n in WEIGHTS],
            *[out_v[n] for n in WEIGHTS])
```

```python
import functools
import math

import numpy as np
import jax
import jax.numpy as jnp
from jax import lax
from jax.experimental import pallas as pl
from jax.experimental.pallas import tpu as pltpu

F32, BF16 = jnp.float32, jnp.bfloat16
HI = lax.Precision.HIGHEST
MESH_ID = pl.DeviceIdType.MESH

D = 1024
CH = 128
N_META = 16
N_PAD = CH - N_META
EPS = 1e-6
N_DEV = 8
DEPTH = 2
SSD_HEADS = 16
RET_HEADS = 4
SB_HEADS = 8
D_FF = 2816
ROPE_BASE = 10000.0

NF = 14336
COL_GATE = 10

ADAM_LR, ADAM_B1, ADAM_B2, ADAM_EPS, ADAM_WD, ADAM_STEP = 0.001, 0.9, 0.999, 1e-08, 0.01, 10

VMEM_BYTES = 48 * 1024 * 1024


def _pick(n, cands):
    for c in cands:
        if n % c == 0:
            return c
    raise ValueError((n, cands))


def _tok_block(t):
    return _pick(t, (384, 128))


def _cparams(ngrid, vmem=VMEM_BYTES):
    return pltpu.CompilerParams(dimension_semantics=("arbitrary",) * ngrid, vmem_limit_bytes=vmem)


def _iota(shape, dim):
    return lax.broadcasted_iota(jnp.int32, shape, dim)


def _sigmoid(x):
    return 1.0 / (1.0 + jnp.exp(-x))


def _silu(x):
    return x * _sigmoid(x)


def _dsilu(x):
    s = _sigmoid(x)
    return s * (1.0 + x * (1.0 - s))


def _softplus(x):
    return jnp.maximum(x, 0.0) + jnp.log(1.0 + jnp.exp(-jnp.abs(x)))


def _dot(a, b):
    return jnp.dot(a.astype(BF16), b.astype(BF16), preferred_element_type=F32)


def _dot_nt(a, b):
    return lax.dot_general(a.astype(BF16), b.astype(BF16), (((1,), (1,)), ((), ())), preferred_element_type=F32)


def _dot_tn(a, b):
    return lax.dot_general(a.astype(BF16), b.astype(BF16), (((0,), (0,)), ((), ())), preferred_element_type=F32)


def _dot_hi(a, b):
    return jnp.dot(a, b, precision=HI, preferred_element_type=F32)


def mm(a, b, *, name, out_dtype=F32, add=None, tm=None, tn=None, tk=None):
    m, k = a.shape
    k2, n = b.shape
    assert k == k2
    tm = tm or _pick(m, (1376, 384, 128))
    tn = tn or _pick(n, (512, 384, 256, 128))
    tk = tk or _pick(k, (1024, 1408, 512, 384, 128))
    nk = k // tk
    has_add = add is not None

    def body(*refs):
        if has_add:
            a_ref, b_ref, c_ref, o_ref = refs[:4]
            scr = refs[4:]
        else:
            a_ref, b_ref, o_ref = refs[:3]
            c_ref = None
            scr = refs[3:]
        x = _dot(a_ref[...], b_ref[...])
        if nk == 1:
            if has_add:
                x = x + c_ref[...]
            o_ref[...] = x.astype(out_dtype)
        else:
            acc = scr[0]
            kk = pl.program_id(2)

            @pl.when(kk == 0)
            def _():
                acc[...] = x

            @pl.when(kk > 0)
            def _():
                acc[...] += x

            @pl.when(kk == nk - 1)
            def _():
                r = acc[...]
                if has_add:
                    r = r + c_ref[...]
                o_ref[...] = r.astype(out_dtype)

    in_specs = [pl.BlockSpec((tm, tk), lambda i, j, kk: (i, kk)), pl.BlockSpec((tk, tn), lambda i, j, kk: (kk, j))]
    args = [a, b]
    if has_add:
        in_specs.append(pl.BlockSpec((tm, tn), lambda i, j, kk: (i, j)))
        args.append(add)
    return pl.pallas_call(
        body, name=name, out_shape=jax.ShapeDtypeStruct((m, n), out_dtype), grid=(m // tm, n // tn, nk),
        in_specs=in_specs, out_specs=pl.BlockSpec((tm, tn), lambda i, j, kk: (i, j)),
        scratch_shapes=[pltpu.VMEM((tm, tn), F32)] if nk > 1 else [],
        compiler_params=_cparams(3))(*args)


def mm_segments(segs, w, add, *, name):
    m = segs[0].shape[0]
    n = w.shape[1]
    tm = _pick(m, (384, 128))
    tk = 1024
    counts = [s.shape[1] // tk for s in segs]
    starts = [sum(counts[:i]) for i in range(len(segs))]
    nk = sum(counts)
    assert all(s.shape[1] % tk == 0 for s in segs) and w.shape[0] == nk * tk

    def body(*refs):
        a_refs = refs[:len(segs)]
        w_ref, c_ref, o_ref, acc = refs[len(segs):]
        kk = pl.program_id(1)

        @pl.when(kk == 0)
        def _():
            acc[...] = c_ref[...]

        for s, a_ref in enumerate(a_refs):
            @pl.when((kk >= starts[s]) & (kk < starts[s] + counts[s]))
            def _(a_ref=a_ref):
                acc[...] += _dot(a_ref[...], w_ref[...])

        @pl.when(kk == nk - 1)
        def _():
            o_ref[...] = acc[...]

    def a_spec(s):
        return pl.BlockSpec((tm, tk), lambda i, kk: (i, jnp.clip(kk - starts[s], 0, counts[s] - 1)))

    full = pl.BlockSpec((tm, n), lambda i, kk: (i, 0))
    return pl.pallas_call(
        body, name=name, out_shape=jax.ShapeDtypeStruct((m, n), F32), grid=(m // tm, nk),
        in_specs=[a_spec(s) for s in range(len(segs))] + [pl.BlockSpec((tk, n), lambda i, kk: (kk, 0)), full],
        out_specs=full, scratch_shapes=[pltpu.VMEM((tm, n), F32)],
        compiler_params=_cparams(2, 60 * 1024 * 1024))(*segs, w, add)


def mm_tn(a, b, *, name, tm=None, tn=None, tk=None):
    t, m = a.shape
    t2, n = b.shape
    assert t == t2
    tm = tm or _pick(m, (1024, 1408, 512, 128))
    tn = tn or _pick(n, (512, 384, 256, 128))
    tk = tk or _pick(t, (1376, 384, 128))
    nk = t // tk

    def body(a_ref, b_ref, o_ref):
        x = _dot_tn(a_ref[...], b_ref[...])
        kk = pl.program_id(2)

        @pl.when(kk == 0)
        def _():
            o_ref[...] = x

        @pl.when(kk > 0)
        def _():
            o_ref[...] += x

    return pl.pallas_call(
        body, name=name, out_shape=jax.ShapeDtypeStruct((m, n), F32), grid=(m // tm, n // tn, nk),
        in_specs=[pl.BlockSpec((tk, tm), lambda i, j, kk: (kk, i)), pl.BlockSpec((tk, tn), lambda i, j, kk: (kk, j))],
        out_specs=pl.BlockSpec((tm, tn), lambda i, j, kk: (i, j)),
        compiler_params=_cparams(3))(a, b)


def rms_fwd(x, w, *, name, out_dtype=F32, res=None):
    t, d = x.shape
    tb = _tok_block(t)
    has_res = res is not None

    def body(*refs):
        if has_res:
            x_ref, w_ref, r_ref, o_ref = refs
        else:
            x_ref, w_ref, o_ref = refs
        xv = x_ref[...]
        y = xv * lax.rsqrt(jnp.mean(xv * xv, axis=-1, keepdims=True) + EPS) * w_ref[...]
        if has_res:
            y = y + r_ref[...]
        o_ref[...] = y.astype(out_dtype)

    blk = pl.BlockSpec((tb, d), lambda i: (i, 0))
    wspec = pl.BlockSpec((1, d), lambda i: (0, 0))
    in_specs = [blk, wspec] + ([blk] if has_res else [])
    args = [x, w] + ([res] if has_res else [])
    return pl.pallas_call(body, name=name, out_shape=jax.ShapeDtypeStruct((t, d), out_dtype), grid=(t // tb,),
                          in_specs=in_specs, out_specs=blk, compiler_params=_cparams(1))(*args)


def rms_bwd(x, w, dy, *, name, add=None):
    t, d = x.shape
    tb = _tok_block(t)
    has_add = add is not None

    def body(*refs):
        if has_add:
            x_ref, w_ref, dy_ref, a_ref, dx_ref, dw_ref = refs
        else:
            x_ref, w_ref, dy_ref, dx_ref, dw_ref = refs
        xv = x_ref[...]
        dyv = dy_ref[...]
        r = lax.rsqrt(jnp.mean(xv * xv, axis=-1, keepdims=True) + EPS)
        g = dyv * w_ref[...]
        dx = r * g - xv * (r * r * r) * jnp.mean(xv * g, axis=-1, keepdims=True)
        if has_add:
            dx = dx + a_ref[...]
        dx_ref[...] = dx
        part = jnp.sum(dyv * xv * r, axis=0, keepdims=True)

        @pl.when(pl.program_id(0) == 0)
        def _():
            dw_ref[...] = part

        @pl.when(pl.program_id(0) > 0)
        def _():
            dw_ref[...] += part

    blk = pl.BlockSpec((tb, d), lambda i: (i, 0))
    wspec = pl.BlockSpec((1, d), lambda i: (0, 0))
    in_specs = [blk, wspec, blk] + ([blk] if has_add else [])
    args = [x, w, dy] + ([add] if has_add else [])
    return pl.pallas_call(body, name=name,
                          out_shape=[jax.ShapeDtypeStruct((t, d), F32), jax.ShapeDtypeStruct((1, d), F32)],
                          grid=(t // tb,), in_specs=in_specs, out_specs=[blk, wspec], compiler_params=_cparams(1))(*args)


def _shift_down(cur, prev8, k):
    z = jnp.concatenate([prev8, cur], axis=0)
    return pltpu.roll(z, k, 0)[8:]


def _shift_up(cur, next8, k):
    n = cur.shape[0] + 8
    z = jnp.concatenate([cur, next8], axis=0)
    return pltpu.roll(z, n - k, 0)[:cur.shape[0]]


def _prev8_spec(tb, width, col):
    return pl.BlockSpec((8, width), lambda i: (jnp.maximum(i * (tb // 8) - 1, 0), col))


def _next8_spec(tb, width, col, t):
    return pl.BlockSpec((8, width), lambda i: (jnp.minimum((i + 1) * (tb // 8), t // 8 - 1), col))


def _row_valid(i, tb, n, offset=0):
    rows = i * tb + offset + _iota((n, 1), 0)
    return (rows >= N_PAD).astype(F32)


def conv_a_fwd(proj, w8, *, name):
    t = proj.shape[0]
    tb = _tok_block(t)

    def body(b_ref, c_ref, x_ref, cp_ref, xp_ref, w_ref, o_ref):
        i = pl.program_id(0)
        u = c_ref[...] * x_ref[...] * _row_valid(i, tb, tb)
        up = cp_ref[...] * xp_ref[...] * _row_valid(i, tb, 8, -8) * (i > 0).astype(F32)
        w = w_ref[...]
        conv = w[2:3] * u + w[1:2] * _shift_down(u, up, 1) + w[0:1] * _shift_down(u, up, 2)
        o_ref[...] = b_ref[...] * conv

    blk = lambda col: pl.BlockSpec((tb, D), lambda i: (i, col))
    return pl.pallas_call(
        body, name=name, out_shape=jax.ShapeDtypeStruct((t, D), F32), grid=(t // tb,),
        in_specs=[blk(0), blk(1), blk(2), _prev8_spec(tb, D, 1), _prev8_spec(tb, D, 2), pl.BlockSpec((8, D), lambda i: (0, 0))],
        out_specs=pl.BlockSpec((tb, D), lambda i: (i, 0)), compiler_params=_cparams(1))(proj, proj, proj, proj, proj, w8)


def conv_a_bwd(proj, w8, dy, *, name):
    t = proj.shape[0]
    tb = _tok_block(t)
    nblk = t // tb

    def body(b_ref, c_ref, x_ref, cp_ref, xp_ref, dy_ref, dyn_ref, bn_ref, w_ref, o_ref, dw_ref):
        i = pl.program_id(0)
        vm = _row_valid(i, tb, tb)
        cv, xv, bv, dyv = c_ref[...], x_ref[...], b_ref[...], dy_ref[...]
        u = cv * xv * vm
        up = cp_ref[...] * xp_ref[...] * _row_valid(i, tb, 8, -8) * (i > 0).astype(F32)
        w = w_ref[...]
        u1 = _shift_down(u, up, 1)
        u2 = _shift_down(u, up, 2)
        conv = w[2:3] * u + w[1:2] * u1 + w[0:1] * u2
        dconv = dyv * bv
        dconv_n = dyn_ref[...] * bn_ref[...] * (i < nblk - 1).astype(F32)
        du = w[2:3] * dconv + w[1:2] * _shift_up(dconv, dconv_n, 1) + w[0:1] * _shift_up(dconv, dconv_n, 2)
        o_ref[:, 0:D] = dyv * conv
        o_ref[:, D:2 * D] = du * xv * vm
        o_ref[:, 2 * D:3 * D] = du * cv * vm

        @pl.when(i == 0)
        def _():
            dw_ref[...] = jnp.zeros_like(dw_ref)

        dw_ref[0:1, :] += jnp.sum(dconv * u2, axis=0, keepdims=True)
        dw_ref[1:2, :] += jnp.sum(dconv * u1, axis=0, keepdims=True)
        dw_ref[2:3, :] += jnp.sum(dconv * u, axis=0, keepdims=True)

    blk = lambda col: pl.BlockSpec((tb, D), lambda i: (i, col))
    w8spec = pl.BlockSpec((8, D), lambda i: (0, 0))
    return pl.pallas_call(
        body, name=name,
        out_shape=[jax.ShapeDtypeStruct((t, 3 * D), F32), jax.ShapeDtypeStruct((8, D), F32)], grid=(nblk,),
        in_specs=[blk(0), blk(1), blk(2), _prev8_spec(tb, D, 1), _prev8_spec(tb, D, 2), blk(0),
                  _next8_spec(tb, D, 0, t), _next8_spec(tb, D, 0, t), w8spec],
        out_specs=[pl.BlockSpec((tb, 3 * D), lambda i: (i, 0)), w8spec],
        compiler_params=_cparams(1))(proj, proj, proj, proj, proj, dy, dy, proj, w8)


XBC_W = 2048


def ssd_conv_fwd(proj, w8, b, *, name):
    t = proj.shape[0]
    tb = _tok_block(t)

    def body(x_ref, xp_ref, w_ref, b_ref, o_ref):
        i = pl.program_id(0)
        xm = x_ref[...] * _row_valid(i, tb, tb)
        xmp = xp_ref[...] * _row_valid(i, tb, 8, -8) * (i > 0).astype(F32)
        w = w_ref[...]
        c = w[3:4] * xm + w[2:3] * _shift_down(xm, xmp, 1) + w[1:2] * _shift_down(xm, xmp, 2) + w[0:1] * _shift_down(xm, xmp, 3)
        o_ref[...] = _silu(c + b_ref[...])

    return pl.pallas_call(
        body, name=name, out_shape=jax.ShapeDtypeStruct((t, XBC_W), F32), grid=(t // tb,),
        in_specs=[pl.BlockSpec((tb, XBC_W), lambda i: (i, 2)), _prev8_spec(tb, XBC_W, 2),
                  pl.BlockSpec((8, XBC_W), lambda i: (0, 0)), pl.BlockSpec((1, XBC_W), lambda i: (0, 0))],
        out_specs=pl.BlockSpec((tb, XBC_W), lambda i: (i, 0)), compiler_params=_cparams(1))(proj, proj, w8, b)


def ssd_conv_bwd_pre(proj, w8, b, dxa, *, name):
    t = proj.shape[0]
    tb = _tok_block(t)

    def body(x_ref, xp_ref, w_ref, b_ref, d_ref, o_ref, dw_ref, db_ref):
        i = pl.program_id(0)
        xm = x_ref[...] * _row_valid(i, tb, tb)
        xmp = xp_ref[...] * _row_valid(i, tb, 8, -8) * (i > 0).astype(F32)
        w = w_ref[...]
        x1, x2, x3 = _shift_down(xm, xmp, 1), _shift_down(xm, xmp, 2), _shift_down(xm, xmp, 3)
        c = w[3:4] * xm + w[2:3] * x1 + w[1:2] * x2 + w[0:1] * x3 + b_ref[...]
        dpre = d_ref[...] * _dsilu(c)
        o_ref[...] = dpre

        @pl.when(i == 0)
        def _():
            dw_ref[...] = jnp.zeros_like(dw_ref)
            db_ref[...] = jnp.zeros_like(db_ref)

        dw_ref[0:1, :] += jnp.sum(dpre * x3, axis=0, keepdims=True)
        dw_ref[1:2, :] += jnp.sum(dpre * x2, axis=0, keepdims=True)
        dw_ref[2:3, :] += jnp.sum(dpre * x1, axis=0, keepdims=True)
        dw_ref[3:4, :] += jnp.sum(dpre * xm, axis=0, keepdims=True)
        db_ref[...] += jnp.sum(dpre, axis=0, keepdims=True)

    w8spec = pl.BlockSpec((8, XBC_W), lambda i: (0, 0))
    bspec = pl.BlockSpec((1, XBC_W), lambda i: (0, 0))
    return pl.pallas_call(
        body, name=name,
        out_shape=[jax.ShapeDtypeStruct((t, XBC_W), F32), jax.ShapeDtypeStruct((8, XBC_W), F32), jax.ShapeDtypeStruct((1, XBC_W), F32)],
        grid=(t // tb,),
        in_specs=[pl.BlockSpec((tb, XBC_W), lambda i: (i, 2)), _prev8_spec(tb, XBC_W, 2), w8spec, bspec,
                  pl.BlockSpec((tb, XBC_W), lambda i: (i, 0))],
        out_specs=[pl.BlockSpec((tb, XBC_W), lambda i: (i, 0)), w8spec, bspec],
        compiler_params=_cparams(1))(proj, proj, w8, b, dxa)


def ssd_conv_bwd_in(dpre, w8, *, name):
    t = dpre.shape[0]
    tb = _tok_block(t)
    nblk = t // tb

    def body(d_ref, dn_ref, w_ref, o_ref):
        i = pl.program_id(0)
        d = d_ref[...]
        dn = dn_ref[...] * (i < nblk - 1).astype(F32)
        w = w_ref[...]
        dx = w[3:4] * d + w[2:3] * _shift_up(d, dn, 1) + w[1:2] * _shift_up(d, dn, 2) + w[0:1] * _shift_up(d, dn, 3)
        o_ref[...] = dx * _row_valid(i, tb, tb)

    return pl.pallas_call(
        body, name=name, out_shape=jax.ShapeDtypeStruct((t, XBC_W), F32), grid=(nblk,),
        in_specs=[pl.BlockSpec((tb, XBC_W), lambda i: (i, 0)), _next8_spec(tb, XBC_W, 0, t), pl.BlockSpec((8, XBC_W), lambda i: (0, 0))],
        out_specs=pl.BlockSpec((tb, XBC_W), lambda i: (i, 0)), compiler_params=_cparams(1))(dpre, dpre, w8)


def _col(x, h):
    return jnp.sum(jnp.where(_iota(x.shape, 1) == h, x, 0.0), axis=1, keepdims=True)


def _row(x, h):
    return jnp.sum(jnp.where(_iota(x.shape, 0) == h, x, 0.0), axis=0, keepdims=True)


def _ssd_common(xa, dtr, dtb, alog, c):
    vm = _row_valid(c, CH, CH)
    xs = xa[:, :D] * vm
    dt = _softplus(dtr + dtb)
    a = -jnp.exp(alog) * dt
    tri = (_iota((CH, CH), 0) >= _iota((CH, CH), 1)).astype(F32)
    acs = _dot_hi(tri, a)
    return vm, xs, dt, a, acs, acs.T


def _pair_lanes(v0, v1):
    lane = _iota((1, CH), 1)
    return jnp.where(lane < 64, v0, v1)


def _ssd_pairs_fwd(xs, xa, dt, acs, acs_t, dsk, hins):
    causal = _iota((CH, CH), 0) >= _iota((CH, CH), 1)
    lane = _iota((CH, CH), 1)
    last = _iota((CH, 1), 0) == CH - 1
    bgs = [xa[:, D + CH * g:D + CH * (g + 1)] for g in range(4)]
    cgs = [xa[:, D + 512 + CH * g:D + 512 + CH * (g + 1)] for g in range(4)]
    g_mats = [_dot_nt(cgs[g], bgs[g]) for g in range(4)]
    ps = []
    for q in range(8):
        h0, h1 = 2 * q, 2 * q + 1
        xs_p = xs[:, CH * q:CH * (q + 1)]
        ac0, ac1 = _col(acs, h0), _col(acs, h1)
        ar0, ar1 = _row(acs_t, h0), _row(acs_t, h1)
        l0 = jnp.exp(jnp.where(causal, ac0 - ar0, -1e30))
        l1 = jnp.exp(jnp.where(causal, ac1 - ar1, -1e30))
        dt_p = _pair_lanes(_col(dt, h0), _col(dt, h1))
        ac_p = _pair_lanes(ac0, ac1)
        al0 = jnp.sum(jnp.where(last, ac0, 0.0), axis=0, keepdims=True)
        al1 = jnp.sum(jnp.where(last, ac1, 0.0), axis=0, keepdims=True)
        ps.append(dict(bg=bgs[q // 2], cg=cgs[q // 2], xs_p=xs_p, l0=l0, l1=l1, dt_p=dt_p, x=xs_p * dt_p, eac=jnp.exp(ac_p),
                       dsv=jnp.exp(_pair_lanes(al0, al1) - ac_p), al0=al0, al1=al1,
                       cd=jnp.where(_iota((CH, 1), 0) < 64, jnp.exp(al0), jnp.exp(al1)),
                       d_p=_pair_lanes(_col(dsk, h0), _col(dsk, h1))))
    for q, p in enumerate(ps):
        p["m0"], p["m1"] = g_mats[q // 2] * p["l0"], g_mats[q // 2] * p["l1"]
    for q, p in enumerate(ps):
        p["yd0"], p["yd1"] = _dot(p["m0"], p["x"]), _dot(p["m1"], p["x"])
        p["yoff_raw"] = _dot_nt(p["cg"], hins[q])
        p["s"] = _dot_tn(p["x"] * p["dsv"], p["bg"])
    for p in ps:
        p["y"] = jnp.where(lane < 64, p["yd0"], p["yd1"]) + p["yoff_raw"] * p["eac"] + p["xs_p"] * p["d_p"]
    return ps


def _ssd_gate_norm(y, z, nw):
    yv = y * _silu(z)
    outs, rs = [], []
    for g in range(4):
        yg = yv[:, 256 * g:256 * (g + 1)]
        r = lax.rsqrt(jnp.mean(yg * yg, axis=-1, keepdims=True) + EPS)
        outs.append(yg * r * nw[:, 256 * g:256 * (g + 1)])
        rs.append(r)
    return yv, jnp.concatenate(outs, axis=1), rs


def ssd_fwd(xa, proj, pdt, dtb, alog, dsk, nw, *, name):
    t = xa.shape[0]
    nc = t // CH

    def body(xa_ref, dtr_ref, z_ref, dtb_ref, alog_ref, dsk_ref, nw_ref, y_ref, hs_ref, h_scr):
        c = pl.program_id(0)

        @pl.when(c == 0)
        def _():
            h_scr[...] = jnp.zeros_like(h_scr)

        xa_v = xa_ref[...]
        vm, xs, dt, a, acs, acs_t = _ssd_common(xa_v, dtr_ref[...], dtb_ref[...], alog_ref[...], c)
        dsk_v = dsk_ref[...]
        hins = [h_scr[q] for q in range(8)]
        ps = _ssd_pairs_fwd(xs, xa_v, dt, acs, acs_t, dsk_v, hins)
        for q, p in enumerate(ps):
            hs_ref[0, q] = hins[q]
            h_scr[q] = hins[q] * p["cd"] + p["s"]
        y = jnp.concatenate([p["y"] for p in ps], axis=1)
        _, out, _ = _ssd_gate_norm(y, z_ref[...], nw_ref[...])
        y_ref[...] = out

    small = pl.BlockSpec((1, CH), lambda c: (0, 0))
    return pl.pallas_call(
        body, name=name,
        out_shape=[jax.ShapeDtypeStruct((t, D), F32), jax.ShapeDtypeStruct((nc, 8, CH, CH), F32)], grid=(nc,),
        in_specs=[pl.BlockSpec((CH, XBC_W), lambda c: (c, 0)), pl.BlockSpec((CH, CH), lambda c: (c, 0)),
                  pl.BlockSpec((CH, D), lambda c: (c, 3)), small, small, small, pl.BlockSpec((1, D), lambda c: (0, 0))],
        out_specs=[pl.BlockSpec((CH, D), lambda c: (c, 0)), pl.BlockSpec((1, 8, CH, CH), lambda c: (c, 0, 0, 0))],
        scratch_shapes=[pltpu.VMEM((8, CH, CH), F32)], compiler_params=_cparams(1))(xa, pdt, proj, dtb, alog, dsk, nw)


def ssd_bwd(xa, proj, pdt, hs, dyb, dtb, alog, dsk, nw, *, name):
    t = xa.shape[0]
    nc = t // CH

    def body(xa_ref, dtr_ref, z_ref, hs_ref, dy_ref, dtb_ref, alog_ref, dsk_ref, nw_ref,
             dz_ref, dxa_ref, ddt_ref, gdtb_ref, galog_ref, gdsk_ref, gnw_ref, dh_scr):
        step = pl.program_id(0)
        c = nc - 1 - step

        @pl.when(step == 0)
        def _():
            dh_scr[...] = jnp.zeros_like(dh_scr)
            gdtb_ref[...] = jnp.zeros_like(gdtb_ref)
            galog_ref[...] = jnp.zeros_like(galog_ref)
            gdsk_ref[...] = jnp.zeros_like(gdsk_ref)
            gnw_ref[...] = jnp.zeros_like(gnw_ref)

        xa_v = xa_ref[...]
        dtr = dtr_ref[...]
        dtb_v = dtb_ref[...]
        alog_v = alog_ref[...]
        vm, xs, dt, a, acs, acs_t = _ssd_common(xa_v, dtr, dtb_v, alog_v, c)
        dsk_v = dsk_ref[...]
        z = z_ref[...]
        nw_v = nw_ref[...]
        lane1 = _iota((1, CH), 1)
        sub1 = _iota((CH, 1), 0)
        lane = _iota((CH, CH), 1)

        hins = [hs_ref[0, q] for q in range(8)]
        pairs = _ssd_pairs_fwd(xs, xa_v, dt, acs, acs_t, dsk_v, hins)
        y_pre = jnp.concatenate([p["y"] for p in pairs], axis=1)

        dout = dy_ref[...]
        sz = _silu(z)
        yv = y_pre * sz
        dyv_parts = []
        gnw_parts = []
        for g in range(4):
            sl = slice(256 * g, 256 * (g + 1))
            yg = yv[:, sl]
            r = lax.rsqrt(jnp.mean(yg * yg, axis=-1, keepdims=True) + EPS)
            gy = dout[:, sl] * nw_v[:, sl]
            dyv_parts.append(r * gy - yg * (r * r * r) * jnp.mean(yg * gy, axis=-1, keepdims=True))
            gnw_parts.append(jnp.sum(dout[:, sl] * yg * r, axis=0, keepdims=True))
        dyv = jnp.concatenate(dyv_parts, axis=1)
        gnw_ref[...] += jnp.concatenate(gnw_parts, axis=1)
        dz_ref[...] = dyv * y_pre * _dsilu(z)
        dy_pre = dyv * sz

        dacs_c = jnp.zeros((CH, CH), F32)
        dacs_r = jnp.zeros((CH, CH), F32)
        ddt = jnp.zeros((CH, CH), F32)
        gdsk = jnp.zeros((1, CH), F32)
        dxs_parts = []
        db_g = [None] * 4
        dc_g = [None] * 4
        dg_g = [None] * 4

        def acc(lst, g, v):
            lst[g] = v if lst[g] is None else lst[g] + v

        m_lo = lane < 64
        dhouts = [dh_scr[q] for q in range(8)]
        mats = []
        for q in range(8):
            p = pairs[q]
            dy = dy_pre[:, CH * q:CH * (q + 1)]
            dye = dy * p["eac"]
            mats.append(dict(
                dy=dy, dye=dye,
                dm0=_dot_nt(jnp.where(m_lo, dy, 0.0), p["x"]), dm1=_dot_nt(jnp.where(m_lo, 0.0, dy), p["x"]),
                dx0=_dot_tn(p["m0"], dy), dx1=_dot_tn(p["m1"], dy),
                dc=_dot(dye, hins[q]), dhin=_dot_tn(dye, p["cg"]),
                w1=_dot_nt(p["bg"], dhouts[q]), db=_dot(p["x"] * p["dsv"], dhouts[q])))

        for q in range(8):
            p = pairs[q]
            mt = mats[q]
            g = q // 2
            h0, h1 = 2 * q, 2 * q + 1
            dy = mt["dy"]
            hin = hins[q]
            dhout = dhouts[q]
            x = p["x"]
            dxs = dy * p["d_p"]
            t_sk = dy * p["xs_p"]
            gdsk = gdsk + jnp.where(lane1 == h0, jnp.sum(jnp.where(m_lo, t_sk, 0.0)), 0.0) \
                        + jnp.where(lane1 == h1, jnp.sum(jnp.where(m_lo, 0.0, t_sk)), 0.0)
            dx = jnp.where(m_lo, mt["dx0"], mt["dx1"])
            for hh, lk, mm_, dm in ((h0, p["l0"], p["m0"], mt["dm0"]), (h1, p["l1"], p["m1"], mt["dm1"])):
                acc(dg_g, g, dm * lk)
                qm = dm * mm_
                dacs_c = dacs_c + jnp.where(lane1 == hh, jnp.sum(qm, axis=1, keepdims=True), 0.0)
                dacs_r = dacs_r - jnp.where(sub1 == hh, jnp.sum(qm, axis=0, keepdims=True), 0.0)
            acc(dc_g, g, mt["dc"])
            t_off = dy * p["yoff_raw"] * p["eac"]
            dacs_c = dacs_c + jnp.where(lane1 == h0, jnp.sum(jnp.where(m_lo, t_off, 0.0), axis=1, keepdims=True), 0.0) \
                            + jnp.where(lane1 == h1, jnp.sum(jnp.where(m_lo, 0.0, t_off), axis=1, keepdims=True), 0.0)
            dhin = mt["dhin"] + dhout * p["cd"]
            w1 = mt["w1"]
            dx = dx + p["dsv"] * w1
            t_ds = x * w1 * p["dsv"]
            dd0 = jnp.sum(jnp.where(m_lo, t_ds, 0.0), axis=1, keepdims=True)
            dd1 = jnp.sum(jnp.where(m_lo, 0.0, t_ds), axis=1, keepdims=True)
            acc(db_g, g, mt["db"])
            t_cd = dhout * hin
            sub_lo = _iota((CH, CH), 0) < 64
            dcd0 = jnp.sum(jnp.where(sub_lo, t_cd, 0.0)) * jnp.exp(p["al0"])
            dcd1 = jnp.sum(jnp.where(sub_lo, 0.0, t_cd)) * jnp.exp(p["al1"])
            last = (sub1 == CH - 1)
            dacs_c = dacs_c + jnp.where(lane1 == h0, jnp.where(last, jnp.sum(dd0) + dcd0, 0.0) - dd0, 0.0) \
                            + jnp.where(lane1 == h1, jnp.where(last, jnp.sum(dd1) + dcd1, 0.0) - dd1, 0.0)
            dh_scr[q] = dhin
            dxs = dxs + dx * p["dt_p"]
            t_dt = dx * p["xs_p"]
            ddt = ddt + jnp.where(lane1 == h0, jnp.sum(jnp.where(m_lo, t_dt, 0.0), axis=1, keepdims=True), 0.0) \
                      + jnp.where(lane1 == h1, jnp.sum(jnp.where(m_lo, 0.0, t_dt), axis=1, keepdims=True), 0.0)
            dxs_parts.append(dxs)

        for g in range(4):
            bg, cg = pairs[2 * g]["bg"], pairs[2 * g]["cg"]
            dc_g[g] = dc_g[g] + _dot(dg_g[g], bg)
            db_g[g] = db_g[g] + _dot_tn(dg_g[g], cg)

        dacs = dacs_c + dacs_r.T
        rtri = (_iota((CH, CH), 1) >= _iota((CH, CH), 0)).astype(F32)
        da = _dot_hi(rtri, dacs)
        ddt = ddt - da * jnp.exp(alog_v)
        galog_ref[...] += jnp.sum(da * a, axis=0, keepdims=True)
        dpre = ddt * _sigmoid(dtr + dtb_v) * (lane1 < SSD_HEADS).astype(F32)
        ddt_ref[...] = dpre
        gdtb_ref[...] += jnp.sum(dpre, axis=0, keepdims=True)
        gdsk_ref[...] += gdsk
        dxa_ref[:, 0:D] = jnp.concatenate(dxs_parts, axis=1) * vm
        dxa_ref[:, D:D + 512] = jnp.concatenate(db_g, axis=1)
        dxa_ref[:, D + 512:D + 1024] = jnp.concatenate(dc_g, axis=1)

    small = pl.BlockSpec((1, CH), lambda s: (0, 0))
    wide = pl.BlockSpec((1, D), lambda s: (0, 0))
    rev = lambda s: nc - 1 - s
    return pl.pallas_call(
        body, name=name,
        out_shape=[jax.ShapeDtypeStruct((t, D), F32), jax.ShapeDtypeStruct((t, XBC_W), F32), jax.ShapeDtypeStruct((t, CH), F32),
                   jax.ShapeDtypeStruct((1, CH), F32), jax.ShapeDtypeStruct((1, CH), F32), jax.ShapeDtypeStruct((1, CH), F32),
                   jax.ShapeDtypeStruct((1, D), F32)],
        grid=(nc,),
        in_specs=[pl.BlockSpec((CH, XBC_W), lambda s: (rev(s), 0)), pl.BlockSpec((CH, CH), lambda s: (rev(s), 0)),
                  pl.BlockSpec((CH, D), lambda s: (rev(s), 3)), pl.BlockSpec((1, 8, CH, CH), lambda s: (rev(s), 0, 0, 0)),
                  pl.BlockSpec((CH, D), lambda s: (rev(s), 0)), small, small, small, wide],
        out_specs=[pl.BlockSpec((CH, D), lambda s: (rev(s), 0)), pl.BlockSpec((CH, XBC_W), lambda s: (rev(s), 0)),
                   pl.BlockSpec((CH, CH), lambda s: (rev(s), 0)), small, small, small, wide],
        scratch_shapes=[pltpu.VMEM((8, CH, CH), F32)], compiler_params=_cparams(1))(xa, pdt, proj, hs, dyb, dtb, alog, dsk, nw)


RET_DK = 256


def _log_gamma(h):
    return math.log(1.0 - 2.0 ** (-5.0 - h))


def _rope(x, cos, sin):
    x1, x2 = x[:, :128], x[:, 128:]
    return jnp.concatenate([x1 * cos - x2 * sin, x1 * sin + x2 * cos], axis=1)


def _unrope(d, cos, sin):
    d1, d2 = d[:, :128], d[:, 128:]
    return jnp.concatenate([d1 * cos + d2 * sin, d2 * cos - d1 * sin], axis=1)


def _ret_heads_fwd(q, k, v, cos, sin, vm, r_ins):
    hs = range(RET_HEADS)
    lgs = [_log_gamma(h) for h in hs]
    sls = [slice(RET_DK * h, RET_DK * (h + 1)) for h in hs]
    qr = [_rope(q[:, sls[h]], cos, sin) for h in hs]
    kr = [_rope(k[:, sls[h]], cos, sin) * (RET_DK ** -0.5) for h in hs]
    vr = [v[:, sls[h]] * vm for h in hs]
    rel = (_iota((CH, CH), 0) - _iota((CH, CH), 1)).astype(F32)
    idx = _iota((CH, 1), 0).astype(F32)
    dmask = [jnp.where(rel >= 0, jnp.exp(lgs[h] * jnp.maximum(rel, 0.0)), 0.0) for h in hs]
    kdec = [jnp.exp(lgs[h] * (CH - 1 - idx)) for h in hs]
    qdec = [jnp.exp(lgs[h] * (idx + 1.0)) for h in hs]
    raw = [_dot_nt(qr[h], kr[h]) for h in hs]
    cross = [_dot(qr[h], r_ins[h]) for h in hs]
    kv = [_dot_tn(kr[h] * kdec[h], vr[h]) for h in hs]
    scores = [raw[h] * dmask[h] for h in hs]
    y = [_dot(scores[h], vr[h]) + cross[h] * qdec[h] for h in hs]
    return [dict(qr=qr[h], kr=kr[h], vr=vr[h], dmask=dmask[h], kdec=kdec[h], qdec=qdec[h], scores=scores[h], y=y[h], kv=kv[h],
                 cdec=math.exp(lgs[h] * CH)) for h in hs]


def _group_norm(y):
    mu = jnp.mean(y, axis=-1, keepdims=True)
    yc = y - mu
    r = lax.rsqrt(jnp.mean(yc * yc, axis=-1, keepdims=True) + EPS)
    return yc * r, r


def ret_fwd(proj, cos, sin, *, name):
    t = proj.shape[0]
    nc = t // CH

    def body(q_ref, k_ref, v_ref, g_ref, cos_ref, sin_ref, y_ref, rs_ref, r_scr):
        c = pl.program_id(0)

        @pl.when(c == 0)
        def _():
            r_scr[...] = jnp.zeros_like(r_scr)

        vm = _row_valid(c, CH, CH)
        q, k, v, gt = q_ref[...], k_ref[...], v_ref[...], g_ref[...]
        cos, sin = cos_ref[...], sin_ref[...]
        r_ins = [r_scr[h] for h in range(RET_HEADS)]
        ps = _ret_heads_fwd(q, k, v, cos, sin, vm, r_ins)
        for h, p in enumerate(ps):
            rs_ref[0, h] = r_ins[h]
            r_scr[h] = r_ins[h] * p["cdec"] + p["kv"]
            yn, _ = _group_norm(p["y"])
            sl = slice(RET_DK * h, RET_DK * (h + 1))
            y_ref[:, sl] = yn * _silu(gt[:, sl])

    blk = lambda col: pl.BlockSpec((CH, D), lambda c: (c, col))
    tab = pl.BlockSpec((CH, CH), lambda c: (c, 0))
    return pl.pallas_call(
        body, name=name,
        out_shape=[jax.ShapeDtypeStruct((t, D), F32), jax.ShapeDtypeStruct((nc, RET_HEADS, RET_DK, RET_DK), F32)], grid=(nc,),
        in_specs=[blk(6), blk(7), blk(8), blk(9), tab, tab],
        out_specs=[pl.BlockSpec((CH, D), lambda c: (c, 0)), pl.BlockSpec((1, RET_HEADS, RET_DK, RET_DK), lambda c: (c, 0, 0, 0))],
        scratch_shapes=[pltpu.VMEM((RET_HEADS, RET_DK, RET_DK), F32)], compiler_params=_cparams(1))(proj, proj, proj, proj, cos, sin)


def ret_bwd(proj, cos, sin, rs, dyc, *, name):
    t = proj.shape[0]
    nc = t // CH

    def body(q_ref, k_ref, v_ref, g_ref, cos_ref, sin_ref, rs_ref, dy_ref, o_ref, dr_scr):
        step = pl.program_id(0)
        c = nc - 1 - step

        @pl.when(step == 0)
        def _():
            dr_scr[...] = jnp.zeros_like(dr_scr)

        vm = _row_valid(c, CH, CH)
        q, k, v, gt = q_ref[...], k_ref[...], v_ref[...], g_ref[...]
        cos, sin = cos_ref[...], sin_ref[...]
        dout = dy_ref[...]
        hs = range(RET_HEADS)
        sls = [slice(RET_DK * h, RET_DK * (h + 1)) for h in hs]
        r_ins = [rs_ref[0, h] for h in hs]
        dr_outs = [dr_scr[h] for h in hs]
        ps = _ret_heads_fwd(q, k, v, cos, sin, vm, r_ins)
        dys, dgs = [], []
        for h in hs:
            yn, r = _group_norm(ps[h]["y"])
            gh = gt[:, sls[h]]
            do = dout[:, sls[h]]
            dgs.append(do * yn * _dsilu(gh))
            dyn = do * _silu(gh)
            dys.append(r * (dyn - jnp.mean(dyn, axis=-1, keepdims=True) - yn * jnp.mean(dyn * yn, axis=-1, keepdims=True)))
        dycs = [dys[h] * ps[h]["qdec"] for h in hs]
        dqr = [_dot_nt(dycs[h], r_ins[h]) for h in hs]
        dr_new = [_dot_tn(ps[h]["qr"], dycs[h]) for h in hs]
        dkr = [_dot_nt(ps[h]["vr"], dr_outs[h]) * ps[h]["kdec"] for h in hs]
        dv = [_dot(ps[h]["kr"] * ps[h]["kdec"], dr_outs[h]) + _dot_tn(ps[h]["scores"], dys[h]) for h in hs]
        ds = [_dot_nt(dys[h], ps[h]["vr"]) * ps[h]["dmask"] for h in hs]
        dqr = [dqr[h] + _dot(ds[h], ps[h]["kr"]) for h in hs]
        dkr = [dkr[h] + _dot_tn(ds[h], ps[h]["qr"]) for h in hs]
        for h in hs:
            dr_scr[h] = dr_outs[h] * ps[h]["cdec"] + dr_new[h]
            o_ref[:, RET_DK * h:RET_DK * (h + 1)] = _unrope(dqr[h], cos, sin)
            o_ref[:, D + RET_DK * h:D + RET_DK * (h + 1)] = _unrope(dkr[h], cos, sin) * (RET_DK ** -0.5)
            o_ref[:, 2 * D + RET_DK * h:2 * D + RET_DK * (h + 1)] = dv[h] * vm
            o_ref[:, 3 * D + RET_DK * h:3 * D + RET_DK * (h + 1)] = dgs[h]

    rev = lambda s: nc - 1 - s
    blk = lambda col: pl.BlockSpec((CH, D), lambda s: (rev(s), col))
    tab = pl.BlockSpec((CH, CH), lambda s: (rev(s), 0))
    return pl.pallas_call(
        body, name=name, out_shape=jax.ShapeDtypeStruct((t, 4 * D), F32), grid=(nc,),
        in_specs=[blk(6), blk(7), blk(8), blk(9), tab, tab,
                  pl.BlockSpec((1, RET_HEADS, RET_DK, RET_DK), lambda s: (rev(s), 0, 0, 0)), pl.BlockSpec((CH, D), lambda s: (rev(s), 0))],
        out_specs=pl.BlockSpec((CH, 4 * D), lambda s: (rev(s), 0)),
        scratch_shapes=[pltpu.VMEM((RET_HEADS, RET_DK, RET_DK), F32)], compiler_params=_cparams(1))(proj, proj, proj, proj, cos, sin, rs, dyc)


SB_D = 128
SB_SCALE = SB_D ** -0.5
SB_CUTOFF = 104.0


def _split_hi_lo(x):
    hi = x.astype(BF16)
    lo = (x - hi.astype(F32)).astype(BF16)
    return hi, lo


def _sum_matrix(kind):
    a, b = _iota((128, 128), 0), _iota((128, 128), 1)
    tri = ((b > a) if kind == "after" else (b < a)).astype(BF16)
    return jnp.concatenate([tri, tri], axis=1)


def _key_sums(x, mat2):
    hi, lo = _split_hi_lo(x)
    return jnp.dot(mat2, jnp.concatenate([hi, lo], axis=0), preferred_element_type=F32)


def _sb_mask(d_kq, key_idx, first_key, q_minus_k):
    return (d_kq < q_minus_k) & (key_idx >= N_PAD - first_key)


def _sb_log_sigmoid(z):
    return jnp.minimum(z, 0.0) - jnp.log(1.0 + jnp.exp(-jnp.abs(z)))


def sb_fwd(qkv, *, name):
    t = qkv.shape[0]
    tq = _tok_block(t)
    nq = t // tq
    per = tq // 128

    def body(q_ref, k_ref, v_ref, o_ref, at_ref, bmin_ref):
        h = pl.program_id(0)
        i = pl.program_id(1)
        top = (i + 1) * per - 1
        mat_after = _sum_matrix("after")
        qs = [q_ref[pl.ds(128 * r, 128), :] for r in range(per)]
        d_kq = _iota((128, 128), 0) - _iota((128, 128), 1)
        key_idx = _iota((128, 128), 0)

        def mask(r, b):
            return _sb_mask(d_kq, key_idx, b * 128, (i * per + r - b) * 128)

        def step(carry):
            b, _, a_runs, accs = carry
            off = pl.multiple_of(b * 128, 128)
            kb = k_ref[pl.ds(off, 128), :]
            vb = v_ref[pl.ds(off, 128), :]
            tiles = range(per)
            zs = [_dot_nt(kb, qs[r]) * SB_SCALE for r in tiles]
            ms = [mask(r, b) for r in tiles]
            lss = [_sb_log_sigmoid(zs[r]) for r in tiles]
            lnegs = [jnp.where(ms[r], lss[r] - zs[r], 0.0) for r in tiles]
            sufs = [_key_sums(lnegs[r], mat_after) for r in tiles]
            ws = [jnp.where(ms[r], jnp.exp(lss[r] + a_runs[r] + sufs[r]), 0.0) for r in tiles]
            a_new = [a_runs[r] + sufs[r][0:1, :] + lnegs[r][0:1, :] for r in tiles]
            acc_new = [accs[r] + _dot_tn(ws[r], vb) for r in tiles]
            a_max = jnp.max(functools.reduce(jnp.maximum, a_new))
            return b - 1, a_max >= -SB_CUTOFF, tuple(a_new), tuple(acc_new)

        zeros = tuple(qs[r].astype(F32) * 0.0 for r in range(per))
        zrow = tuple(z[0:1, :] for z in zeros)
        b_end, _, a_runs, accs = lax.while_loop(lambda c: jnp.logical_and(c[0] >= 0, c[1]), step, (top, top >= 0, zrow, zeros))
        bmin_ref[h, i] = b_end + 1
        at_ref[...] = jnp.zeros_like(at_ref)
        for r in range(per):
            o_ref[pl.ds(128 * r, 128), :] = accs[r]
            at_ref[0, 0, r:r + 1, :] = a_runs[r]

    blk = pl.BlockSpec((tq, 128), lambda h, i: (i, h))
    return pl.pallas_call(
        body, name=name,
        out_shape=[jax.ShapeDtypeStruct((t, D), F32), jax.ShapeDtypeStruct((SB_HEADS, nq, 8, 128), F32),
                   jax.ShapeDtypeStruct((SB_HEADS, nq), jnp.int32)],
        grid=(SB_HEADS, nq),
        in_specs=[blk, pl.BlockSpec((t, 128), lambda h, i: (0, SB_HEADS + h)),
                  pl.BlockSpec((t, 128), lambda h, i: (0, 2 * SB_HEADS + h))],
        out_specs=[blk, pl.BlockSpec((1, 1, 8, 128), lambda h, i: (h, i, 0, 0)), pl.BlockSpec(memory_space=pltpu.SMEM)],
        compiler_params=_cparams(2))(qkv, qkv, qkv)


def sb_bwd(qkv, atot, bmin, dout, *, name):
    t = qkv.shape[0]
    tq = _tok_block(t)
    nq = t // tq
    per = tq // 128

    nblk = t // 128

    def body(bmin_ref, q_ref, k_ref, v_ref, at_ref, do_ref, dq_ref, dk_ref, dv_ref):
        i = pl.program_id(1)
        top = (i + 1) * per - 1
        b_first = bmin_ref[pl.program_id(0), i]

        @pl.when(i == 0)
        def _():
            dk_ref[...] = jnp.zeros_like(dk_ref)
            dv_ref[...] = jnp.zeros_like(dv_ref)

        qs = [q_ref[pl.ds(128 * r, 128), :] for r in range(per)]
        dos = [do_ref[pl.ds(128 * r, 128), :].astype(BF16) for r in range(per)]
        q_all = q_ref[...]
        do_all = do_ref[...].astype(BF16)
        a_tots = [at_ref[0, 0, r:r + 1, :] for r in range(per)]
        mat_after = _sum_matrix("after")
        mat_before = _sum_matrix("before")
        d_kq = _iota((128, 128), 0) - _iota((128, 128), 1)
        key_idx = _iota((128, 128), 0)

        def offset(b):
            return pl.multiple_of(jnp.clip(b, 0, nblk - 1) * 128, 128)

        def mask(r, b):
            return _sb_mask(d_kq, key_idx, b * 128, (i * per + r - b) * 128)

        def step(b, carry):
            lss, dws, sufs, tots, p_runs, e_runs, dqs = carry
            off = offset(b)
            kb = k_ref[pl.ds(off, 128), :]
            vb = v_ref[pl.ds(off, 128), :]
            zs = [_dot_nt(kb, qs[r]) * SB_SCALE for r in range(per)]
            dw_new = tuple(_dot_nt(vb, dos[r]) for r in range(per))
            off1 = offset(b - 1)
            k1 = k_ref[pl.ds(off1, 128), :]
            ws, es, epres, sigs, p_new = [], [], [], [], []
            for r in range(per):
                p = p_runs[r] + tots[r]
                w = jnp.where(mask(r, b - 1), jnp.exp(lss[r] + (a_tots[r] - p) + sufs[r]), 0.0)
                e = w * dws[r]
                p_new.append(p)
                ws.append(w.astype(BF16))
                es.append(e)
                epres.append(_key_sums(e, mat_before))
                sigs.append(jnp.exp(lss[r]))
            dv_ref[pl.ds(off1, 128), :] += _dot(jnp.concatenate(ws, axis=1), do_all)
            ls_new, suf_new, tot_new = [], [], []
            for r in range(per):
                ls = _sb_log_sigmoid(zs[r])
                lneg = jnp.where(mask(r, b), ls - zs[r], 0.0)
                suf = _key_sums(lneg, mat_after)
                ls_new.append(ls)
                suf_new.append(suf)
                tot_new.append(suf[0:1, :] + lneg[0:1, :])
            dzs, e_new, dq_new = [], [], []
            for r in range(per):
                t2 = jnp.where(mask(r, b - 1), (e_runs[r] + epres[r]) * sigs[r], 0.0)
                dz = ((es[r] * (1.0 - sigs[r]) - t2) * SB_SCALE).astype(BF16)
                e_new.append(e_runs[r] + epres[r][127:128, :] + es[r][127:128, :])
                dq_new.append(dqs[r] + _dot_tn(dz, k1))
                dzs.append(dz)
            dk_ref[pl.ds(off1, 128), :] += _dot(jnp.concatenate(dzs, axis=1), q_all)
            return tuple(ls_new), dw_new, tuple(suf_new), tuple(tot_new), tuple(p_new), tuple(e_new), tuple(dq_new)

        zeros = tuple(qs[r].astype(F32) * 0.0 for r in range(per))
        zrow = tuple(z[0:1, :] for z in zeros)
        carry = lax.fori_loop(b_first, top + 2, step, (zeros, zeros, zeros, zrow, zrow, zrow, zeros))
        for r in range(per):
            dq_ref[pl.ds(128 * r, 128), :] = carry[6][r]

    head_blk = pl.BlockSpec((t, 128), lambda h, i: (0, h))
    return pl.pallas_call(
        body, name=name, out_shape=[jax.ShapeDtypeStruct((t, D), F32)] * 3, grid=(SB_HEADS, nq),
        in_specs=[pl.BlockSpec(memory_space=pltpu.SMEM),
                  pl.BlockSpec((tq, 128), lambda h, i: (i, h)), pl.BlockSpec((t, 128), lambda h, i: (0, SB_HEADS + h)),
                  pl.BlockSpec((t, 128), lambda h, i: (0, 2 * SB_HEADS + h)),
                  pl.BlockSpec((1, 1, 8, 128), lambda h, i: (h, i, 0, 0)), pl.BlockSpec((tq, 128), lambda h, i: (i, h))],
        out_specs=[pl.BlockSpec((tq, 128), lambda h, i: (i, h)), head_blk, head_blk],
        compiler_params=_cparams(2, 60 * 1024 * 1024))(bmin, qkv, qkv, qkv, atot, dout)


def branch_fwd(y, proj, w, n, *, name, add=None):
    t = y.shape[0]
    tb = _tok_block(t)
    has_add = add is not None

    def body(*refs):
        if has_add:
            y_ref, g_ref, w_ref, a_ref, o_ref = refs
        else:
            y_ref, g_ref, w_ref, o_ref = refs
        r = _sigmoid(g_ref[...]) * _dot(y_ref[...], w_ref[...])
        if has_add:
            r = r + a_ref[...]
        o_ref[...] = r

    blk = pl.BlockSpec((tb, D), lambda i: (i, 0))
    in_specs = [blk, pl.BlockSpec((tb, D), lambda i: (i, COL_GATE + n)), pl.BlockSpec((D, D), lambda i: (0, 0))] + ([blk] if has_add else [])
    args = [y, proj, w] + ([add] if has_add else [])
    return pl.pallas_call(body, name=name, out_shape=jax.ShapeDtypeStruct((t, D), F32), grid=(t // tb,),
                          in_specs=in_specs, out_specs=blk, compiler_params=_cparams(1))(*args)


def branch_bwd(y, proj, w, wt, n, dmerged, *, name):
    t = y.shape[0]
    tb = _tok_block(t)

    def body(y_ref, g_ref, w_ref, wt_ref, dm_ref, dg_ref, dup_ref, dy_ref):
        up = _dot(y_ref[...], w_ref[...])
        gate = _sigmoid(g_ref[...])
        dm = dm_ref[...]
        dg_ref[...] = dm * up * gate * (1.0 - gate)
        dup = (dm * gate).astype(BF16)
        dup_ref[...] = dup
        dy_ref[...] = _dot(dup, wt_ref[...])

    blk = pl.BlockSpec((tb, D), lambda i: (i, 0))
    wspec = pl.BlockSpec((D, D), lambda i: (0, 0))
    return pl.pallas_call(
        body, name=name,
        out_shape=[jax.ShapeDtypeStruct((t, D), F32), jax.ShapeDtypeStruct((t, D), BF16), jax.ShapeDtypeStruct((t, D), F32)],
        grid=(t // tb,),
        in_specs=[blk, pl.BlockSpec((tb, D), lambda i: (i, COL_GATE + n)), wspec, wspec, blk],
        out_specs=[blk, blk, blk], compiler_params=_cparams(1))(y, proj, w, wt, dmerged)


def swiglu_fwd(f, *, name):
    t = f.shape[0]
    tb = _tok_block(t)

    def body(g_ref, u_ref, o_ref):
        o_ref[...] = (_silu(g_ref[...]) * u_ref[...]).astype(BF16)

    return pl.pallas_call(body, name=name, out_shape=jax.ShapeDtypeStruct((t, D_FF), BF16), grid=(t // tb,),
                          in_specs=[pl.BlockSpec((tb, D_FF), lambda i: (i, 0)), pl.BlockSpec((tb, D_FF), lambda i: (i, 1))],
                          out_specs=pl.BlockSpec((tb, D_FF), lambda i: (i, 0)), compiler_params=_cparams(1))(f, f)


def swiglu_bwd(f, dact, *, name):
    t = f.shape[0]
    tb = _tok_block(t)

    def body(g_ref, u_ref, d_ref, o_ref):
        g, u, d = g_ref[...], u_ref[...], d_ref[...]
        o_ref[:, 0:D_FF] = d * u * _dsilu(g)
        o_ref[:, D_FF:2 * D_FF] = d * _silu(g)

    return pl.pallas_call(body, name=name, out_shape=jax.ShapeDtypeStruct((t, 2 * D_FF), F32), grid=(t // tb,),
                          in_specs=[pl.BlockSpec((tb, D_FF), lambda i: (i, 0)), pl.BlockSpec((tb, D_FF), lambda i: (i, 1)),
                                    pl.BlockSpec((tb, D_FF), lambda i: (i, 0))],
                          out_specs=pl.BlockSpec((tb, 2 * D_FF), lambda i: (i, 0)), compiler_params=_cparams(1))(f, f, dact)


def loss_head(h, target, *, name):
    t = h.shape[0]
    nb = t // CH

    def body(h_ref, t_ref, l_ref, d_ref):
        i = pl.program_id(0)

        @pl.when(i == 0)
        def _():
            l_ref[...] = jnp.zeros_like(l_ref)
            d_ref[...] = jnp.zeros_like(d_ref)

        @pl.when(i > 0)
        def _():
            err = h_ref[...] - t_ref[...]
            d_ref[...] = err * (1.0 / D)
            l_ref[...] += jnp.sum(err * err) * (0.5 / D)

    return pl.pallas_call(
        body, name=name, out_shape=[jax.ShapeDtypeStruct((8, 128), F32), jax.ShapeDtypeStruct((t, D), F32)], grid=(nb,),
        in_specs=[pl.BlockSpec((CH, D), lambda i: (i, 0)), pl.BlockSpec((CH, D), lambda i: (jnp.maximum(i - 1, 0), 0))],
        out_specs=[pl.BlockSpec((8, 128), lambda i: (0, 0)), pl.BlockSpec((CH, D), lambda i: (i, 0))],
        compiler_params=_cparams(1))(h, target)


def _pad_rows8(w):
    return jnp.concatenate([w, jnp.zeros((8 - w.shape[0], w.shape[1]), w.dtype)], axis=0)


def _pad_lanes(v, n=CH):
    return jnp.concatenate([v, jnp.zeros((n - v.shape[0],), v.dtype)])[None, :]


def prep_layer(p):
    w = p["w_in"]
    zeros = jnp.zeros((D, CH - SSD_HEADS), w.dtype)
    w_f = jnp.concatenate([w[:, :6144], w[:, 6160:10256], w[:, 13328:17424]], axis=1)
    w_dt = jnp.concatenate([w[:, 6144:6160], zeros], axis=1)
    w_sb = w[:, 10256:13328]
    return dict(
        w_f=w_f, w_sb=w_sb, w_dt=w_dt, w_f_t=w_f.T, w_sb_t=w_sb.T, w_dt_t=w_dt.T,
        w_br=p["w_branch"], w_br_t=jnp.swapaxes(p["w_branch"], 1, 2), w_out=p["w_out"], w_out_t=p["w_out"].T,
        w_fi=p["w_ffn_in"], w_fi_t=p["w_ffn_in"].T, w_fo=p["w_ffn_out"], w_fo_t=p["w_ffn_out"].T,
        conv_a8=_pad_rows8(p["conv_a"]), conv_s8=_pad_rows8(p["ssd_conv_w"]), conv_sb=p["ssd_conv_b"][None, :],
        dtb=_pad_lanes(p["ssd_dt_bias"]), alog=_pad_lanes(p["ssd_a_log"]), dsk=_pad_lanes(p["ssd_d"]), nw=p["ssd_norm"][None, :],
        n1=p["norm_mix_pre"][None, :], n2=p["norm_mix_post"][None, :], n3=p["norm_ffn_pre"][None, :], n4=p["norm_ffn_post"][None, :])


def layer_fwd(h0, w, cos, sin, l):
    nm = lambda s: f"l{l}_{s}"
    hn = rms_fwd(h0, w["n1"], name=nm("rms1"), out_dtype=BF16)
    proj = mm(hn, w["w_f"], name=nm("proj_f"))
    qkv = mm(hn, w["w_sb"], name=nm("proj_sb"), out_dtype=BF16)
    pdt = mm(hn, w["w_dt"], name=nm("proj_dt"))
    y_a = conv_a_fwd(proj, w["conv_a8"], name=nm("conv_a"))
    xa = ssd_conv_fwd(proj, w["conv_s8"], w["conv_sb"], name=nm("ssd_conv"))
    y_b, hs = ssd_fwd(xa, proj, pdt, w["dtb"], w["alog"], w["dsk"], w["nw"], name=nm("ssd"))
    y_c, rs = ret_fwd(proj, cos, sin, name=nm("ret"))
    y_d, sb_atot, sb_bmin = sb_fwd(qkv, name=nm("sb"))
    ys = (y_a, y_b, y_c, y_d)
    merged = None
    for n in range(4):
        merged = branch_fwd(ys[n], proj, w["w_br"][n], n, name=nm(f"branch{n}"), add=merged)
    mix = mm(merged, w["w_out"], name=nm("mix"))
    h1 = rms_fwd(mix, w["n2"], name=nm("rms2"), res=h0)
    hn2 = rms_fwd(h1, w["n3"], name=nm("rms3"), out_dtype=BF16)
    f = mm(hn2, w["w_fi"], name=nm("ffn_in"))
    act = swiglu_fwd(f, name=nm("swiglu"))
    f2 = mm(act, w["w_fo"], name=nm("ffn_out"))
    h2 = rms_fwd(f2, w["n4"], name=nm("rms4"), res=h1)
    saved = dict(h0=h0, hn=hn, proj=proj, qkv=qkv, pdt=pdt, xa=xa, hs=hs, rs=rs, ys=ys, sb_atot=sb_atot, sb_bmin=sb_bmin, merged=merged, mix=mix, h1=h1, hn2=hn2,
                 f=f, act=act, f2=f2)
    return h2, saved


def layer_bwd(dh2, s, w, cos, sin, l):
    nm = lambda t: f"l{l}_{t}"
    g = {}
    df2, g["n4"] = rms_bwd(s["f2"], w["n4"], dh2, name=nm("rms4_b"))
    g["w_fo"] = mm_tn(s["act"], df2, name=nm("ffn_out_dw"))
    dact = mm(df2, w["w_fo_t"], name=nm("ffn_out_dx"), out_dtype=BF16)
    df = swiglu_bwd(s["f"], dact, name=nm("swiglu_b"))
    g["w_fi"] = mm_tn(s["hn2"], df, name=nm("ffn_in_dw"))
    dhn2 = mm(df, w["w_fi_t"], name=nm("ffn_in_dx"))
    dh1, g["n3"] = rms_bwd(s["h1"], w["n3"], dhn2, name=nm("rms3_b"), add=dh2)
    dmix, g["n2"] = rms_bwd(s["mix"], w["n2"], dh1, name=nm("rms2_b"))
    g["w_out"] = mm_tn(s["merged"], dmix, name=nm("mix_dw"))
    dmerged = mm(dmix, w["w_out_t"], name=nm("mix_dx"))
    dgate, dys, dwb = [], [], []
    for n in range(4):
        dg_n, dup_n, dy_n = branch_bwd(s["ys"][n], s["proj"], w["w_br"][n], w["w_br_t"][n], n, dmerged, name=nm(f"branch{n}_b"))
        dgate.append(dg_n)
        dys.append(dy_n)
        dwb.append(mm_tn(s["ys"][n], dup_n, name=nm(f"branch{n}_dw")))
    g["w_br"] = jnp.stack(dwb)
    d_a, g["conv_a8"] = conv_a_bwd(s["proj"], w["conv_a8"], dys[0], name=nm("conv_a_b"))
    dz, dxa, ddt, g["dtb"], g["alog"], g["dsk"], g["nw"] = ssd_bwd(
        s["xa"], s["proj"], s["pdt"], s["hs"], dys[1], w["dtb"], w["alog"], w["dsk"], w["nw"], name=nm("ssd_b"))
    dpre, g["conv_s8"], g["conv_sb"] = ssd_conv_bwd_pre(s["proj"], w["conv_s8"], w["conv_sb"], dxa, name=nm("ssd_conv_b1"))
    dxbc = ssd_conv_bwd_in(dpre, w["conv_s8"], name=nm("ssd_conv_b2"))
    d_r = ret_bwd(s["proj"], cos, sin, s["rs"], dys[2], name=nm("ret_b"))
    d_sb = sb_bwd(s["qkv"], s["sb_atot"], s["sb_bmin"], dys[3], name=nm("sb_b"))
    segs = [(d_a, 0), (dz, 3072), (dxbc, 4096), (d_r, 6144), (dgate[0], 10240), (dgate[1], 11264), (dgate[2], 12288),
            (dgate[3], 13312)]
    dws = [mm_tn(s["hn"], d, name=nm(f"proj_dw{k}")) for k, (d, _) in enumerate(segs)]
    g["w_f"] = jnp.concatenate(dws, axis=1)
    g["w_sb"] = jnp.concatenate([mm_tn(s["hn"], d, name=nm(f"proj_dw_sb{k}")) for k, d in enumerate(d_sb)], axis=1)
    g["w_dt"] = mm_tn(s["hn"], ddt, name=nm("proj_dw_dt"))
    dhn = mm(ddt, w["w_dt_t"], name=nm("proj_dx_dt"))
    dhn = mm_segments([d for d, _ in segs] + list(d_sb), jnp.concatenate([w["w_f_t"], w["w_sb_t"]], axis=0), dhn,
                      name=nm("proj_dx"))
    dh0, g["n1"] = rms_bwd(s["h0"], w["n1"], dhn, name=nm("rms1_b"), add=dh1)
    return dh0, g


def layer_grads_to_params(g):
    wf, wsb = g["w_f"], g["w_sb"]
    w_in = jnp.concatenate([wf[:, :6144], g["w_dt"][:, :SSD_HEADS], wf[:, 6144:10240], wsb, wf[:, 10240:14336]], axis=1)
    return dict(
        w_in=w_in, conv_a=g["conv_a8"][:3], ssd_conv_w=g["conv_s8"][:4], ssd_conv_b=g["conv_sb"][0],
        ssd_dt_bias=g["dtb"][0, :SSD_HEADS], ssd_a_log=g["alog"][0, :SSD_HEADS], ssd_d=g["dsk"][0, :SSD_HEADS], ssd_norm=g["nw"][0],
        w_branch=g["w_br"], w_out=g["w_out"], w_ffn_in=g["w_fi"], w_ffn_out=g["w_fo"],
        norm_mix_pre=g["n1"][0], norm_mix_post=g["n2"][0], norm_ffn_pre=g["n3"][0], norm_ffn_post=g["n4"][0])


def rope_tables(t):
    half = RET_DK // 2
    inv = ROPE_BASE ** (-jnp.arange(half, dtype=F32) / half)
    ang = jnp.arange(t).astype(F32)[:, None] * inv[None, :]
    return jnp.cos(ang), jnp.sin(ang)


def local_step(x, target, meta, layers):
    h = jnp.concatenate([jnp.zeros((N_PAD, D), F32), meta, x], axis=0)
    t = h.shape[0]
    cos, sin = rope_tables(t)
    ws = [prep_layer(p) for p in layers]
    saved = []
    for l, w in enumerate(ws):
        h, s = layer_fwd(h, w, cos, sin, l)
        saved.append(s)
    loss, dh = loss_head(h, target, name="loss_head")
    grads = [None] * len(ws)
    for l in reversed(range(len(ws))):
        dh, g = layer_bwd(dh, saved[l], ws[l], cos, sin, l)
        grads[l] = layer_grads_to_params(g)
    return loss, dh[CH:], dh[N_PAD:CH], grads


def _my_place():
    return lax.axis_index("x"), lax.axis_index("y"), lax.axis_index("c")


def _flat(px, py, pc):
    return 4 * px + 2 * py + pc


ANY = pl.BlockSpec(memory_space=pl.ANY)


def all_gather(x_shard, *, name):
    shape = x_shard.shape

    def body(x_ref, out_ref, send_sems, recv_sems, local_sem):
        x, y, c = _my_place()
        me, sibling = (x, y, c), (x, y, 1 - c)
        chips = [(1 - x, y), (x, 1 - y), (1 - x, 1 - y)]

        def rows(px, py, pc):
            return out_ref.at[_flat(px, py, pc)]

        def copy(k, block, to, src=None):
            return pltpu.make_async_remote_copy(
                src_ref=rows(*block) if src is None else src, dst_ref=rows(*block),
                send_sem=send_sems.at[k], recv_sem=recv_sems.at[k], device_id=to, device_id_type=MESH_ID)

        mine = pltpu.make_async_copy(x_ref, rows(*me), local_sem)
        mine.start()
        first = [copy(0, me, sibling, src=x_ref)]
        first += [copy(1 + j, me, (*chip, c), src=x_ref) for j, chip in enumerate(chips)]
        for cp in first:
            cp.start()
        passed = [copy(4 + j, (*chip, c), sibling) for j, chip in enumerate(chips)]
        for j, chip in enumerate(chips):
            copy(1 + j, (*chip, c), me).wait_recv()
            passed[j].start()
        copy(0, sibling, me).wait_recv()
        for j, chip in enumerate(chips):
            copy(4 + j, (*chip, 1 - c), me).wait_recv()
        for cp in first + passed:
            cp.wait_send()
        mine.wait()

    return pl.pallas_call(
        body, name=name, out_shape=jax.ShapeDtypeStruct((N_DEV,) + shape, x_shard.dtype),
        in_specs=[ANY], out_specs=ANY,
        scratch_shapes=[pltpu.SemaphoreType.DMA((7,)), pltpu.SemaphoreType.DMA((7,)), pltpu.SemaphoreType.DMA],
    )(x_shard)


N_CHIP = 4


def pair_exchange(g, *, name):
    shape = (N_CHIP,) + g.shape[1:]

    def body(g_ref, recv_ref, send_sems, recv_sems):
        x, y, c = _my_place()
        sibling = (x, y, 1 - c)
        sends = [pltpu.make_async_remote_copy(
            src_ref=g_ref.at[2 * j + 1 - c], dst_ref=recv_ref.at[j], send_sem=send_sems.at[j], recv_sem=recv_sems.at[j],
            device_id=sibling, device_id_type=MESH_ID) for j in range(N_CHIP)]
        for cp in sends:
            cp.start()
        for cp in sends:
            cp.wait_recv()
        for cp in sends:
            cp.wait_send()

    return pl.pallas_call(
        body, name=name, out_shape=jax.ShapeDtypeStruct(shape, g.dtype), in_specs=[ANY], out_specs=ANY,
        scratch_shapes=[pltpu.SemaphoreType.DMA((N_CHIP,)), pltpu.SemaphoreType.DMA((N_CHIP,))],
    )(g)


def pair_sum(g, from_sibling, core, *, name):
    n, r, cols = from_sibling.shape
    tb = _pick(r, (512, 256, 128))

    def body(core_ref, a_ref, b_ref, o_ref):
        o_ref[...] = (a_ref[...].astype(F32) + b_ref[...].astype(F32)).astype(o_ref.dtype)

    blk = pl.BlockSpec((1, tb, cols), lambda j, i, core_ref: (j, i, 0))
    grid_spec = pltpu.PrefetchScalarGridSpec(
        num_scalar_prefetch=1, grid=(n, r // tb),
        in_specs=[pl.BlockSpec((1, tb, cols), lambda j, i, core_ref: (2 * j + core_ref[0], i, 0)), blk], out_specs=blk)
    return pl.pallas_call(body, name=name, out_shape=jax.ShapeDtypeStruct(from_sibling.shape, from_sibling.dtype),
                          grid_spec=grid_spec, compiler_params=_cparams(2))(core, g, from_sibling)


def chip_exchange(s, *, name):
    def body(s_ref, out_ref, send_sems, recv_sems, local_sem):
        x, y, c = _my_place()
        me = 2 * x + y
        mine = pltpu.make_async_copy(s_ref.at[me], out_ref.at[me], local_sem)
        mine.start()
        peers = [(jnp.bitwise_xor(x, k >> 1), jnp.bitwise_xor(y, k & 1)) for k in range(1, N_CHIP)]
        sends = []
        for k, (px, py) in enumerate(peers):
            cp = pltpu.make_async_remote_copy(
                src_ref=s_ref.at[2 * px + py], dst_ref=out_ref.at[me], send_sem=send_sems.at[k], recv_sem=recv_sems.at[k],
                device_id=(px, py, c), device_id_type=MESH_ID)
            cp.start()
            sends.append(cp)
        for k, (px, py) in enumerate(peers):
            slot = out_ref.at[2 * px + py]
            pltpu.make_async_remote_copy(
                src_ref=slot, dst_ref=slot, send_sem=send_sems.at[k], recv_sem=recv_sems.at[k],
                device_id=(px, py, c), device_id_type=MESH_ID).wait_recv()
        for cp in sends:
            cp.wait_send()
        mine.wait()

    return pl.pallas_call(
        body, name=name, out_shape=jax.ShapeDtypeStruct(s.shape, s.dtype), in_specs=[ANY], out_specs=ANY,
        scratch_shapes=[pltpu.SemaphoreType.DMA((N_CHIP - 1,)), pltpu.SemaphoreType.DMA((N_CHIP - 1,)), pltpu.SemaphoreType.DMA],
    )(s)


def sum_slots(a, *, name):
    def body(a_ref, o_ref):
        s = a_ref[0]
        for d in range(1, N_DEV):
            s = s + a_ref[d]
        o_ref[...] = s

    return pl.pallas_call(body, name=name, out_shape=jax.ShapeDtypeStruct(a.shape[1:], a.dtype))(a)


def _adamw_math(w, g, m, v):
    m = ADAM_B1 * m + (1.0 - ADAM_B1) * g
    v = ADAM_B2 * v + (1.0 - ADAM_B2) * (g * g)
    m_hat = m / (1.0 - ADAM_B1 ** ADAM_STEP)
    v_hat = v / (1.0 - ADAM_B2 ** ADAM_STEP)
    delta = -ADAM_LR * (m_hat / (jnp.sqrt(v_hat) + ADAM_EPS) + ADAM_WD * w)
    return delta, m, v


def adamw_big(recv, w, m, v, *, name):
    r, cols = w.shape
    tb = 128
    n = recv.shape[0]

    def body(r_ref, w_ref, m_ref, v_ref, g_ref, d_ref, nm_ref, nv_ref):
        g = r_ref[0].astype(F32)
        for d in range(1, n):
            g = g + r_ref[d].astype(F32)
        g_ref[...] = g
        d_ref[...], nm_ref[...], nv_ref[...] = _adamw_math(w_ref[...], g, m_ref[...], v_ref[...])

    blk = pl.BlockSpec((tb, cols), lambda i: (i, 0))
    out = jax.ShapeDtypeStruct((r, cols), F32)
    return pl.pallas_call(
        body, name=name, out_shape=[out] * 4, grid=(r // tb,),
        in_specs=[pl.BlockSpec((n, tb, cols), lambda i: (0, i, 0)), blk, blk, blk], out_specs=[blk] * 4,
        compiler_params=_cparams(1))(recv, w, m, v)


def adamw_small(w, g, m, v, *, name):
    def body(w_ref, g_ref, m_ref, v_ref, d_ref, nm_ref, nv_ref):
        d_ref[...], nm_ref[...], nv_ref[...] = _adamw_math(w_ref[...], g_ref[...], m_ref[...], v_ref[...])

    out = jax.ShapeDtypeStruct(w.shape, F32)
    return pl.pallas_call(body, name=name, out_shape=[out] * 3)(w, g, m, v)


BIG = ("w_in", "w_branch", "w_out", "w_ffn_in", "w_ffn_out")
BIG_SHARD = {"w_in": (DEPTH, D, 2178), "w_branch": (DEPTH, 4, 128, D), "w_out": (DEPTH, 128, D),
             "w_ffn_in": (DEPTH, D, 704), "w_ffn_out": (DEPTH, 352, D)}
BIG_FULL = {"w_in": ((1, 2, 0, 3), (DEPTH, D, 17424)), "w_branch": ((1, 2, 0, 3, 4), (DEPTH, 4, D, D)),
            "w_out": ((1, 0, 2, 3), (DEPTH, D, D)), "w_ffn_in": ((1, 2, 0, 3), (DEPTH, D, 2 * D_FF)),
            "w_ffn_out": ((1, 0, 2, 3), (DEPTH, D_FF, D))}
BIG_ROWS = {n: int(np.prod(s)) // D for n, s in BIG_SHARD.items()}
BIG_R = 7808


def pack_big(shards, dtype):
    parts = [shards[n].astype(dtype).reshape(BIG_ROWS[n], D) for n in BIG]
    parts.append(jnp.zeros((BIG_R - sum(BIG_ROWS.values()), D), dtype))
    return jnp.concatenate(parts, axis=0)


def unpack_big(flat):
    out, o = {}, 0
    for n in BIG:
        out[n] = flat[o:o + BIG_ROWS[n]].reshape(BIG_SHARD[n])
        o += BIG_ROWS[n]
    return out


def unpack_big_full(gathered):
    out, o = {}, 0
    for n in BIG:
        perm, full = BIG_FULL[n]
        out[n] = gathered[:, o:o + BIG_ROWS[n]].reshape((N_DEV,) + BIG_SHARD[n]).transpose(perm).reshape(full)
        o += BIG_ROWS[n]
    return out


def pack_big_full(full, dtype):
    parts = []
    for n in BIG:
        perm, _ = BIG_FULL[n]
        split = tuple(int(v) for v in np.array((N_DEV,) + BIG_SHARD[n])[list(perm)])
        inv = tuple(int(i) for i in np.argsort(perm))
        parts.append(full[n].astype(dtype).reshape(split).transpose(inv).reshape(N_DEV, BIG_ROWS[n], D))
    parts.append(jnp.zeros((N_DEV, BIG_R - sum(BIG_ROWS.values()), D), dtype))
    return jnp.concatenate(parts, axis=1)


def _rows128(a):
    a = a.reshape(-1)
    pad = (-a.shape[0]) % CH
    if pad:
        a = jnp.concatenate([a, jnp.zeros((pad,), a.dtype)])
    return a.reshape(-1, CH)


def _pack_rows(arrs, total):
    parts = [_rows128(a) for a in arrs]
    n = sum(p.shape[0] for p in parts)
    parts.append(jnp.zeros((total - n, CH), F32))
    return jnp.concatenate(parts, axis=0)


def _unpack_rows(flat, shapes):
    out, o = [], 0
    for s in shapes:
        size = int(np.prod(s))
        rows = -(-size // CH)
        out.append(flat[o:o + rows].reshape(-1)[:size].reshape(s))
        o += rows
    return out


SMALL_SHARDED = ("meta", "conv_a", "ssd_conv_w")
SMALL_SHARD_SHAPE = {"meta": (N_META, 128), "conv_a": (DEPTH, 3, 128), "ssd_conv_w": (DEPTH, 4, 256)}
SMALL_FULL_SHAPE = {"meta": (N_META, D), "conv_a": (DEPTH, 3, D), "ssd_conv_w": (DEPTH, 4, 2048)}
SMALL_REPL = ("ssd_conv_b", "ssd_dt_bias", "ssd_a_log", "ssd_d", "ssd_norm", "norm_mix_pre", "norm_mix_post", "norm_ffn_pre",
              "norm_ffn_post")
SMALL_REPL_SHAPE = {"ssd_conv_b": (DEPTH, 2048), "ssd_dt_bias": (DEPTH, SSD_HEADS), "ssd_a_log": (DEPTH, SSD_HEADS),
                    "ssd_d": (DEPTH, SSD_HEADS), "ssd_norm": (DEPTH, D), "norm_mix_pre": (DEPTH, D), "norm_mix_post": (DEPTH, D),
                    "norm_ffn_pre": (DEPTH, D), "norm_ffn_post": (DEPTH, D)}


def _gather_small_full(gathered, n):
    nd = gathered.ndim
    perm = tuple(range(1, nd - 1)) + (0, nd - 1)
    return gathered.transpose(perm).reshape(SMALL_FULL_SHAPE[n])


WEIGHTS = ("meta", "w_in", "conv_a", "ssd_conv_w", "ssd_conv_b", "ssd_dt_bias", "ssd_a_log", "ssd_d", "ssd_norm", "w_branch", "w_out",
           "w_ffn_in", "w_ffn_out", "norm_mix_pre", "norm_mix_post", "norm_ffn_pre", "norm_ffn_post")


def kernel(x, meta, w_in, conv_a, ssd_conv_w, ssd_conv_b, ssd_dt_bias, ssd_a_log, ssd_d, ssd_norm, w_branch, w_out, w_ffn_in, w_ffn_out, norm_mix_pre, norm_mix_post, norm_ffn_pre, norm_ffn_post, loss_target, m_meta, m_w_in, m_conv_a, m_ssd_conv_w, m_ssd_conv_b, m_ssd_dt_bias, m_ssd_a_log, m_ssd_d, m_ssd_norm, m_w_branch, m_w_out, m_w_ffn_in, m_w_ffn_out, m_norm_mix_pre, m_norm_mix_post, m_norm_ffn_pre, m_norm_ffn_post, v_meta, v_w_in, v_conv_a, v_ssd_conv_w, v_ssd_conv_b, v_ssd_dt_bias, v_ssd_a_log, v_ssd_d, v_ssd_norm, v_w_branch, v_w_out, v_w_ffn_in, v_w_ffn_out, v_norm_mix_pre, v_norm_mix_post, v_norm_ffn_pre, v_norm_ffn_post):
    w = dict(meta=meta, w_in=w_in, conv_a=conv_a, ssd_conv_w=ssd_conv_w, ssd_conv_b=ssd_conv_b, ssd_dt_bias=ssd_dt_bias,
             ssd_a_log=ssd_a_log, ssd_d=ssd_d, ssd_norm=ssd_norm, w_branch=w_branch, w_out=w_out, w_ffn_in=w_ffn_in,
             w_ffn_out=w_ffn_out, norm_mix_pre=norm_mix_pre, norm_mix_post=norm_mix_post, norm_ffn_pre=norm_ffn_pre,
             norm_ffn_post=norm_ffn_post)
    m = dict(meta=m_meta, w_in=m_w_in, conv_a=m_conv_a, ssd_conv_w=m_ssd_conv_w, ssd_conv_b=m_ssd_conv_b, ssd_dt_bias=m_ssd_dt_bias,
             ssd_a_log=m_ssd_a_log, ssd_d=m_ssd_d, ssd_norm=m_ssd_norm, w_branch=m_w_branch, w_out=m_w_out, w_ffn_in=m_w_ffn_in,
             w_ffn_out=m_w_ffn_out, norm_mix_pre=m_norm_mix_pre, norm_mix_post=m_norm_mix_post, norm_ffn_pre=m_norm_ffn_pre,
             norm_ffn_post=m_norm_ffn_post)
    v = dict(meta=v_meta, w_in=v_w_in, conv_a=v_conv_a, ssd_conv_w=v_ssd_conv_w, ssd_conv_b=v_ssd_conv_b, ssd_dt_bias=v_ssd_dt_bias,
             ssd_a_log=v_ssd_a_log, ssd_d=v_ssd_d, ssd_norm=v_ssd_norm, w_branch=v_w_branch, w_out=v_w_out, w_ffn_in=v_w_ffn_in,
             w_ffn_out=v_w_ffn_out, norm_mix_pre=v_norm_mix_pre, norm_mix_post=v_norm_mix_post, norm_ffn_pre=v_norm_ffn_pre,
             norm_ffn_post=v_norm_ffn_post)
    xi, yi, ci = _my_place()
    dev = _flat(xi, yi, ci)

    full = unpack_big_full(all_gather(pack_big(w, BF16), name="gather_big"))
    small_shard = _pack_rows([w[n] for n in SMALL_SHARDED], 40)
    small_all = all_gather(small_shard, name="gather_small")
    small_full = {}
    o = 0
    for n in SMALL_SHARDED:
        rows = int(np.prod(SMALL_SHARD_SHAPE[n])) // CH
        small_full[n] = _gather_small_full(small_all[:, o:o + rows].reshape((N_DEV,) + SMALL_SHARD_SHAPE[n]), n)
        o += rows

    layers = []
    for l in range(DEPTH):
        p = {n: full[n][l] for n in BIG}
        p["conv_a"] = small_full["conv_a"][l]
        p["ssd_conv_w"] = small_full["ssd_conv_w"][l]
        for n in SMALL_REPL:
            p[n] = w[n][l]
        layers.append(p)

    loss_blk, grad_x, gmeta, grads = local_step(x[0], loss_target[0], small_full["meta"], layers)

    gfull = {n: jnp.stack([grads[l][n] for l in range(DEPTH)]) for n in BIG}
    partials = pack_big_full(gfull, BF16)
    from_sibling = pair_exchange(partials, name="exchange_pair")
    core = jnp.reshape(ci, (1,)).astype(jnp.int32)
    recv = chip_exchange(pair_sum(partials, from_sibling, core, name="sum_pair"), name="exchange_chip")
    g_flat, d_flat, nm_flat, nv_flat = adamw_big(recv, pack_big(w, F32), pack_big(m, F32), pack_big(v, F32), name="adamw_big")
    out_g, out_d, out_m, out_v = unpack_big(g_flat), unpack_big(d_flat), unpack_big(nm_flat), unpack_big(nv_flat)

    small_names = SMALL_SHARDED + SMALL_REPL
    small_grads = [gmeta] + [jnp.stack([grads[l][n] for l in range(DEPTH)]) for n in small_names[1:]]
    small_shapes = [SMALL_FULL_SHAPE[n] for n in SMALL_SHARDED] + [SMALL_REPL_SHAPE[n] for n in SMALL_REPL]
    sm = _pack_rows(small_grads + [loss_blk[0:1]], 424)
    sm_sum = sum_slots(all_gather(sm, name="gather_small_grads"), name="sum_small_grads")
    summed = _unpack_rows(sm_sum, small_shapes + [(1, CH)])
    loss = summed[-1][0, 0]
    sg = dict(zip(small_names, summed[:-1]))
    for n in SMALL_SHARDED:
        width = SMALL_SHARD_SHAPE[n][-1]
        sg[n] = lax.dynamic_slice_in_dim(sg[n], dev * width, width, axis=sg[n].ndim - 1)
    pk = lambda d: _pack_rows([d[n] for n in small_names], 160)
    sd, snm, snv = adamw_small(pk(w), pk(sg), pk(m), pk(v), name="adamw_small")
    shard_shapes = [SMALL_SHARD_SHAPE[n] for n in SMALL_SHARDED] + [SMALL_REPL_SHAPE[n] for n in SMALL_REPL]
    for dst, flat in ((out_d, sd), (out_m, snm), (out_v, snv)):
        dst.update(zip(small_names, _unpack_rows(flat, shard_shapes)))
    out_g.update(sg)

    return (loss, grad_x[None], *[out_g[n] for n in WEIGHTS], *[out_d[n] for n in WEIGHTS], *[out_m[n] for n in WEIGHTS],
            *[out_v[n] for n in WEIGHTS])
```
